```python
import jax, jax.numpy as jnp
from jax import lax
import numpy as np

D_MODEL = 1024
BATCH = 1
SEQ = 16384
DEPTH = 1

POOL_WIDTH = D_MODEL // 2
POOL_WINDOWS = (2, 4, 8, 16)
N_POOL_GROUPS = len(POOL_WINDOWS)
POOL_GROUP = POOL_WIDTH // N_POOL_GROUPS
HEAD_DIM = 64
N_HEADS = (D_MODEL - POOL_WIDTH) // HEAD_DIM
N_KV = 2
GQA_GROUP = N_HEADS // N_KV
KV_WIDTH = N_KV * HEAD_DIM
N_BRANCH = 3
CMP_LEN = 32
CMP_STRIDE = 16
CMP_HIDDEN = 4 * HEAD_DIM
SEL_BLOCK = 64
N_SEL = 16
WINDOW = 512
Q_BLOCK = 128
MIX_WIDTH = POOL_WIDTH + N_HEADS * HEAD_DIM
IN_WIDTH = POOL_WIDTH + N_HEADS * HEAD_DIM + 6 * KV_WIDTH + N_BRANCH * N_HEADS
D_FF = 4 * D_MODEL
NEG_INF = -1e30
FORCE_BONUS = 1e4
EPS = 1e-6

kernel_name = "hybrid_pool_nsa_adaln_block"


def rms_norm(x, g):
    xf = x.astype(jnp.float32)
    y = xf * lax.rsqrt(jnp.mean(xf * xf, axis=-1, keepdims=True) + EPS)
    return (y * g.astype(jnp.float32)).astype(x.dtype)


def alibi_slopes(n):
    return jnp.asarray([2.0 ** (-8.0 * (h + 1) / n) for h in range(n)], jnp.float32)


def causal_multiscale_pool(u, w_pool, pool_scale):
    B, S, _ = u.shape
    ug = u.reshape(B, S, N_POOL_GROUPS, POOL_GROUP)
    cs = jnp.cumsum(ug.astype(jnp.float32), axis=1)
    cs = jnp.concatenate([jnp.zeros_like(cs[:, :1]), cs], axis=1)
    t = jnp.arange(S)
    means = []
    for gi, w in enumerate(POOL_WINDOWS):
        lo = jnp.maximum(t + 1 - w, 0)
        win_sum = cs[:, t + 1, gi] - cs[:, lo, gi]
        cnt = (t + 1 - lo).astype(jnp.float32)[None, :, None]
        means.append(win_sum / cnt)
    pooled = jnp.stack(means, axis=2).astype(u.dtype) - ug
    y = jnp.einsum('bsgc,gcd->bsgd', pooled, w_pool)
    return y.reshape(B, S, POOL_WIDTH) * pool_scale


def compress_blocks(kv, pos_emb, w1, b1, w2, b2):
    B, S = kv.shape[0], kv.shape[1]
    n_cmp = (S - CMP_LEN) // CMP_STRIDE + 1
    idx = jnp.arange(n_cmp)[:, None] * CMP_STRIDE + jnp.arange(CMP_LEN)[None, :]
    blocks = kv[:, idx] + pos_emb[:, None, :]
    flat = blocks.transpose(0, 1, 3, 2, 4).reshape(B, n_cmp, N_KV, CMP_LEN * HEAD_DIM)
    hid = jax.nn.gelu(flat @ w1 + b1)
    return hid @ w2 + b2


def nsa_attention(q, k_cmp, v_cmp, k_sel, v_sel, k_win, v_win, gates):
    B, S = q.shape[0], q.shape[1]
    n_cmp = k_cmp.shape[1]
    n_blk = S // SEL_BLOCK
    n_top = min(N_SEL, n_blk)
    n_qb = S // Q_BLOCK
    scale = HEAD_DIM ** -0.5
    slopes_gr = alibi_slopes(N_HEADS).reshape(N_KV, GQA_GROUP)

    cmp_start = jnp.arange(n_cmp) * CMP_STRIDE
    cmp_end = cmp_start + CMP_LEN - 1
    sel_start = jnp.arange(n_blk) * SEL_BLOCK
    overlap = jnp.clip(jnp.minimum(cmp_start[:, None] + CMP_LEN, sel_start[None, :] + SEL_BLOCK)
                       - jnp.maximum(cmp_start[:, None], sel_start[None, :]), 0, None)
    overlap = (overlap / CMP_LEN).astype(jnp.float32)

    qg = q.reshape(B, S, N_KV, GQA_GROUP, HEAD_DIM)
    gg = gates.reshape(B, S, N_KV, GQA_GROUP, N_BRANCH)
    k_sel_b = k_sel.reshape(B, n_blk, SEL_BLOCK, N_KV, HEAD_DIM).transpose(0, 3, 1, 2, 4)
    v_sel_b = v_sel.reshape(B, n_blk, SEL_BLOCK, N_KV, HEAD_DIM).transpose(0, 3, 1, 2, 4)
    pad = jnp.zeros((B, WINDOW, N_KV, HEAD_DIM), k_win.dtype)
    k_win_p = jnp.concatenate([pad, k_win], axis=1)
    v_win_p = jnp.concatenate([pad.astype(v_win.dtype), v_win], axis=1)
    b_ix = jnp.arange(B)[:, None, None, None]
    g_ix = jnp.arange(N_KV)[None, :, None, None]
    blk = jnp.arange(n_blk)

    def block_fn(qb):
        q0 = qb * Q_BLOCK
        t = q0 + jnp.arange(Q_BLOCK)
        qblk = lax.dynamic_slice_in_dim(qg, q0, Q_BLOCK, axis=1)
        gblk = lax.dynamic_slice_in_dim(gg, q0, Q_BLOCK, axis=1)

        d_c = (t[:, None] - cmp_end[None, :]).astype(jnp.float32)
        ok_c = d_c >= 0
        s_c = (jnp.einsum('btgrd,bngd->bgrtn', qblk, k_cmp, preferred_element_type=jnp.float32) * scale
               - slopes_gr[:, :, None, None] * d_c)
        s_c = jnp.where(ok_c, s_c, NEG_INF)
        p_c = jnp.where(ok_c, jax.nn.softmax(s_c, axis=-1), 0.0)
        o_c = jnp.einsum('bgrtn,bngd->btgrd', p_c.astype(v_cmp.dtype), v_cmp)

        imp = jnp.einsum('bgrtn,nj->bgtj', p_c, overlap)
        cur = t // SEL_BLOCK
        causal = blk[None, :] <= cur[:, None]
        forced = (blk[None, :] == 0) | (blk[None, :] == cur[:, None]) | (blk[None, :] == cur[:, None] - 1)
        imp = jnp.where(causal, imp + FORCE_BONUS * forced, -1.0)
        top_val, top_idx = lax.top_k(imp, n_top)
        ok_blk = top_val >= 0

        k_g = k_sel_b[b_ix, g_ix, top_idx]
        v_g = v_sel_b[b_ix, g_ix, top_idx]
        pos = top_idx[..., None] * SEL_BLOCK + jnp.arange(SEL_BLOCK)
        d_s = (t[None, None, :, None, None] - pos).astype(jnp.float32)
        ok_s = (d_s >= 0) & ok_blk[..., None]
        s_s = (jnp.einsum('btgrd,bgtnld->bgrtnl', qblk, k_g, preferred_element_type=jnp.float32) * scale
               - slopes_gr[None, :, :, None, None, None] * d_s[:, :, None])
        s_s = jnp.where(ok_s[:, :, None], s_s, NEG_INF)
        p_s = jax.nn.softmax(s_s.reshape(B, N_KV, GQA_GROUP, Q_BLOCK, n_top * SEL_BLOCK), axis=-1)
        p_s = p_s.reshape(s_s.shape)
        o_s = jnp.einsum('bgrtnl,bgtnld->btgrd', p_s.astype(v_g.dtype), v_g)

        k_w = lax.dynamic_slice_in_dim(k_win_p, q0, Q_BLOCK + WINDOW, axis=1)
        v_w = lax.dynamic_slice_in_dim(v_win_p, q0, Q_BLOCK + WINDOW, axis=1)
        src = q0 - WINDOW + jnp.arange(Q_BLOCK + WINDOW)
        d_w = t[:, None] - src[None, :]
        ok_w = (d_w >= 0) & (d_w < WINDOW) & (src[None, :] >= 0)
        s_w = (jnp.einsum('btgrd,bkgd->bgrtk', qblk, k_w, preferred_element_type=jnp.float32) * scale
               - slopes_gr[:, :, None, None] * d_w.astype(jnp.float32))
        s_w = jnp.where(ok_w, s_w, NEG_INF)
        p_w = jax.nn.softmax(s_w, axis=-1)
        o_w = jnp.einsum('bgrtk,bkgd->btgrd', p_w.astype(v_w.dtype), v_w)

        return gblk[..., 0:1] * o_c + gblk[..., 1:2] * o_s + gblk[..., 2:3] * o_w

    out = lax.map(block_fn, jnp.arange(n_qb))
    return out.transpose(1, 0, 2, 3, 4, 5).reshape(B, S, N_HEADS * HEAD_DIM)


def hybrid_mixer(h, w_in, w_pool, pool_scale, q_gain, kc_gain, ks_gain, kw_gain, cmp_k, cmp_v, w_out):
    B, S, _ = h.shape
    proj = h @ w_in
    splits = list(np.cumsum([POOL_WIDTH, N_HEADS * HEAD_DIM] + [KV_WIDTH] * 6))
    u, q, kc, vc, ksl, vsl, kw, vw, g = jnp.split(proj, splits, axis=-1)
    kv4 = lambda a: a.reshape(B, S, N_KV, HEAD_DIM)
    q = rms_norm(q.reshape(B, S, N_HEADS, HEAD_DIM), q_gain)
    k_cmp = rms_norm(compress_blocks(kv4(kc), *cmp_k), kc_gain)
    v_cmp = compress_blocks(kv4(vc), *cmp_v)
    k_sel = rms_norm(kv4(ksl), ks_gain)
    k_win = rms_norm(kv4(kw), kw_gain)
    gates = jax.nn.sigmoid(g.reshape(B, S, N_HEADS, N_BRANCH))
    pool_out = causal_multiscale_pool(u, w_pool, pool_scale)
    attn_out = nsa_attention(q, k_cmp, v_cmp, k_sel, kv4(vsl), k_win, kv4(vw), gates)
    return jnp.concatenate([pool_out, attn_out], axis=-1) @ w_out


def setup_inputs(seed: int = 0) -> dict:
    key = jax.random.key(seed)
    ks = jax.random.split(key, 26)
    L = DEPTH
    f32 = jnp.float32
    nrm = lambda k, shape, fan_in: jax.random.normal(k, shape, f32) * fan_in ** -0.5
    gain = lambda k, shape: 1.0 + 0.02 * jax.random.normal(k, shape, f32)
    small = lambda k, shape, s: s * jax.random.normal(k, shape, f32)
    return {
        "x": jax.random.normal(ks[0], (BATCH, SEQ, D_MODEL), f32),
        "c": jax.random.normal(ks[1], (BATCH, D_MODEL), f32),
        "w_ada": 0.5 * nrm(ks[2], (L, D_MODEL, 6 * D_MODEL), D_MODEL),
        "b_ada": small(ks[3], (L, 6 * D_MODEL), 0.01),
        "norm1_g": gain(ks[4], (L, D_MODEL)),
        "norm2_g": gain(ks[5], (L, D_MODEL)),
        "w_in": nrm(ks[6], (L, D_MODEL, IN_WIDTH), D_MODEL),
        "w_pool": nrm(ks[7], (L, N_POOL_GROUPS, POOL_GROUP, POOL_GROUP), POOL_GROUP),
        "pool_scale": gain(ks[8], (L, POOL_WIDTH)),
        "q_gain": gain(ks[9], (L, HEAD_DIM)),
        "kc_gain": gain(ks[10], (L, HEAD_DIM)),
        "ks_gain": gain(ks[11], (L, HEAD_DIM)),
        "kw_gain": gain(ks[12], (L, HEAD_DIM)),
        "cmp_pos_k": small(ks[13], (L, CMP_LEN, HEAD_DIM), 0.1),
        "cmp_w1_k": nrm(ks[14], (L, CMP_LEN * HEAD_DIM, CMP_HIDDEN), CMP_LEN * HEAD_DIM),
        "cmp_b1_k": small(ks[15], (L, CMP_HIDDEN), 0.01),
        "cmp_w2_k": nrm(ks[16], (L, CMP_HIDDEN, HEAD_DIM), CMP_HIDDEN),
        "cmp_b2_k": small(ks[17], (L, HEAD_DIM), 0.01),
        "cmp_pos_v": small(ks[18], (L, CMP_LEN, HEAD_DIM), 0.1),
        "cmp_w1_v": nrm(ks[19], (L, CMP_LEN * HEAD_DIM, CMP_HIDDEN), CMP_LEN * HEAD_DIM),
        "cmp_b1_v": small(ks[20], (L, CMP_HIDDEN), 0.01),
        "cmp_w2_v": nrm(ks[21], (L, CMP_HIDDEN, HEAD_DIM), CMP_HIDDEN),
        "cmp_b2_v": small(ks[22], (L, HEAD_DIM), 0.01),
        "w_out": nrm(ks[23], (L, MIX_WIDTH, D_MODEL), MIX_WIDTH),
        "w_ff1": nrm(ks[24], (L, D_MODEL, D_FF), D_MODEL),
        "w_ff2": nrm(ks[25], (L, D_FF, D_MODEL), D_FF),
    }


def reference(x, c, w_ada, b_ada, norm1_g, norm2_g, w_in, w_pool, pool_scale, q_gain, kc_gain,
              ks_gain, kw_gain, cmp_pos_k, cmp_w1_k, cmp_b1_k, cmp_w2_k, cmp_b2_k, cmp_pos_v,
              cmp_w1_v, cmp_b1_v, cmp_w2_v, cmp_b2_v, w_out, w_ff1, w_ff2):
    for l in range(DEPTH):
        mod = c @ w_ada[l] + b_ada[l]
        sh1, sc1, ga1, sh2, sc2, ga2 = [m[:, None, :] for m in jnp.split(mod, 6, axis=-1)]
        h = rms_norm(x, norm1_g[l]) * (1.0 + sc1) + sh1
        cmp_k = (cmp_pos_k[l], cmp_w1_k[l], cmp_b1_k[l], cmp_w2_k[l], cmp_b2_k[l])
        cmp_v = (cmp_pos_v[l], cmp_w1_v[l], cmp_b1_v[l], cmp_w2_v[l], cmp_b2_v[l])
        x = x + ga1 * hybrid_mixer(h, w_in[l], w_pool[l], pool_scale[l], q_gain[l], kc_gain[l],
                                   ks_gain[l], kw_gain[l], cmp_k, cmp_v, w_out[l])
        h = rms_norm(x, norm2_g[l]) * (1.0 + sc2) + sh2
        x = x + ga2 * (jnp.square(jax.nn.relu(h @ w_ff1[l])) @ w_ff2[l])
    return x
```

```python
import functools

import jax
import jax.numpy as jnp
import numpy as np
from jax import lax
from jax.experimental import pallas as pl
from jax.experimental.pallas import tpu as pltpu

F32 = jnp.float32
BF16 = jnp.bfloat16

LANES = 128
VMEM_LIMIT_BYTES = 56 * 1024 * 1024

D_MODEL = 1024
POOL_WIDTH = 512
POOL_WINDOWS = (2, 4, 8, 16)
POOL_GROUP = POOL_WIDTH // len(POOL_WINDOWS)
POOL_HALO = 16
HEAD_DIM = 64
N_HEADS = 8
N_KV = 2
GQA_GROUP = N_HEADS // N_KV
KV_WIDTH = N_KV * HEAD_DIM
N_BRANCH = 3
CMP_LEN = 32
CMP_STRIDE = 16
CMP_HIDDEN = 4 * HEAD_DIM
SEL_BLOCK = 64
N_SEL = 16
WINDOW = 512
D_FF = 4 * D_MODEL
NEG_INF = -1e30
FORCE_BONUS = 1e4
EPS = 1e-6

TQ = 128
SEL_CHUNK = 256
BLOCKS_PER_CHUNK = SEL_CHUNK // SEL_BLOCK
WIN_SPAN = WINDOW + TQ
CMP_PER_SEL = SEL_BLOCK // CMP_STRIDE

_QPAD = N_HEADS * LANES
_OFF_U = 0
_OFF_Q = POOL_WIDTH
_OFF_KC = _OFF_Q + _QPAD
_OFF_VC = _OFF_KC + KV_WIDTH
_OFF_KS = _OFF_VC + KV_WIDTH
_OFF_VS = _OFF_KS + KV_WIDTH
_OFF_KW = _OFF_VS + KV_WIDTH
_OFF_VW = _OFF_KW + KV_WIDTH
_OFF_G = _OFF_VW + KV_WIDTH
_IN_PAD = _OFF_G + N_KV * LANES


def _dot(a, b):
    return jnp.dot(a, b, preferred_element_type=F32)


def _dot_nt(a, b):
    return lax.dot_general(a, b, (((1,), (1,)), ((), ())), preferred_element_type=F32)


def _const_spec(shape):
    nd = len(shape)
    return pl.BlockSpec(shape, lambda *_: (0,) * nd, pipeline_mode=pl.Buffered(1))


def _ada_kernel(c_ref, w_ref, b_ref, o_ref):
    o_ref[...] = jnp.dot(c_ref[...], w_ref[...], preferred_element_type=F32,
                         precision=lax.Precision.HIGHEST) + b_ref[...]


def _ada(c8, w, b):
    n = w.shape[1]
    bn = 1024
    return pl.pallas_call(
        _ada_kernel,
        grid=(n // bn,),
        in_specs=[pl.BlockSpec((8, D_MODEL), lambda j: (0, 0)),
                  pl.BlockSpec((D_MODEL, bn), lambda j: (0, j)),
                  pl.BlockSpec((1, bn), lambda j: (0, j))],
        out_specs=pl.BlockSpec((8, bn), lambda j: (0, j)),
        out_shape=jax.ShapeDtypeStruct((8, n), F32),
        name="ada",
    )(c8, w, b)


def _rms_modulate(x, g, sc, sh):
    ms = jnp.mean(x * x, axis=-1, keepdims=True)
    return (x * lax.rsqrt(ms + EPS)) * (g * (1.0 + sc)) + sh


def _head_norm_pair(x, gain2):
    lane = lax.broadcasted_iota(jnp.int32, x.shape, 1)
    lo = lane < HEAD_DIM
    sq = x * x
    s_lo = jnp.sum(jnp.where(lo, sq, 0.0), axis=-1, keepdims=True)
    s_hi = jnp.sum(jnp.where(lo, 0.0, sq), axis=-1, keepdims=True)
    ms = jnp.where(lo, s_lo, s_hi) * (1.0 / HEAD_DIM)
    return x * lax.rsqrt(ms + EPS) * gain2


def _in_proj_kernel(x_ref, g_ref, sc_ref, sh_ref, w_ref, wpool_ref, pscale_ref, qg_ref, ksg_ref,
                    kwg_ref, pool_ref, q_ref, kc_ref, vc_ref, ks_ref, vs_ref, kw_ref, vw_ref,
                    gate_ref, ebuf_ref, *, tm):
    i = pl.program_id(0)
    h = _rms_modulate(x_ref[...], g_ref[...], sc_ref[...], sh_ref[...]).astype(BF16)
    proj = _dot(h, w_ref[...])

    for hh in range(N_HEADS):
        qs = proj[:, _OFF_Q + hh * LANES:_OFF_Q + (hh + 1) * LANES]
        ms = jnp.sum(qs * qs, axis=-1, keepdims=True) * (1.0 / HEAD_DIM)
        qn = qs * lax.rsqrt(ms + EPS) * qg_ref[...]
        q_ref[:, hh * LANES:(hh + 1) * LANES] = (qn * (HEAD_DIM ** -0.5)).astype(BF16)

    kc_ref[...] = proj[:, _OFF_KC:_OFF_KC + KV_WIDTH]
    vc_ref[...] = proj[:, _OFF_VC:_OFF_VC + KV_WIDTH]
    ks_ref[...] = _head_norm_pair(proj[:, _OFF_KS:_OFF_KS + KV_WIDTH], ksg_ref[...]).astype(BF16)
    vs_ref[...] = proj[:, _OFF_VS:_OFF_VS + KV_WIDTH].astype(BF16)
    kw_ref[...] = _head_norm_pair(proj[:, _OFF_KW:_OFF_KW + KV_WIDTH], kwg_ref[...]).astype(BF16)
    vw_ref[...] = proj[:, _OFF_VW:_OFF_VW + KV_WIDTH].astype(BF16)
    gate_ref[...] = jax.nn.sigmoid(proj[:, _OFF_G:_OFF_G + N_KV * LANES])

    @pl.when(i == 0)
    def _():
        ebuf_ref[0:POOL_HALO, :] = jnp.zeros((POOL_HALO, POOL_WIDTH), F32)

    u = proj[:, _OFF_U:_OFF_U + POOL_WIDTH]
    ebuf_ref[POOL_HALO:POOL_HALO + tm, :] = u
    t1 = i * tm + lax.broadcasted_iota(jnp.int32, (tm, POOL_GROUP), 0) + 1
    for gi, w in enumerate(POOL_WINDOWS):
        c0 = gi * POOL_GROUP
        win = u[:, c0:c0 + POOL_GROUP]
        for k in range(1, w):
            win = win + ebuf_ref[POOL_HALO - k:POOL_HALO - k + tm, c0:c0 + POOL_GROUP]
        cnt = jnp.minimum(t1, w).astype(F32)
        pooled = win / cnt - u[:, c0:c0 + POOL_GROUP]
        y = _dot(pooled.astype(BF16), wpool_ref[gi])
        pool_ref[:, c0:c0 + POOL_GROUP] = (y * pscale_ref[:, c0:c0 + POOL_GROUP]).astype(BF16)
    ebuf_ref[0:POOL_HALO, :] = ebuf_ref[tm:tm + POOL_HALO, :]


def _in_proj(x, g1, sc1, sh1, w_in_p, w_pool, pool_scale, qg, ksg, kwg, tm):
    s = x.shape[0]
    row = lambda w: pl.BlockSpec((tm, w), lambda i: (i, 0))
    vec = lambda w: pl.BlockSpec((1, w), lambda i: (0, 0))
    out_shape = [
        jax.ShapeDtypeStruct((s, POOL_WIDTH), BF16),
        jax.ShapeDtypeStruct((s, _QPAD), BF16),
        jax.ShapeDtypeStruct((s, KV_WIDTH), F32),
        jax.ShapeDtypeStruct((s, KV_WIDTH), F32),
        jax.ShapeDtypeStruct((s, KV_WIDTH), BF16),
        jax.ShapeDtypeStruct((s, KV_WIDTH), BF16),
        jax.ShapeDtypeStruct((s, KV_WIDTH), BF16),
        jax.ShapeDtypeStruct((s, KV_WIDTH), BF16),
        jax.ShapeDtypeStruct((s, N_KV * LANES), F32),
    ]
    return pl.pallas_call(
        functools.partial(_in_proj_kernel, tm=tm),
        grid=(s // tm,),
        in_specs=[row(D_MODEL), vec(D_MODEL), vec(D_MODEL), vec(D_MODEL),
                  _const_spec(w_in_p.shape), _const_spec(w_pool.shape), vec(POOL_WIDTH),
                  vec(LANES), vec(LANES), vec(LANES)],
        out_specs=[row(POOL_WIDTH), row(_QPAD), row(KV_WIDTH), row(KV_WIDTH), row(KV_WIDTH),
                   row(KV_WIDTH), row(KV_WIDTH), row(KV_WIDTH), row(N_KV * LANES)],
        out_shape=out_shape,
        scratch_shapes=[pltpu.VMEM((tm + POOL_HALO, POOL_WIDTH), F32)],
        compiler_params=pltpu.CompilerParams(dimension_semantics=("arbitrary",),
                                             vmem_limit_bytes=VMEM_LIMIT_BYTES),
        name="in_proj",
    )(x, g1, sc1, sh1, w_in_p, w_pool, pool_scale, qg, ksg, kwg)


def _compress_kernel(flat_ref, pos_ref, w1_ref, b1_ref, w2_ref, b2_ref, gain_ref, o_ref, *,
                     normalize):
    xb = (flat_ref[0] + pos_ref[...]).astype(BF16)
    hid = jax.nn.gelu(_dot(xb, w1_ref[...]) + b1_ref[...])
    y = _dot(hid.astype(BF16), w2_ref[...]) + b2_ref[...]
    if normalize:
        ms = jnp.mean(y * y, axis=-1, keepdims=True)
        y = y * lax.rsqrt(ms + EPS) * gain_ref[...]
    o_ref[0] = y.astype(BF16)


def _compress(flat, pos, w1, b1, w2, b2, gain, normalize):
    _, rows, width = flat.shape
    tr = min(rows, 256)
    vec = lambda w: pl.BlockSpec((1, w), lambda g, i: (0, 0))
    return pl.pallas_call(
        functools.partial(_compress_kernel, normalize=normalize),
        grid=(N_KV, rows // tr),
        in_specs=[pl.BlockSpec((1, tr, width), lambda g, i: (g, i, 0)),
                  vec(width), _const_spec(w1.shape), vec(CMP_HIDDEN), _const_spec(w2.shape),
                  vec(HEAD_DIM), vec(HEAD_DIM)],
        out_specs=pl.BlockSpec((1, tr, HEAD_DIM), lambda g, i: (g, i, 0)),
        out_shape=jax.ShapeDtypeStruct((N_KV, rows, HEAD_DIM), BF16),
        compiler_params=pltpu.CompilerParams(vmem_limit_bytes=VMEM_LIMIT_BYTES),
        name="compress",
    )(flat, pos, w1, b1, w2, b2, gain)


def _softmax_rows(s, ok):
    s = jnp.where(ok, s, NEG_INF)
    m = jnp.max(s, axis=-1, keepdims=True)
    e = jnp.where(ok, jnp.exp(s - m), 0.0)
    l = jnp.sum(e, axis=-1, keepdims=True)
    return e, l


def _attn_kernel(q_ref, gate_ref, kc_ref, vc_ref, ks_ref, vs_ref, kw_ref, vw_ref, o_ref,
                 msel_ref, m_ref, l_ref, acc_ref, *, nb):
    i = pl.program_id(0)
    g = pl.program_id(1)
    q0 = i * TQ
    nc = CMP_PER_SEL * nb
    gslope = jnp.where(g == 0, 1.0, 2.0 ** -GQA_GROUP).astype(F32)
    slopes = [gslope * (2.0 ** -(r + 1)) for r in range(GQA_GROUP)]

    q = jnp.concatenate([q_ref[:, r * LANES:(r + 1) * LANES] for r in range(GQA_GROUP)], axis=0)
    t_row = q0 + lax.broadcasted_iota(jnp.int32, (TQ, 1), 0)

    col = lax.broadcasted_iota(jnp.int32, (1, nc), 1)
    n_idx = CMP_PER_SEL * (col & (nb - 1)) + lax.shift_right_logical(col, nb.bit_length() - 1)
    cend = CMP_STRIDE * n_idx + (CMP_LEN - 1)
    ok_c = cend <= t_row
    rel_c = (cend - q0).astype(F32)
    s_c = _dot_nt(q, kc_ref[...])
    p_rows = []
    p_sum = jnp.zeros((TQ, nc), F32)
    for r in range(GQA_GROUP):
        e, l = _softmax_rows(s_c[r * TQ:(r + 1) * TQ] + slopes[r] * rel_c, ok_c)
        p = e * (1.0 / jnp.where(l > 0.0, l, 1.0))
        p_sum = p_sum + p
        p_rows.append(p.astype(BF16))
    o_cmp = _dot(jnp.concatenate(p_rows, axis=0), vc_ref[...])

    p3 = p_sum[:, 3 * nb:4 * nb]
    blk = lax.broadcasted_iota(jnp.int32, (TQ, nb), 1)
    p3_prev = jnp.where(blk == 0, 0.0, pltpu.roll(p3, 1, axis=1))
    imp = (p_sum[:, 0:nb] + p_sum[:, nb:2 * nb] + p_sum[:, 2 * nb:3 * nb]
           + 0.5 * p3 + 0.5 * p3_prev)
    cur = lax.shift_right_logical(t_row, SEL_BLOCK.bit_length() - 1)
    causal = blk <= cur
    forced = jnp.where(blk == 0, 1.0, 0.0) + jnp.where(blk == cur, 1.0, 0.0) \
        + jnp.where(blk == cur - 1, 1.0, 0.0)
    val = jnp.where(causal, imp + jnp.where(forced > 0.0, FORCE_BONUS, 0.0), -1.0)
    blk_f = blk.astype(F32)

    def pick(_, carry):
        v, sel = carry
        mx = jnp.max(v, axis=-1, keepdims=True)
        idx = jnp.min(jnp.where(v == mx, blk_f, float(nb)), axis=-1, keepdims=True)
        hit = blk_f == idx
        return jnp.where(hit, -2.0, v), jnp.where(hit, 1.0, sel)

    _, sel = lax.fori_loop(0, min(N_SEL, nb), pick, (val, jnp.zeros((TQ, nb), F32)))
    msel_ref[...] = jnp.where(causal, sel, 0.0).astype(BF16)

    m_ref[...] = jnp.full(m_ref.shape, NEG_INF, F32)
    l_ref[...] = jnp.zeros(l_ref.shape, F32)
    acc_ref[...] = jnp.zeros(acc_ref.shape, F32)

    def sel_chunk(c, diagonal):
        k0 = pl.multiple_of(c * SEL_CHUNK, SEL_CHUNK)
        kb = ks_ref[pl.ds(k0, SEL_CHUNK), :]
        vb = vs_ref[pl.ds(k0, SEL_CHUNK), :]
        s = _dot_nt(q, kb)
        kcol = lax.broadcasted_iota(jnp.int32, (nb, SEL_CHUNK), 1)
        brow = lax.broadcasted_iota(jnp.int32, (nb, SEL_CHUNK), 0)
        expand = jnp.where(
            brow == c * BLOCKS_PER_CHUNK + lax.shift_right_logical(kcol, SEL_BLOCK.bit_length() - 1),
            1.0, 0.0).astype(BF16)
        mexp = _dot(msel_ref[...], expand)
        pos = k0 + lax.broadcasted_iota(jnp.int32, (1, SEL_CHUNK), 1)
        if diagonal:
            mexp = jnp.where(pos <= t_row, mexp, 0.0)
        ok = mexp > 0.5
        rel = (pos - q0).astype(F32)
        p_rows = []
        for r in range(GQA_GROUP):
            rows = slice(r * TQ, (r + 1) * TQ)
            sr = jnp.where(ok, s[rows] + slopes[r] * rel, NEG_INF)
            m_old = m_ref[rows]
            m_new = jnp.maximum(m_old, jnp.max(sr, axis=-1, keepdims=True))
            alpha = jnp.exp(m_old - m_new)
            p = jnp.where(ok, jnp.exp(sr - m_new), 0.0)
            l_ref[rows] = alpha * l_ref[rows] + jnp.sum(p, axis=-1, keepdims=True)
            m_ref[rows] = m_new
            acc_ref[rows] = alpha * acc_ref[rows]
            p_rows.append(p.astype(BF16))
        acc_ref[...] += _dot(jnp.concatenate(p_rows, axis=0), vb)

    c_diag = lax.div(q0, SEL_CHUNK)

    def sel_body(c, carry):
        sel_chunk(c, diagonal=False)
        return carry

    lax.fori_loop(0, c_diag, sel_body, 0)
    sel_chunk(c_diag, diagonal=True)

    w0 = pl.multiple_of(jnp.maximum(q0 - WINDOW, 0), TQ)
    kwb = kw_ref[pl.ds(w0, WIN_SPAN), :]
    vwb = vw_ref[pl.ds(w0, WIN_SPAN), :]
    s_w = _dot_nt(q, kwb)
    pos_w = w0 + lax.broadcasted_iota(jnp.int32, (1, WIN_SPAN), 1)
    d_w = (t_row - pos_w).astype(F32)
    ok_w = jnp.abs(d_w - (WINDOW - 1) * 0.5) < WINDOW * 0.5
    rel_w = (pos_w - q0).astype(F32)
    p_rows = []
    l_w = []
    for r in range(GQA_GROUP):
        e, l = _softmax_rows(s_w[r * TQ:(r + 1) * TQ] + slopes[r] * rel_w, ok_w)
        p_rows.append(e.astype(BF16))
        l_w.append(l)
    o_win = _dot(jnp.concatenate(p_rows, axis=0), vwb)

    for r in range(GQA_GROUP):
        rows = slice(r * TQ, (r + 1) * TQ)
        gc = gate_ref[:, 3 * r + 0:3 * r + 1]
        gs = gate_ref[:, 3 * r + 1:3 * r + 2]
        gw = gate_ref[:, 3 * r + 2:3 * r + 3]
        out = (gc * o_cmp[rows] + gs * (acc_ref[rows] * (1.0 / l_ref[rows]))
               + gw * (o_win[rows] * (1.0 / l_w[r])))
        o_ref[:, r * LANES:(r + 1) * LANES] = out.astype(BF16)


def _attention(qpad, gates, kcmp, vcmp, ksel, vsel, kwin, vwin):
    s = qpad.shape[0]
    nb = s // SEL_BLOCK
    gw = GQA_GROUP * LANES
    return pl.pallas_call(
        functools.partial(_attn_kernel, nb=nb),
        grid=(s // TQ, N_KV),
        in_specs=[pl.BlockSpec((TQ, gw), lambda i, g: (i, g)),
                  pl.BlockSpec((TQ, LANES), lambda i, g: (i, g)),
                  _const_spec(kcmp.shape), _const_spec(vcmp.shape),
                  _const_spec(ksel.shape), _const_spec(vsel.shape),
                  _const_spec(kwin.shape), _const_spec(vwin.shape)],
        out_specs=pl.BlockSpec((TQ, gw), lambda i, g: (i, g)),
        out_shape=jax.ShapeDtypeStruct((s, N_HEADS * LANES), BF16),
        scratch_shapes=[pltpu.VMEM((TQ, nb), BF16),
                        pltpu.VMEM((GQA_GROUP * TQ, 1), F32),
                        pltpu.VMEM((GQA_GROUP * TQ, 1), F32),
                        pltpu.VMEM((GQA_GROUP * TQ, LANES), F32)],
        compiler_params=pltpu.CompilerParams(dimension_semantics=("arbitrary", "arbitrary"),
                                             vmem_limit_bytes=VMEM_LIMIT_BYTES),
        name="attn",
    )(qpad, gates, kcmp, vcmp, ksel, vsel, kwin, vwin)


def _out_proj_kernel(pool_ref, attn_ref, wp_ref, wa_ref, x_ref, ga_ref, g_ref, sc_ref, sh_ref,
                     x1_ref, h2_ref):
    mix = _dot(pool_ref[...], wp_ref[...]) + _dot(attn_ref[...], wa_ref[...])
    x1 = x_ref[...] + ga_ref[...] * mix
    x1_ref[...] = x1
    h2_ref[...] = _rms_modulate(x1, g_ref[...], sc_ref[...], sh_ref[...]).astype(BF16)


def _out_proj(pool_out, attn_pad, w_pool_rows, w_attn_rows, x, ga1, g2, sc2, sh2, tm):
    s = x.shape[0]
    row = lambda w: pl.BlockSpec((tm, w), lambda i: (i, 0))
    vec = lambda w: pl.BlockSpec((1, w), lambda i: (0, 0))
    return pl.pallas_call(
        _out_proj_kernel,
        grid=(s // tm,),
        in_specs=[row(POOL_WIDTH), row(_QPAD), _const_spec(w_pool_rows.shape),
                  _const_spec(w_attn_rows.shape), row(D_MODEL), vec(D_MODEL), vec(D_MODEL),
                  vec(D_MODEL), vec(D_MODEL)],
        out_specs=[row(D_MODEL), row(D_MODEL)],
        out_shape=[jax.ShapeDtypeStruct((s, D_MODEL), F32),
                   jax.ShapeDtypeStruct((s, D_MODEL), BF16)],
        compiler_params=pltpu.CompilerParams(vmem_limit_bytes=VMEM_LIMIT_BYTES),
        name="out_proj",
    )(pool_out, attn_pad, w_pool_rows, w_attn_rows, x, ga1, g2, sc2, sh2)


def _mlp_kernel(h_ref, w1_ref, w2_ref, x_ref, ga_ref, o_ref, *, ff_chunk):
    h = h_ref[...]
    acc = jnp.zeros(x_ref.shape, F32)
    for c in range(D_FF // ff_chunk):
        a = _dot(h, w1_ref[:, c * ff_chunk:(c + 1) * ff_chunk])
        a = jnp.square(jnp.maximum(a, 0.0)).astype(BF16)
        acc = acc + _dot(a, w2_ref[c * ff_chunk:(c + 1) * ff_chunk, :])
    o_ref[...] = x_ref[...] + ga_ref[...] * acc


def _mlp(h2, w1, w2, x1, ga2, tm):
    s = x1.shape[0]
    row = lambda w: pl.BlockSpec((tm, w), lambda i: (i, 0))
    return pl.pallas_call(
        functools.partial(_mlp_kernel, ff_chunk=1024),
        grid=(s // tm,),
        in_specs=[row(D_MODEL), _const_spec(w1.shape), _const_spec(w2.shape), row(D_MODEL),
                  pl.BlockSpec((1, D_MODEL), lambda i: (0, 0))],
        out_specs=row(D_MODEL),
        out_shape=jax.ShapeDtypeStruct((s, D_MODEL), F32),
        compiler_params=pltpu.CompilerParams(vmem_limit_bytes=VMEM_LIMIT_BYTES),
        name="mlp",
    )(h2, w1, w2, x1, ga2)


def _pad_in_proj_weight(w_in):
    w = jnp.zeros((D_MODEL, _IN_PAD), F32)
    w = w.at[:, _OFF_U:_OFF_U + POOL_WIDTH].set(w_in[:, :POOL_WIDTH])
    src_q = POOL_WIDTH
    for hh in range(N_HEADS):
        dst = _OFF_Q + hh * LANES + (hh // GQA_GROUP) * HEAD_DIM
        w = w.at[:, dst:dst + HEAD_DIM].set(w_in[:, src_q + hh * HEAD_DIM:src_q + (hh + 1) * HEAD_DIM])
    src_kv = src_q + N_HEADS * HEAD_DIM
    w = w.at[:, _OFF_KC:_OFF_KC + 6 * KV_WIDTH].set(w_in[:, src_kv:src_kv + 6 * KV_WIDTH])
    src_g = src_kv + 6 * KV_WIDTH
    per_group = GQA_GROUP * N_BRANCH
    for gg in range(N_KV):
        w = w.at[:, _OFF_G + gg * LANES:_OFF_G + gg * LANES + per_group].set(
            w_in[:, src_g + gg * per_group:src_g + (gg + 1) * per_group])
    return w.astype(BF16)


def _pad_out_proj_weight(w_out):
    wa = jnp.zeros((N_HEADS * LANES, D_MODEL), F32)
    for hh in range(N_HEADS):
        dst = hh * LANES + (hh // GQA_GROUP) * HEAD_DIM
        wa = wa.at[dst:dst + HEAD_DIM].set(
            w_out[POOL_WIDTH + hh * HEAD_DIM:POOL_WIDTH + (hh + 1) * HEAD_DIM])
    return w_out[:POOL_WIDTH].astype(BF16), wa.astype(BF16)


def _compress_windows(kv, nb):
    s = kv.shape[0]
    chunks = kv.reshape(s // CMP_STRIDE, CMP_STRIDE, N_KV, HEAD_DIM).transpose(2, 0, 1, 3)
    chunks = chunks.reshape(N_KV, s // CMP_STRIDE, CMP_STRIDE * HEAD_DIM)
    nxt = jnp.concatenate([chunks[:, 1:], jnp.zeros_like(chunks[:, :1])], axis=1)
    flat = jnp.concatenate([chunks, nxt], axis=-1)
    flat = flat.reshape(N_KV, nb, CMP_PER_SEL, CMP_LEN * HEAD_DIM).transpose(0, 2, 1, 3)
    return flat.reshape(N_KV, CMP_PER_SEL * nb, CMP_LEN * HEAD_DIM)


def _group_lanes(a):
    return a.transpose(1, 0, 2).reshape(a.shape[1], KV_WIDTH)


def kernel(x, c, w_ada, b_ada, norm1_g, norm2_g, w_in, w_pool, pool_scale, q_gain, kc_gain,
           ks_gain, kw_gain, cmp_pos_k, cmp_w1_k, cmp_b1_k, cmp_w2_k, cmp_b2_k, cmp_pos_v,
           cmp_w1_v, cmp_b1_v, cmp_w2_v, cmp_b2_v, w_out, w_ff1, w_ff2):
    batch, s, _ = x.shape
    assert batch == 1 and w_ada.shape[0] == 1
    assert s % SEL_CHUNK == 0 and s >= WIN_SPAN and (s // SEL_BLOCK) & (s // SEL_BLOCK - 1) == 0
    nb = s // SEL_BLOCK
    tm = min(s, 512)
    x2 = x[0]

    mod = _ada(jnp.broadcast_to(c, (8, D_MODEL)), w_ada[0], b_ada)[0:1]
    sh1, sc1, ga1, sh2, sc2, ga2 = [mod[:, k * D_MODEL:(k + 1) * D_MODEL] for k in range(6)]

    pair = lambda gain: jnp.tile(gain, (1, 2))
    (pool_out, qpad, kc, vc, ksel, vsel, kwin, vwin, gates) = _in_proj(
        x2, norm1_g, sc1, sh1, _pad_in_proj_weight(w_in[0]), w_pool[0].astype(BF16), pool_scale,
        pair(q_gain), pair(ks_gain), pair(kw_gain), tm)

    kcmp = _compress(_compress_windows(kc, nb), cmp_pos_k.reshape(1, -1), cmp_w1_k[0].astype(BF16),
                     cmp_b1_k, cmp_w2_k[0].astype(BF16), cmp_b2_k, kc_gain, True)
    vcmp = _compress(_compress_windows(vc, nb), cmp_pos_v.reshape(1, -1), cmp_w1_v[0].astype(BF16),
                     cmp_b1_v, cmp_w2_v[0].astype(BF16), cmp_b2_v, kc_gain, False)

    attn_pad = _attention(qpad, gates, _group_lanes(kcmp), _group_lanes(vcmp), ksel, vsel, kwin,
                          vwin)

    w_pool_rows, w_attn_rows = _pad_out_proj_weight(w_out[0])
    x1, h2 = _out_proj(pool_out, attn_pad, w_pool_rows, w_attn_rows, x2, ga1, norm2_g, sc2, sh2, tm)
    out = _mlp(h2, w_ff1[0].astype(BF16), w_ff2[0].astype(BF16), x1, ga2, tm)
    return out[None]
```

```python
import functools

import jax
import jax.numpy as jnp
import numpy as np
from jax import lax
from jax.experimental import pallas as pl
from jax.experimental.pallas import tpu as pltpu

F32 = jnp.float32
BF16 = jnp.bfloat16

LANES = 128
VMEM_LIMIT_BYTES = 56 * 1024 * 1024

D_MODEL = 1024
POOL_WIDTH = 512
POOL_WINDOWS = (2, 4, 8, 16)
POOL_GROUP = POOL_WIDTH // len(POOL_WINDOWS)
POOL_HALO = 16
HEAD_DIM = 64
N_HEADS = 8
N_KV = 2
GQA_GROUP = N_HEADS // N_KV
KV_WIDTH = N_KV * HEAD_DIM
N_BRANCH = 3
CMP_LEN = 32
CMP_STRIDE = 16
CMP_HIDDEN = 4 * HEAD_DIM
SEL_BLOCK = 64
N_SEL = 16
WINDOW = 512
D_FF = 4 * D_MODEL
NEG_INF = -1e30
FORCE_BONUS = 1e4
EPS = 1e-6

TQ = 128
SEL_CHUNK = 256
BLOCKS_PER_CHUNK = SEL_CHUNK // SEL_BLOCK
WIN_SPAN = WINDOW + TQ
CMP_PER_SEL = SEL_BLOCK // CMP_STRIDE
GATE_ROWS = 16

_QPAD = N_HEADS * LANES
_OFF_U = 0
_OFF_Q = POOL_WIDTH
_OFF_KC = _OFF_Q + _QPAD
_OFF_VC = _OFF_KC + KV_WIDTH
_OFF_KS = _OFF_VC + KV_WIDTH
_OFF_VS = _OFF_KS + KV_WIDTH
_OFF_KW = _OFF_VS + KV_WIDTH
_OFF_VW = _OFF_KW + KV_WIDTH
_OFF_G = _OFF_VW + KV_WIDTH
_IN_PAD = _OFF_G + N_KV * LANES


def _dot(a, b):
    return jnp.dot(a, b, preferred_element_type=F32)


def _dot_nt(a, b):
    return lax.dot_general(a, b, (((1,), (1,)), ((), ())), preferred_element_type=F32)


def _const_spec(shape):
    nd = len(shape)
    return pl.BlockSpec(shape, lambda *_: (0,) * nd, pipeline_mode=pl.Buffered(1))


def _ada_kernel(c_ref, w_ref, b_ref, o_ref):
    o_ref[...] = jnp.dot(c_ref[...], w_ref[...], preferred_element_type=F32,
                         precision=lax.Precision.HIGHEST) + b_ref[...]


def _ada(c8, w, b):
    n = w.shape[1]
    bn = 1024
    return pl.pallas_call(
        _ada_kernel,
        grid=(n // bn,),
        in_specs=[pl.BlockSpec((8, D_MODEL), lambda j: (0, 0)),
                  pl.BlockSpec((D_MODEL, bn), lambda j: (0, j)),
                  pl.BlockSpec((1, bn), lambda j: (0, j))],
        out_specs=pl.BlockSpec((8, bn), lambda j: (0, j)),
        out_shape=jax.ShapeDtypeStruct((8, n), F32),
        name="ada",
    )(c8, w, b)


def _rms_modulate(x, g, sc, sh):
    ms = jnp.mean(x * x, axis=-1, keepdims=True)
    return (x * lax.rsqrt(ms + EPS)) * (g * (1.0 + sc)) + sh


def _head_norm_pair(x, gain2):
    lane = lax.broadcasted_iota(jnp.int32, x.shape, 1)
    lo = lane < HEAD_DIM
    sq = x * x
    s_lo = jnp.sum(jnp.where(lo, sq, 0.0), axis=-1, keepdims=True)
    s_hi = jnp.sum(jnp.where(lo, 0.0, sq), axis=-1, keepdims=True)
    ms = jnp.where(lo, s_lo, s_hi) * (1.0 / HEAD_DIM)
    return x * lax.rsqrt(ms + EPS) * gain2


def _in_proj_kernel(x_ref, g_ref, sc_ref, sh_ref, w_ref, wpool_ref, pscale_ref, qg_ref, ksg_ref,
                    kwg_ref, pool_ref, q_ref, kc_ref, vc_ref, ks_ref, vs_ref, kw_ref, vw_ref,
                    gate_ref, ebuf_ref, *, tm):
    i = pl.program_id(0)
    h = _rms_modulate(x_ref[...], g_ref[...], sc_ref[...], sh_ref[...]).astype(BF16)
    proj = _dot(h, w_ref[...])

    for hh in range(N_HEADS):
        qs = proj[:, _OFF_Q + hh * LANES:_OFF_Q + (hh + 1) * LANES]
        ms = jnp.sum(qs * qs, axis=-1, keepdims=True) * (1.0 / HEAD_DIM)
        qn = qs * lax.rsqrt(ms + EPS) * qg_ref[...]
        q_ref[:, hh * LANES:(hh + 1) * LANES] = (qn * (HEAD_DIM ** -0.5)).astype(BF16)

    kc_ref[...] = proj[:, _OFF_KC:_OFF_KC + KV_WIDTH]
    vc_ref[...] = proj[:, _OFF_VC:_OFF_VC + KV_WIDTH]
    ks_ref[...] = _head_norm_pair(proj[:, _OFF_KS:_OFF_KS + KV_WIDTH], ksg_ref[...]).astype(BF16)
    vs_ref[...] = proj[:, _OFF_VS:_OFF_VS + KV_WIDTH].astype(BF16)
    kw_ref[...] = _head_norm_pair(proj[:, _OFF_KW:_OFF_KW + KV_WIDTH], kwg_ref[...]).astype(BF16)
    vw_ref[...] = proj[:, _OFF_VW:_OFF_VW + KV_WIDTH].astype(BF16)
    gate_ref[...] = jax.nn.sigmoid(proj[:, _OFF_G:_OFF_G + N_KV * LANES])

    @pl.when(i == 0)
    def _():
        ebuf_ref[0:POOL_HALO, :] = jnp.zeros((POOL_HALO, POOL_WIDTH), F32)

    u = proj[:, _OFF_U:_OFF_U + POOL_WIDTH]
    ebuf_ref[POOL_HALO:POOL_HALO + tm, :] = u
    t1 = i * tm + lax.broadcasted_iota(jnp.int32, (tm, POOL_GROUP), 0) + 1
    for gi, w in enumerate(POOL_WINDOWS):
        c0 = gi * POOL_GROUP
        win = u[:, c0:c0 + POOL_GROUP]
        for k in range(1, w):
            win = win + ebuf_ref[POOL_HALO - k:POOL_HALO - k + tm, c0:c0 + POOL_GROUP]
        cnt = jnp.minimum(t1, w).astype(F32)
        pooled = win / cnt - u[:, c0:c0 + POOL_GROUP]
        y = _dot(pooled.astype(BF16), wpool_ref[gi])
        pool_ref[:, c0:c0 + POOL_GROUP] = (y * pscale_ref[:, c0:c0 + POOL_GROUP]).astype(BF16)
    ebuf_ref[0:POOL_HALO, :] = ebuf_ref[tm:tm + POOL_HALO, :]


def _in_proj(x, g1, sc1, sh1, w_in_p, w_pool, pool_scale, qg, ksg, kwg, tm):
    s = x.shape[0]
    row = lambda w: pl.BlockSpec((tm, w), lambda i: (i, 0))
    vec = lambda w: pl.BlockSpec((1, w), lambda i: (0, 0))
    out_shape = [
        jax.ShapeDtypeStruct((s, POOL_WIDTH), BF16),
        jax.ShapeDtypeStruct((s, _QPAD), BF16),
        jax.ShapeDtypeStruct((s, KV_WIDTH), F32),
        jax.ShapeDtypeStruct((s, KV_WIDTH), F32),
        jax.ShapeDtypeStruct((s, KV_WIDTH), BF16),
        jax.ShapeDtypeStruct((s, KV_WIDTH), BF16),
        jax.ShapeDtypeStruct((s, KV_WIDTH), BF16),
        jax.ShapeDtypeStruct((s, KV_WIDTH), BF16),
        jax.ShapeDtypeStruct((s, N_KV * LANES), F32),
    ]
    return pl.pallas_call(
        functools.partial(_in_proj_kernel, tm=tm),
        grid=(s // tm,),
        in_specs=[row(D_MODEL), vec(D_MODEL), vec(D_MODEL), vec(D_MODEL),
                  _const_spec(w_in_p.shape), _const_spec(w_pool.shape), vec(POOL_WIDTH),
                  vec(LANES), vec(LANES), vec(LANES)],
        out_specs=[row(POOL_WIDTH), row(_QPAD), row(KV_WIDTH), row(KV_WIDTH), row(KV_WIDTH),
                   row(KV_WIDTH), row(KV_WIDTH), row(KV_WIDTH), row(N_KV * LANES)],
        out_shape=out_shape,
        scratch_shapes=[pltpu.VMEM((tm + POOL_HALO, POOL_WIDTH), F32)],
        compiler_params=pltpu.CompilerParams(dimension_semantics=("arbitrary",),
                                             vmem_limit_bytes=VMEM_LIMIT_BYTES),
        name="in_proj",
    )(x, g1, sc1, sh1, w_in_p, w_pool, pool_scale, qg, ksg, kwg)


def _compress_kernel(flat_ref, pos_ref, w1_ref, b1_ref, w2_ref, b2_ref, gain_ref, o_ref, *,
                     normalize):
    xb = (flat_ref[0] + pos_ref[...]).astype(BF16)
    hid = jax.nn.gelu(_dot(xb, w1_ref[...]) + b1_ref[...])
    y = _dot(hid.astype(BF16), w2_ref[...]) + b2_ref[...]
    if normalize:
        ms = jnp.mean(y * y, axis=-1, keepdims=True)
        y = y * lax.rsqrt(ms + EPS) * gain_ref[...]
    o_ref[0] = y.astype(BF16)


def _compress(flat, pos, w1, b1, w2, b2, gain, normalize):
    _, rows, width = flat.shape
    tr = min(rows, 256)
    vec = lambda w: pl.BlockSpec((1, w), lambda g, i: (0, 0))
    return pl.pallas_call(
        functools.partial(_compress_kernel, normalize=normalize),
        grid=(N_KV, rows // tr),
        in_specs=[pl.BlockSpec((1, tr, width), lambda g, i: (g, i, 0)),
                  vec(width), _const_spec(w1.shape), vec(CMP_HIDDEN), _const_spec(w2.shape),
                  vec(HEAD_DIM), vec(HEAD_DIM)],
        out_specs=pl.BlockSpec((1, tr, HEAD_DIM), lambda g, i: (g, i, 0)),
        out_shape=jax.ShapeDtypeStruct((N_KV, rows, HEAD_DIM), BF16),
        compiler_params=pltpu.CompilerParams(vmem_limit_bytes=VMEM_LIMIT_BYTES),
        name="compress",
    )(flat, pos, w1, b1, w2, b2, gain)


def _attn_kernel(q_ref, gate_ref, cend_ref, rowt_ref, kc_ref, vct_ref, ks_ref, vst_ref, kw_ref,
                 vwt_ref, o_ref, madd_ref, bias_ref, m_ref, l_ref, acc_ref, flag_ref, *, nb):
    i = pl.program_id(0)
    g = pl.program_id(1)
    q0 = i * TQ
    nc = CMP_PER_SEL * nb
    n_chunks = nb // BLOCKS_PER_CHUNK
    gslope = jnp.where(g == 0, 1.0, 2.0 ** -GQA_GROUP).astype(F32)
    slopes = [gslope * (2.0 ** -(r + 1)) for r in range(GQA_GROUP)]
    heads = [slice(r * TQ, (r + 1) * TQ) for r in range(GQA_GROUP)]

    q = jnp.concatenate([q_ref[:, r * LANES:(r + 1) * LANES] for r in range(GQA_GROUP)], axis=0)
    t_lane = q0 + lax.broadcasted_iota(jnp.int32, (1, TQ), 1)

    cend = cend_ref[...]
    ok_c = cend <= t_lane
    rel_c = (cend - q0).astype(F32)
    s_c = _dot_nt(kc_ref[...], q)
    p_sum = jnp.zeros((nc, TQ), F32)
    p_cols = []
    for r in range(GQA_GROUP):
        s = jnp.where(ok_c, s_c[:, heads[r]] + slopes[r] * rel_c, NEG_INF)
        m = jnp.max(s, axis=0, keepdims=True)
        e = jnp.exp(s - m)
        l = jnp.sum(e, axis=0, keepdims=True)
        p = e * jnp.where(m > 0.5 * NEG_INF, 1.0 / l, 0.0)
        p_sum = p_sum + p
        p_cols.append(p.astype(BF16))
    o_cmp = _dot(vct_ref[...], jnp.concatenate(p_cols, axis=1))

    p3 = p_sum[3 * nb:4 * nb]
    blk = lax.broadcasted_iota(jnp.int32, (nb, TQ), 0)
    p3_prev = jnp.where(blk == 0, 0.0, pltpu.roll(p3, 1, axis=0))
    imp = p_sum[0:nb] + p_sum[nb:2 * nb] + p_sum[2 * nb:3 * nb] + 0.5 * p3 + 0.5 * p3_prev
    cur = lax.shift_right_logical(t_lane, SEL_BLOCK.bit_length() - 1)
    causal = blk <= cur
    forced = jnp.where(blk == 0, 1.0, 0.0) + jnp.where(blk == cur, 1.0, 0.0) \
        + jnp.where(blk == cur - 1, 1.0, 0.0)
    val = jnp.where(causal, imp + jnp.where(forced > 0.0, FORCE_BONUS, 0.0), -1.0)
    blk_f = blk.astype(F32)

    def pick(_, carry):
        v, sel = carry
        mx = jnp.max(v, axis=0, keepdims=True)
        idx = jnp.min(jnp.where(v == mx, blk_f, float(nb)), axis=0, keepdims=True)
        hit = blk_f == idx
        return jnp.where(hit, -2.0, v), jnp.where(hit, 1.0, sel)

    _, sel = lax.fori_loop(0, min(N_SEL, nb), pick, (val, jnp.zeros((nb, TQ), F32)))
    madd_ref[...] = jnp.where(causal, jnp.where(sel > 0.5, 0.0, NEG_INF), NEG_INF)
    for c in range(n_chunks):
        rows = madd_ref[c * BLOCKS_PER_CHUNK:(c + 1) * BLOCKS_PER_CHUNK, :]
        flag_ref[c] = (jnp.max(rows) > 0.5 * NEG_INF).astype(jnp.int32)

    key_row = rowt_ref[0:SEL_CHUNK, :]
    for r in range(GQA_GROUP):
        bias_ref[r] = slopes[r] * key_row
    m_ref[...] = jnp.full(m_ref.shape, NEG_INF, F32)
    l_ref[...] = jnp.zeros(l_ref.shape, F32)
    acc_ref[...] = jnp.zeros(acc_ref.shape, F32)

    def sel_chunk(c, diagonal):
        k0 = pl.multiple_of(c * SEL_CHUNK, SEL_CHUNK)
        s = _dot_nt(ks_ref[pl.ds(k0, SEL_CHUNK), :], q)
        madd = jnp.concatenate(
            [jnp.broadcast_to(madd_ref[pl.ds(c * BLOCKS_PER_CHUNK + b, 1), :], (SEL_BLOCK, TQ))
             for b in range(BLOCKS_PER_CHUNK)], axis=0)
        if diagonal:
            pos = k0 + lax.broadcasted_iota(jnp.int32, (SEL_CHUNK, TQ), 0)
            madd = jnp.where(pos <= t_lane, madd, NEG_INF)
        rel0 = (k0 - q0).astype(F32)
        p_cols = []
        for r in range(GQA_GROUP):
            shift = slopes[r] * rel0
            u = s[:, heads[r]] + bias_ref[r] + madd
            m_old = m_ref[:, heads[r]]
            m_new = jnp.maximum(m_old, jnp.max(u, axis=0, keepdims=True) + shift)
            alpha = jnp.exp(m_old - m_new)
            p = jnp.exp(u - (m_new - shift))
            l_ref[:, heads[r]] = alpha * l_ref[:, heads[r]] + jnp.sum(p, axis=0, keepdims=True)
            m_ref[:, heads[r]] = m_new
            acc_ref[:, heads[r]] = alpha * acc_ref[:, heads[r]]
            p_cols.append(p.astype(BF16))
        acc_ref[...] += _dot(vst_ref[c], jnp.concatenate(p_cols, axis=1))

    c_diag = q0 // SEL_CHUNK

    def sel_body(c, carry):
        @pl.when(flag_ref[c] > 0)
        def _():
            sel_chunk(c, diagonal=False)
        return carry

    lax.fori_loop(0, c_diag, sel_body, 0)
    sel_chunk(c_diag, diagonal=True)

    w0 = pl.multiple_of(jnp.maximum(q0 - WINDOW, 0), TQ)
    s_w = _dot_nt(kw_ref[pl.ds(w0, WIN_SPAN), :], q)
    row_w = rowt_ref[...]
    lane_w = lax.broadcasted_iota(jnp.int32, (WIN_SPAN, TQ), 1).astype(F32)
    dq = (q0 - w0).astype(F32)
    d_w = lane_w - row_w + dq
    ok_w = jnp.abs(d_w - (WINDOW - 1) * 0.5) < WINDOW * 0.5
    rel_w = row_w - dq
    p_cols = []
    l_w = []
    for r in range(GQA_GROUP):
        s = jnp.where(ok_w, s_w[:, heads[r]] + slopes[r] * rel_w, NEG_INF)
        m = jnp.max(s, axis=0, keepdims=True)
        e = jnp.exp(s - m)
        l_w.append(jnp.sum(e, axis=0, keepdims=True))
        p_cols.append(e.astype(BF16))
    p_w = jnp.concatenate(p_cols, axis=1)
    wb = w0 // TQ
    o_win = _dot(vwt_ref[wb], p_w[0:TQ])
    for b in range(1, WIN_SPAN // TQ):
        o_win = o_win + _dot(vwt_ref[wb + b], p_w[b * TQ:(b + 1) * TQ])

    for r in range(GQA_GROUP):
        gc = gate_ref[0, 3 * r + 0:3 * r + 1, :]
        gs = gate_ref[0, 3 * r + 1:3 * r + 2, :]
        gw = gate_ref[0, 3 * r + 2:3 * r + 3, :]
        out_t = (gc * o_cmp[:, heads[r]]
                 + (gs / l_ref[:, heads[r]]) * acc_ref[:, heads[r]]
                 + (gw / l_w[r]) * o_win[:, heads[r]])
        o_ref[:, r * LANES:(r + 1) * LANES] = out_t.T.astype(BF16)


def _attention(qpad, gates_t, kcmp, vcmp_t, ksel, vsel_t, kwin, vwin_t):
    s = qpad.shape[0]
    nb = s // SEL_BLOCK
    nc = CMP_PER_SEL * nb
    gw = GQA_GROUP * LANES
    col = np.arange(nc)
    cend = CMP_STRIDE * (CMP_PER_SEL * (col % nb) + col // nb) + (CMP_LEN - 1)
    cend_tile = jnp.asarray(np.broadcast_to(cend[:, None], (nc, TQ)).astype(np.int32))
    row_tile = jnp.asarray(np.broadcast_to(np.arange(WIN_SPAN)[:, None], (WIN_SPAN, TQ))
                           .astype(np.float32))
    return pl.pallas_call(
        functools.partial(_attn_kernel, nb=nb),
        grid=(s // TQ, N_KV),
        in_specs=[pl.BlockSpec((TQ, gw), lambda i, g: (i, g)),
                  pl.BlockSpec((1, GATE_ROWS, TQ), lambda i, g: (g, 0, i)),
                  _const_spec(cend_tile.shape), _const_spec(row_tile.shape),
                  _const_spec(kcmp.shape), _const_spec(vcmp_t.shape),
                  _const_spec(ksel.shape), _const_spec(vsel_t.shape),
                  _const_spec(kwin.shape), _const_spec(vwin_t.shape)],
        out_specs=pl.BlockSpec((TQ, gw), lambda i, g: (i, g)),
        out_shape=jax.ShapeDtypeStruct((s, N_HEADS * LANES), BF16),
        scratch_shapes=[pltpu.VMEM((nb, TQ), F32),
                        pltpu.VMEM((GQA_GROUP, SEL_CHUNK, TQ), F32),
                        pltpu.VMEM((1, GQA_GROUP * TQ), F32),
                        pltpu.VMEM((1, GQA_GROUP * TQ), F32),
                        pltpu.VMEM((LANES, GQA_GROUP * TQ), F32),
                        pltpu.SMEM((nb // BLOCKS_PER_CHUNK,), jnp.int32)],
        compiler_params=pltpu.CompilerParams(dimension_semantics=("arbitrary", "arbitrary"),
                                             vmem_limit_bytes=VMEM_LIMIT_BYTES),
        name="attn",
    )(qpad, gates_t, cend_tile, row_tile, kcmp, vcmp_t, ksel, vsel_t, kwin, vwin_t)


def _out_proj_kernel(pool_ref, attn_ref, wp_ref, wa_ref, x_ref, ga_ref, g_ref, sc_ref, sh_ref,
                     x1_ref, h2_ref):
    mix = _dot(pool_ref[...], wp_ref[...]) + _dot(attn_ref[...], wa_ref[...])
    x1 = x_ref[...] + ga_ref[...] * mix
    x1_ref[...] = x1
    h2_ref[...] = _rms_modulate(x1, g_ref[...], sc_ref[...], sh_ref[...]).astype(BF16)


def _out_proj(pool_out, attn_pad, w_pool_rows, w_attn_rows, x, ga1, g2, sc2, sh2, tm):
    s = x.shape[0]
    row = lambda w: pl.BlockSpec((tm, w), lambda i: (i, 0))
    vec = lambda w: pl.BlockSpec((1, w), lambda i: (0, 0))
    return pl.pallas_call(
        _out_proj_kernel,
        grid=(s // tm,),
        in_specs=[row(POOL_WIDTH), row(_QPAD), _const_spec(w_pool_rows.shape),
                  _const_spec(w_attn_rows.shape), row(D_MODEL), vec(D_MODEL), vec(D_MODEL),
                  vec(D_MODEL), vec(D_MODEL)],
        out_specs=[row(D_MODEL), row(D_MODEL)],
        out_shape=[jax.ShapeDtypeStruct((s, D_MODEL), F32),
                   jax.ShapeDtypeStruct((s, D_MODEL), BF16)],
        compiler_params=pltpu.CompilerParams(vmem_limit_bytes=VMEM_LIMIT_BYTES),
        name="out_proj",
    )(pool_out, attn_pad, w_pool_rows, w_attn_rows, x, ga1, g2, sc2, sh2)


def _mlp_kernel(h_ref, w1_ref, w2_ref, x_ref, ga_ref, o_ref, *, ff_chunk):
    h = h_ref[...]
    acc = jnp.zeros(x_ref.shape, F32)
    for c in range(D_FF // ff_chunk):
        a = _dot(h, w1_ref[:, c * ff_chunk:(c + 1) * ff_chunk])
        a = jnp.square(jnp.maximum(a, 0.0)).astype(BF16)
        acc = acc + _dot(a, w2_ref[c * ff_chunk:(c + 1) * ff_chunk, :])
    o_ref[...] = x_ref[...] + ga_ref[...] * acc


def _mlp(h2, w1, w2, x1, ga2, tm):
    s = x1.shape[0]
    row = lambda w: pl.BlockSpec((tm, w), lambda i: (i, 0))
    return pl.pallas_call(
        functools.partial(_mlp_kernel, ff_chunk=1024),
        grid=(s // tm,),
        in_specs=[row(D_MODEL), _const_spec(w1.shape), _const_spec(w2.shape), row(D_MODEL),
                  pl.BlockSpec((1, D_MODEL), lambda i: (0, 0))],
        out_specs=row(D_MODEL),
        out_shape=jax.ShapeDtypeStruct((s, D_MODEL), F32),
        compiler_params=pltpu.CompilerParams(vmem_limit_bytes=VMEM_LIMIT_BYTES),
        name="mlp",
    )(h2, w1, w2, x1, ga2)


def _pad_in_proj_weight(w_in):
    w = jnp.zeros((D_MODEL, _IN_PAD), F32)
    w = w.at[:, _OFF_U:_OFF_U + POOL_WIDTH].set(w_in[:, :POOL_WIDTH])
    src_q = POOL_WIDTH
    for hh in range(N_HEADS):
        dst = _OFF_Q + hh * LANES + (hh // GQA_GROUP) * HEAD_DIM
        w = w.at[:, dst:dst + HEAD_DIM].set(w_in[:, src_q + hh * HEAD_DIM:src_q + (hh + 1) * HEAD_DIM])
    src_kv = src_q + N_HEADS * HEAD_DIM
    w = w.at[:, _OFF_KC:_OFF_KC + 6 * KV_WIDTH].set(w_in[:, src_kv:src_kv + 6 * KV_WIDTH])
    src_g = src_kv + 6 * KV_WIDTH
    per_group = GQA_GROUP * N_BRANCH
    for gg in range(N_KV):
        w = w.at[:, _OFF_G + gg * LANES:_OFF_G + gg * LANES + per_group].set(
            w_in[:, src_g + gg * per_group:src_g + (gg + 1) * per_group])
    return w.astype(BF16)


def _pad_out_proj_weight(w_out):
    wa = jnp.zeros((N_HEADS * LANES, D_MODEL), F32)
    for hh in range(N_HEADS):
        dst = hh * LANES + (hh // GQA_GROUP) * HEAD_DIM
        wa = wa.at[dst:dst + HEAD_DIM].set(
            w_out[POOL_WIDTH + hh * HEAD_DIM:POOL_WIDTH + (hh + 1) * HEAD_DIM])
    return w_out[:POOL_WIDTH].astype(BF16), wa.astype(BF16)


def _compress_windows(kv, nb):
    s = kv.shape[0]
    chunks = kv.reshape(s // CMP_STRIDE, CMP_STRIDE, N_KV, HEAD_DIM).transpose(2, 0, 1, 3)
    chunks = chunks.reshape(N_KV, s // CMP_STRIDE, CMP_STRIDE * HEAD_DIM)
    nxt = jnp.concatenate([chunks[:, 1:], jnp.zeros_like(chunks[:, :1])], axis=1)
    flat = jnp.concatenate([chunks, nxt], axis=-1)
    flat = flat.reshape(N_KV, nb, CMP_PER_SEL, CMP_LEN * HEAD_DIM).transpose(0, 2, 1, 3)
    return flat.reshape(N_KV, CMP_PER_SEL * nb, CMP_LEN * HEAD_DIM)


def _group_lanes(a):
    return a.transpose(1, 0, 2).reshape(a.shape[1], KV_WIDTH)


def _key_chunks_t(v, chunk):
    return v.reshape(v.shape[0] // chunk, chunk, v.shape[1]).transpose(0, 2, 1)


def kernel(x, c, w_ada, b_ada, norm1_g, norm2_g, w_in, w_pool, pool_scale, q_gain, kc_gain,
           ks_gain, kw_gain, cmp_pos_k, cmp_w1_k, cmp_b1_k, cmp_w2_k, cmp_b2_k, cmp_pos_v,
           cmp_w1_v, cmp_b1_v, cmp_w2_v, cmp_b2_v, w_out, w_ff1, w_ff2):
    batch, s, _ = x.shape
    assert batch == 1 and w_ada.shape[0] == 1
    assert s % SEL_CHUNK == 0 and s >= WIN_SPAN and (s // SEL_BLOCK) & (s // SEL_BLOCK - 1) == 0
    nb = s // SEL_BLOCK
    tm = min(s, 512)
    x2 = x[0]

    mod = _ada(jnp.broadcast_to(c, (8, D_MODEL)), w_ada[0], b_ada)[0:1]
    sh1, sc1, ga1, sh2, sc2, ga2 = [mod[:, k * D_MODEL:(k + 1) * D_MODEL] for k in range(6)]

    pair = lambda gain: jnp.tile(gain, (1, 2))
    (pool_out, qpad, kc, vc, ksel, vsel, kwin, vwin, gates) = _in_proj(
        x2, norm1_g, sc1, sh1, _pad_in_proj_weight(w_in[0]), w_pool[0].astype(BF16), pool_scale,
        pair(q_gain), pair(ks_gain), pair(kw_gain), tm)

    kcmp = _compress(_compress_windows(kc, nb), cmp_pos_k.reshape(1, -1), cmp_w1_k[0].astype(BF16),
                     cmp_b1_k, cmp_w2_k[0].astype(BF16), cmp_b2_k, kc_gain, True)
    vcmp = _compress(_compress_windows(vc, nb), cmp_pos_v.reshape(1, -1), cmp_w1_v[0].astype(BF16),
                     cmp_b1_v, cmp_w2_v[0].astype(BF16), cmp_b2_v, kc_gain, False)

    gates_t = gates.reshape(s, N_KV, LANES)[:, :, :GATE_ROWS].transpose(1, 2, 0)
    attn_pad = _attention(qpad, gates_t, _group_lanes(kcmp), _group_lanes(vcmp).T, ksel,
                          _key_chunks_t(vsel, SEL_CHUNK), kwin, _key_chunks_t(vwin, TQ))

    w_pool_rows, w_attn_rows = _pad_out_proj_weight(w_out[0])
    x1, h2 = _out_proj(pool_out, attn_pad, w_pool_rows, w_attn_rows, x2, ga1, norm2_g, sc2, sh2, tm)
    out = _mlp(h2, w_ff1[0].astype(BF16), w_ff2[0].astype(BF16), x1, ga2, tm)
    return out[None]
```

```python
import functools

import jax
import jax.numpy as jnp
import numpy as np
from jax import lax
from jax.experimental import pallas as pl
from jax.experimental.pallas import tpu as pltpu

F32 = jnp.float32
BF16 = jnp.bfloat16

LANES = 128
VMEM_LIMIT_BYTES = 56 * 1024 * 1024

D_MODEL = 1024
POOL_WIDTH = 512
POOL_WINDOWS = (2, 4, 8, 16)
POOL_GROUP = POOL_WIDTH // len(POOL_WINDOWS)
POOL_HALO = 16
HEAD_DIM = 64
N_HEADS = 8
N_KV = 2
GQA_GROUP = N_HEADS // N_KV
KV_WIDTH = N_KV * HEAD_DIM
N_BRANCH = 3
CMP_LEN = 32
CMP_STRIDE = 16
CMP_HIDDEN = 4 * HEAD_DIM
SEL_BLOCK = 64
N_SEL = 16
WINDOW = 512
D_FF = 4 * D_MODEL
NEG_INF = -1e30
FORCE_BONUS = 1e4
EPS = 1e-6
LOG2E = 1.4426950408889634

TQ = 128
SEL_CHUNK = 256
BLOCKS_PER_CHUNK = SEL_CHUNK // SEL_BLOCK
SEL_GROUP = 4
WIN_SPAN = WINDOW + TQ
CMP_PER_SEL = SEL_BLOCK // CMP_STRIDE
GATE_ROWS = 16

_QPAD = N_HEADS * LANES
_OFF_U = 0
_OFF_Q = POOL_WIDTH
_OFF_KC = _OFF_Q + _QPAD
_OFF_VC = _OFF_KC + KV_WIDTH
_OFF_KS = _OFF_VC + KV_WIDTH
_OFF_VS = _OFF_KS + KV_WIDTH
_OFF_KW = _OFF_VS + KV_WIDTH
_OFF_VW = _OFF_KW + KV_WIDTH
_OFF_G = _OFF_VW + KV_WIDTH
_IN_PAD = _OFF_G + N_KV * LANES


def _dot(a, b):
    return jnp.dot(a, b, preferred_element_type=F32)


def _dot_nt(a, b):
    return lax.dot_general(a, b, (((1,), (1,)), ((), ())), preferred_element_type=F32)


def _const_spec(shape):
    nd = len(shape)
    return pl.BlockSpec(shape, lambda *_: (0,) * nd, pipeline_mode=pl.Buffered(1))


def _ada_kernel(c_ref, w_ref, b_ref, o_ref):
    o_ref[...] = jnp.dot(c_ref[...], w_ref[...], preferred_element_type=F32,
                         precision=lax.Precision.HIGHEST) + b_ref[...]


def _ada(c8, w, b):
    n = w.shape[1]
    bn = 1024
    return pl.pallas_call(
        _ada_kernel,
        grid=(n // bn,),
        in_specs=[pl.BlockSpec((8, D_MODEL), lambda j: (0, 0)),
                  pl.BlockSpec((D_MODEL, bn), lambda j: (0, j)),
                  pl.BlockSpec((1, bn), lambda j: (0, j))],
        out_specs=pl.BlockSpec((8, bn), lambda j: (0, j)),
        out_shape=jax.ShapeDtypeStruct((8, n), F32),
        name="ada",
    )(c8, w, b)


def _rms_modulate(x, g, sc, sh):
    ms = jnp.mean(x * x, axis=-1, keepdims=True)
    return (x * lax.rsqrt(ms + EPS)) * (g * (1.0 + sc)) + sh


def _head_norm_pair(x, gain2):
    lane = lax.broadcasted_iota(jnp.int32, x.shape, 1)
    lo = lane < HEAD_DIM
    sq = x * x
    s_lo = jnp.sum(jnp.where(lo, sq, 0.0), axis=-1, keepdims=True)
    s_hi = jnp.sum(jnp.where(lo, 0.0, sq), axis=-1, keepdims=True)
    ms = jnp.where(lo, s_lo, s_hi) * (1.0 / HEAD_DIM)
    return x * lax.rsqrt(ms + EPS) * gain2


def _in_proj_kernel(x_ref, g_ref, sc_ref, sh_ref, w_ref, wpool_ref, pscale_ref, qg_ref, ksg_ref,
                    kwg_ref, pool_ref, q_ref, kc_ref, vc_ref, ks_ref, vs_ref, kw_ref, vw_ref,
                    gate_ref, ebuf_ref, *, tm):
    i = pl.program_id(0)
    h = _rms_modulate(x_ref[...], g_ref[...], sc_ref[...], sh_ref[...]).astype(BF16)
    proj = _dot(h, w_ref[...])

    for hh in range(N_HEADS):
        qs = proj[:, _OFF_Q + hh * LANES:_OFF_Q + (hh + 1) * LANES]
        ms = jnp.sum(qs * qs, axis=-1, keepdims=True) * (1.0 / HEAD_DIM)
        qn = qs * lax.rsqrt(ms + EPS) * qg_ref[...]
        q_ref[:, hh * LANES:(hh + 1) * LANES] = (qn * (HEAD_DIM ** -0.5 * LOG2E)).astype(BF16)

    for gg in range(N_KV):
        kc_ref[gg] = proj[:, _OFF_KC + gg * HEAD_DIM:_OFF_KC + (gg + 1) * HEAD_DIM]
        vc_ref[gg] = proj[:, _OFF_VC + gg * HEAD_DIM:_OFF_VC + (gg + 1) * HEAD_DIM]
    ks_ref[...] = _head_norm_pair(proj[:, _OFF_KS:_OFF_KS + KV_WIDTH], ksg_ref[...]).astype(BF16)
    kw_ref[...] = _head_norm_pair(proj[:, _OFF_KW:_OFF_KW + KV_WIDTH], kwg_ref[...]).astype(BF16)
    for k in range(tm // SEL_CHUNK):
        rows = slice(k * SEL_CHUNK, (k + 1) * SEL_CHUNK)
        vs_ref[k] = proj[rows, _OFF_VS:_OFF_VS + KV_WIDTH].T.astype(BF16)
    for k in range(tm // TQ):
        rows = slice(k * TQ, (k + 1) * TQ)
        vw_ref[k] = proj[rows, _OFF_VW:_OFF_VW + KV_WIDTH].T.astype(BF16)
    gates_t = jax.nn.sigmoid(proj[:, _OFF_G:_OFF_G + N_KV * LANES]).T
    for gg in range(N_KV):
        gate_ref[gg] = gates_t[gg * LANES:gg * LANES + GATE_ROWS]

    @pl.when(i == 0)
    def _():
        ebuf_ref[0:POOL_HALO, :] = jnp.zeros((POOL_HALO, POOL_WIDTH), F32)

    u = proj[:, _OFF_U:_OFF_U + POOL_WIDTH]
    ebuf_ref[POOL_HALO:POOL_HALO + tm, :] = u
    t1 = i * tm + lax.broadcasted_iota(jnp.int32, (tm, POOL_GROUP), 0) + 1
    for gi, w in enumerate(POOL_WINDOWS):
        c0 = gi * POOL_GROUP
        win = u[:, c0:c0 + POOL_GROUP]
        for k in range(1, w):
            win = win + ebuf_ref[POOL_HALO - k:POOL_HALO - k + tm, c0:c0 + POOL_GROUP]
        cnt = jnp.minimum(t1, w).astype(F32)
        pooled = win / cnt - u[:, c0:c0 + POOL_GROUP]
        y = _dot(pooled.astype(BF16), wpool_ref[gi])
        pool_ref[:, c0:c0 + POOL_GROUP] = (y * pscale_ref[:, c0:c0 + POOL_GROUP]).astype(BF16)
    ebuf_ref[0:POOL_HALO, :] = ebuf_ref[tm:tm + POOL_HALO, :]


def _in_proj(x, g1, sc1, sh1, w_in_p, w_pool, pool_scale, qg, ksg, kwg, tm):
    s = x.shape[0]
    row = lambda w: pl.BlockSpec((tm, w), lambda i: (i, 0))
    vec = lambda w: pl.BlockSpec((1, w), lambda i: (0, 0))
    per_head = pl.BlockSpec((N_KV, tm, HEAD_DIM), lambda i: (0, i, 0))
    chunks_t = lambda ch: pl.BlockSpec((tm // ch, KV_WIDTH, ch), lambda i: (i, 0, 0))
    out_shape = [
        jax.ShapeDtypeStruct((s, POOL_WIDTH), BF16),
        jax.ShapeDtypeStruct((s, _QPAD), BF16),
        jax.ShapeDtypeStruct((N_KV, s, HEAD_DIM), F32),
        jax.ShapeDtypeStruct((N_KV, s, HEAD_DIM), F32),
        jax.ShapeDtypeStruct((s, KV_WIDTH), BF16),
        jax.ShapeDtypeStruct((s // SEL_CHUNK, KV_WIDTH, SEL_CHUNK), BF16),
        jax.ShapeDtypeStruct((s, KV_WIDTH), BF16),
        jax.ShapeDtypeStruct((s // TQ, KV_WIDTH, TQ), BF16),
        jax.ShapeDtypeStruct((N_KV, GATE_ROWS, s), F32),
    ]
    return pl.pallas_call(
        functools.partial(_in_proj_kernel, tm=tm),
        grid=(s // tm,),
        in_specs=[row(D_MODEL), vec(D_MODEL), vec(D_MODEL), vec(D_MODEL),
                  _const_spec(w_in_p.shape), _const_spec(w_pool.shape), vec(POOL_WIDTH),
                  vec(LANES), vec(LANES), vec(LANES)],
        out_specs=[row(POOL_WIDTH), row(_QPAD), per_head, per_head, row(KV_WIDTH),
                   chunks_t(SEL_CHUNK), row(KV_WIDTH), chunks_t(TQ),
                   pl.BlockSpec((N_KV, GATE_ROWS, tm), lambda i: (0, 0, i))],
        out_shape=out_shape,
        scratch_shapes=[pltpu.VMEM((tm + POOL_HALO, POOL_WIDTH), F32)],
        compiler_params=pltpu.CompilerParams(dimension_semantics=("arbitrary",),
                                             vmem_limit_bytes=VMEM_LIMIT_BYTES),
        name="in_proj",
    )(x, g1, sc1, sh1, w_in_p, w_pool, pool_scale, qg, ksg, kwg)


def _compress_kernel(c_ref, pos_ref, w1_ref, b1_ref, w2_ref, b2_ref, gain_ref, o_ref, *,
                     normalize):
    half = CMP_STRIDE * HEAD_DIM
    n_rows = c_ref.shape[1]
    c = c_ref[0]
    first = _dot((c + pos_ref[:, 0:half]).astype(BF16), w1_ref[0:half, :])
    second = _dot((c + pos_ref[:, half:2 * half]).astype(BF16), w1_ref[half:2 * half, :])
    hid = jax.nn.gelu(first + pltpu.roll(second, n_rows - 1, axis=0) + b1_ref[...])
    y = _dot(hid.astype(BF16), w2_ref[...]) + b2_ref[...]
    if normalize:
        ms = jnp.mean(y * y, axis=-1, keepdims=True)
        y = y * lax.rsqrt(ms + EPS) * gain_ref[...]
    o_ref[0] = y.astype(BF16)


def _compress(chunks, pos, w1, b1, w2, b2, gain, normalize):
    _, n_chunks, width = chunks.shape
    vec = lambda w: pl.BlockSpec((1, w), lambda g: (0, 0))
    return pl.pallas_call(
        functools.partial(_compress_kernel, normalize=normalize),
        grid=(N_KV,),
        in_specs=[pl.BlockSpec((1, n_chunks, width), lambda g: (g, 0, 0)),
                  vec(2 * width), _const_spec(w1.shape), vec(CMP_HIDDEN), _const_spec(w2.shape),
                  vec(HEAD_DIM), vec(HEAD_DIM)],
        out_specs=pl.BlockSpec((1, n_chunks, HEAD_DIM), lambda g: (g, 0, 0)),
        out_shape=jax.ShapeDtypeStruct((N_KV, n_chunks, HEAD_DIM), BF16),
        compiler_params=pltpu.CompilerParams(vmem_limit_bytes=VMEM_LIMIT_BYTES),
        name="compress",
    )(chunks, pos, w1, b1, w2, b2, gain)


def _attn_kernel(q_ref, gate_ref, rowt_ref, kc_ref, vct_ref, ks_ref, vst_ref, kw_ref, vwt_ref,
                 o_ref, psum_ref, madd_ref, bias_ref, m_ref, l_ref, acc_ref, flag_ref, list_ref, *,
                 nb):
    i = pl.program_id(0)
    g = pl.program_id(1)
    q0 = i * TQ
    nc = CMP_PER_SEL * nb
    n_chunks = nb // BLOCKS_PER_CHUNK
    gslope = jnp.where(g == 0, LOG2E, LOG2E * 2.0 ** -GQA_GROUP).astype(F32)
    slopes = [gslope * (2.0 ** -(r + 1)) for r in range(GQA_GROUP)]
    heads = [slice(r * TQ, (r + 1) * TQ) for r in range(GQA_GROUP)]

    q = jnp.concatenate([q_ref[:, r * LANES:(r + 1) * LANES] for r in range(GQA_GROUP)], axis=0)
    t_lane = q0 + lax.broadcasted_iota(jnp.int32, (1, TQ), 1)

    rel_c = CMP_STRIDE * rowt_ref[0:nc, :] + (CMP_LEN - 1) - q0.astype(F32)
    ok_c = rel_c <= lax.broadcasted_iota(jnp.int32, (1, TQ), 1).astype(F32)
    s_c = _dot_nt(kc_ref[...], q)
    p_sum = jnp.zeros((nc, TQ), F32)
    p_cols = []
    for r in range(GQA_GROUP):
        s = jnp.where(ok_c, s_c[:, heads[r]] + slopes[r] * rel_c, NEG_INF)
        m = jnp.max(s, axis=0, keepdims=True)
        e = jnp.exp2(s - m)
        l = jnp.sum(e, axis=0, keepdims=True)
        p = e * jnp.where(m > 0.5 * NEG_INF, 1.0 / l, 0.0)
        p_sum = p_sum + p
        p_cols.append(p.astype(BF16))
    o_cmp = _dot(vct_ref[...], jnp.concatenate(p_cols, axis=1))

    psum_ref[0:8, :] = jnp.zeros((8, TQ), F32)
    psum_ref[8:8 + nc, :] = p_sum
    every4 = lambda start: psum_ref[pl.ds(8 + start, nb, stride=CMP_PER_SEL), :]
    imp = every4(0) + every4(1) + every4(2) + 0.5 * every4(3) + 0.5 * every4(-1)
    blk = lax.broadcasted_iota(jnp.int32, (nb, TQ), 0)
    cur = lax.shift_right_logical(t_lane, SEL_BLOCK.bit_length() - 1)
    causal = blk <= cur
    forced = jnp.where(blk == 0, 1.0, 0.0) + jnp.where(blk == cur, 1.0, 0.0) \
        + jnp.where(blk == cur - 1, 1.0, 0.0)
    val = jnp.where(causal, imp + jnp.where(forced > 0.0, FORCE_BONUS, 0.0), -1.0)

    def pick(_, v):
        blk_f = rowt_ref[0:nb, :]
        mx = jnp.max(v, axis=0, keepdims=True)
        idx = jnp.min(jnp.where(v == mx, blk_f, float(nb)), axis=0, keepdims=True)
        return jnp.where(blk_f == idx, -2.0, v)

    picked = lax.fori_loop(0, min(N_SEL, nb), pick, val)
    madd_ref[0:nb, :] = jnp.where(causal, jnp.where(picked < -1.5, 0.0, NEG_INF), NEG_INF)
    madd_ref[nb:nb + BLOCKS_PER_CHUNK, :] = jnp.full((BLOCKS_PER_CHUNK, TQ), NEG_INF, F32)
    for c in range(n_chunks):
        rows = madd_ref[c * BLOCKS_PER_CHUNK:(c + 1) * BLOCKS_PER_CHUNK, :]
        flag_ref[c] = (jnp.max(rows) > 0.5 * NEG_INF).astype(jnp.int32)

    key_row = rowt_ref[0:SEL_CHUNK, :]
    for r in range(GQA_GROUP):
        bias_ref[r] = slopes[r] * key_row
    m_ref[...] = jnp.full(m_ref.shape, NEG_INF, F32)
    l_ref[...] = jnp.zeros(l_ref.shape, F32)
    acc_ref[...] = jnp.zeros(acc_ref.shape, F32)

    def sel_group(chunks, diagonal):
        s, madds, rel0, kcs = [], [], [], []
        for c in chunks:
            kc = jnp.minimum(c, n_chunks - 1)
            k0 = pl.multiple_of(kc * SEL_CHUNK, SEL_CHUNK)
            s.append(_dot_nt(ks_ref[pl.ds(k0, SEL_CHUNK), :], q))
            madd = jnp.concatenate(
                [jnp.broadcast_to(madd_ref[pl.ds(c * BLOCKS_PER_CHUNK + b, 1), :], (SEL_BLOCK, TQ))
                 for b in range(BLOCKS_PER_CHUNK)], axis=0)
            if diagonal:
                pos = k0 + lax.broadcasted_iota(jnp.int32, (SEL_CHUNK, TQ), 0)
                madd = jnp.where(pos <= t_lane, madd, NEG_INF)
            madds.append(madd)
            rel0.append((k0 - q0).astype(F32))
            kcs.append(kc)
        p_cols = [[] for _ in chunks]
        for r in range(GQA_GROUP):
            shifts = [slopes[r] * x for x in rel0]
            us = [sk[:, heads[r]] + bias_ref[r] + mk for sk, mk in zip(s, madds)]
            m_old = m_ref[:, heads[r]]
            m_new = m_old
            for u, sh in zip(us, shifts):
                m_new = jnp.maximum(m_new, jnp.max(u, axis=0, keepdims=True) + sh)
            alpha = jnp.exp2(m_old - m_new)
            l_new = alpha * l_ref[:, heads[r]]
            for k, (u, sh) in enumerate(zip(us, shifts)):
                p = jnp.exp2(u - (m_new - sh))
                l_new = l_new + jnp.sum(p, axis=0, keepdims=True)
                p_cols[k].append(p.astype(BF16))
            l_ref[:, heads[r]] = l_new
            m_ref[:, heads[r]] = m_new
            acc_ref[:, heads[r]] = alpha * acc_ref[:, heads[r]]
        pv = _dot(vst_ref[kcs[0]], jnp.concatenate(p_cols[0], axis=1))
        for k in range(1, len(chunks)):
            pv = pv + _dot(vst_ref[kcs[k]], jnp.concatenate(p_cols[k], axis=1))
        acc_ref[...] += pv

    c_diag = q0 // SEL_CHUNK

    def compact(c, n):
        list_ref[n] = c
        return n + flag_ref[c]

    n_active = lax.fori_loop(0, c_diag, compact, jnp.int32(0))
    for k in range(SEL_GROUP - 1):
        list_ref[n_active + k] = n_chunks

    def group_body(p, carry):
        sel_group([list_ref[p * SEL_GROUP + k] for k in range(SEL_GROUP)], diagonal=False)
        return carry

    lax.fori_loop(0, (n_active + SEL_GROUP - 1) // SEL_GROUP, group_body, 0)
    sel_group([c_diag], diagonal=True)

    w0 = pl.multiple_of(jnp.maximum(q0 - WINDOW, 0), TQ)
    s_w = _dot_nt(kw_ref[pl.ds(w0, WIN_SPAN), :], q)
    row_w = rowt_ref[0:WIN_SPAN, :]
    lane_w = lax.broadcasted_iota(jnp.int32, (WIN_SPAN, TQ), 1).astype(F32)
    dq = (q0 - w0).astype(F32)
    d_w = lane_w - row_w + dq
    ok_w = jnp.abs(d_w - (WINDOW - 1) * 0.5) < WINDOW * 0.5
    rel_w = row_w - dq
    p_cols = []
    l_w = []
    for r in range(GQA_GROUP):
        s = jnp.where(ok_w, s_w[:, heads[r]] + slopes[r] * rel_w, NEG_INF)
        m = jnp.max(s, axis=0, keepdims=True)
        e = jnp.exp2(s - m)
        l_w.append(jnp.sum(e, axis=0, keepdims=True))
        p_cols.append(e.astype(BF16))
    p_w = jnp.concatenate(p_cols, axis=1)
    wb = w0 // TQ
    o_win = _dot(vwt_ref[wb], p_w[0:TQ])
    for b in range(1, WIN_SPAN // TQ):
        o_win = o_win + _dot(vwt_ref[wb + b], p_w[b * TQ:(b + 1) * TQ])

    for r in range(GQA_GROUP):
        gc = gate_ref[0, 3 * r + 0:3 * r + 1, :]
        gs = gate_ref[0, 3 * r + 1:3 * r + 2, :]
        gw = gate_ref[0, 3 * r + 2:3 * r + 3, :]
        out_t = (gc * o_cmp[:, heads[r]]
                 + (gs / l_ref[:, heads[r]]) * acc_ref[:, heads[r]]
                 + (gw / l_w[r]) * o_win[:, heads[r]])
        o_ref[:, r * LANES:(r + 1) * LANES] = out_t.T.astype(BF16)


def _attention(qpad, gates_t, kcmp, vcmp_t, ksel, vsel_t, kwin, vwin_t):
    s = qpad.shape[0]
    nb = s // SEL_BLOCK
    nc = CMP_PER_SEL * nb
    gw = GQA_GROUP * LANES
    n_rows = max(nc, WIN_SPAN)
    row_tile = jnp.asarray(np.broadcast_to(np.arange(n_rows)[:, None], (n_rows, TQ))
                           .astype(np.float32))
    return pl.pallas_call(
        functools.partial(_attn_kernel, nb=nb),
        grid=(s // TQ, N_KV),
        in_specs=[pl.BlockSpec((TQ, gw), lambda i, g: (i, g)),
                  pl.BlockSpec((1, GATE_ROWS, TQ), lambda i, g: (g, 0, i)),
                  _const_spec(row_tile.shape),
                  _const_spec(kcmp.shape), _const_spec(vcmp_t.shape),
                  _const_spec(ksel.shape), _const_spec(vsel_t.shape),
                  _const_spec(kwin.shape), _const_spec(vwin_t.shape)],
        out_specs=pl.BlockSpec((TQ, gw), lambda i, g: (i, g)),
        out_shape=jax.ShapeDtypeStruct((s, N_HEADS * LANES), BF16),
        scratch_shapes=[pltpu.VMEM((8 + nc, TQ), F32),
                        pltpu.VMEM((nb + BLOCKS_PER_CHUNK, TQ), F32),
                        pltpu.VMEM((GQA_GROUP, SEL_CHUNK, TQ), F32),
                        pltpu.VMEM((1, GQA_GROUP * TQ), F32),
                        pltpu.VMEM((1, GQA_GROUP * TQ), F32),
                        pltpu.VMEM((LANES, GQA_GROUP * TQ), F32),
                        pltpu.SMEM((nb // BLOCKS_PER_CHUNK,), jnp.int32),
                        pltpu.SMEM((nb // BLOCKS_PER_CHUNK + SEL_GROUP,), jnp.int32)],
        compiler_params=pltpu.CompilerParams(dimension_semantics=("arbitrary", "arbitrary"),
                                             vmem_limit_bytes=VMEM_LIMIT_BYTES),
        name="attn",
    )(qpad, gates_t, row_tile, kcmp, vcmp_t, ksel, vsel_t, kwin, vwin_t)


def _out_proj_kernel(pool_ref, attn_ref, wp_ref, wa_ref, x_ref, ga_ref, g_ref, sc_ref, sh_ref,
                     x1_ref, h2_ref):
    mix = _dot(pool_ref[...], wp_ref[...]) + _dot(attn_ref[...], wa_ref[...])
    x1 = x_ref[...] + ga_ref[...] * mix
    x1_ref[...] = x1
    h2_ref[...] = _rms_modulate(x1, g_ref[...], sc_ref[...], sh_ref[...]).astype(BF16)


def _out_proj(pool_out, attn_pad, w_pool_rows, w_attn_rows, x, ga1, g2, sc2, sh2, tm):
    s = x.shape[0]
    row = lambda w: pl.BlockSpec((tm, w), lambda i: (i, 0))
    vec = lambda w: pl.BlockSpec((1, w), lambda i: (0, 0))
    return pl.pallas_call(
        _out_proj_kernel,
        grid=(s // tm,),
        in_specs=[row(POOL_WIDTH), row(_QPAD), _const_spec(w_pool_rows.shape),
                  _const_spec(w_attn_rows.shape), row(D_MODEL), vec(D_MODEL), vec(D_MODEL),
                  vec(D_MODEL), vec(D_MODEL)],
        out_specs=[row(D_MODEL), row(D_MODEL)],
        out_shape=[jax.ShapeDtypeStruct((s, D_MODEL), F32),
                   jax.ShapeDtypeStruct((s, D_MODEL), BF16)],
        compiler_params=pltpu.CompilerParams(vmem_limit_bytes=VMEM_LIMIT_BYTES),
        name="out_proj",
    )(pool_out, attn_pad, w_pool_rows, w_attn_rows, x, ga1, g2, sc2, sh2)


def _mlp_kernel(h_ref, w1_ref, w2_ref, x_ref, ga_ref, o_ref, *, ff_chunk):
    h = h_ref[...]
    acc = jnp.zeros(x_ref.shape, F32)
    for c in range(D_FF // ff_chunk):
        a = _dot(h, w1_ref[:, c * ff_chunk:(c + 1) * ff_chunk])
        a = jnp.square(jnp.maximum(a, 0.0)).astype(BF16)
        acc = acc + _dot(a, w2_ref[c * ff_chunk:(c + 1) * ff_chunk, :])
    o_ref[...] = x_ref[...] + ga_ref[...] * acc


def _mlp(h2, w1, w2, x1, ga2, tm):
    s = x1.shape[0]
    row = lambda w: pl.BlockSpec((tm, w), lambda i: (i, 0))
    return pl.pallas_call(
        functools.partial(_mlp_kernel, ff_chunk=1024),
        grid=(s // tm,),
        in_specs=[row(D_MODEL), _const_spec(w1.shape), _const_spec(w2.shape), row(D_MODEL),
                  pl.BlockSpec((1, D_MODEL), lambda i: (0, 0))],
        out_specs=row(D_MODEL),
        out_shape=jax.ShapeDtypeStruct((s, D_MODEL), F32),
        compiler_params=pltpu.CompilerParams(vmem_limit_bytes=VMEM_LIMIT_BYTES),
        name="mlp",
    )(h2, w1, w2, x1, ga2)


def _pad_in_proj_weight(w_in):
    zeros = lambda n: jnp.zeros((D_MODEL, n), w_in.dtype)
    src_q = POOL_WIDTH
    cols = [w_in[:, :POOL_WIDTH]]
    for hh in range(N_HEADS):
        wq = w_in[:, src_q + hh * HEAD_DIM:src_q + (hh + 1) * HEAD_DIM]
        cols += [wq, zeros(HEAD_DIM)] if hh // GQA_GROUP == 0 else [zeros(HEAD_DIM), wq]
    src_kv = src_q + N_HEADS * HEAD_DIM
    cols.append(w_in[:, src_kv:src_kv + 6 * KV_WIDTH])
    src_g = src_kv + 6 * KV_WIDTH
    per_group = GQA_GROUP * N_BRANCH
    for gg in range(N_KV):
        cols += [w_in[:, src_g + gg * per_group:src_g + (gg + 1) * per_group],
                 zeros(LANES - per_group)]
    return jnp.concatenate(cols, axis=1).astype(BF16)


def _pad_out_proj_weight(w_out):
    zeros = jnp.zeros((HEAD_DIM, D_MODEL), w_out.dtype)
    rows = []
    for hh in range(N_HEADS):
        wr = w_out[POOL_WIDTH + hh * HEAD_DIM:POOL_WIDTH + (hh + 1) * HEAD_DIM]
        rows += [wr, zeros] if hh // GQA_GROUP == 0 else [zeros, wr]
    return w_out[:POOL_WIDTH].astype(BF16), jnp.concatenate(rows, axis=0).astype(BF16)


def _group_lanes(a):
    return a.transpose(1, 0, 2).reshape(a.shape[1], KV_WIDTH)


def kernel(x, c, w_ada, b_ada, norm1_g, norm2_g, w_in, w_pool, pool_scale, q_gain, kc_gain,
           ks_gain, kw_gain, cmp_pos_k, cmp_w1_k, cmp_b1_k, cmp_w2_k, cmp_b2_k, cmp_pos_v,
           cmp_w1_v, cmp_b1_v, cmp_w2_v, cmp_b2_v, w_out, w_ff1, w_ff2):
    batch, s, _ = x.shape
    assert batch == 1 and w_ada.shape[0] == 1
    assert s % SEL_CHUNK == 0 and s >= WIN_SPAN and (s // SEL_BLOCK) & (s // SEL_BLOCK - 1) == 0
    tm = min(s, 512)
    x2 = x[0]

    mod = _ada(jnp.broadcast_to(c, (8, D_MODEL)), w_ada[0], b_ada)[0:1]
    sh1, sc1, ga1, sh2, sc2, ga2 = [mod[:, k * D_MODEL:(k + 1) * D_MODEL] for k in range(6)]

    pair = lambda gain: jnp.tile(gain, (1, 2))
    (pool_out, qpad, kc, vc, ksel, vsel_t, kwin, vwin_t, gates_t) = _in_proj(
        x2, norm1_g, sc1, sh1, _pad_in_proj_weight(w_in[0]), w_pool[0].astype(BF16), pool_scale,
        pair(q_gain), pair(ks_gain), pair(kw_gain), tm)

    chunks = lambda a: a.reshape(N_KV, s // CMP_STRIDE, CMP_STRIDE * HEAD_DIM)
    kcmp = _compress(chunks(kc), cmp_pos_k.reshape(1, -1), cmp_w1_k[0].astype(BF16), cmp_b1_k,
                     cmp_w2_k[0].astype(BF16), cmp_b2_k, kc_gain, True)
    vcmp = _compress(chunks(vc), cmp_pos_v.reshape(1, -1), cmp_w1_v[0].astype(BF16), cmp_b1_v,
                     cmp_w2_v[0].astype(BF16), cmp_b2_v, kc_gain, False)

    attn_pad = _attention(qpad, gates_t, _group_lanes(kcmp), _group_lanes(vcmp).T, ksel, vsel_t,
                          kwin, vwin_t)

    w_pool_rows, w_attn_rows = _pad_out_proj_weight(w_out[0])
    x1, h2 = _out_proj(pool_out, attn_pad, w_pool_rows, w_attn_rows, x2, ga1, norm2_g, sc2, sh2, tm)
    out = _mlp(h2, w_ff1[0].astype(BF16), w_ff2[0].astype(BF16), x1, ga2, tm)
    return out[None]
```

```python
import functools

import jax
import jax.numpy as jnp
import numpy as np
from jax import lax
from jax.experimental import pallas as pl
from jax.experimental.pallas import tpu as pltpu

F32 = jnp.float32
BF16 = jnp.bfloat16

LANES = 128
VMEM_LIMIT_BYTES = 56 * 1024 * 1024

D_MODEL = 1024
POOL_WIDTH = 512
POOL_WINDOWS = (2, 4, 8, 16)
POOL_GROUP = POOL_WIDTH // len(POOL_WINDOWS)
POOL_HALO = 16
HEAD_DIM = 64
N_HEADS = 8
N_KV = 2
GQA_GROUP = N_HEADS // N_KV
KV_WIDTH = N_KV * HEAD_DIM
N_BRANCH = 3
CMP_LEN = 32
CMP_STRIDE = 16
CMP_HIDDEN = 4 * HEAD_DIM
SEL_BLOCK = 64
N_SEL = 16
WINDOW = 512
D_FF = 4 * D_MODEL
NEG_INF = -1e30
FORCE_BONUS = 1e4
EPS = 1e-6
LOG2E = 1.4426950408889634

TQ = 128
SEL_CHUNK = 256
BLOCKS_PER_CHUNK = SEL_CHUNK // SEL_BLOCK
SEL_GROUP = 4
WIN_SPAN = WINDOW + TQ
CMP_PER_SEL = SEL_BLOCK // CMP_STRIDE
GATE_ROWS = 16

_QPAD = N_HEADS * LANES
_OFF_U = 0
_OFF_Q = POOL_WIDTH
_OFF_KC = _OFF_Q + _QPAD
_OFF_VC = _OFF_KC + KV_WIDTH
_OFF_KS = _OFF_VC + KV_WIDTH
_OFF_VS = _OFF_KS + KV_WIDTH
_OFF_KW = _OFF_VS + KV_WIDTH
_OFF_VW = _OFF_KW + KV_WIDTH
_OFF_G = _OFF_VW + KV_WIDTH
_IN_PAD = _OFF_G + N_KV * LANES


def _dot(a, b):
    return jnp.dot(a, b, preferred_element_type=F32)


def _dot_nt(a, b):
    return lax.dot_general(a, b, (((1,), (1,)), ((), ())), preferred_element_type=F32)


def _const_spec(shape):
    nd = len(shape)
    return pl.BlockSpec(shape, lambda *_: (0,) * nd, pipeline_mode=pl.Buffered(1))


def _ada_kernel(c_ref, w_ref, b_ref, o_ref):
    o_ref[...] = jnp.dot(c_ref[...], w_ref[...], preferred_element_type=F32,
                         precision=lax.Precision.HIGHEST) + b_ref[...]


def _ada(c8, w, b):
    n = w.shape[1]
    bn = 1024
    return pl.pallas_call(
        _ada_kernel,
        grid=(n // bn,),
        in_specs=[pl.BlockSpec((8, D_MODEL), lambda j: (0, 0)),
                  pl.BlockSpec((D_MODEL, bn), lambda j: (0, j)),
                  pl.BlockSpec((1, bn), lambda j: (0, j))],
        out_specs=pl.BlockSpec((8, bn), lambda j: (0, j)),
        out_shape=jax.ShapeDtypeStruct((8, n), F32),
        name="ada",
    )(c8, w, b)


def _rms_modulate(x, g, sc, sh):
    ms = jnp.mean(x * x, axis=-1, keepdims=True)
    return (x * lax.rsqrt(ms + EPS)) * (g * (1.0 + sc)) + sh


def _head_norm_pair(x, gain2):
    lane = lax.broadcasted_iota(jnp.int32, x.shape, 1)
    lo = lane < HEAD_DIM
    sq = x * x
    s_lo = jnp.sum(jnp.where(lo, sq, 0.0), axis=-1, keepdims=True)
    s_hi = jnp.sum(jnp.where(lo, 0.0, sq), axis=-1, keepdims=True)
    ms = jnp.where(lo, s_lo, s_hi) * (1.0 / HEAD_DIM)
    return x * lax.rsqrt(ms + EPS) * gain2


def _in_proj_kernel(x_ref, g_ref, sc_ref, sh_ref, w_ref, wpool_ref, pscale_ref, qg_ref, ksg_ref,
                    kwg_ref, pool_ref, q_ref, kc_ref, vc_ref, ks_ref, vs_ref, kw_ref, vw_ref,
                    gate_ref, ebuf_ref, *, tm):
    i = pl.program_id(0)
    h = _rms_modulate(x_ref[...], g_ref[...], sc_ref[...], sh_ref[...]).astype(BF16)
    proj = _dot(h, w_ref[...])

    for hh in range(N_HEADS):
        qs = proj[:, _OFF_Q + hh * LANES:_OFF_Q + (hh + 1) * LANES]
        ms = jnp.sum(qs * qs, axis=-1, keepdims=True) * (1.0 / HEAD_DIM)
        qn = qs * lax.rsqrt(ms + EPS) * qg_ref[...]
        q_ref[:, hh * LANES:(hh + 1) * LANES] = (qn * (HEAD_DIM ** -0.5 * LOG2E)).astype(BF16)

    for gg in range(N_KV):
        kc_ref[gg] = proj[:, _OFF_KC + gg * HEAD_DIM:_OFF_KC + (gg + 1) * HEAD_DIM]
        vc_ref[gg] = proj[:, _OFF_VC + gg * HEAD_DIM:_OFF_VC + (gg + 1) * HEAD_DIM]
    ks_ref[...] = _head_norm_pair(proj[:, _OFF_KS:_OFF_KS + KV_WIDTH], ksg_ref[...]).astype(BF16)
    kw_ref[...] = _head_norm_pair(proj[:, _OFF_KW:_OFF_KW + KV_WIDTH], kwg_ref[...]).astype(BF16)
    def value_tiles(ref, off, chunk):
        ones_row = jnp.where(lax.broadcasted_iota(jnp.int32, (HEAD_DIM, chunk), 0) == 0, 1.0, 0.0)
        for k in range(tm // chunk):
            vt = proj[k * chunk:(k + 1) * chunk, off:off + KV_WIDTH].T
            for gg in range(N_KV):
                ref[gg, k] = jnp.concatenate(
                    [vt[gg * HEAD_DIM:(gg + 1) * HEAD_DIM], ones_row], axis=0).astype(BF16)

    value_tiles(vs_ref, _OFF_VS, SEL_CHUNK)
    value_tiles(vw_ref, _OFF_VW, TQ)
    gates_t = jax.nn.sigmoid(proj[:, _OFF_G:_OFF_G + N_KV * LANES]).T
    for gg in range(N_KV):
        gate_ref[gg] = gates_t[gg * LANES:gg * LANES + GATE_ROWS]

    @pl.when(i == 0)
    def _():
        ebuf_ref[0:POOL_HALO, :] = jnp.zeros((POOL_HALO, POOL_WIDTH), F32)

    u = proj[:, _OFF_U:_OFF_U + POOL_WIDTH]
    ebuf_ref[POOL_HALO:POOL_HALO + tm, :] = u
    t1 = i * tm + lax.broadcasted_iota(jnp.int32, (tm, POOL_GROUP), 0) + 1
    for gi, w in enumerate(POOL_WINDOWS):
        c0 = gi * POOL_GROUP
        win = u[:, c0:c0 + POOL_GROUP]
        for k in range(1, w):
            win = win + ebuf_ref[POOL_HALO - k:POOL_HALO - k + tm, c0:c0 + POOL_GROUP]
        cnt = jnp.minimum(t1, w).astype(F32)
        pooled = win / cnt - u[:, c0:c0 + POOL_GROUP]
        y = _dot(pooled.astype(BF16), wpool_ref[gi])
        pool_ref[:, c0:c0 + POOL_GROUP] = (y * pscale_ref[:, c0:c0 + POOL_GROUP]).astype(BF16)
    ebuf_ref[0:POOL_HALO, :] = ebuf_ref[tm:tm + POOL_HALO, :]


def _in_proj(x, g1, sc1, sh1, w_in_p, w_pool, pool_scale, qg, ksg, kwg, tm):
    s = x.shape[0]
    row = lambda w: pl.BlockSpec((tm, w), lambda i: (i, 0))
    vec = lambda w: pl.BlockSpec((1, w), lambda i: (0, 0))
    per_head = pl.BlockSpec((N_KV, tm, HEAD_DIM), lambda i: (0, i, 0))
    chunks_t = lambda ch: pl.BlockSpec((N_KV, tm // ch, LANES, ch), lambda i: (0, i, 0, 0))
    out_shape = [
        jax.ShapeDtypeStruct((s, POOL_WIDTH), BF16),
        jax.ShapeDtypeStruct((s, _QPAD), BF16),
        jax.ShapeDtypeStruct((N_KV, s, HEAD_DIM), F32),
        jax.ShapeDtypeStruct((N_KV, s, HEAD_DIM), F32),
        jax.ShapeDtypeStruct((s, KV_WIDTH), BF16),
        jax.ShapeDtypeStruct((N_KV, s // SEL_CHUNK, LANES, SEL_CHUNK), BF16),
        jax.ShapeDtypeStruct((s, KV_WIDTH), BF16),
        jax.ShapeDtypeStruct((N_KV, s // TQ, LANES, TQ), BF16),
        jax.ShapeDtypeStruct((N_KV, GATE_ROWS, s), F32),
    ]
    return pl.pallas_call(
        functools.partial(_in_proj_kernel, tm=tm),
        grid=(s // tm,),
        in_specs=[row(D_MODEL), vec(D_MODEL), vec(D_MODEL), vec(D_MODEL),
                  _const_spec(w_in_p.shape), _const_spec(w_pool.shape), vec(POOL_WIDTH),
                  vec(LANES), vec(LANES), vec(LANES)],
        out_specs=[row(POOL_WIDTH), row(_QPAD), per_head, per_head, row(KV_WIDTH),
                   chunks_t(SEL_CHUNK), row(KV_WIDTH), chunks_t(TQ),
                   pl.BlockSpec((N_KV, GATE_ROWS, tm), lambda i: (0, 0, i))],
        out_shape=out_shape,
        scratch_shapes=[pltpu.VMEM((tm + POOL_HALO, POOL_WIDTH), F32)],
        compiler_params=pltpu.CompilerParams(dimension_semantics=("arbitrary",),
                                             vmem_limit_bytes=VMEM_LIMIT_BYTES),
        name="in_proj",
    )(x, g1, sc1, sh1, w_in_p, w_pool, pool_scale, qg, ksg, kwg)


def _compress_kernel(c_ref, pos_ref, w1_ref, b1_ref, w2_ref, b2_ref, gain_ref, o_ref, *,
                     normalize):
    half = CMP_STRIDE * HEAD_DIM
    n_rows = c_ref.shape[1]
    c = c_ref[0]
    first = _dot((c + pos_ref[:, 0:half]).astype(BF16), w1_ref[0:half, :])
    second = _dot((c + pos_ref[:, half:2 * half]).astype(BF16), w1_ref[half:2 * half, :])
    hid = jax.nn.gelu(first + pltpu.roll(second, n_rows - 1, axis=0) + b1_ref[...])
    y = _dot(hid.astype(BF16), w2_ref[...]) + b2_ref[...]
    if normalize:
        ms = jnp.mean(y * y, axis=-1, keepdims=True)
        y = y * lax.rsqrt(ms + EPS) * gain_ref[...]
    o_ref[0] = y.astype(BF16)


def _compress(chunks, pos, w1, b1, w2, b2, gain, normalize):
    _, n_chunks, width = chunks.shape
    vec = lambda w: pl.BlockSpec((1, w), lambda g: (0, 0))
    return pl.pallas_call(
        functools.partial(_compress_kernel, normalize=normalize),
        grid=(N_KV,),
        in_specs=[pl.BlockSpec((1, n_chunks, width), lambda g: (g, 0, 0)),
                  vec(2 * width), _const_spec(w1.shape), vec(CMP_HIDDEN), _const_spec(w2.shape),
                  vec(HEAD_DIM), vec(HEAD_DIM)],
        out_specs=pl.BlockSpec((1, n_chunks, HEAD_DIM), lambda g: (g, 0, 0)),
        out_shape=jax.ShapeDtypeStruct((N_KV, n_chunks, HEAD_DIM), BF16),
        compiler_params=pltpu.CompilerParams(vmem_limit_bytes=VMEM_LIMIT_BYTES),
        name="compress",
    )(chunks, pos, w1, b1, w2, b2, gain)


def _attn_kernel(q_ref, gate_ref, rowt_ref, kc_ref, vct_ref, ks_ref, vst_ref, kw_ref, vwt_ref,
                 o_ref, psum_ref, madd_ref, bias_ref, m_ref, acc_ref, ocmp_ref, owin_ref, flag_ref,
                 list_ref, *, nb):
    g = pl.program_id(0)
    i = pl.program_id(1)
    q0 = i * TQ
    nc = CMP_PER_SEL * nb
    n_chunks = nb // BLOCKS_PER_CHUNK
    gslope = jnp.where(g == 0, LOG2E, LOG2E * 2.0 ** -GQA_GROUP).astype(F32)
    slopes = [gslope * (2.0 ** -(r + 1)) for r in range(GQA_GROUP)]
    heads = [slice(r * TQ, (r + 1) * TQ) for r in range(GQA_GROUP)]

    q = jnp.concatenate([q_ref[:, r * LANES:(r + 1) * LANES] for r in range(GQA_GROUP)], axis=0)
    t_lane = q0 + lax.broadcasted_iota(jnp.int32, (1, TQ), 1)

    rel_c = CMP_STRIDE * rowt_ref[0:nc, :] + (CMP_LEN - 1) - q0.astype(F32)
    ok_c = rel_c <= lax.broadcasted_iota(jnp.int32, (1, TQ), 1).astype(F32)
    s_c = _dot_nt(kc_ref[...], q)
    p_sum = jnp.zeros((nc, TQ), F32)
    p_cols = []
    for r in range(GQA_GROUP):
        s = jnp.where(ok_c, s_c[:, heads[r]] + slopes[r] * rel_c, NEG_INF)
        m = jnp.max(s, axis=0, keepdims=True)
        e = jnp.exp2(s - m)
        l = jnp.sum(e, axis=0, keepdims=True)
        p = e * jnp.where(m > 0.5 * NEG_INF, 1.0 / l, 0.0)
        p_sum = p_sum + p
        p_cols.append(p.astype(BF16))
    ocmp_ref[...] = _dot(vct_ref[0], jnp.concatenate(p_cols, axis=1))

    w0 = pl.multiple_of(jnp.maximum(q0 - WINDOW, 0), TQ)
    s_w = _dot_nt(kw_ref[pl.ds(w0, WIN_SPAN), :], q)
    row_w = rowt_ref[0:WIN_SPAN, :]
    lane_w = lax.broadcasted_iota(jnp.int32, (WIN_SPAN, TQ), 1).astype(F32)
    dq = (q0 - w0).astype(F32)
    d_w = lane_w - row_w + dq
    ok_w = jnp.abs(d_w - (WINDOW - 1) * 0.5) < WINDOW * 0.5
    rel_w = row_w - dq
    p_cols = []
    for r in range(GQA_GROUP):
        s = jnp.where(ok_w, s_w[:, heads[r]] + slopes[r] * rel_w, NEG_INF)
        m = jnp.max(s, axis=0, keepdims=True)
        p_cols.append(jnp.exp2((s - m).astype(BF16)))
    p_w = jnp.concatenate(p_cols, axis=1)
    wb = w0 // TQ
    o_win = _dot(vwt_ref[0, wb], p_w[0:TQ])
    for b in range(1, WIN_SPAN // TQ):
        o_win = o_win + _dot(vwt_ref[0, wb + b], p_w[b * TQ:(b + 1) * TQ])
    owin_ref[...] = o_win

    psum_ref[0:8, :] = jnp.zeros((8, TQ), F32)
    psum_ref[8:8 + nc, :] = p_sum
    every4 = lambda start: psum_ref[pl.ds(8 + start, nb, stride=CMP_PER_SEL), :]
    imp = every4(0) + every4(1) + every4(2) + 0.5 * every4(3) + 0.5 * every4(-1)
    blk = lax.broadcasted_iota(jnp.int32, (nb, TQ), 0)
    cur = lax.shift_right_logical(t_lane, SEL_BLOCK.bit_length() - 1)
    causal = blk <= cur
    forced = jnp.where(blk == 0, 1.0, 0.0) + jnp.where(blk == cur, 1.0, 0.0) \
        + jnp.where(blk == cur - 1, 1.0, 0.0)
    val = jnp.where(causal, imp + jnp.where(forced > 0.0, FORCE_BONUS, 0.0), -1.0)

    def pick(_, v):
        blk_f = rowt_ref[0:nb, :]
        mx = jnp.max(v, axis=0, keepdims=True)
        idx = jnp.min(jnp.where(v == mx, blk_f, float(nb)), axis=0, keepdims=True)
        return jnp.where(blk_f == idx, -2.0, v)

    picked = lax.fori_loop(0, min(N_SEL, nb), pick, val)
    madd_ref[0:nb, :] = jnp.where(causal, jnp.where(picked < -1.5, 0.0, NEG_INF), NEG_INF)
    madd_ref[nb:nb + BLOCKS_PER_CHUNK, :] = jnp.full((BLOCKS_PER_CHUNK, TQ), NEG_INF, F32)
    for c in range(n_chunks):
        rows = madd_ref[c * BLOCKS_PER_CHUNK:(c + 1) * BLOCKS_PER_CHUNK, :]
        flag_ref[c] = (jnp.max(rows) > 0.5 * NEG_INF).astype(jnp.int32)

    key_row = rowt_ref[0:SEL_CHUNK, :]
    for r in range(GQA_GROUP):
        bias_ref[r] = slopes[r] * key_row
    m_ref[...] = jnp.full(m_ref.shape, NEG_INF, F32)
    acc_ref[...] = jnp.zeros(acc_ref.shape, F32)

    def sel_group(chunks, diagonal):
        s, madds, rel0, kcs = [], [], [], []
        for c in chunks:
            kc = jnp.minimum(c, n_chunks - 1)
            k0 = pl.multiple_of(kc * SEL_CHUNK, SEL_CHUNK)
            s.append(_dot_nt(ks_ref[pl.ds(k0, SEL_CHUNK), :], q))
            madd = jnp.concatenate(
                [jnp.broadcast_to(madd_ref[pl.ds(c * BLOCKS_PER_CHUNK + b, 1), :], (SEL_BLOCK, TQ))
                 for b in range(BLOCKS_PER_CHUNK)], axis=0)
            if diagonal:
                pos = k0 + lax.broadcasted_iota(jnp.int32, (SEL_CHUNK, TQ), 0)
                madd = jnp.where(pos <= t_lane, madd, NEG_INF)
            madds.append(madd)
            rel0.append((k0 - q0).astype(F32))
            kcs.append(kc)
        p_cols = [[] for _ in chunks]
        for r in range(GQA_GROUP):
            shifts = [slopes[r] * x for x in rel0]
            us = [sk[:, heads[r]] + bias_ref[r] + mk for sk, mk in zip(s, madds)]
            m_old = m_ref[:, heads[r]]
            m_new = m_old
            for u, sh in zip(us, shifts):
                m_new = jnp.maximum(m_new, jnp.max(u, axis=0, keepdims=True) + sh)
            alpha = jnp.exp2(m_old - m_new)
            for k, (u, sh) in enumerate(zip(us, shifts)):
                p_cols[k].append(jnp.exp2((u - (m_new - sh)).astype(BF16)))
            m_ref[:, heads[r]] = m_new
            acc_ref[:, heads[r]] = alpha * acc_ref[:, heads[r]]
        pv = _dot(vst_ref[0, kcs[0]], jnp.concatenate(p_cols[0], axis=1))
        for k in range(1, len(chunks)):
            pv = pv + _dot(vst_ref[0, kcs[k]], jnp.concatenate(p_cols[k], axis=1))
        acc_ref[...] += pv

    c_diag = q0 // SEL_CHUNK

    def compact(c, n):
        list_ref[n] = c
        return n + flag_ref[c]

    n_active = lax.fori_loop(0, c_diag, compact, jnp.int32(0))
    for k in range(SEL_GROUP - 1):
        list_ref[n_active + k] = n_chunks

    def group_body(p, carry):
        sel_group([list_ref[p * SEL_GROUP + k] for k in range(SEL_GROUP)], diagonal=False)
        return carry

    lax.fori_loop(0, (n_active + SEL_GROUP - 1) // SEL_GROUP, group_body, 0)
    sel_group([c_diag], diagonal=True)

    for r in range(GQA_GROUP):
        gc = gate_ref[0, 3 * r + 0:3 * r + 1, :]
        gs = gate_ref[0, 3 * r + 1:3 * r + 2, :]
        gw = gate_ref[0, 3 * r + 2:3 * r + 3, :]
        acc = acc_ref[:, heads[r]]
        win = owin_ref[:, heads[r]]
        out_t = (gc * ocmp_ref[:, heads[r]]
                 + (gs / acc[HEAD_DIM:HEAD_DIM + 1]) * acc
                 + (gw / win[HEAD_DIM:HEAD_DIM + 1]) * win)
        o_ref[:, r * LANES:(r + 1) * LANES] = out_t.T.astype(BF16)


def _attention(qpad, gates_t, kcmp, vcmp_t, ksel, vsel_t, kwin, vwin_t):
    s = qpad.shape[0]
    nb = s // SEL_BLOCK
    nc = CMP_PER_SEL * nb
    gw = GQA_GROUP * LANES
    n_rows = max(nc, WIN_SPAN)
    row_tile = jnp.asarray(np.broadcast_to(np.arange(n_rows)[:, None], (n_rows, TQ))
                           .astype(np.float32))
    per_group = lambda a: pl.BlockSpec((1,) + a.shape[1:], lambda g, i: (g,) + (0,) * (a.ndim - 1),
                                       pipeline_mode=pl.Buffered(1))
    return pl.pallas_call(
        functools.partial(_attn_kernel, nb=nb),
        grid=(N_KV, s // TQ),
        in_specs=[pl.BlockSpec((TQ, gw), lambda g, i: (i, g)),
                  pl.BlockSpec((1, GATE_ROWS, TQ), lambda g, i: (g, 0, i)),
                  _const_spec(row_tile.shape),
                  _const_spec(kcmp.shape), per_group(vcmp_t),
                  _const_spec(ksel.shape), per_group(vsel_t),
                  _const_spec(kwin.shape), per_group(vwin_t)],
        out_specs=pl.BlockSpec((TQ, gw), lambda g, i: (i, g)),
        out_shape=jax.ShapeDtypeStruct((s, N_HEADS * LANES), BF16),
        scratch_shapes=[pltpu.VMEM((8 + nc, TQ), F32),
                        pltpu.VMEM((nb + BLOCKS_PER_CHUNK, TQ), F32),
                        pltpu.VMEM((GQA_GROUP, SEL_CHUNK, TQ), F32),
                        pltpu.VMEM((1, GQA_GROUP * TQ), F32),
                        pltpu.VMEM((LANES, GQA_GROUP * TQ), F32),
                        pltpu.VMEM((LANES, GQA_GROUP * TQ), F32),
                        pltpu.VMEM((LANES, GQA_GROUP * TQ), F32),
                        pltpu.SMEM((nb // BLOCKS_PER_CHUNK,), jnp.int32),
                        pltpu.SMEM((nb // BLOCKS_PER_CHUNK + SEL_GROUP,), jnp.int32)],
        compiler_params=pltpu.CompilerParams(dimension_semantics=("arbitrary", "arbitrary"),
                                             vmem_limit_bytes=VMEM_LIMIT_BYTES),
        name="attn",
    )(qpad, gates_t, row_tile, kcmp, vcmp_t, ksel, vsel_t, kwin, vwin_t)


def _out_proj_kernel(pool_ref, attn_ref, wp_ref, wa_ref, x_ref, ga_ref, g_ref, sc_ref, sh_ref,
                     x1_ref, h2_ref):
    mix = _dot(pool_ref[...], wp_ref[...]) + _dot(attn_ref[...], wa_ref[...])
    x1 = x_ref[...] + ga_ref[...] * mix
    x1_ref[...] = x1
    h2_ref[...] = _rms_modulate(x1, g_ref[...], sc_ref[...], sh_ref[...]).astype(BF16)


def _out_proj(pool_out, attn_pad, w_pool_rows, w_attn_rows, x, ga1, g2, sc2, sh2, tm):
    s = x.shape[0]
    row = lambda w: pl.BlockSpec((tm, w), lambda i: (i, 0))
    vec = lambda w: pl.BlockSpec((1, w), lambda i: (0, 0))
    return pl.pallas_call(
        _out_proj_kernel,
        grid=(s // tm,),
        in_specs=[row(POOL_WIDTH), row(_QPAD), _const_spec(w_pool_rows.shape),
                  _const_spec(w_attn_rows.shape), row(D_MODEL), vec(D_MODEL), vec(D_MODEL),
                  vec(D_MODEL), vec(D_MODEL)],
        out_specs=[row(D_MODEL), row(D_MODEL)],
        out_shape=[jax.ShapeDtypeStruct((s, D_MODEL), F32),
                   jax.ShapeDtypeStruct((s, D_MODEL), BF16)],
        compiler_params=pltpu.CompilerParams(vmem_limit_bytes=VMEM_LIMIT_BYTES),
        name="out_proj",
    )(pool_out, attn_pad, w_pool_rows, w_attn_rows, x, ga1, g2, sc2, sh2)


def _mlp_kernel(h_ref, w1_ref, w2_ref, x_ref, ga_ref, o_ref, *, ff_chunk):
    h = h_ref[...]
    acc = jnp.zeros(x_ref.shape, F32)
    for c in range(D_FF // ff_chunk):
        a = _dot(h, w1_ref[:, c * ff_chunk:(c + 1) * ff_chunk])
        a = jnp.square(jnp.maximum(a, 0.0)).astype(BF16)
        acc = acc + _dot(a, w2_ref[c * ff_chunk:(c + 1) * ff_chunk, :])
    o_ref[...] = x_ref[...] + ga_ref[...] * acc


def _mlp(h2, w1, w2, x1, ga2, tm):
    s = x1.shape[0]
    row = lambda w: pl.BlockSpec((tm, w), lambda i: (i, 0))
    return pl.pallas_call(
        functools.partial(_mlp_kernel, ff_chunk=1024),
        grid=(s // tm,),
        in_specs=[row(D_MODEL), _const_spec(w1.shape), _const_spec(w2.shape), row(D_MODEL),
                  pl.BlockSpec((1, D_MODEL), lambda i: (0, 0))],
        out_specs=row(D_MODEL),
        out_shape=jax.ShapeDtypeStruct((s, D_MODEL), F32),
        compiler_params=pltpu.CompilerParams(vmem_limit_bytes=VMEM_LIMIT_BYTES),
        name="mlp",
    )(h2, w1, w2, x1, ga2)


def _pad_in_proj_weight(w_in):
    zeros = lambda n: jnp.zeros((D_MODEL, n), w_in.dtype)
    src_q = POOL_WIDTH
    cols = [w_in[:, :POOL_WIDTH]]
    for hh in range(N_HEADS):
        wq = w_in[:, src_q + hh * HEAD_DIM:src_q + (hh + 1) * HEAD_DIM]
        cols += [wq, zeros(HEAD_DIM)] if hh // GQA_GROUP == 0 else [zeros(HEAD_DIM), wq]
    src_kv = src_q + N_HEADS * HEAD_DIM
    cols.append(w_in[:, src_kv:src_kv + 6 * KV_WIDTH])
    src_g = src_kv + 6 * KV_WIDTH
    per_group = GQA_GROUP * N_BRANCH
    for gg in range(N_KV):
        cols += [w_in[:, src_g + gg * per_group:src_g + (gg + 1) * per_group],
                 zeros(LANES - per_group)]
    return jnp.concatenate(cols, axis=1).astype(BF16)


def _pad_out_proj_weight(w_out):
    wa = w_out[POOL_WIDTH:].reshape(N_HEADS, HEAD_DIM, D_MODEL)
    wa = jnp.pad(wa, ((0, 0), (0, LANES - HEAD_DIM), (0, 0))).reshape(N_HEADS * LANES, D_MODEL)
    return w_out[:POOL_WIDTH].astype(BF16), wa.astype(BF16)


def _group_lanes(a):
    return a.transpose(1, 0, 2).reshape(a.shape[1], KV_WIDTH)


def kernel(x, c, w_ada, b_ada, norm1_g, norm2_g, w_in, w_pool, pool_scale, q_gain, kc_gain,
           ks_gain, kw_gain, cmp_pos_k, cmp_w1_k, cmp_b1_k, cmp_w2_k, cmp_b2_k, cmp_pos_v,
           cmp_w1_v, cmp_b1_v, cmp_w2_v, cmp_b2_v, w_out, w_ff1, w_ff2):
    batch, s, _ = x.shape
    assert batch == 1 and w_ada.shape[0] == 1
    assert s % SEL_CHUNK == 0 and s >= WIN_SPAN and (s // SEL_BLOCK) & (s // SEL_BLOCK - 1) == 0
    tm = min(s, 512)
    x2 = x[0]

    mod = _ada(jnp.broadcast_to(c, (8, D_MODEL)), w_ada[0], b_ada)[0:1]
    sh1, sc1, ga1, sh2, sc2, ga2 = [mod[:, k * D_MODEL:(k + 1) * D_MODEL] for k in range(6)]

    pair = lambda gain: jnp.tile(gain, (1, 2))
    (pool_out, qpad, kc, vc, ksel, vsel_t, kwin, vwin_t, gates_t) = _in_proj(
        x2, norm1_g, sc1, sh1, _pad_in_proj_weight(w_in[0]), w_pool[0].astype(BF16), pool_scale,
        pair(q_gain), pair(ks_gain), pair(kw_gain), tm)

    chunks = lambda a: a.reshape(N_KV, s // CMP_STRIDE, CMP_STRIDE * HEAD_DIM)
    kcmp = _compress(chunks(kc), cmp_pos_k.reshape(1, -1), cmp_w1_k[0].astype(BF16), cmp_b1_k,
                     cmp_w2_k[0].astype(BF16), cmp_b2_k, kc_gain, True)
    vcmp = _compress(chunks(vc), cmp_pos_v.reshape(1, -1), cmp_w1_v[0].astype(BF16), cmp_b1_v,
                     cmp_w2_v[0].astype(BF16), cmp_b2_v, kc_gain, False)

    vcmp_t = jnp.pad(vcmp.transpose(0, 2, 1), ((0, 0), (0, LANES - HEAD_DIM), (0, 0)))
    attn_pad = _attention(qpad, gates_t, _group_lanes(kcmp), vcmp_t, ksel, vsel_t, kwin, vwin_t)

    w_pool_rows, w_attn_rows = _pad_out_proj_weight(w_out[0])
    x1, h2 = _out_proj(pool_out, attn_pad, w_pool_rows, w_attn_rows, x2, ga1, norm2_g, sc2, sh2, tm)
    out = _mlp(h2, w_ff1[0].astype(BF16), w_ff2[0].astype(BF16), x1, ga2, tm)
    return out[None]
```

```python
import functools

import jax
import jax.numpy as jnp
import numpy as np
from jax import lax
from jax.experimental import pallas as pl
from jax.experimental.pallas import tpu as pltpu

F32 = jnp.float32
BF16 = jnp.bfloat16

LANES = 128
VMEM_LIMIT_BYTES = 56 * 1024 * 1024

D_MODEL = 1024
POOL_WIDTH = 512
POOL_WINDOWS = (2, 4, 8, 16)
POOL_GROUP = POOL_WIDTH // len(POOL_WINDOWS)
POOL_HALO = 16
HEAD_DIM = 64
N_HEADS = 8
N_KV = 2
GQA_GROUP = N_HEADS // N_KV
KV_WIDTH = N_KV * HEAD_DIM
N_BRANCH = 3
CMP_LEN = 32
CMP_STRIDE = 16
CMP_HIDDEN = 4 * HEAD_DIM
SEL_BLOCK = 64
N_SEL = 16
WINDOW = 512
D_FF = 4 * D_MODEL
NEG_INF = -1e30
FORCE_BONUS = 1e4
EPS = 1e-6
LOG2E = 1.4426950408889634

TQ = 128
SEL_CHUNK = 256
BLOCKS_PER_CHUNK = SEL_CHUNK // SEL_BLOCK
SEL_GROUP = 4
CMP_BUCKETS = 4
WIN_SPAN = WINDOW + TQ
CMP_PER_SEL = SEL_BLOCK // CMP_STRIDE
GATE_ROWS = 16

_QPAD = N_HEADS * LANES
_OFF_U = 0
_OFF_Q = POOL_WIDTH
_OFF_KC = _OFF_Q + _QPAD
_OFF_VC = _OFF_KC + KV_WIDTH
_OFF_KS = _OFF_VC + KV_WIDTH
_OFF_VS = _OFF_KS + KV_WIDTH
_OFF_KW = _OFF_VS + KV_WIDTH
_OFF_VW = _OFF_KW + KV_WIDTH
_OFF_G = _OFF_VW + KV_WIDTH
_IN_PAD = _OFF_G + N_KV * LANES


def _dot(a, b):
    return jnp.dot(a, b, preferred_element_type=F32)


def _dot_nt(a, b):
    return lax.dot_general(a, b, (((1,), (1,)), ((), ())), preferred_element_type=F32)


def _const_spec(shape):
    nd = len(shape)
    return pl.BlockSpec(shape, lambda *_: (0,) * nd, pipeline_mode=pl.Buffered(1))


def _ada_kernel(c_ref, w_ref, b_ref, o_ref):
    o_ref[...] = jnp.dot(c_ref[...], w_ref[...], preferred_element_type=F32,
                         precision=lax.Precision.HIGHEST) + b_ref[...]


def _ada(c8, w, b):
    n = w.shape[1]
    bn = 1024
    return pl.pallas_call(
        _ada_kernel,
        grid=(n // bn,),
        in_specs=[pl.BlockSpec((8, D_MODEL), lambda j: (0, 0)),
                  pl.BlockSpec((D_MODEL, bn), lambda j: (0, j)),
                  pl.BlockSpec((1, bn), lambda j: (0, j))],
        out_specs=pl.BlockSpec((8, bn), lambda j: (0, j)),
        out_shape=jax.ShapeDtypeStruct((8, n), F32),
        name="ada",
    )(c8, w, b)


def _rms_modulate(x, g, sc, sh):
    ms = jnp.mean(x * x, axis=-1, keepdims=True)
    return (x * lax.rsqrt(ms + EPS)) * (g * (1.0 + sc)) + sh


def _head_norm_pair(x, gain2):
    lane = lax.broadcasted_iota(jnp.int32, x.shape, 1)
    lo = lane < HEAD_DIM
    sq = x * x
    s_lo = jnp.sum(jnp.where(lo, sq, 0.0), axis=-1, keepdims=True)
    s_hi = jnp.sum(jnp.where(lo, 0.0, sq), axis=-1, keepdims=True)
    ms = jnp.where(lo, s_lo, s_hi) * (1.0 / HEAD_DIM)
    return x * lax.rsqrt(ms + EPS) * gain2


def _in_proj_kernel(x_ref, g_ref, sc_ref, sh_ref, w_ref, wpool_ref, pscale_ref, qg_ref, ksg_ref,
                    kwg_ref, pool_ref, q_ref, kc_ref, vc_ref, ks_ref, vs_ref, kw_ref, vw_ref,
                    gate_ref, ebuf_ref, *, tm):
    i = pl.program_id(0)
    h = _rms_modulate(x_ref[...], g_ref[...], sc_ref[...], sh_ref[...]).astype(BF16)
    proj = _dot(h, w_ref[...])

    for hh in range(N_HEADS):
        qs = proj[:, _OFF_Q + hh * LANES:_OFF_Q + (hh + 1) * LANES]
        ms = jnp.sum(qs * qs, axis=-1, keepdims=True) * (1.0 / HEAD_DIM)
        qn = qs * lax.rsqrt(ms + EPS) * qg_ref[...]
        q_ref[:, hh * LANES:(hh + 1) * LANES] = (qn * (HEAD_DIM ** -0.5 * LOG2E)).astype(BF16)

    for gg in range(N_KV):
        kc_ref[gg] = proj[:, _OFF_KC + gg * HEAD_DIM:_OFF_KC + (gg + 1) * HEAD_DIM]
        vc_ref[gg] = proj[:, _OFF_VC + gg * HEAD_DIM:_OFF_VC + (gg + 1) * HEAD_DIM]
    ks_ref[...] = _head_norm_pair(proj[:, _OFF_KS:_OFF_KS + KV_WIDTH], ksg_ref[...]).astype(BF16)
    kw_ref[...] = _head_norm_pair(proj[:, _OFF_KW:_OFF_KW + KV_WIDTH], kwg_ref[...]).astype(BF16)
    def value_tiles(ref, off, chunk):
        ones_row = jnp.where(lax.broadcasted_iota(jnp.int32, (HEAD_DIM, chunk), 0) == 0, 1.0, 0.0)
        for k in range(tm // chunk):
            vt = proj[k * chunk:(k + 1) * chunk, off:off + KV_WIDTH].T
            for gg in range(N_KV):
                ref[gg, k] = jnp.concatenate(
                    [vt[gg * HEAD_DIM:(gg + 1) * HEAD_DIM], ones_row], axis=0).astype(BF16)

    value_tiles(vs_ref, _OFF_VS, SEL_CHUNK)
    value_tiles(vw_ref, _OFF_VW, TQ)
    gates_t = jax.nn.sigmoid(proj[:, _OFF_G:_OFF_G + N_KV * LANES]).T
    for gg in range(N_KV):
        gate_ref[gg] = gates_t[gg * LANES:gg * LANES + GATE_ROWS]

    @pl.when(i == 0)
    def _():
        ebuf_ref[0:POOL_HALO, :] = jnp.zeros((POOL_HALO, POOL_WIDTH), F32)

    u = proj[:, _OFF_U:_OFF_U + POOL_WIDTH]
    ebuf_ref[POOL_HALO:POOL_HALO + tm, :] = u
    t1 = i * tm + lax.broadcasted_iota(jnp.int32, (tm, POOL_GROUP), 0) + 1
    for gi, w in enumerate(POOL_WINDOWS):
        c0 = gi * POOL_GROUP
        win = u[:, c0:c0 + POOL_GROUP]
        for k in range(1, w):
            win = win + ebuf_ref[POOL_HALO - k:POOL_HALO - k + tm, c0:c0 + POOL_GROUP]
        cnt = jnp.minimum(t1, w).astype(F32)
        pooled = win / cnt - u[:, c0:c0 + POOL_GROUP]
        y = _dot(pooled.astype(BF16), wpool_ref[gi])
        pool_ref[:, c0:c0 + POOL_GROUP] = (y * pscale_ref[:, c0:c0 + POOL_GROUP]).astype(BF16)
    ebuf_ref[0:POOL_HALO, :] = ebuf_ref[tm:tm + POOL_HALO, :]


def _in_proj(x, g1, sc1, sh1, w_in_p, w_pool, pool_scale, qg, ksg, kwg, tm):
    s = x.shape[0]
    row = lambda w: pl.BlockSpec((tm, w), lambda i: (i, 0))
    vec = lambda w: pl.BlockSpec((1, w), lambda i: (0, 0))
    per_head = pl.BlockSpec((N_KV, tm, HEAD_DIM), lambda i: (0, i, 0))
    chunks_t = lambda ch: pl.BlockSpec((N_KV, tm // ch, LANES, ch), lambda i: (0, i, 0, 0))
    out_shape = [
        jax.ShapeDtypeStruct((s, POOL_WIDTH), BF16),
        jax.ShapeDtypeStruct((s, _QPAD), BF16),
        jax.ShapeDtypeStruct((N_KV, s, HEAD_DIM), F32),
        jax.ShapeDtypeStruct((N_KV, s, HEAD_DIM), F32),
        jax.ShapeDtypeStruct((s, KV_WIDTH), BF16),
        jax.ShapeDtypeStruct((N_KV, s // SEL_CHUNK, LANES, SEL_CHUNK), BF16),
        jax.ShapeDtypeStruct((s, KV_WIDTH), BF16),
        jax.ShapeDtypeStruct((N_KV, s // TQ, LANES, TQ), BF16),
        jax.ShapeDtypeStruct((N_KV, GATE_ROWS, s), F32),
    ]
    return pl.pallas_call(
        functools.partial(_in_proj_kernel, tm=tm),
        grid=(s // tm,),
        in_specs=[row(D_MODEL), vec(D_MODEL), vec(D_MODEL), vec(D_MODEL),
                  _const_spec(w_in_p.shape), _const_spec(w_pool.shape), vec(POOL_WIDTH),
                  vec(LANES), vec(LANES), vec(LANES)],
        out_specs=[row(POOL_WIDTH), row(_QPAD), per_head, per_head, row(KV_WIDTH),
                   chunks_t(SEL_CHUNK), row(KV_WIDTH), chunks_t(TQ),
                   pl.BlockSpec((N_KV, GATE_ROWS, tm), lambda i: (0, 0, i))],
        out_shape=out_shape,
        scratch_shapes=[pltpu.VMEM((tm + POOL_HALO, POOL_WIDTH), F32)],
        compiler_params=pltpu.CompilerParams(dimension_semantics=("arbitrary",),
                                             vmem_limit_bytes=VMEM_LIMIT_BYTES),
        name="in_proj",
    )(x, g1, sc1, sh1, w_in_p, w_pool, pool_scale, qg, ksg, kwg)


def _compress_kernel(c_ref, pos_ref, w1_ref, b1_ref, w2_ref, b2_ref, gain_ref, o_ref, *,
                     normalize):
    half = CMP_STRIDE * HEAD_DIM
    n_rows = c_ref.shape[1]
    c = c_ref[0]
    first = _dot((c + pos_ref[:, 0:half]).astype(BF16), w1_ref[0:half, :])
    second = _dot((c + pos_ref[:, half:2 * half]).astype(BF16), w1_ref[half:2 * half, :])
    hid = jax.nn.gelu(first + pltpu.roll(second, n_rows - 1, axis=0) + b1_ref[...])
    y = _dot(hid.astype(BF16), w2_ref[...]) + b2_ref[...]
    if normalize:
        ms = jnp.mean(y * y, axis=-1, keepdims=True)
        y = y * lax.rsqrt(ms + EPS) * gain_ref[...]
    o_ref[0] = y.astype(BF16)


def _compress(chunks, pos, w1, b1, w2, b2, gain, normalize):
    _, n_chunks, width = chunks.shape
    vec = lambda w: pl.BlockSpec((1, w), lambda g: (0, 0))
    return pl.pallas_call(
        functools.partial(_compress_kernel, normalize=normalize),
        grid=(N_KV,),
        in_specs=[pl.BlockSpec((1, n_chunks, width), lambda g: (g, 0, 0)),
                  vec(2 * width), _const_spec(w1.shape), vec(CMP_HIDDEN), _const_spec(w2.shape),
                  vec(HEAD_DIM), vec(HEAD_DIM)],
        out_specs=pl.BlockSpec((1, n_chunks, HEAD_DIM), lambda g: (g, 0, 0)),
        out_shape=jax.ShapeDtypeStruct((N_KV, n_chunks, HEAD_DIM), BF16),
        compiler_params=pltpu.CompilerParams(vmem_limit_bytes=VMEM_LIMIT_BYTES),
        name="compress",
    )(chunks, pos, w1, b1, w2, b2, gain)


def _attn_kernel(q_ref, gate_ref, rowt_ref, kc_ref, vct_ref, ks_ref, vst_ref, kw_ref, vwt_ref,
                 o_ref, psum_ref, madd_ref, bias_ref, m_ref, acc_ref, ocmp_ref, owin_ref,
                 flag_ref, list_ref, *, nb):
    g = pl.program_id(0)
    i = pl.program_id(1)
    q0 = i * TQ
    nc = CMP_PER_SEL * nb
    n_chunks = nb // BLOCKS_PER_CHUNK
    gslope = jnp.where(g == 0, LOG2E, LOG2E * 2.0 ** -GQA_GROUP).astype(F32)
    slopes = [gslope * (2.0 ** -(r + 1)) for r in range(GQA_GROUP)]
    heads = [slice(r * TQ, (r + 1) * TQ) for r in range(GQA_GROUP)]

    q = jnp.concatenate([q_ref[:, r * LANES:(r + 1) * LANES] for r in range(GQA_GROUP)], axis=0)
    t_lane = q0 + lax.broadcasted_iota(jnp.int32, (1, TQ), 1)

    w0 = pl.multiple_of(jnp.maximum(q0 - WINDOW, 0), TQ)
    s_w = _dot_nt(kw_ref[pl.ds(w0, WIN_SPAN), :], q)
    row_w = rowt_ref[0:WIN_SPAN, :]
    lane_w = lax.broadcasted_iota(jnp.int32, (WIN_SPAN, TQ), 1).astype(F32)
    dq = (q0 - w0).astype(F32)
    d_w = lane_w - row_w + dq
    ok_w = jnp.abs(d_w - (WINDOW - 1) * 0.5) < WINDOW * 0.5
    rel_w = row_w - dq
    p_cols = []
    for r in range(GQA_GROUP):
        s = jnp.where(ok_w, s_w[:, heads[r]] + slopes[r] * rel_w, NEG_INF)
        m = jnp.max(s, axis=0, keepdims=True)
        p_cols.append(jnp.exp2((s - m).astype(BF16)))
    p_w = jnp.concatenate(p_cols, axis=1)
    wb = w0 // TQ
    o_win = _dot(vwt_ref[0, wb], p_w[0:TQ])
    for b in range(1, WIN_SPAN // TQ):
        o_win = o_win + _dot(vwt_ref[0, wb + b], p_w[b * TQ:(b + 1) * TQ])
    owin_ref[...] = o_win

    def compress_and_select(n_rows):
        n_blk = n_rows // CMP_PER_SEL
        rel_c = CMP_STRIDE * rowt_ref[0:n_rows, :] + (CMP_LEN - 1) - q0.astype(F32)
        ok_c = rel_c <= lax.broadcasted_iota(jnp.int32, (1, TQ), 1).astype(F32)
        s_c = _dot_nt(kc_ref[0:n_rows, :], q)
        p_sum = jnp.zeros((n_rows, TQ), F32)
        p_cols = []
        for r in range(GQA_GROUP):
            s = jnp.where(ok_c, s_c[:, heads[r]] + slopes[r] * rel_c, NEG_INF)
            m = jnp.max(s, axis=0, keepdims=True)
            e = jnp.exp2(s - m)
            l = jnp.sum(e, axis=0, keepdims=True)
            p = e * jnp.where(m > 0.5 * NEG_INF, 1.0 / l, 0.0)
            p_sum = p_sum + p
            p_cols.append(p.astype(BF16))
        ocmp_ref[...] = _dot(vct_ref[0, :, 0:n_rows], jnp.concatenate(p_cols, axis=1))

        psum_ref[0:8, :] = jnp.zeros((8, TQ), F32)
        psum_ref[8:8 + n_rows, :] = p_sum
        every4 = lambda start: psum_ref[pl.ds(8 + start, n_blk, stride=CMP_PER_SEL), :]
        imp = every4(0) + every4(1) + every4(2) + 0.5 * every4(3) + 0.5 * every4(-1)
        blk = lax.broadcasted_iota(jnp.int32, (n_blk, TQ), 0)
        cur = lax.shift_right_logical(t_lane, SEL_BLOCK.bit_length() - 1)
        causal = blk <= cur
        forced = jnp.where(blk == 0, 1.0, 0.0) + jnp.where(blk == cur, 1.0, 0.0) \
            + jnp.where(blk == cur - 1, 1.0, 0.0)
        val = jnp.where(causal, imp + jnp.where(forced > 0.0, FORCE_BONUS, 0.0), -1.0)

        def pick(_, v):
            blk_f = rowt_ref[0:n_blk, :]
            mx = jnp.max(v, axis=0, keepdims=True)
            idx = jnp.min(jnp.where(v == mx, blk_f, float(n_blk)), axis=0, keepdims=True)
            return jnp.where(blk_f == idx, -2.0, v)

        picked = lax.fori_loop(0, min(N_SEL, n_blk), pick, val)
        madd_ref[0:n_blk, :] = jnp.where(causal, jnp.where(picked < -1.5, 0.0, NEG_INF), NEG_INF)
        for c in range(n_blk // BLOCKS_PER_CHUNK):
            rows = madd_ref[c * BLOCKS_PER_CHUNK:(c + 1) * BLOCKS_PER_CHUNK, :]
            flag_ref[c] = (jnp.max(rows) > 0.5 * NEG_INF).astype(jnp.int32)

    rows_per_bucket = nc // CMP_BUCKETS
    bucket = ((q0 + TQ) // CMP_STRIDE - 1) // rows_per_bucket
    for b in range(CMP_BUCKETS):
        pl.when(bucket == b)(functools.partial(compress_and_select, (b + 1) * rows_per_bucket))
    madd_ref[nb:nb + BLOCKS_PER_CHUNK, :] = jnp.full((BLOCKS_PER_CHUNK, TQ), NEG_INF, F32)

    key_row = rowt_ref[0:SEL_CHUNK, :]
    for r in range(GQA_GROUP):
        bias_ref[r] = slopes[r] * key_row
    m_ref[...] = jnp.full(m_ref.shape, NEG_INF, F32)
    acc_ref[...] = jnp.zeros(acc_ref.shape, F32)

    def key_chunk(c):
        return jnp.minimum(c, n_chunks - 1)

    def scores(chunks):
        keys = [ks_ref[pl.ds(pl.multiple_of(key_chunk(c) * SEL_CHUNK, SEL_CHUNK), SEL_CHUNK), :]
                for c in chunks]
        s = _dot_nt(jnp.concatenate(keys, axis=0) if len(keys) > 1 else keys[0], q)
        return [s[k * SEL_CHUNK:(k + 1) * SEL_CHUNK] for k in range(len(chunks))]

    def softmax_pv(chunks, s, diagonal):
        madds, rel0 = [], []
        for c in chunks:
            k0 = key_chunk(c) * SEL_CHUNK
            madd = jnp.concatenate(
                [jnp.broadcast_to(madd_ref[pl.ds(c * BLOCKS_PER_CHUNK + b, 1), :], (SEL_BLOCK, TQ))
                 for b in range(BLOCKS_PER_CHUNK)], axis=0)
            if diagonal:
                pos = k0 + lax.broadcasted_iota(jnp.int32, (SEL_CHUNK, TQ), 0)
                madd = jnp.where(pos <= t_lane, madd, NEG_INF)
            madds.append(madd)
            rel0.append((k0 - q0).astype(F32))
        p_cols = [[] for _ in chunks]
        alphas = []
        for r in range(GQA_GROUP):
            shifts = [slopes[r] * x for x in rel0]
            us = [sk[:, heads[r]] + bias_ref[r] + mk for sk, mk in zip(s, madds)]
            m_old = m_ref[:, heads[r]]
            m_new = m_old
            for u, sh in zip(us, shifts):
                m_new = jnp.maximum(m_new, jnp.max(u, axis=0, keepdims=True) + sh)
            alphas.append(jnp.exp2(m_old - m_new))
            for k, (u, sh) in enumerate(zip(us, shifts)):
                p_cols[k].append(jnp.exp2((u - (m_new - sh)).astype(BF16)))
            m_ref[:, heads[r]] = m_new
        pv = _dot(vst_ref[0, key_chunk(chunks[0])], jnp.concatenate(p_cols[0], axis=1))
        for k in range(1, len(chunks)):
            pv = pv + _dot(vst_ref[0, key_chunk(chunks[k])], jnp.concatenate(p_cols[k], axis=1))
        acc_ref[...] = jnp.concatenate(alphas, axis=1) * acc_ref[...] + pv

    c_diag = q0 // SEL_CHUNK

    def compact(c, n):
        list_ref[n] = c
        return n + flag_ref[c]

    n_active = lax.fori_loop(0, c_diag, compact, jnp.int32(0))
    for k in range(SEL_GROUP - 1):
        list_ref[n_active + k] = n_chunks

    def group_body(p, carry):
        chunks = [list_ref[p * SEL_GROUP + k] for k in range(SEL_GROUP)]
        softmax_pv(chunks, scores(chunks), diagonal=False)
        return carry

    lax.fori_loop(0, (n_active + SEL_GROUP - 1) // SEL_GROUP, group_body, 0)
    softmax_pv([c_diag], scores([c_diag]), diagonal=True)

    for r in range(GQA_GROUP):
        gc = gate_ref[0, 3 * r + 0:3 * r + 1, :]
        gs = gate_ref[0, 3 * r + 1:3 * r + 2, :]
        gw = gate_ref[0, 3 * r + 2:3 * r + 3, :]
        acc = acc_ref[:, heads[r]]
        win = owin_ref[:, heads[r]]
        out_t = (gc * ocmp_ref[:, heads[r]]
                 + (gs / acc[HEAD_DIM:HEAD_DIM + 1]) * acc
                 + (gw / win[HEAD_DIM:HEAD_DIM + 1]) * win)
        o_ref[:, r * LANES:(r + 1) * LANES] = out_t.T.astype(BF16)


def _attention(qpad, gates_t, kcmp, vcmp_t, ksel, vsel_t, kwin, vwin_t):
    s = qpad.shape[0]
    nb = s // SEL_BLOCK
    nc = CMP_PER_SEL * nb
    gw = GQA_GROUP * LANES
    n_rows = max(nc, WIN_SPAN)
    row_tile = jnp.asarray(np.broadcast_to(np.arange(n_rows)[:, None], (n_rows, TQ))
                           .astype(np.float32))
    per_group = lambda a: pl.BlockSpec((1,) + a.shape[1:], lambda g, i: (g,) + (0,) * (a.ndim - 1),
                                       pipeline_mode=pl.Buffered(1))
    return pl.pallas_call(
        functools.partial(_attn_kernel, nb=nb),
        grid=(N_KV, s // TQ),
        in_specs=[pl.BlockSpec((TQ, gw), lambda g, i: (i, g)),
                  pl.BlockSpec((1, GATE_ROWS, TQ), lambda g, i: (g, 0, i)),
                  _const_spec(row_tile.shape),
                  _const_spec(kcmp.shape), per_group(vcmp_t),
                  _const_spec(ksel.shape), per_group(vsel_t),
                  _const_spec(kwin.shape), per_group(vwin_t)],
        out_specs=pl.BlockSpec((TQ, gw), lambda g, i: (i, g)),
        out_shape=jax.ShapeDtypeStruct((s, N_HEADS * LANES), BF16),
        scratch_shapes=[pltpu.VMEM((8 + nc, TQ), F32),
                        pltpu.VMEM((nb + BLOCKS_PER_CHUNK, TQ), F32),
                        pltpu.VMEM((GQA_GROUP, SEL_CHUNK, TQ), F32),
                        pltpu.VMEM((1, GQA_GROUP * TQ), F32),
                        pltpu.VMEM((LANES, GQA_GROUP * TQ), F32),
                        pltpu.VMEM((LANES, GQA_GROUP * TQ), F32),
                        pltpu.VMEM((LANES, GQA_GROUP * TQ), F32),
                        pltpu.SMEM((nb // BLOCKS_PER_CHUNK,), jnp.int32),
                        pltpu.SMEM((nb // BLOCKS_PER_CHUNK + SEL_GROUP,), jnp.int32)],
        compiler_params=pltpu.CompilerParams(dimension_semantics=("arbitrary", "arbitrary"),
                                             vmem_limit_bytes=VMEM_LIMIT_BYTES),
        name="attn",
    )(qpad, gates_t, row_tile, kcmp, vcmp_t, ksel, vsel_t, kwin, vwin_t)


def _out_proj_kernel(pool_ref, attn_ref, wp_ref, wa_ref, x_ref, ga_ref, g_ref, sc_ref, sh_ref,
                     x1_ref, h2_ref):
    mix = _dot(pool_ref[...], wp_ref[...]) + _dot(attn_ref[...], wa_ref[...])
    x1 = x_ref[...] + ga_ref[...] * mix
    x1_ref[...] = x1
    h2_ref[...] = _rms_modulate(x1, g_ref[...], sc_ref[...], sh_ref[...]).astype(BF16)


def _out_proj(pool_out, attn_pad, w_pool_rows, w_attn_rows, x, ga1, g2, sc2, sh2, tm):
    s = x.shape[0]
    row = lambda w: pl.BlockSpec((tm, w), lambda i: (i, 0))
    vec = lambda w: pl.BlockSpec((1, w), lambda i: (0, 0))
    return pl.pallas_call(
        _out_proj_kernel,
        grid=(s // tm,),
        in_specs=[row(POOL_WIDTH), row(_QPAD), _const_spec(w_pool_rows.shape),
                  _const_spec(w_attn_rows.shape), row(D_MODEL), vec(D_MODEL), vec(D_MODEL),
                  vec(D_MODEL), vec(D_MODEL)],
        out_specs=[row(D_MODEL), row(D_MODEL)],
        out_shape=[jax.ShapeDtypeStruct((s, D_MODEL), F32),
                   jax.ShapeDtypeStruct((s, D_MODEL), BF16)],
        compiler_params=pltpu.CompilerParams(vmem_limit_bytes=VMEM_LIMIT_BYTES),
        name="out_proj",
    )(pool_out, attn_pad, w_pool_rows, w_attn_rows, x, ga1, g2, sc2, sh2)


def _mlp_kernel(h_ref, w1_ref, w2_ref, x_ref, ga_ref, o_ref, *, ff_chunk):
    h = h_ref[...]
    acc = jnp.zeros(x_ref.shape, F32)
    for c in range(D_FF // ff_chunk):
        a = _dot(h, w1_ref[:, c * ff_chunk:(c + 1) * ff_chunk])
        a = jnp.square(jnp.maximum(a, 0.0)).astype(BF16)
        acc = acc + _dot(a, w2_ref[c * ff_chunk:(c + 1) * ff_chunk, :])
    o_ref[...] = x_ref[...] + ga_ref[...] * acc


def _mlp(h2, w1, w2, x1, ga2, tm):
    s = x1.shape[0]
    row = lambda w: pl.BlockSpec((tm, w), lambda i: (i, 0))
    return pl.pallas_call(
        functools.partial(_mlp_kernel, ff_chunk=1024),
        grid=(s // tm,),
        in_specs=[row(D_MODEL), _const_spec(w1.shape), _const_spec(w2.shape), row(D_MODEL),
                  pl.BlockSpec((1, D_MODEL), lambda i: (0, 0))],
        out_specs=row(D_MODEL),
        out_shape=jax.ShapeDtypeStruct((s, D_MODEL), F32),
        compiler_params=pltpu.CompilerParams(vmem_limit_bytes=VMEM_LIMIT_BYTES),
        name="mlp",
    )(h2, w1, w2, x1, ga2)


def _pad_in_proj_weight(w_in):
    zeros = lambda n: jnp.zeros((D_MODEL, n), w_in.dtype)
    src_q = POOL_WIDTH
    cols = [w_in[:, :POOL_WIDTH]]
    for hh in range(N_HEADS):
        wq = w_in[:, src_q + hh * HEAD_DIM:src_q + (hh + 1) * HEAD_DIM]
        cols += [wq, zeros(HEAD_DIM)] if hh // GQA_GROUP == 0 else [zeros(HEAD_DIM), wq]
    src_kv = src_q + N_HEADS * HEAD_DIM
    cols.append(w_in[:, src_kv:src_kv + 6 * KV_WIDTH])
    src_g = src_kv + 6 * KV_WIDTH
    per_group = GQA_GROUP * N_BRANCH
    for gg in range(N_KV):
        cols += [w_in[:, src_g + gg * per_group:src_g + (gg + 1) * per_group],
                 zeros(LANES - per_group)]
    return jnp.concatenate(cols, axis=1).astype(BF16)


def _pad_out_proj_weight(w_out):
    wa = w_out[POOL_WIDTH:].reshape(N_HEADS, HEAD_DIM, D_MODEL)
    wa = jnp.pad(wa, ((0, 0), (0, LANES - HEAD_DIM), (0, 0))).reshape(N_HEADS * LANES, D_MODEL)
    return w_out[:POOL_WIDTH].astype(BF16), wa.astype(BF16)


def _group_lanes(a):
    return a.transpose(1, 0, 2).reshape(a.shape[1], KV_WIDTH)


def kernel(x, c, w_ada, b_ada, norm1_g, norm2_g, w_in, w_pool, pool_scale, q_gain, kc_gain,
           ks_gain, kw_gain, cmp_pos_k, cmp_w1_k, cmp_b1_k, cmp_w2_k, cmp_b2_k, cmp_pos_v,
           cmp_w1_v, cmp_b1_v, cmp_w2_v, cmp_b2_v, w_out, w_ff1, w_ff2):
    batch, s, _ = x.shape
    assert batch == 1 and w_ada.shape[0] == 1
    assert s % SEL_CHUNK == 0 and s >= WIN_SPAN and (s // SEL_BLOCK) & (s // SEL_BLOCK - 1) == 0
    tm = min(s, 512)
    x2 = x[0]

    mod = _ada(jnp.broadcast_to(c, (8, D_MODEL)), w_ada[0], b_ada)[0:1]
    sh1, sc1, ga1, sh2, sc2, ga2 = [mod[:, k * D_MODEL:(k + 1) * D_MODEL] for k in range(6)]

    pair = lambda gain: jnp.tile(gain, (1, 2))
    (pool_out, qpad, kc, vc, ksel, vsel_t, kwin, vwin_t, gates_t) = _in_proj(
        x2, norm1_g, sc1, sh1, _pad_in_proj_weight(w_in[0]), w_pool[0].astype(BF16), pool_scale,
        pair(q_gain), pair(ks_gain), pair(kw_gain), tm)

    chunks = lambda a: a.reshape(N_KV, s // CMP_STRIDE, CMP_STRIDE * HEAD_DIM)
    kcmp = _compress(chunks(kc), cmp_pos_k.reshape(1, -1), cmp_w1_k[0].astype(BF16), cmp_b1_k,
                     cmp_w2_k[0].astype(BF16), cmp_b2_k, kc_gain, True)
    vcmp = _compress(chunks(vc), cmp_pos_v.reshape(1, -1), cmp_w1_v[0].astype(BF16), cmp_b1_v,
                     cmp_w2_v[0].astype(BF16), cmp_b2_v, kc_gain, False)

    vcmp_t = jnp.pad(vcmp.transpose(0, 2, 1), ((0, 0), (0, LANES - HEAD_DIM), (0, 0)))
    attn_pad = _attention(qpad, gates_t, _group_lanes(kcmp), vcmp_t, ksel, vsel_t, kwin, vwin_t)

    w_pool_rows, w_attn_rows = _pad_out_proj_weight(w_out[0])
    x1, h2 = _out_proj(pool_out, attn_pad, w_pool_rows, w_attn_rows, x2, ga1, norm2_g, sc2, sh2, tm)
    out = _mlp(h2, w_ff1[0].astype(BF16), w_ff2[0].astype(BF16), x1, ga2, tm)
    return out[None]
```

```python
import functools

import jax
import jax.numpy as jnp
import numpy as np
from jax import lax
from jax.experimental import pallas as pl
from jax.experimental.pallas import tpu as pltpu

F32 = jnp.float32
BF16 = jnp.bfloat16

LANES = 128
VMEM_LIMIT_BYTES = 56 * 1024 * 1024

D_MODEL = 1024
POOL_WIDTH = 512
POOL_WINDOWS = (2, 4, 8, 16)
POOL_GROUP = POOL_WIDTH // len(POOL_WINDOWS)
POOL_HALO = 16
HEAD_DIM = 64
N_HEADS = 8
N_KV = 2
GQA_GROUP = N_HEADS // N_KV
KV_WIDTH = N_KV * HEAD_DIM
N_BRANCH = 3
CMP_LEN = 32
CMP_STRIDE = 16
CMP_HIDDEN = 4 * HEAD_DIM
SEL_BLOCK = 64
N_SEL = 16
WINDOW = 512
D_FF = 4 * D_MODEL
NEG_INF = -1e30
FORCE_BONUS = 1e4
EPS = 1e-6
LOG2E = 1.4426950408889634

TQ = 256
SEL_CHUNK = 256
BLOCKS_PER_CHUNK = SEL_CHUNK // SEL_BLOCK
SEL_GROUP = 4
CMP_BUCKETS = 4
WIN_SPAN = WINDOW + TQ
CMP_PER_SEL = SEL_BLOCK // CMP_STRIDE
GATE_ROWS = 16

_QPAD = N_HEADS * LANES
_OFF_U = 0
_OFF_Q = POOL_WIDTH
_OFF_KC = _OFF_Q + _QPAD
_OFF_VC = _OFF_KC + KV_WIDTH
_OFF_KS = _OFF_VC + KV_WIDTH
_OFF_VS = _OFF_KS + KV_WIDTH
_OFF_KW = _OFF_VS + KV_WIDTH
_OFF_VW = _OFF_KW + KV_WIDTH
_OFF_G = _OFF_VW + KV_WIDTH
_IN_PAD = _OFF_G + N_KV * LANES


def _dot(a, b):
    return jnp.dot(a, b, preferred_element_type=F32)


def _dot_nt(a, b):
    return lax.dot_general(a, b, (((1,), (1,)), ((), ())), preferred_element_type=F32)


def _const_spec(shape):
    nd = len(shape)
    return pl.BlockSpec(shape, lambda *_: (0,) * nd, pipeline_mode=pl.Buffered(1))


def _ada_kernel(c_ref, w_ref, b_ref, o_ref):
    o_ref[...] = jnp.dot(c_ref[...], w_ref[...], preferred_element_type=F32,
                         precision=lax.Precision.HIGHEST) + b_ref[...]


def _ada(c8, w, b):
    n = w.shape[1]
    bn = 1024
    return pl.pallas_call(
        _ada_kernel,
        grid=(n // bn,),
        in_specs=[pl.BlockSpec((8, D_MODEL), lambda j: (0, 0)),
                  pl.BlockSpec((D_MODEL, bn), lambda j: (0, j)),
                  pl.BlockSpec((1, bn), lambda j: (0, j))],
        out_specs=pl.BlockSpec((8, bn), lambda j: (0, j)),
        out_shape=jax.ShapeDtypeStruct((8, n), F32),
        name="ada",
    )(c8, w, b)


def _rms_modulate(x, g, sc, sh):
    ms = jnp.mean(x * x, axis=-1, keepdims=True)
    return (x * lax.rsqrt(ms + EPS)) * (g * (1.0 + sc)) + sh


def _head_norm_pair(x, gain2):
    lane = lax.broadcasted_iota(jnp.int32, x.shape, 1)
    lo = lane < HEAD_DIM
    sq = x * x
    s_lo = jnp.sum(jnp.where(lo, sq, 0.0), axis=-1, keepdims=True)
    s_hi = jnp.sum(jnp.where(lo, 0.0, sq), axis=-1, keepdims=True)
    ms = jnp.where(lo, s_lo, s_hi) * (1.0 / HEAD_DIM)
    return x * lax.rsqrt(ms + EPS) * gain2


def _in_proj_kernel(x_ref, g_ref, sc_ref, sh_ref, w_ref, wpool_ref, pscale_ref, qg_ref, ksg_ref,
                    kwg_ref, pool_ref, q_ref, kc_ref, vc_ref, ks_ref, vs_ref, kw_ref, vw_ref,
                    gate_ref, ebuf_ref, *, tm):
    i = pl.program_id(0)
    h = _rms_modulate(x_ref[...], g_ref[...], sc_ref[...], sh_ref[...]).astype(BF16)
    proj = _dot(h, w_ref[...])

    for hh in range(N_HEADS):
        qs = proj[:, _OFF_Q + hh * LANES:_OFF_Q + (hh + 1) * LANES]
        ms = jnp.sum(qs * qs, axis=-1, keepdims=True) * (1.0 / HEAD_DIM)
        qn = qs * lax.rsqrt(ms + EPS) * qg_ref[...]
        q_ref[:, hh * LANES:(hh + 1) * LANES] = (qn * (HEAD_DIM ** -0.5 * LOG2E)).astype(BF16)

    for gg in range(N_KV):
        kc_ref[gg] = proj[:, _OFF_KC + gg * HEAD_DIM:_OFF_KC + (gg + 1) * HEAD_DIM]
        vc_ref[gg] = proj[:, _OFF_VC + gg * HEAD_DIM:_OFF_VC + (gg + 1) * HEAD_DIM]
    ks_ref[...] = _head_norm_pair(proj[:, _OFF_KS:_OFF_KS + KV_WIDTH], ksg_ref[...]).astype(BF16)
    kw_ref[...] = _head_norm_pair(proj[:, _OFF_KW:_OFF_KW + KV_WIDTH], kwg_ref[...]).astype(BF16)
    def value_tiles(ref, off, chunk):
        ones_row = jnp.where(lax.broadcasted_iota(jnp.int32, (HEAD_DIM, chunk), 0) == 0, 1.0, 0.0)
        for k in range(tm // chunk):
            vt = proj[k * chunk:(k + 1) * chunk, off:off + KV_WIDTH].T
            for gg in range(N_KV):
                ref[gg, k] = jnp.concatenate(
                    [vt[gg * HEAD_DIM:(gg + 1) * HEAD_DIM], ones_row], axis=0).astype(BF16)

    value_tiles(vs_ref, _OFF_VS, SEL_CHUNK)
    value_tiles(vw_ref, _OFF_VW, TQ)
    gates_t = jax.nn.sigmoid(proj[:, _OFF_G:_OFF_G + N_KV * LANES]).T
    for gg in range(N_KV):
        gate_ref[gg] = gates_t[gg * LANES:gg * LANES + GATE_ROWS]

    @pl.when(i == 0)
    def _():
        ebuf_ref[0:POOL_HALO, :] = jnp.zeros((POOL_HALO, POOL_WIDTH), F32)

    u = proj[:, _OFF_U:_OFF_U + POOL_WIDTH]
    ebuf_ref[POOL_HALO:POOL_HALO + tm, :] = u
    t1 = i * tm + lax.broadcasted_iota(jnp.int32, (tm, POOL_GROUP), 0) + 1
    for gi, w in enumerate(POOL_WINDOWS):
        c0 = gi * POOL_GROUP
        win = u[:, c0:c0 + POOL_GROUP]
        for k in range(1, w):
            win = win + ebuf_ref[POOL_HALO - k:POOL_HALO - k + tm, c0:c0 + POOL_GROUP]
        cnt = jnp.minimum(t1, w).astype(F32)
        pooled = win / cnt - u[:, c0:c0 + POOL_GROUP]
        y = _dot(pooled.astype(BF16), wpool_ref[gi])
        pool_ref[:, c0:c0 + POOL_GROUP] = (y * pscale_ref[:, c0:c0 + POOL_GROUP]).astype(BF16)
    ebuf_ref[0:POOL_HALO, :] = ebuf_ref[tm:tm + POOL_HALO, :]


def _in_proj(x, g1, sc1, sh1, w_in_p, w_pool, pool_scale, qg, ksg, kwg, tm):
    s = x.shape[0]
    row = lambda w: pl.BlockSpec((tm, w), lambda i: (i, 0))
    vec = lambda w: pl.BlockSpec((1, w), lambda i: (0, 0))
    per_head = pl.BlockSpec((N_KV, tm, HEAD_DIM), lambda i: (0, i, 0))
    chunks_t = lambda ch: pl.BlockSpec((N_KV, tm // ch, LANES, ch), lambda i: (0, i, 0, 0))
    out_shape = [
        jax.ShapeDtypeStruct((s, POOL_WIDTH), BF16),
        jax.ShapeDtypeStruct((s, _QPAD), BF16),
        jax.ShapeDtypeStruct((N_KV, s, HEAD_DIM), F32),
        jax.ShapeDtypeStruct((N_KV, s, HEAD_DIM), F32),
        jax.ShapeDtypeStruct((s, KV_WIDTH), BF16),
        jax.ShapeDtypeStruct((N_KV, s // SEL_CHUNK, LANES, SEL_CHUNK), BF16),
        jax.ShapeDtypeStruct((s, KV_WIDTH), BF16),
        jax.ShapeDtypeStruct((N_KV, s // TQ, LANES, TQ), BF16),
        jax.ShapeDtypeStruct((N_KV, GATE_ROWS, s), F32),
    ]
    return pl.pallas_call(
        functools.partial(_in_proj_kernel, tm=tm),
        grid=(s // tm,),
        in_specs=[row(D_MODEL), vec(D_MODEL), vec(D_MODEL), vec(D_MODEL),
                  _const_spec(w_in_p.shape), _const_spec(w_pool.shape), vec(POOL_WIDTH),
                  vec(LANES), vec(LANES), vec(LANES)],
        out_specs=[row(POOL_WIDTH), row(_QPAD), per_head, per_head, row(KV_WIDTH),
                   chunks_t(SEL_CHUNK), row(KV_WIDTH), chunks_t(TQ),
                   pl.BlockSpec((N_KV, GATE_ROWS, tm), lambda i: (0, 0, i))],
        out_shape=out_shape,
        scratch_shapes=[pltpu.VMEM((tm + POOL_HALO, POOL_WIDTH), F32)],
        compiler_params=pltpu.CompilerParams(dimension_semantics=("arbitrary",),
                                             vmem_limit_bytes=VMEM_LIMIT_BYTES),
        name="in_proj",
    )(x, g1, sc1, sh1, w_in_p, w_pool, pool_scale, qg, ksg, kwg)


def _compress_kernel(c_ref, pos_ref, w1_ref, b1_ref, w2_ref, b2_ref, gain_ref, o_ref, *,
                     normalize):
    half = CMP_STRIDE * HEAD_DIM
    n_rows = c_ref.shape[1]
    c = c_ref[0]
    first = _dot((c + pos_ref[:, 0:half]).astype(BF16), w1_ref[0:half, :])
    second = _dot((c + pos_ref[:, half:2 * half]).astype(BF16), w1_ref[half:2 * half, :])
    hid = jax.nn.gelu(first + pltpu.roll(second, n_rows - 1, axis=0) + b1_ref[...])
    y = _dot(hid.astype(BF16), w2_ref[...]) + b2_ref[...]
    if normalize:
        ms = jnp.mean(y * y, axis=-1, keepdims=True)
        y = y * lax.rsqrt(ms + EPS) * gain_ref[...]
    o_ref[0] = y.astype(BF16)


def _compress(chunks, pos, w1, b1, w2, b2, gain, normalize):
    _, n_chunks, width = chunks.shape
    vec = lambda w: pl.BlockSpec((1, w), lambda g: (0, 0))
    return pl.pallas_call(
        functools.partial(_compress_kernel, normalize=normalize),
        grid=(N_KV,),
        in_specs=[pl.BlockSpec((1, n_chunks, width), lambda g: (g, 0, 0)),
                  vec(2 * width), _const_spec(w1.shape), vec(CMP_HIDDEN), _const_spec(w2.shape),
                  vec(HEAD_DIM), vec(HEAD_DIM)],
        out_specs=pl.BlockSpec((1, n_chunks, HEAD_DIM), lambda g: (g, 0, 0)),
        out_shape=jax.ShapeDtypeStruct((N_KV, n_chunks, HEAD_DIM), BF16),
        compiler_params=pltpu.CompilerParams(vmem_limit_bytes=VMEM_LIMIT_BYTES),
        name="compress",
    )(chunks, pos, w1, b1, w2, b2, gain)


def _attn_kernel(q_ref, gate_ref, rowt_ref, kc_ref, vct_ref, ks_ref, vst_ref, kw_ref, vwt_ref,
                 o_ref, psum_ref, madd_ref, bias_ref, m_ref, acc_ref, ocmp_ref, owin_ref,
                 flag_ref, list_ref, *, nb):
    g = pl.program_id(0)
    i = pl.program_id(1)
    q0 = i * TQ
    nc = CMP_PER_SEL * nb
    n_chunks = nb // BLOCKS_PER_CHUNK
    gslope = jnp.where(g == 0, LOG2E, LOG2E * 2.0 ** -GQA_GROUP).astype(F32)
    slopes = [gslope * (2.0 ** -(r + 1)) for r in range(GQA_GROUP)]
    heads = [slice(r * TQ, (r + 1) * TQ) for r in range(GQA_GROUP)]

    q = jnp.concatenate([q_ref[:, r * LANES:(r + 1) * LANES] for r in range(GQA_GROUP)], axis=0)
    t_lane = q0 + lax.broadcasted_iota(jnp.int32, (1, TQ), 1)

    w0 = pl.multiple_of(jnp.maximum(q0 - WINDOW, 0), TQ)
    s_w = _dot_nt(kw_ref[pl.ds(w0, WIN_SPAN), :], q)
    row_w = rowt_ref[0:WIN_SPAN, :]
    lane_w = lax.broadcasted_iota(jnp.int32, (WIN_SPAN, TQ), 1).astype(F32)
    dq = (q0 - w0).astype(F32)
    d_w = lane_w - row_w + dq
    ok_w = jnp.abs(d_w - (WINDOW - 1) * 0.5) < WINDOW * 0.5
    rel_w = row_w - dq
    p_cols = []
    for r in range(GQA_GROUP):
        s = jnp.where(ok_w, s_w[:, heads[r]] + slopes[r] * rel_w, NEG_INF)
        m = jnp.max(s, axis=0, keepdims=True)
        p_cols.append(jnp.exp2((s - m).astype(BF16)))
    p_w = jnp.concatenate(p_cols, axis=1)
    wb = w0 // TQ
    o_win = _dot(vwt_ref[0, wb], p_w[0:TQ])
    for b in range(1, WIN_SPAN // TQ):
        o_win = o_win + _dot(vwt_ref[0, wb + b], p_w[b * TQ:(b + 1) * TQ])
    owin_ref[...] = o_win

    def compress_and_select(n_rows):
        n_blk = n_rows // CMP_PER_SEL
        rel_c = CMP_STRIDE * rowt_ref[0:n_rows, :] + (CMP_LEN - 1) - q0.astype(F32)
        ok_c = rel_c <= lax.broadcasted_iota(jnp.int32, (1, TQ), 1).astype(F32)
        s_c = _dot_nt(kc_ref[0:n_rows, :], q)
        p_sum = jnp.zeros((n_rows, TQ), F32)
        p_cols = []
        for r in range(GQA_GROUP):
            s = jnp.where(ok_c, s_c[:, heads[r]] + slopes[r] * rel_c, NEG_INF)
            m = jnp.max(s, axis=0, keepdims=True)
            e = jnp.exp2(s - m)
            l = jnp.sum(e, axis=0, keepdims=True)
            p = e * jnp.where(m > 0.5 * NEG_INF, 1.0 / l, 0.0)
            p_sum = p_sum + p
            p_cols.append(p.astype(BF16))
        ocmp_ref[...] = _dot(vct_ref[0, :, 0:n_rows], jnp.concatenate(p_cols, axis=1))

        for h in range(TQ // LANES):
            psum_ref[h, 0:8, :] = jnp.zeros((8, LANES), F32)
            psum_ref[h, 8:8 + n_rows, :] = p_sum[:, h * LANES:(h + 1) * LANES]

        def every4(start):
            parts = [psum_ref[h, pl.ds(8 + start, n_blk, stride=CMP_PER_SEL), :]
                     for h in range(TQ // LANES)]
            return parts[0] if len(parts) == 1 else jnp.concatenate(parts, axis=1)

        imp = every4(0) + every4(1) + every4(2) + 0.5 * every4(3) + 0.5 * every4(-1)
        blk = lax.broadcasted_iota(jnp.int32, (n_blk, TQ), 0)
        cur = lax.shift_right_logical(t_lane, SEL_BLOCK.bit_length() - 1)
        causal = blk <= cur
        forced = jnp.where(blk == 0, 1.0, 0.0) + jnp.where(blk == cur, 1.0, 0.0) \
            + jnp.where(blk == cur - 1, 1.0, 0.0)
        val = jnp.where(causal, imp + jnp.where(forced > 0.0, FORCE_BONUS, 0.0), -1.0)

        def pick(_, v):
            blk_f = rowt_ref[0:n_blk, :]
            mx = jnp.max(v, axis=0, keepdims=True)
            idx = jnp.min(jnp.where(v == mx, blk_f, float(n_blk)), axis=0, keepdims=True)
            return jnp.where(blk_f == idx, -2.0, v)

        picked = lax.fori_loop(0, min(N_SEL, n_blk), pick, val)
        madd_ref[0:n_blk, :] = jnp.where(causal, jnp.where(picked < -1.5, 0.0, NEG_INF), NEG_INF)
        for c in range(n_blk // BLOCKS_PER_CHUNK):
            rows = madd_ref[c * BLOCKS_PER_CHUNK:(c + 1) * BLOCKS_PER_CHUNK, :]
            flag_ref[c] = (jnp.max(rows) > 0.5 * NEG_INF).astype(jnp.int32)

    rows_per_bucket = nc // CMP_BUCKETS
    bucket = ((q0 + TQ) // CMP_STRIDE - 1) // rows_per_bucket
    for b in range(CMP_BUCKETS):
        pl.when(bucket == b)(functools.partial(compress_and_select, (b + 1) * rows_per_bucket))
    madd_ref[nb:nb + BLOCKS_PER_CHUNK, :] = jnp.full((BLOCKS_PER_CHUNK, TQ), NEG_INF, F32)

    key_row = rowt_ref[0:SEL_CHUNK, :]
    for r in range(GQA_GROUP):
        bias_ref[r] = slopes[r] * key_row
    m_ref[...] = jnp.full(m_ref.shape, NEG_INF, F32)
    acc_ref[...] = jnp.zeros(acc_ref.shape, F32)

    def key_chunk(c):
        return jnp.minimum(c, n_chunks - 1)

    def scores(chunks):
        keys = [ks_ref[pl.ds(pl.multiple_of(key_chunk(c) * SEL_CHUNK, SEL_CHUNK), SEL_CHUNK), :]
                for c in chunks]
        s = _dot_nt(jnp.concatenate(keys, axis=0) if len(keys) > 1 else keys[0], q)
        return [s[k * SEL_CHUNK:(k + 1) * SEL_CHUNK] for k in range(len(chunks))]

    def softmax_pv(chunks, s, diagonal):
        madds, rel0 = [], []
        for c in chunks:
            k0 = key_chunk(c) * SEL_CHUNK
            madd = jnp.concatenate(
                [jnp.broadcast_to(madd_ref[pl.ds(c * BLOCKS_PER_CHUNK + b, 1), :], (SEL_BLOCK, TQ))
                 for b in range(BLOCKS_PER_CHUNK)], axis=0)
            if diagonal:
                pos = k0 + lax.broadcasted_iota(jnp.int32, (SEL_CHUNK, TQ), 0)
                madd = jnp.where(pos <= t_lane, madd, NEG_INF)
            madds.append(madd)
            rel0.append((k0 - q0).astype(F32))
        p_cols = [[] for _ in chunks]
        alphas = []
        for r in range(GQA_GROUP):
            shifts = [slopes[r] * x for x in rel0]
            us = [sk[:, heads[r]] + bias_ref[r] + mk for sk, mk in zip(s, madds)]
            m_old = m_ref[:, heads[r]]
            m_new = m_old
            for u, sh in zip(us, shifts):
                m_new = jnp.maximum(m_new, jnp.max(u, axis=0, keepdims=True) + sh)
            alphas.append(jnp.exp2(m_old - m_new))
            for k, (u, sh) in enumerate(zip(us, shifts)):
                p_cols[k].append(jnp.exp2((u - (m_new - sh)).astype(BF16)))
            m_ref[:, heads[r]] = m_new
        pv = _dot(vst_ref[0, key_chunk(chunks[0])], jnp.concatenate(p_cols[0], axis=1))
        for k in range(1, len(chunks)):
            pv = pv + _dot(vst_ref[0, key_chunk(chunks[k])], jnp.concatenate(p_cols[k], axis=1))
        acc_ref[...] = jnp.concatenate(alphas, axis=1) * acc_ref[...] + pv

    c_diag = q0 // SEL_CHUNK

    def compact(c, n):
        list_ref[n] = c
        return n + flag_ref[c]

    n_active = lax.fori_loop(0, c_diag, compact, jnp.int32(0))
    for k in range(SEL_GROUP - 1):
        list_ref[n_active + k] = n_chunks

    def group_body(p, carry):
        chunks = [list_ref[p * SEL_GROUP + k] for k in range(SEL_GROUP)]
        softmax_pv(chunks, scores(chunks), diagonal=False)
        return carry

    lax.fori_loop(0, (n_active + SEL_GROUP - 1) // SEL_GROUP, group_body, 0)
    softmax_pv([c_diag], scores([c_diag]), diagonal=True)

    for r in range(GQA_GROUP):
        gc = gate_ref[0, 3 * r + 0:3 * r + 1, :]
        gs = gate_ref[0, 3 * r + 1:3 * r + 2, :]
        gw = gate_ref[0, 3 * r + 2:3 * r + 3, :]
        acc = acc_ref[:, heads[r]]
        win = owin_ref[:, heads[r]]
        out_t = (gc * ocmp_ref[:, heads[r]]
                 + (gs / acc[HEAD_DIM:HEAD_DIM + 1]) * acc
                 + (gw / win[HEAD_DIM:HEAD_DIM + 1]) * win)
        o_ref[:, r * LANES:(r + 1) * LANES] = out_t.T.astype(BF16)


def _attention(qpad, gates_t, kcmp, vcmp_t, ksel, vsel_t, kwin, vwin_t):
    s = qpad.shape[0]
    nb = s // SEL_BLOCK
    nc = CMP_PER_SEL * nb
    gw = GQA_GROUP * LANES
    n_rows = max(nc, WIN_SPAN)
    row_tile = jnp.asarray(np.broadcast_to(np.arange(n_rows)[:, None], (n_rows, TQ))
                           .astype(np.float32))
    per_group = lambda a: pl.BlockSpec((1,) + a.shape[1:], lambda g, i: (g,) + (0,) * (a.ndim - 1),
                                       pipeline_mode=pl.Buffered(1))
    return pl.pallas_call(
        functools.partial(_attn_kernel, nb=nb),
        grid=(N_KV, s // TQ),
        in_specs=[pl.BlockSpec((TQ, gw), lambda g, i: (i, g)),
                  pl.BlockSpec((1, GATE_ROWS, TQ), lambda g, i: (g, 0, i)),
                  _const_spec(row_tile.shape),
                  _const_spec(kcmp.shape), per_group(vcmp_t),
                  _const_spec(ksel.shape), per_group(vsel_t),
                  _const_spec(kwin.shape), per_group(vwin_t)],
        out_specs=pl.BlockSpec((TQ, gw), lambda g, i: (i, g)),
        out_shape=jax.ShapeDtypeStruct((s, N_HEADS * LANES), BF16),
        scratch_shapes=[pltpu.VMEM((TQ // LANES, 8 + nc, LANES), F32),
                        pltpu.VMEM((nb + BLOCKS_PER_CHUNK, TQ), F32),
                        pltpu.VMEM((GQA_GROUP, SEL_CHUNK, TQ), F32),
                        pltpu.VMEM((1, GQA_GROUP * TQ), F32),
                        pltpu.VMEM((LANES, GQA_GROUP * TQ), F32),
                        pltpu.VMEM((LANES, GQA_GROUP * TQ), F32),
                        pltpu.VMEM((LANES, GQA_GROUP * TQ), F32),
                        pltpu.SMEM((nb // BLOCKS_PER_CHUNK,), jnp.int32),
                        pltpu.SMEM((nb // BLOCKS_PER_CHUNK + SEL_GROUP,), jnp.int32)],
        compiler_params=pltpu.CompilerParams(dimension_semantics=("arbitrary", "arbitrary"),
                                             vmem_limit_bytes=VMEM_LIMIT_BYTES),
        name="attn",
    )(qpad, gates_t, row_tile, kcmp, vcmp_t, ksel, vsel_t, kwin, vwin_t)


def _out_proj_kernel(pool_ref, attn_ref, wp_ref, wa_ref, x_ref, ga_ref, g_ref, sc_ref, sh_ref,
                     x1_ref, h2_ref):
    mix = _dot(pool_ref[...], wp_ref[...]) + _dot(attn_ref[...], wa_ref[...])
    x1 = x_ref[...] + ga_ref[...] * mix
    x1_ref[...] = x1
    h2_ref[...] = _rms_modulate(x1, g_ref[...], sc_ref[...], sh_ref[...]).astype(BF16)


def _out_proj(pool_out, attn_pad, w_pool_rows, w_attn_rows, x, ga1, g2, sc2, sh2, tm):
    s = x.shape[0]
    row = lambda w: pl.BlockSpec((tm, w), lambda i: (i, 0))
    vec = lambda w: pl.BlockSpec((1, w), lambda i: (0, 0))
    return pl.pallas_call(
        _out_proj_kernel,
        grid=(s // tm,),
        in_specs=[row(POOL_WIDTH), row(_QPAD), _const_spec(w_pool_rows.shape),
                  _const_spec(w_attn_rows.shape), row(D_MODEL), vec(D_MODEL), vec(D_MODEL),
                  vec(D_MODEL), vec(D_MODEL)],
        out_specs=[row(D_MODEL), row(D_MODEL)],
        out_shape=[jax.ShapeDtypeStruct((s, D_MODEL), F32),
                   jax.ShapeDtypeStruct((s, D_MODEL), BF16)],
        compiler_params=pltpu.CompilerParams(vmem_limit_bytes=VMEM_LIMIT_BYTES),
        name="out_proj",
    )(pool_out, attn_pad, w_pool_rows, w_attn_rows, x, ga1, g2, sc2, sh2)


def _mlp_kernel(h_ref, w1_ref, w2_ref, x_ref, ga_ref, o_ref, *, ff_chunk):
    h = h_ref[...]
    acc = jnp.zeros(x_ref.shape, F32)
    for c in range(D_FF // ff_chunk):
        a = _dot(h, w1_ref[:, c * ff_chunk:(c + 1) * ff_chunk])
        a = jnp.square(jnp.maximum(a, 0.0)).astype(BF16)
        acc = acc + _dot(a, w2_ref[c * ff_chunk:(c + 1) * ff_chunk, :])
    o_ref[...] = x_ref[...] + ga_ref[...] * acc


def _mlp(h2, w1, w2, x1, ga2, tm):
    s = x1.shape[0]
    row = lambda w: pl.BlockSpec((tm, w), lambda i: (i, 0))
    return pl.pallas_call(
        functools.partial(_mlp_kernel, ff_chunk=1024),
        grid=(s // tm,),
        in_specs=[row(D_MODEL), _const_spec(w1.shape), _const_spec(w2.shape), row(D_MODEL),
                  pl.BlockSpec((1, D_MODEL), lambda i: (0, 0))],
        out_specs=row(D_MODEL),
        out_shape=jax.ShapeDtypeStruct((s, D_MODEL), F32),
        compiler_params=pltpu.CompilerParams(vmem_limit_bytes=VMEM_LIMIT_BYTES),
        name="mlp",
    )(h2, w1, w2, x1, ga2)


def _pad_in_proj_weight(w_in):
    zeros = lambda n: jnp.zeros((D_MODEL, n), w_in.dtype)
    src_q = POOL_WIDTH
    cols = [w_in[:, :POOL_WIDTH]]
    for hh in range(N_HEADS):
        wq = w_in[:, src_q + hh * HEAD_DIM:src_q + (hh + 1) * HEAD_DIM]
        cols += [wq, zeros(HEAD_DIM)] if hh // GQA_GROUP == 0 else [zeros(HEAD_DIM), wq]
    src_kv = src_q + N_HEADS * HEAD_DIM
    cols.append(w_in[:, src_kv:src_kv + 6 * KV_WIDTH])
    src_g = src_kv + 6 * KV_WIDTH
    per_group = GQA_GROUP * N_BRANCH
    for gg in range(N_KV):
        cols += [w_in[:, src_g + gg * per_group:src_g + (gg + 1) * per_group],
                 zeros(LANES - per_group)]
    return jnp.concatenate(cols, axis=1).astype(BF16)


def _pad_out_proj_weight(w_out):
    wa = w_out[POOL_WIDTH:].reshape(N_HEADS, HEAD_DIM, D_MODEL)
    wa = jnp.pad(wa, ((0, 0), (0, LANES - HEAD_DIM), (0, 0))).reshape(N_HEADS * LANES, D_MODEL)
    return w_out[:POOL_WIDTH].astype(BF16), wa.astype(BF16)


def _group_lanes(a):
    return a.transpose(1, 0, 2).reshape(a.shape[1], KV_WIDTH)


def kernel(x, c, w_ada, b_ada, norm1_g, norm2_g, w_in, w_pool, pool_scale, q_gain, kc_gain,
           ks_gain, kw_gain, cmp_pos_k, cmp_w1_k, cmp_b1_k, cmp_w2_k, cmp_b2_k, cmp_pos_v,
           cmp_w1_v, cmp_b1_v, cmp_w2_v, cmp_b2_v, w_out, w_ff1, w_ff2):
    batch, s, _ = x.shape
    assert batch == 1 and w_ada.shape[0] == 1
    assert s % SEL_CHUNK == 0 and s >= WIN_SPAN and (s // SEL_BLOCK) & (s // SEL_BLOCK - 1) == 0
    tm = min(s, 512)
    x2 = x[0]

    mod = _ada(jnp.broadcast_to(c, (8, D_MODEL)), w_ada[0], b_ada)[0:1]
    sh1, sc1, ga1, sh2, sc2, ga2 = [mod[:, k * D_MODEL:(k + 1) * D_MODEL] for k in range(6)]

    pair = lambda gain: jnp.tile(gain, (1, 2))
    (pool_out, qpad, kc, vc, ksel, vsel_t, kwin, vwin_t, gates_t) = _in_proj(
        x2, norm1_g, sc1, sh1, _pad_in_proj_weight(w_in[0]), w_pool[0].astype(BF16), pool_scale,
        pair(q_gain), pair(ks_gain), pair(kw_gain), tm)

    chunks = lambda a: a.reshape(N_KV, s // CMP_STRIDE, CMP_STRIDE * HEAD_DIM)
    kcmp = _compress(chunks(kc), cmp_pos_k.reshape(1, -1), cmp_w1_k[0].astype(BF16), cmp_b1_k,
                     cmp_w2_k[0].astype(BF16), cmp_b2_k, kc_gain, True)
    vcmp = _compress(chunks(vc), cmp_pos_v.reshape(1, -1), cmp_w1_v[0].astype(BF16), cmp_b1_v,
                     cmp_w2_v[0].astype(BF16), cmp_b2_v, kc_gain, False)

    vcmp_t = jnp.pad(vcmp.transpose(0, 2, 1), ((0, 0), (0, LANES - HEAD_DIM), (0, 0)))
    attn_pad = _attention(qpad, gates_t, _group_lanes(kcmp), vcmp_t, ksel, vsel_t, kwin, vwin_t)

    w_pool_rows, w_attn_rows = _pad_out_proj_weight(w_out[0])
    x1, h2 = _out_proj(pool_out, attn_pad, w_pool_rows, w_attn_rows, x2, ga1, norm2_g, sc2, sh2, tm)
    out = _mlp(h2, w_ff1[0].astype(BF16), w_ff2[0].astype(BF16), x1, ga2, tm)
    return out[None]
```

```python
import functools

import jax
import jax.numpy as jnp
import numpy as np
from jax import lax
from jax.experimental import pallas as pl
from jax.experimental.pallas import tpu as pltpu

F32 = jnp.float32
BF16 = jnp.bfloat16

LANES = 128
VMEM_LIMIT_BYTES = 56 * 1024 * 1024

D_MODEL = 1024
POOL_WIDTH = 512
POOL_WINDOWS = (2, 4, 8, 16)
POOL_GROUP = POOL_WIDTH // len(POOL_WINDOWS)
POOL_HALO = 16
HEAD_DIM = 64
N_HEADS = 8
N_KV = 2
GQA_GROUP = N_HEADS // N_KV
KV_WIDTH = N_KV * HEAD_DIM
N_BRANCH = 3
CMP_LEN = 32
CMP_STRIDE = 16
CMP_HIDDEN = 4 * HEAD_DIM
SEL_BLOCK = 64
N_SEL = 16
WINDOW = 512
D_FF = 4 * D_MODEL
NEG_INF = -1e30
FORCE_BONUS = 1e4
EPS = 1e-6
LOG2E = 1.4426950408889634

TQ = 256
SEL_CHUNK = 256
BLOCKS_PER_CHUNK = SEL_CHUNK // SEL_BLOCK
SEL_GROUP = 4
CMP_BUCKETS = 4
WIN_SPAN = WINDOW + TQ
CMP_PER_SEL = SEL_BLOCK // CMP_STRIDE
GATE_ROWS = 16

_Q_WIDTH = N_HEADS * HEAD_DIM
_OFF_U = 0
_OFF_Q = POOL_WIDTH
_OFF_KC = _OFF_Q + _Q_WIDTH
_OFF_VC = _OFF_KC + KV_WIDTH
_OFF_KS = _OFF_VC + KV_WIDTH
_OFF_VS = _OFF_KS + KV_WIDTH
_OFF_KW = _OFF_VS + KV_WIDTH
_OFF_VW = _OFF_KW + KV_WIDTH
_OFF_G = _OFF_VW + KV_WIDTH
_IN_PAD = _OFF_G + N_KV * LANES


def _dot(a, b):
    return jnp.dot(a, b, preferred_element_type=F32)


def _dot_nt(a, b):
    return lax.dot_general(a, b, (((1,), (1,)), ((), ())), preferred_element_type=F32)


def _const_spec(shape):
    nd = len(shape)
    return pl.BlockSpec(shape, lambda *_: (0,) * nd, pipeline_mode=pl.Buffered(1))


def _ada_kernel(c_ref, w_ref, b_ref, o_ref):
    o_ref[...] = jnp.dot(c_ref[...], w_ref[...], preferred_element_type=F32,
                         precision=lax.Precision.HIGHEST) + b_ref[...]


def _ada(c8, w, b):
    n = w.shape[1]
    bn = 1024
    return pl.pallas_call(
        _ada_kernel,
        grid=(n // bn,),
        in_specs=[pl.BlockSpec((8, D_MODEL), lambda j: (0, 0)),
                  pl.BlockSpec((D_MODEL, bn), lambda j: (0, j)),
                  pl.BlockSpec((1, bn), lambda j: (0, j))],
        out_specs=pl.BlockSpec((8, bn), lambda j: (0, j)),
        out_shape=jax.ShapeDtypeStruct((8, n), F32),
        name="ada",
    )(c8, w, b)


def _rms_modulate(x, g, sc, sh):
    ms = jnp.mean(x * x, axis=-1, keepdims=True)
    return (x * lax.rsqrt(ms + EPS)) * (g * (1.0 + sc)) + sh


def _head_norm_pair(x, gain2):
    lane = lax.broadcasted_iota(jnp.int32, x.shape, 1)
    lo = lane < HEAD_DIM
    sq = x * x
    s_lo = jnp.sum(jnp.where(lo, sq, 0.0), axis=-1, keepdims=True)
    s_hi = jnp.sum(jnp.where(lo, 0.0, sq), axis=-1, keepdims=True)
    ms = jnp.where(lo, s_lo, s_hi) * (1.0 / HEAD_DIM)
    return x * lax.rsqrt(ms + EPS) * gain2


def _in_proj_kernel(x_ref, g_ref, sc_ref, sh_ref, w_ref, wpool_ref, pscale_ref, qg_ref, ksg_ref,
                    kwg_ref, pool_ref, q_ref, kc_ref, vc_ref, ks_ref, vs_ref, kw_ref, vw_ref,
                    gate_ref, ebuf_ref, *, tm):
    i = pl.program_id(0)
    h = _rms_modulate(x_ref[...], g_ref[...], sc_ref[...], sh_ref[...]).astype(BF16)
    proj = _dot(h, w_ref[...])

    for pair in range(N_HEADS // 2):
        cols = slice(_OFF_Q + pair * LANES, _OFF_Q + (pair + 1) * LANES)
        qn = _head_norm_pair(proj[:, cols], qg_ref[...])
        q_ref[:, pair * LANES:(pair + 1) * LANES] = (qn * (HEAD_DIM ** -0.5 * LOG2E)).astype(BF16)

    for gg in range(N_KV):
        kc_ref[gg] = proj[:, _OFF_KC + gg * HEAD_DIM:_OFF_KC + (gg + 1) * HEAD_DIM]
        vc_ref[gg] = proj[:, _OFF_VC + gg * HEAD_DIM:_OFF_VC + (gg + 1) * HEAD_DIM]
    ks_ref[...] = _head_norm_pair(proj[:, _OFF_KS:_OFF_KS + KV_WIDTH], ksg_ref[...]).astype(BF16)
    kw_ref[...] = _head_norm_pair(proj[:, _OFF_KW:_OFF_KW + KV_WIDTH], kwg_ref[...]).astype(BF16)
    def value_tiles(ref, off, chunk):
        ones_row = jnp.where(lax.broadcasted_iota(jnp.int32, (HEAD_DIM, chunk), 0) == 0, 1.0, 0.0)
        for k in range(tm // chunk):
            vt = proj[k * chunk:(k + 1) * chunk, off:off + KV_WIDTH].T
            for gg in range(N_KV):
                ref[gg, k] = jnp.concatenate(
                    [vt[gg * HEAD_DIM:(gg + 1) * HEAD_DIM], ones_row], axis=0).astype(BF16)

    value_tiles(vs_ref, _OFF_VS, SEL_CHUNK)
    value_tiles(vw_ref, _OFF_VW, TQ)
    gates_t = jax.nn.sigmoid(proj[:, _OFF_G:_OFF_G + N_KV * LANES]).T
    for gg in range(N_KV):
        gate_ref[gg] = gates_t[gg * LANES:gg * LANES + GATE_ROWS]

    @pl.when(i == 0)
    def _():
        ebuf_ref[0:POOL_HALO, :] = jnp.zeros((POOL_HALO, POOL_WIDTH), F32)

    u = proj[:, _OFF_U:_OFF_U + POOL_WIDTH]
    ebuf_ref[POOL_HALO:POOL_HALO + tm, :] = u
    t1 = i * tm + lax.broadcasted_iota(jnp.int32, (tm, POOL_GROUP), 0) + 1
    for gi, w in enumerate(POOL_WINDOWS):
        c0 = gi * POOL_GROUP
        win = u[:, c0:c0 + POOL_GROUP]
        for k in range(1, w):
            win = win + ebuf_ref[POOL_HALO - k:POOL_HALO - k + tm, c0:c0 + POOL_GROUP]
        cnt = jnp.minimum(t1, w).astype(F32)
        pooled = win / cnt - u[:, c0:c0 + POOL_GROUP]
        y = _dot(pooled.astype(BF16), wpool_ref[gi])
        pool_ref[:, c0:c0 + POOL_GROUP] = (y * pscale_ref[:, c0:c0 + POOL_GROUP]).astype(BF16)
    ebuf_ref[0:POOL_HALO, :] = ebuf_ref[tm:tm + POOL_HALO, :]


def _in_proj(x, g1, sc1, sh1, w_in_p, w_pool, pool_scale, qg, ksg, kwg, tm):
    s = x.shape[0]
    row = lambda w: pl.BlockSpec((tm, w), lambda i: (i, 0))
    vec = lambda w: pl.BlockSpec((1, w), lambda i: (0, 0))
    per_head = pl.BlockSpec((N_KV, tm, HEAD_DIM), lambda i: (0, i, 0))
    chunks_t = lambda ch: pl.BlockSpec((N_KV, tm // ch, LANES, ch), lambda i: (0, i, 0, 0))
    out_shape = [
        jax.ShapeDtypeStruct((s, POOL_WIDTH), BF16),
        jax.ShapeDtypeStruct((s, _Q_WIDTH), BF16),
        jax.ShapeDtypeStruct((N_KV, s, HEAD_DIM), F32),
        jax.ShapeDtypeStruct((N_KV, s, HEAD_DIM), F32),
        jax.ShapeDtypeStruct((s, KV_WIDTH), BF16),
        jax.ShapeDtypeStruct((N_KV, s // SEL_CHUNK, LANES, SEL_CHUNK), BF16),
        jax.ShapeDtypeStruct((s, KV_WIDTH), BF16),
        jax.ShapeDtypeStruct((N_KV, s // TQ, LANES, TQ), BF16),
        jax.ShapeDtypeStruct((N_KV, GATE_ROWS, s), F32),
    ]
    return pl.pallas_call(
        functools.partial(_in_proj_kernel, tm=tm),
        grid=(s // tm,),
        in_specs=[row(D_MODEL), vec(D_MODEL), vec(D_MODEL), vec(D_MODEL),
                  _const_spec(w_in_p.shape), _const_spec(w_pool.shape), vec(POOL_WIDTH),
                  vec(LANES), vec(LANES), vec(LANES)],
        out_specs=[row(POOL_WIDTH), row(_Q_WIDTH), per_head, per_head, row(KV_WIDTH),
                   chunks_t(SEL_CHUNK), row(KV_WIDTH), chunks_t(TQ),
                   pl.BlockSpec((N_KV, GATE_ROWS, tm), lambda i: (0, 0, i))],
        out_shape=out_shape,
        scratch_shapes=[pltpu.VMEM((tm + POOL_HALO, POOL_WIDTH), F32)],
        compiler_params=pltpu.CompilerParams(dimension_semantics=("arbitrary",),
                                             vmem_limit_bytes=VMEM_LIMIT_BYTES),
        name="in_proj",
    )(x, g1, sc1, sh1, w_in_p, w_pool, pool_scale, qg, ksg, kwg)


def _compress_kernel(c_ref, pos_ref, w1_ref, b1_ref, w2_ref, b2_ref, gain_ref, o_ref, *,
                     normalize):
    half = CMP_STRIDE * HEAD_DIM
    n_rows = c_ref.shape[1]
    c = c_ref[0]
    first = _dot((c + pos_ref[:, 0:half]).astype(BF16), w1_ref[0:half, :])
    second = _dot((c + pos_ref[:, half:2 * half]).astype(BF16), w1_ref[half:2 * half, :])
    hid = jax.nn.gelu(first + pltpu.roll(second, n_rows - 1, axis=0) + b1_ref[...])
    y = _dot(hid.astype(BF16), w2_ref[...]) + b2_ref[...]
    if normalize:
        ms = jnp.mean(y * y, axis=-1, keepdims=True)
        y = y * lax.rsqrt(ms + EPS) * gain_ref[...]
    o_ref[0] = y.astype(BF16)


def _compress(chunks, pos, w1, b1, w2, b2, gain, normalize):
    _, n_chunks, width = chunks.shape
    vec = lambda w: pl.BlockSpec((1, w), lambda g: (0, 0))
    return pl.pallas_call(
        functools.partial(_compress_kernel, normalize=normalize),
        grid=(N_KV,),
        in_specs=[pl.BlockSpec((1, n_chunks, width), lambda g: (g, 0, 0)),
                  vec(2 * width), _const_spec(w1.shape), vec(CMP_HIDDEN), _const_spec(w2.shape),
                  vec(HEAD_DIM), vec(HEAD_DIM)],
        out_specs=pl.BlockSpec((1, n_chunks, HEAD_DIM), lambda g: (g, 0, 0)),
        out_shape=jax.ShapeDtypeStruct((N_KV, n_chunks, HEAD_DIM), BF16),
        compiler_params=pltpu.CompilerParams(vmem_limit_bytes=VMEM_LIMIT_BYTES),
        name="compress",
    )(chunks, pos, w1, b1, w2, b2, gain)


def _attn_kernel(q_ref, gate_ref, rowt_ref, kc_ref, vct_ref, ks_ref, vst_ref, kw_ref, vwt_ref,
                 o_ref, psum_ref, madd_ref, bias_ref, m_ref, acc_ref, ocmp_ref, owin_ref,
                 flag_ref, list_ref, *, nb):
    g = pl.program_id(0)
    i = pl.program_id(1)
    q0 = i * TQ
    nc = CMP_PER_SEL * nb
    n_chunks = nb // BLOCKS_PER_CHUNK
    gslope = jnp.where(g == 0, LOG2E, LOG2E * 2.0 ** -GQA_GROUP).astype(F32)
    slopes = [gslope * (2.0 ** -(r + 1)) for r in range(GQA_GROUP)]
    heads = [slice(r * TQ, (r + 1) * TQ) for r in range(GQA_GROUP)]

    lane = lax.broadcasted_iota(jnp.int32, (TQ, LANES), 1)
    own_half = (lane >= HEAD_DIM) == (g == 1)
    q_rows = []
    for r in range(GQA_GROUP):
        pair = q_ref[:, (r // 2) * LANES:(r // 2 + 1) * LANES]
        swapped = jnp.concatenate([pair[:, HEAD_DIM:], pair[:, :HEAD_DIM]], axis=1)
        q_rows.append(jnp.where(own_half, jnp.where(g == r % 2, pair, swapped), 0.0))
    q = jnp.concatenate(q_rows, axis=0)
    t_lane = q0 + lax.broadcasted_iota(jnp.int32, (1, TQ), 1)

    w0 = pl.multiple_of(jnp.maximum(q0 - WINDOW, 0), TQ)
    s_w = _dot_nt(kw_ref[pl.ds(w0, WIN_SPAN), :], q)
    row_w = rowt_ref[0:WIN_SPAN, :]
    lane_w = lax.broadcasted_iota(jnp.int32, (WIN_SPAN, TQ), 1).astype(F32)
    dq = (q0 - w0).astype(F32)
    d_w = lane_w - row_w + dq
    ok_w = jnp.abs(d_w - (WINDOW - 1) * 0.5) < WINDOW * 0.5
    rel_w = row_w - dq
    wb = w0 // TQ
    for r in range(GQA_GROUP):
        s = jnp.where(ok_w, s_w[:, heads[r]] + slopes[r] * rel_w, NEG_INF)
        m = jnp.max(s, axis=0, keepdims=True)
        p = jnp.exp2((s - m).astype(BF16))
        o_win = _dot(vwt_ref[0, wb], p[0:TQ])
        for b in range(1, WIN_SPAN // TQ):
            o_win = o_win + _dot(vwt_ref[0, wb + b], p[b * TQ:(b + 1) * TQ])
        owin_ref[:, heads[r]] = o_win

    def compress_and_select(n_rows):
        n_blk = n_rows // CMP_PER_SEL
        rel_c = CMP_STRIDE * rowt_ref[0:n_rows, :] + (CMP_LEN - 1) - q0.astype(F32)
        ok_c = rel_c <= lax.broadcasted_iota(jnp.int32, (1, TQ), 1).astype(F32)
        s_c = _dot_nt(kc_ref[0:n_rows, :], q)
        p_sum = jnp.zeros((n_rows, TQ), F32)
        p_cols = []
        for r in range(GQA_GROUP):
            s = jnp.where(ok_c, s_c[:, heads[r]] + slopes[r] * rel_c, NEG_INF)
            m = jnp.max(s, axis=0, keepdims=True)
            e = jnp.exp2(s - m)
            l = jnp.sum(e, axis=0, keepdims=True)
            p = e * jnp.where(m > 0.5 * NEG_INF, 1.0 / l, 0.0)
            p_sum = p_sum + p
            p_cols.append(p.astype(BF16))
        ocmp_ref[...] = _dot(vct_ref[0, :, 0:n_rows], jnp.concatenate(p_cols, axis=1))

        for h in range(TQ // LANES):
            psum_ref[h, 0:8, :] = jnp.zeros((8, LANES), F32)
            psum_ref[h, 8:8 + n_rows, :] = p_sum[:, h * LANES:(h + 1) * LANES]

        def every4(start):
            parts = [psum_ref[h, pl.ds(8 + start, n_blk, stride=CMP_PER_SEL), :]
                     for h in range(TQ // LANES)]
            return parts[0] if len(parts) == 1 else jnp.concatenate(parts, axis=1)

        imp = every4(0) + every4(1) + every4(2) + 0.5 * every4(3) + 0.5 * every4(-1)
        blk = lax.broadcasted_iota(jnp.int32, (n_blk, TQ), 0)
        cur = lax.shift_right_logical(t_lane, SEL_BLOCK.bit_length() - 1)
        causal = blk <= cur
        forced = jnp.where(blk == 0, 1.0, 0.0) + jnp.where(blk == cur, 1.0, 0.0) \
            + jnp.where(blk == cur - 1, 1.0, 0.0)
        val = jnp.where(causal, imp + jnp.where(forced > 0.0, FORCE_BONUS, 0.0), -1.0)

        def pick(_, v):
            blk_f = rowt_ref[0:n_blk, :]
            mx = jnp.max(v, axis=0, keepdims=True)
            idx = jnp.min(jnp.where(v == mx, blk_f, float(n_blk)), axis=0, keepdims=True)
            return jnp.where(blk_f == idx, -2.0, v)

        picked = lax.fori_loop(0, min(N_SEL, n_blk), pick, val)
        madd_ref[0:n_blk, :] = jnp.where(causal, jnp.where(picked < -1.5, 0.0, NEG_INF), NEG_INF)
        for c in range(n_blk // BLOCKS_PER_CHUNK):
            rows = madd_ref[c * BLOCKS_PER_CHUNK:(c + 1) * BLOCKS_PER_CHUNK, :]
            flag_ref[c] = (jnp.max(rows) > 0.5 * NEG_INF).astype(jnp.int32)

    rows_per_bucket = nc // CMP_BUCKETS
    bucket = ((q0 + TQ) // CMP_STRIDE - 1) // rows_per_bucket
    for b in range(CMP_BUCKETS):
        pl.when(bucket == b)(functools.partial(compress_and_select, (b + 1) * rows_per_bucket))
    madd_ref[nb:nb + BLOCKS_PER_CHUNK, :] = jnp.full((BLOCKS_PER_CHUNK, TQ), NEG_INF, F32)

    key_row = rowt_ref[0:SEL_CHUNK, :]
    for r in range(GQA_GROUP):
        bias_ref[r] = slopes[r] * key_row
    m_ref[...] = jnp.full(m_ref.shape, NEG_INF, F32)
    acc_ref[...] = jnp.zeros(acc_ref.shape, F32)

    def key_chunk(c):
        return jnp.minimum(c, n_chunks - 1)

    def scores(chunks):
        keys = [ks_ref[pl.ds(pl.multiple_of(key_chunk(c) * SEL_CHUNK, SEL_CHUNK), SEL_CHUNK), :]
                for c in chunks]
        s = _dot_nt(jnp.concatenate(keys, axis=0) if len(keys) > 1 else keys[0], q)
        return [s[k * SEL_CHUNK:(k + 1) * SEL_CHUNK] for k in range(len(chunks))]

    def softmax_pv(chunks, s, diagonal):
        madds, rel0 = [], []
        for c in chunks:
            k0 = key_chunk(c) * SEL_CHUNK
            madd = jnp.concatenate(
                [jnp.broadcast_to(madd_ref[pl.ds(c * BLOCKS_PER_CHUNK + b, 1), :], (SEL_BLOCK, TQ))
                 for b in range(BLOCKS_PER_CHUNK)], axis=0)
            if diagonal:
                pos = k0 + lax.broadcasted_iota(jnp.int32, (SEL_CHUNK, TQ), 0)
                madd = jnp.where(pos <= t_lane, madd, NEG_INF)
            madds.append(madd)
            rel0.append((k0 - q0).astype(F32))
        p_cols = [[] for _ in chunks]
        alphas = []
        for r in range(GQA_GROUP):
            shifts = [slopes[r] * x for x in rel0]
            us = [sk[:, heads[r]] + bias_ref[r] + mk for sk, mk in zip(s, madds)]
            m_old = m_ref[:, heads[r]]
            m_new = m_old
            for u, sh in zip(us, shifts):
                m_new = jnp.maximum(m_new, jnp.max(u, axis=0, keepdims=True) + sh)
            alphas.append(jnp.exp2(m_old - m_new))
            for k, (u, sh) in enumerate(zip(us, shifts)):
                p_cols[k].append(jnp.exp2((u - (m_new - sh)).astype(BF16)))
            m_ref[:, heads[r]] = m_new
        pv = _dot(vst_ref[0, key_chunk(chunks[0])], jnp.concatenate(p_cols[0], axis=1))
        for k in range(1, len(chunks)):
            pv = pv + _dot(vst_ref[0, key_chunk(chunks[k])], jnp.concatenate(p_cols[k], axis=1))
        acc_ref[...] = jnp.concatenate(alphas, axis=1) * acc_ref[...] + pv

    c_diag = q0 // SEL_CHUNK

    def compact(c, n):
        list_ref[n] = c
        return n + flag_ref[c]

    n_active = lax.fori_loop(0, c_diag, compact, jnp.int32(0))
    for k in range(SEL_GROUP - 1):
        list_ref[n_active + k] = n_chunks

    def group_body(p, carry):
        chunks = [list_ref[p * SEL_GROUP + k] for k in range(SEL_GROUP)]
        softmax_pv(chunks, scores(chunks), diagonal=False)
        return carry

    lax.fori_loop(0, (n_active + SEL_GROUP - 1) // SEL_GROUP, group_body, 0)
    softmax_pv([c_diag], scores([c_diag]), diagonal=True)

    head_out = []
    for r in range(GQA_GROUP):
        gc = gate_ref[0, 3 * r + 0:3 * r + 1, :]
        gs = gate_ref[0, 3 * r + 1:3 * r + 2, :]
        gw = gate_ref[0, 3 * r + 2:3 * r + 3, :]
        acc = acc_ref[:, heads[r]]
        win = owin_ref[:, heads[r]]
        out_t = (gc * ocmp_ref[:, heads[r]]
                 + (gs / acc[HEAD_DIM:HEAD_DIM + 1]) * acc
                 + (gw / win[HEAD_DIM:HEAD_DIM + 1]) * win)
        head_out.append(out_t.T[:, 0:HEAD_DIM])
    for pair in range(GQA_GROUP // 2):
        o_ref[:, pair * LANES:(pair + 1) * LANES] = jnp.concatenate(
            head_out[2 * pair:2 * pair + 2], axis=1).astype(BF16)


def _attention(q, gates_t, kcmp, vcmp_t, ksel, vsel_t, kwin, vwin_t):
    s = q.shape[0]
    nb = s // SEL_BLOCK
    nc = CMP_PER_SEL * nb
    gw = GQA_GROUP * HEAD_DIM
    n_rows = max(nc, WIN_SPAN)
    row_tile = jnp.asarray(np.broadcast_to(np.arange(n_rows)[:, None], (n_rows, TQ))
                           .astype(np.float32))
    per_group = lambda a: pl.BlockSpec((1,) + a.shape[1:], lambda g, i: (g,) + (0,) * (a.ndim - 1),
                                       pipeline_mode=pl.Buffered(1))
    return pl.pallas_call(
        functools.partial(_attn_kernel, nb=nb),
        grid=(N_KV, s // TQ),
        in_specs=[pl.BlockSpec((TQ, gw), lambda g, i: (i, g)),
                  pl.BlockSpec((1, GATE_ROWS, TQ), lambda g, i: (g, 0, i)),
                  _const_spec(row_tile.shape),
                  _const_spec(kcmp.shape), per_group(vcmp_t),
                  _const_spec(ksel.shape), per_group(vsel_t),
                  _const_spec(kwin.shape), per_group(vwin_t)],
        out_specs=pl.BlockSpec((TQ, gw), lambda g, i: (i, g)),
        out_shape=jax.ShapeDtypeStruct((s, N_HEADS * HEAD_DIM), BF16),
        scratch_shapes=[pltpu.VMEM((TQ // LANES, 8 + nc, LANES), F32),
                        pltpu.VMEM((nb + BLOCKS_PER_CHUNK, TQ), F32),
                        pltpu.VMEM((GQA_GROUP, SEL_CHUNK, TQ), F32),
                        pltpu.VMEM((1, GQA_GROUP * TQ), F32),
                        pltpu.VMEM((LANES, GQA_GROUP * TQ), F32),
                        pltpu.VMEM((LANES, GQA_GROUP * TQ), F32),
                        pltpu.VMEM((LANES, GQA_GROUP * TQ), F32),
                        pltpu.SMEM((nb // BLOCKS_PER_CHUNK,), jnp.int32),
                        pltpu.SMEM((nb // BLOCKS_PER_CHUNK + SEL_GROUP,), jnp.int32)],
        compiler_params=pltpu.CompilerParams(dimension_semantics=("arbitrary", "arbitrary"),
                                             vmem_limit_bytes=VMEM_LIMIT_BYTES),
        name="attn",
    )(q, gates_t, row_tile, kcmp, vcmp_t, ksel, vsel_t, kwin, vwin_t)


def _out_mlp_kernel(pool_ref, attn_ref, wout_ref, x_ref, ga1_ref, g_ref, sc_ref, sh_ref, w1_ref,
                    w2_ref, ga2_ref, o_ref, *, ff_chunk):
    mix = (_dot(pool_ref[...], wout_ref[0:POOL_WIDTH, :])
           + _dot(attn_ref[...], wout_ref[POOL_WIDTH:, :]))
    x1 = x_ref[...] + ga1_ref[...] * mix
    h = _rms_modulate(x1, g_ref[...], sc_ref[...], sh_ref[...]).astype(BF16)
    acc = jnp.zeros(x1.shape, F32)
    for c in range(D_FF // ff_chunk):
        a = _dot(h, w1_ref[:, c * ff_chunk:(c + 1) * ff_chunk])
        a = jnp.square(jnp.maximum(a, 0.0)).astype(BF16)
        acc = acc + _dot(a, w2_ref[c * ff_chunk:(c + 1) * ff_chunk, :])
    o_ref[...] = x1 + ga2_ref[...] * acc


def _out_mlp(pool_out, attn_out, w_out, x, ga1, g2, sc2, sh2, w1, w2, ga2, tm):
    s = x.shape[0]
    row = lambda w: pl.BlockSpec((tm, w), lambda i: (i, 0))
    vec = lambda w: pl.BlockSpec((1, w), lambda i: (0, 0))
    return pl.pallas_call(
        functools.partial(_out_mlp_kernel, ff_chunk=1024),
        grid=(s // tm,),
        in_specs=[row(POOL_WIDTH), row(N_HEADS * HEAD_DIM), _const_spec(w_out.shape),
                  row(D_MODEL), vec(D_MODEL), vec(D_MODEL), vec(D_MODEL), vec(D_MODEL),
                  _const_spec(w1.shape), _const_spec(w2.shape), vec(D_MODEL)],
        out_specs=row(D_MODEL),
        out_shape=jax.ShapeDtypeStruct((s, D_MODEL), F32),
        compiler_params=pltpu.CompilerParams(vmem_limit_bytes=VMEM_LIMIT_BYTES),
        name="out_mlp",
    )(pool_out, attn_out, w_out, x, ga1, g2, sc2, sh2, w1, w2, ga2)


def _pad_in_proj_weight(w_in):
    src_g = _OFF_G
    per_group = GQA_GROUP * N_BRANCH
    cols = [w_in[:, :src_g]]
    for gg in range(N_KV):
        cols += [w_in[:, src_g + gg * per_group:src_g + (gg + 1) * per_group],
                 jnp.zeros((D_MODEL, LANES - per_group), w_in.dtype)]
    return jnp.concatenate(cols, axis=1).astype(BF16)


def _group_lanes(a):
    return a.transpose(1, 0, 2).reshape(a.shape[1], KV_WIDTH)


def kernel(x, c, w_ada, b_ada, norm1_g, norm2_g, w_in, w_pool, pool_scale, q_gain, kc_gain,
           ks_gain, kw_gain, cmp_pos_k, cmp_w1_k, cmp_b1_k, cmp_w2_k, cmp_b2_k, cmp_pos_v,
           cmp_w1_v, cmp_b1_v, cmp_w2_v, cmp_b2_v, w_out, w_ff1, w_ff2):
    batch, s, _ = x.shape
    assert batch == 1 and w_ada.shape[0] == 1
    assert s % SEL_CHUNK == 0 and s >= WIN_SPAN and (s // SEL_BLOCK) & (s // SEL_BLOCK - 1) == 0
    tm = min(s, 512)
    x2 = x[0]

    mod = _ada(jnp.broadcast_to(c, (8, D_MODEL)), w_ada[0], b_ada)[0:1]
    sh1, sc1, ga1, sh2, sc2, ga2 = [mod[:, k * D_MODEL:(k + 1) * D_MODEL] for k in range(6)]

    pair = lambda gain: jnp.tile(gain, (1, 2))
    (pool_out, q, kc, vc, ksel, vsel_t, kwin, vwin_t, gates_t) = _in_proj(
        x2, norm1_g, sc1, sh1, _pad_in_proj_weight(w_in[0]), w_pool[0].astype(BF16), pool_scale,
        pair(q_gain), pair(ks_gain), pair(kw_gain), tm)

    chunks = lambda a: a.reshape(N_KV, s // CMP_STRIDE, CMP_STRIDE * HEAD_DIM)
    kcmp = _compress(chunks(kc), cmp_pos_k.reshape(1, -1), cmp_w1_k[0].astype(BF16), cmp_b1_k,
                     cmp_w2_k[0].astype(BF16), cmp_b2_k, kc_gain, True)
    vcmp = _compress(chunks(vc), cmp_pos_v.reshape(1, -1), cmp_w1_v[0].astype(BF16), cmp_b1_v,
                     cmp_w2_v[0].astype(BF16), cmp_b2_v, kc_gain, False)

    vcmp_t = jnp.pad(vcmp.transpose(0, 2, 1), ((0, 0), (0, LANES - HEAD_DIM), (0, 0)))
    attn_out = _attention(q, gates_t, _group_lanes(kcmp), vcmp_t, ksel, vsel_t, kwin, vwin_t)

    out = _out_mlp(pool_out, attn_out, w_out[0].astype(BF16), x2, ga1, norm2_g, sc2, sh2,
                   w_ff1[0].astype(BF16), w_ff2[0].astype(BF16), ga2, tm)
    return out[None]
```

```python
import functools

import jax
import jax.numpy as jnp
import numpy as np
from jax import lax
from jax.experimental import pallas as pl
from jax.experimental.pallas import tpu as pltpu

F32 = jnp.float32
BF16 = jnp.bfloat16

LANES = 128
VMEM_LIMIT_BYTES = 56 * 1024 * 1024

D_MODEL = 1024
POOL_WIDTH = 512
POOL_WINDOWS = (2, 4, 8, 16)
POOL_GROUP = POOL_WIDTH // len(POOL_WINDOWS)
POOL_HALO = 16
HEAD_DIM = 64
N_HEADS = 8
N_KV = 2
GQA_GROUP = N_HEADS // N_KV
KV_WIDTH = N_KV * HEAD_DIM
N_BRANCH = 3
CMP_LEN = 32
CMP_STRIDE = 16
CMP_HIDDEN = 4 * HEAD_DIM
SEL_BLOCK = 64
N_SEL = 16
WINDOW = 512
D_FF = 4 * D_MODEL
NEG_INF = -1e30
N_FORCED = 3
EPS = 1e-6
LOG2E = 1.4426950408889634

TQ = 256
SEL_CHUNK = 256
BLOCKS_PER_CHUNK = SEL_CHUNK // SEL_BLOCK
SEL_GROUP = 4
CMP_BUCKETS = 4
WIN_SPAN = WINDOW + TQ
CMP_PER_SEL = SEL_BLOCK // CMP_STRIDE
GATE_ROWS = 16

_Q_WIDTH = N_HEADS * HEAD_DIM
_OFF_U = 0
_OFF_Q = POOL_WIDTH
_OFF_KC = _OFF_Q + _Q_WIDTH
_OFF_VC = _OFF_KC + KV_WIDTH
_OFF_KS = _OFF_VC + KV_WIDTH
_OFF_VS = _OFF_KS + KV_WIDTH
_OFF_KW = _OFF_VS + KV_WIDTH
_OFF_VW = _OFF_KW + KV_WIDTH
_OFF_G = _OFF_VW + KV_WIDTH
_IN_PAD = _OFF_G + N_KV * LANES


def _dot(a, b):
    return jnp.dot(a, b, preferred_element_type=F32)


def _dot_nt(a, b):
    return lax.dot_general(a, b, (((1,), (1,)), ((), ())), preferred_element_type=F32)


def _const_spec(shape):
    nd = len(shape)
    return pl.BlockSpec(shape, lambda *_: (0,) * nd, pipeline_mode=pl.Buffered(1))


def _ada_kernel(c_ref, w_ref, b_ref, o_ref):
    o_ref[...] = jnp.dot(c_ref[...], w_ref[...], preferred_element_type=F32,
                         precision=lax.Precision.HIGHEST) + b_ref[...]


def _ada(c8, w, b):
    n = w.shape[1]
    bn = 1024
    return pl.pallas_call(
        _ada_kernel,
        grid=(n // bn,),
        in_specs=[pl.BlockSpec((8, D_MODEL), lambda j: (0, 0)),
                  pl.BlockSpec((D_MODEL, bn), lambda j: (0, j)),
                  pl.BlockSpec((1, bn), lambda j: (0, j))],
        out_specs=pl.BlockSpec((8, bn), lambda j: (0, j)),
        out_shape=jax.ShapeDtypeStruct((8, n), F32),
        name="ada",
    )(c8, w, b)


def _rms_modulate(x, g, sc, sh):
    ms = jnp.mean(x * x, axis=-1, keepdims=True)
    return (x * lax.rsqrt(ms + EPS)) * (g * (1.0 + sc)) + sh


def _head_norm_pair(x, gain2):
    lane = lax.broadcasted_iota(jnp.int32, x.shape, 1)
    lo = lane < HEAD_DIM
    sq = x * x
    s_lo = jnp.sum(jnp.where(lo, sq, 0.0), axis=-1, keepdims=True)
    s_hi = jnp.sum(jnp.where(lo, 0.0, sq), axis=-1, keepdims=True)
    ms = jnp.where(lo, s_lo, s_hi) * (1.0 / HEAD_DIM)
    return x * lax.rsqrt(ms + EPS) * gain2


def _in_proj_kernel(x_ref, g_ref, sc_ref, sh_ref, w_ref, wpool_ref, pscale_ref, qg_ref, ksg_ref,
                    kwg_ref, pool_ref, q_ref, kc_ref, vc_ref, ks_ref, vs_ref, kw_ref, vw_ref,
                    gate_ref, ebuf_ref, *, tm):
    i = pl.program_id(0)
    h = _rms_modulate(x_ref[...], g_ref[...], sc_ref[...], sh_ref[...]).astype(BF16)
    proj = _dot(h, w_ref[...])

    for pair in range(N_HEADS // 2):
        cols = slice(_OFF_Q + pair * LANES, _OFF_Q + (pair + 1) * LANES)
        qn = _head_norm_pair(proj[:, cols], qg_ref[...])
        q_ref[:, pair * LANES:(pair + 1) * LANES] = (qn * (HEAD_DIM ** -0.5 * LOG2E)).astype(BF16)

    for gg in range(N_KV):
        kc_ref[gg] = proj[:, _OFF_KC + gg * HEAD_DIM:_OFF_KC + (gg + 1) * HEAD_DIM]
        vc_ref[gg] = proj[:, _OFF_VC + gg * HEAD_DIM:_OFF_VC + (gg + 1) * HEAD_DIM]
    ks_ref[...] = _head_norm_pair(proj[:, _OFF_KS:_OFF_KS + KV_WIDTH], ksg_ref[...]).astype(BF16)
    kw_ref[...] = _head_norm_pair(proj[:, _OFF_KW:_OFF_KW + KV_WIDTH], kwg_ref[...]).astype(BF16)
    def value_tiles(ref, off, chunk):
        ones_row = jnp.where(lax.broadcasted_iota(jnp.int32, (HEAD_DIM, chunk), 0) == 0, 1.0, 0.0)
        for k in range(tm // chunk):
            vt = proj[k * chunk:(k + 1) * chunk, off:off + KV_WIDTH].T
            for gg in range(N_KV):
                ref[gg, k] = jnp.concatenate(
                    [vt[gg * HEAD_DIM:(gg + 1) * HEAD_DIM], ones_row], axis=0).astype(BF16)

    value_tiles(vs_ref, _OFF_VS, SEL_CHUNK)
    value_tiles(vw_ref, _OFF_VW, TQ)
    gates_t = jax.nn.sigmoid(proj[:, _OFF_G:_OFF_G + N_KV * LANES]).T
    for gg in range(N_KV):
        gate_ref[gg] = gates_t[gg * LANES:gg * LANES + GATE_ROWS]

    @pl.when(i == 0)
    def _():
        ebuf_ref[0:POOL_HALO, :] = jnp.zeros((POOL_HALO, POOL_WIDTH), F32)

    u = proj[:, _OFF_U:_OFF_U + POOL_WIDTH]
    ebuf_ref[POOL_HALO:POOL_HALO + tm, :] = u
    t1 = i * tm + lax.broadcasted_iota(jnp.int32, (tm, POOL_GROUP), 0) + 1
    for gi, w in enumerate(POOL_WINDOWS):
        c0 = gi * POOL_GROUP
        win = u[:, c0:c0 + POOL_GROUP]
        for k in range(1, w):
            win = win + ebuf_ref[POOL_HALO - k:POOL_HALO - k + tm, c0:c0 + POOL_GROUP]
        cnt = jnp.minimum(t1, w).astype(F32)
        pooled = win / cnt - u[:, c0:c0 + POOL_GROUP]
        y = _dot(pooled.astype(BF16), wpool_ref[gi])
        pool_ref[:, c0:c0 + POOL_GROUP] = (y * pscale_ref[:, c0:c0 + POOL_GROUP]).astype(BF16)
    ebuf_ref[0:POOL_HALO, :] = ebuf_ref[tm:tm + POOL_HALO, :]


def _in_proj(x, g1, sc1, sh1, w_in_p, w_pool, pool_scale, qg, ksg, kwg, tm):
    s = x.shape[0]
    row = lambda w: pl.BlockSpec((tm, w), lambda i: (i, 0))
    vec = lambda w: pl.BlockSpec((1, w), lambda i: (0, 0))
    per_head = pl.BlockSpec((N_KV, tm, HEAD_DIM), lambda i: (0, i, 0))
    chunks_t = lambda ch: pl.BlockSpec((N_KV, tm // ch, LANES, ch), lambda i: (0, i, 0, 0))
    out_shape = [
        jax.ShapeDtypeStruct((s, POOL_WIDTH), BF16),
        jax.ShapeDtypeStruct((s, _Q_WIDTH), BF16),
        jax.ShapeDtypeStruct((N_KV, s, HEAD_DIM), F32),
        jax.ShapeDtypeStruct((N_KV, s, HEAD_DIM), F32),
        jax.ShapeDtypeStruct((s, KV_WIDTH), BF16),
        jax.ShapeDtypeStruct((N_KV, s // SEL_CHUNK, LANES, SEL_CHUNK), BF16),
        jax.ShapeDtypeStruct((s, KV_WIDTH), BF16),
        jax.ShapeDtypeStruct((N_KV, s // TQ, LANES, TQ), BF16),
        jax.ShapeDtypeStruct((N_KV, GATE_ROWS, s), F32),
    ]
    return pl.pallas_call(
        functools.partial(_in_proj_kernel, tm=tm),
        grid=(s // tm,),
        in_specs=[row(D_MODEL), vec(D_MODEL), vec(D_MODEL), vec(D_MODEL),
                  _const_spec(w_in_p.shape), _const_spec(w_pool.shape), vec(POOL_WIDTH),
                  vec(LANES), vec(LANES), vec(LANES)],
        out_specs=[row(POOL_WIDTH), row(_Q_WIDTH), per_head, per_head, row(KV_WIDTH),
                   chunks_t(SEL_CHUNK), row(KV_WIDTH), chunks_t(TQ),
                   pl.BlockSpec((N_KV, GATE_ROWS, tm), lambda i: (0, 0, i))],
        out_shape=out_shape,
        scratch_shapes=[pltpu.VMEM((tm + POOL_HALO, POOL_WIDTH), F32)],
        compiler_params=pltpu.CompilerParams(dimension_semantics=("arbitrary",),
                                             vmem_limit_bytes=VMEM_LIMIT_BYTES),
        name="in_proj",
    )(x, g1, sc1, sh1, w_in_p, w_pool, pool_scale, qg, ksg, kwg)


def _compress_kernel(c_ref, pos_ref, w1_ref, b1_ref, w2_ref, b2_ref, gain_ref, o_ref, *,
                     normalize):
    half = CMP_STRIDE * HEAD_DIM
    n_rows = c_ref.shape[1]
    c = c_ref[0]
    first = _dot((c + pos_ref[:, 0:half]).astype(BF16), w1_ref[0:half, :])
    second = _dot((c + pos_ref[:, half:2 * half]).astype(BF16), w1_ref[half:2 * half, :])
    hid = jax.nn.gelu(first + pltpu.roll(second, n_rows - 1, axis=0) + b1_ref[...])
    y = _dot(hid.astype(BF16), w2_ref[...]) + b2_ref[...]
    if normalize:
        ms = jnp.mean(y * y, axis=-1, keepdims=True)
        y = y * lax.rsqrt(ms + EPS) * gain_ref[...]
    o_ref[0] = y.astype(BF16)


def _compress(chunks, pos, w1, b1, w2, b2, gain, normalize):
    _, n_chunks, width = chunks.shape
    vec = lambda w: pl.BlockSpec((1, w), lambda g: (0, 0))
    return pl.pallas_call(
        functools.partial(_compress_kernel, normalize=normalize),
        grid=(N_KV,),
        in_specs=[pl.BlockSpec((1, n_chunks, width), lambda g: (g, 0, 0)),
                  vec(2 * width), _const_spec(w1.shape), vec(CMP_HIDDEN), _const_spec(w2.shape),
                  vec(HEAD_DIM), vec(HEAD_DIM)],
        out_specs=pl.BlockSpec((1, n_chunks, HEAD_DIM), lambda g: (g, 0, 0)),
        out_shape=jax.ShapeDtypeStruct((N_KV, n_chunks, HEAD_DIM), BF16),
        compiler_params=pltpu.CompilerParams(vmem_limit_bytes=VMEM_LIMIT_BYTES),
        name="compress",
    )(chunks, pos, w1, b1, w2, b2, gain)


def _attn_kernel(q_ref, gate_ref, rowt_ref, kc_ref, vct_ref, ks_ref, vst_ref, kw_ref, vwt_ref,
                 o_ref, psum_ref, madd_ref, bias_ref, m_ref, acc_ref, ocmp_ref, owin_ref,
                 flag_ref, list_ref, *, nb):
    g = pl.program_id(0)
    i = pl.program_id(1)
    q0 = i * TQ
    nc = CMP_PER_SEL * nb
    n_chunks = nb // BLOCKS_PER_CHUNK
    gslope = jnp.where(g == 0, LOG2E, LOG2E * 2.0 ** -GQA_GROUP).astype(F32)
    slopes = [gslope * (2.0 ** -(r + 1)) for r in range(GQA_GROUP)]
    heads = [slice(r * TQ, (r + 1) * TQ) for r in range(GQA_GROUP)]

    lane = lax.broadcasted_iota(jnp.int32, (TQ, LANES), 1)
    own_half = (lane >= HEAD_DIM) == (g == 1)
    q_rows = []
    for r in range(GQA_GROUP):
        pair = q_ref[:, (r // 2) * LANES:(r // 2 + 1) * LANES]
        swapped = jnp.concatenate([pair[:, HEAD_DIM:], pair[:, :HEAD_DIM]], axis=1)
        q_rows.append(jnp.where(own_half, jnp.where(g == r % 2, pair, swapped), 0.0))
    q = jnp.concatenate(q_rows, axis=0)
    t_lane = q0 + lax.broadcasted_iota(jnp.int32, (1, TQ), 1)

    w0 = pl.multiple_of(jnp.maximum(q0 - WINDOW, 0), TQ)
    s_w = _dot_nt(kw_ref[pl.ds(w0, WIN_SPAN), :], q)
    row_w = rowt_ref[0:WIN_SPAN, :]
    lane_w = lax.broadcasted_iota(jnp.int32, (WIN_SPAN, TQ), 1).astype(F32)
    dq = (q0 - w0).astype(F32)
    d_w = lane_w - row_w + dq
    ok_w = jnp.abs(d_w - (WINDOW - 1) * 0.5) < WINDOW * 0.5
    rel_w = row_w - dq
    wb = w0 // TQ
    for r in range(GQA_GROUP):
        s = jnp.where(ok_w, s_w[:, heads[r]] + slopes[r] * rel_w, NEG_INF)
        m = jnp.max(s, axis=0, keepdims=True)
        p = jnp.exp2((s - m).astype(BF16))
        o_win = _dot(vwt_ref[0, wb], p[0:TQ])
        for b in range(1, WIN_SPAN // TQ):
            o_win = o_win + _dot(vwt_ref[0, wb + b], p[b * TQ:(b + 1) * TQ])
        owin_ref[:, heads[r]] = o_win

    def compress_and_select(n_rows):
        n_blk = n_rows // CMP_PER_SEL
        rel_c = CMP_STRIDE * rowt_ref[0:n_rows, :] + (CMP_LEN - 1) - q0.astype(F32)
        ok_c = rel_c <= lax.broadcasted_iota(jnp.int32, (1, TQ), 1).astype(F32)
        s_c = _dot_nt(kc_ref[0:n_rows, :], q)
        p_sum = jnp.zeros((n_rows, TQ), F32)
        p_cols = []
        for r in range(GQA_GROUP):
            s = jnp.where(ok_c, s_c[:, heads[r]] + slopes[r] * rel_c, NEG_INF)
            m = jnp.max(s, axis=0, keepdims=True)
            e = jnp.exp2(s - m)
            l = jnp.sum(e, axis=0, keepdims=True)
            p = e * jnp.where(m > 0.5 * NEG_INF, 1.0 / l, 0.0)
            p_sum = p_sum + p
            p_cols.append(p.astype(BF16))
        ocmp_ref[...] = _dot(vct_ref[0, :, 0:n_rows], jnp.concatenate(p_cols, axis=1))

        for h in range(TQ // LANES):
            psum_ref[h, 0:8, :] = jnp.zeros((8, LANES), F32)
            psum_ref[h, 8:8 + n_rows, :] = p_sum[:, h * LANES:(h + 1) * LANES]

        def every4(start):
            parts = [psum_ref[h, pl.ds(8 + start, n_blk, stride=CMP_PER_SEL), :]
                     for h in range(TQ // LANES)]
            return parts[0] if len(parts) == 1 else jnp.concatenate(parts, axis=1)

        imp = every4(0) + every4(1) + every4(2) + 0.5 * every4(3) + 0.5 * every4(-1)
        blk = lax.broadcasted_iota(jnp.int32, (n_blk, TQ), 0)
        cur = lax.shift_right_logical(t_lane, SEL_BLOCK.bit_length() - 1)
        causal = blk <= cur
        forced = jnp.where(blk == 0, 1.0, 0.0) + jnp.where(blk == cur, 1.0, 0.0) \
            + jnp.where(blk == cur - 1, 1.0, 0.0)
        val = jnp.where(causal, jnp.where(forced > 0.0, -2.0, imp), -1.0)

        def pick(_, v):
            blk_f = rowt_ref[0:n_blk, :]
            mx = jnp.max(v, axis=0, keepdims=True)
            idx = jnp.min(jnp.where(v == mx, blk_f, float(n_blk)), axis=0, keepdims=True)
            return jnp.where(blk_f == idx, -2.0, v)

        picked = lax.fori_loop(0, min(N_SEL, n_blk) - N_FORCED, pick, val)
        madd_ref[0:n_blk, :] = jnp.where(causal, jnp.where(picked < -1.5, 0.0, NEG_INF), NEG_INF)
        for c in range(n_blk // BLOCKS_PER_CHUNK):
            rows = madd_ref[c * BLOCKS_PER_CHUNK:(c + 1) * BLOCKS_PER_CHUNK, :]
            flag_ref[c] = (jnp.max(rows) > 0.5 * NEG_INF).astype(jnp.int32)

    rows_per_bucket = nc // CMP_BUCKETS
    bucket = ((q0 + TQ) // CMP_STRIDE - 1) // rows_per_bucket
    for b in range(CMP_BUCKETS):
        pl.when(bucket == b)(functools.partial(compress_and_select, (b + 1) * rows_per_bucket))

    key_row = rowt_ref[0:SEL_CHUNK, :]
    for r in range(GQA_GROUP):
        bias_ref[r] = slopes[r] * key_row
    m_ref[...] = jnp.full(m_ref.shape, NEG_INF, F32)
    acc_ref[...] = jnp.zeros(acc_ref.shape, F32)

    def scores(chunks):
        keys = [ks_ref[pl.ds(pl.multiple_of(c * SEL_CHUNK, SEL_CHUNK), SEL_CHUNK), :]
                for c in chunks]
        s = _dot_nt(jnp.concatenate(keys, axis=0) if len(keys) > 1 else keys[0], q)
        return [s[k * SEL_CHUNK:(k + 1) * SEL_CHUNK] for k in range(len(chunks))]

    def softmax_pv(chunks, s, diagonal):
        madds, rel0 = [], []
        for c in chunks:
            k0 = c * SEL_CHUNK
            madd = jnp.concatenate(
                [jnp.broadcast_to(madd_ref[pl.ds(c * BLOCKS_PER_CHUNK + b, 1), :], (SEL_BLOCK, TQ))
                 for b in range(BLOCKS_PER_CHUNK)], axis=0)
            if diagonal:
                pos = k0 + lax.broadcasted_iota(jnp.int32, (SEL_CHUNK, TQ), 0)
                madd = jnp.where(pos <= t_lane, madd, NEG_INF)
            madds.append(madd)
            rel0.append((k0 - q0).astype(F32))
        p_cols = [[] for _ in chunks]
        alphas = []
        for r in range(GQA_GROUP):
            shifts = [slopes[r] * x for x in rel0]
            us = [sk[:, heads[r]] + bias_ref[r] + mk for sk, mk in zip(s, madds)]
            m_old = m_ref[:, heads[r]]
            m_new = m_old
            for u, sh in zip(us, shifts):
                m_new = jnp.maximum(m_new, jnp.max(u, axis=0, keepdims=True) + sh)
            alphas.append(jnp.exp2(m_old - m_new))
            for k, (u, sh) in enumerate(zip(us, shifts)):
                p_cols[k].append(jnp.exp2((u - (m_new - sh)).astype(BF16)))
            m_ref[:, heads[r]] = m_new
        pv = _dot(vst_ref[0, chunks[0]], jnp.concatenate(p_cols[0], axis=1))
        for k in range(1, len(chunks)):
            pv = pv + _dot(vst_ref[0, chunks[k]], jnp.concatenate(p_cols[k], axis=1))
        acc_ref[...] = jnp.concatenate(alphas, axis=1) * acc_ref[...] + pv

    c_diag = q0 // SEL_CHUNK

    def compact(c, n):
        list_ref[n] = c
        return n + flag_ref[c]

    n_active = lax.fori_loop(0, c_diag, compact, jnp.int32(0))

    def active_group(first, size):
        chunks = [list_ref[first + k] for k in range(size)]
        softmax_pv(chunks, scores(chunks), diagonal=False)

    def group_body(p, carry):
        active_group(p * SEL_GROUP, SEL_GROUP)
        return carry

    n_full = n_active // SEL_GROUP
    lax.fori_loop(0, n_full, group_body, 0)
    done = n_full * SEL_GROUP
    size = SEL_GROUP // 2
    while size >= 1:
        pl.when(((n_active - done) & size) != 0)(functools.partial(active_group, done, size))
        done = done + ((n_active - done) & size)
        size //= 2
    softmax_pv([c_diag], scores([c_diag]), diagonal=True)

    head_out = []
    for r in range(GQA_GROUP):
        gc = gate_ref[0, 3 * r + 0:3 * r + 1, :]
        gs = gate_ref[0, 3 * r + 1:3 * r + 2, :]
        gw = gate_ref[0, 3 * r + 2:3 * r + 3, :]
        acc = acc_ref[:, heads[r]]
        win = owin_ref[:, heads[r]]
        out_t = (gc * ocmp_ref[:, heads[r]]
                 + (gs / acc[HEAD_DIM:HEAD_DIM + 1]) * acc
                 + (gw / win[HEAD_DIM:HEAD_DIM + 1]) * win)
        head_out.append(out_t.T[:, 0:HEAD_DIM])
    for pair in range(GQA_GROUP // 2):
        o_ref[:, pair * LANES:(pair + 1) * LANES] = jnp.concatenate(
            head_out[2 * pair:2 * pair + 2], axis=1).astype(BF16)


def _attention(q, gates_t, kcmp, vcmp_t, ksel, vsel_t, kwin, vwin_t):
    s = q.shape[0]
    nb = s // SEL_BLOCK
    nc = CMP_PER_SEL * nb
    gw = GQA_GROUP * HEAD_DIM
    n_rows = max(nc, WIN_SPAN)
    row_tile = jnp.asarray(np.broadcast_to(np.arange(n_rows)[:, None], (n_rows, TQ))
                           .astype(np.float32))
    per_group = lambda a: pl.BlockSpec((1,) + a.shape[1:], lambda g, i: (g,) + (0,) * (a.ndim - 1),
                                       pipeline_mode=pl.Buffered(1))
    return pl.pallas_call(
        functools.partial(_attn_kernel, nb=nb),
        grid=(N_KV, s // TQ),
        in_specs=[pl.BlockSpec((TQ, gw), lambda g, i: (i, g)),
                  pl.BlockSpec((1, GATE_ROWS, TQ), lambda g, i: (g, 0, i)),
                  _const_spec(row_tile.shape),
                  _const_spec(kcmp.shape), per_group(vcmp_t),
                  _const_spec(ksel.shape), per_group(vsel_t),
                  _const_spec(kwin.shape), per_group(vwin_t)],
        out_specs=pl.BlockSpec((TQ, gw), lambda g, i: (i, g)),
        out_shape=jax.ShapeDtypeStruct((s, N_HEADS * HEAD_DIM), BF16),
        scratch_shapes=[pltpu.VMEM((TQ // LANES, 8 + nc, LANES), F32),
                        pltpu.VMEM((nb, TQ), F32),
                        pltpu.VMEM((GQA_GROUP, SEL_CHUNK, TQ), F32),
                        pltpu.VMEM((1, GQA_GROUP * TQ), F32),
                        pltpu.VMEM((LANES, GQA_GROUP * TQ), F32),
                        pltpu.VMEM((LANES, GQA_GROUP * TQ), F32),
                        pltpu.VMEM((LANES, GQA_GROUP * TQ), F32),
                        pltpu.SMEM((nb // BLOCKS_PER_CHUNK,), jnp.int32),
                        pltpu.SMEM((nb // BLOCKS_PER_CHUNK,), jnp.int32)],
        compiler_params=pltpu.CompilerParams(dimension_semantics=("arbitrary", "arbitrary"),
                                             vmem_limit_bytes=VMEM_LIMIT_BYTES),
        name="attn",
    )(q, gates_t, row_tile, kcmp, vcmp_t, ksel, vsel_t, kwin, vwin_t)


def _out_mlp_kernel(pool_ref, attn_ref, wout_ref, x_ref, ga1_ref, g_ref, sc_ref, sh_ref, w1_ref,
                    w2_ref, ga2_ref, o_ref, *, ff_chunk):
    mix = (_dot(pool_ref[...], wout_ref[0:POOL_WIDTH, :])
           + _dot(attn_ref[...], wout_ref[POOL_WIDTH:, :]))
    x1 = x_ref[...] + ga1_ref[...] * mix
    h = _rms_modulate(x1, g_ref[...], sc_ref[...], sh_ref[...]).astype(BF16)
    acc = jnp.zeros(x1.shape, F32)
    for c in range(D_FF // ff_chunk):
        a = _dot(h, w1_ref[:, c * ff_chunk:(c + 1) * ff_chunk])
        a = jnp.square(jnp.maximum(a, 0.0)).astype(BF16)
        acc = acc + _dot(a, w2_ref[c * ff_chunk:(c + 1) * ff_chunk, :])
    o_ref[...] = x1 + ga2_ref[...] * acc


def _out_mlp(pool_out, attn_out, w_out, x, ga1, g2, sc2, sh2, w1, w2, ga2, tm):
    s = x.shape[0]
    row = lambda w: pl.BlockSpec((tm, w), lambda i: (i, 0))
    vec = lambda w: pl.BlockSpec((1, w), lambda i: (0, 0))
    return pl.pallas_call(
        functools.partial(_out_mlp_kernel, ff_chunk=1024),
        grid=(s // tm,),
        in_specs=[row(POOL_WIDTH), row(N_HEADS * HEAD_DIM), _const_spec(w_out.shape),
                  row(D_MODEL), vec(D_MODEL), vec(D_MODEL), vec(D_MODEL), vec(D_MODEL),
                  _const_spec(w1.shape), _const_spec(w2.shape), vec(D_MODEL)],
        out_specs=row(D_MODEL),
        out_shape=jax.ShapeDtypeStruct((s, D_MODEL), F32),
        compiler_params=pltpu.CompilerParams(vmem_limit_bytes=VMEM_LIMIT_BYTES),
        name="out_mlp",
    )(pool_out, attn_out, w_out, x, ga1, g2, sc2, sh2, w1, w2, ga2)


def _pad_in_proj_weight(w_in):
    src_g = _OFF_G
    per_group = GQA_GROUP * N_BRANCH
    cols = [w_in[:, :src_g]]
    for gg in range(N_KV):
        cols += [w_in[:, src_g + gg * per_group:src_g + (gg + 1) * per_group],
                 jnp.zeros((D_MODEL, LANES - per_group), w_in.dtype)]
    return jnp.concatenate(cols, axis=1).astype(BF16)


def _group_lanes(a):
    return a.transpose(1, 0, 2).reshape(a.shape[1], KV_WIDTH)


def kernel(x, c, w_ada, b_ada, norm1_g, norm2_g, w_in, w_pool, pool_scale, q_gain, kc_gain,
           ks_gain, kw_gain, cmp_pos_k, cmp_w1_k, cmp_b1_k, cmp_w2_k, cmp_b2_k, cmp_pos_v,
           cmp_w1_v, cmp_b1_v, cmp_w2_v, cmp_b2_v, w_out, w_ff1, w_ff2):
    batch, s, _ = x.shape
    assert batch == 1 and w_ada.shape[0] == 1
    assert s % SEL_CHUNK == 0 and s >= WIN_SPAN and (s // SEL_BLOCK) & (s // SEL_BLOCK - 1) == 0
    tm = min(s, 512)
    x2 = x[0]

    mod = _ada(jnp.broadcast_to(c, (8, D_MODEL)), w_ada[0], b_ada)[0:1]
    sh1, sc1, ga1, sh2, sc2, ga2 = [mod[:, k * D_MODEL:(k + 1) * D_MODEL] for k in range(6)]

    pair = lambda gain: jnp.tile(gain, (1, 2))
    (pool_out, q, kc, vc, ksel, vsel_t, kwin, vwin_t, gates_t) = _in_proj(
        x2, norm1_g, sc1, sh1, _pad_in_proj_weight(w_in[0]), w_pool[0].astype(BF16), pool_scale,
        pair(q_gain), pair(ks_gain), pair(kw_gain), tm)

    chunks = lambda a: a.reshape(N_KV, s // CMP_STRIDE, CMP_STRIDE * HEAD_DIM)
    kcmp = _compress(chunks(kc), cmp_pos_k.reshape(1, -1), cmp_w1_k[0].astype(BF16), cmp_b1_k,
                     cmp_w2_k[0].astype(BF16), cmp_b2_k, kc_gain, True)
    vcmp = _compress(chunks(vc), cmp_pos_v.reshape(1, -1), cmp_w1_v[0].astype(BF16), cmp_b1_v,
                     cmp_w2_v[0].astype(BF16), cmp_b2_v, kc_gain, False)

    vcmp_t = jnp.pad(vcmp.transpose(0, 2, 1), ((0, 0), (0, LANES - HEAD_DIM), (0, 0)))
    attn_out = _attention(q, gates_t, _group_lanes(kcmp), vcmp_t, ksel, vsel_t, kwin, vwin_t)

    out = _out_mlp(pool_out, attn_out, w_out[0].astype(BF16), x2, ga1, norm2_g, sc2, sh2,
                   w_ff1[0].astype(BF16), w_ff2[0].astype(BF16), ga2, tm)
    return out[None]
```

```python
import functools

import jax
import jax.numpy as jnp
import numpy as np
from jax import lax
from jax.experimental import pallas as pl
from jax.experimental.pallas import tpu as pltpu

F32 = jnp.float32
BF16 = jnp.bfloat16

LANES = 128
VMEM_LIMIT_BYTES = 56 * 1024 * 1024

D_MODEL = 1024
POOL_WIDTH = 512
POOL_WINDOWS = (2, 4, 8, 16)
POOL_GROUP = POOL_WIDTH // len(POOL_WINDOWS)
POOL_HALO = 16
HEAD_DIM = 64
N_HEADS = 8
N_KV = 2
GQA_GROUP = N_HEADS // N_KV
KV_WIDTH = N_KV * HEAD_DIM
N_BRANCH = 3
CMP_LEN = 32
CMP_STRIDE = 16
CMP_HIDDEN = 4 * HEAD_DIM
SEL_BLOCK = 64
N_SEL = 16
WINDOW = 512
D_FF = 4 * D_MODEL
NEG_INF = -1e30
N_FORCED = 3
EPS = 1e-6
LOG2E = 1.4426950408889634

TQ = 256
SEL_CHUNK = 256
BLOCKS_PER_CHUNK = SEL_CHUNK // SEL_BLOCK
SEL_GROUP = 4
CMP_BUCKETS = 4
WIN_SPAN = WINDOW + TQ
CMP_PER_SEL = SEL_BLOCK // CMP_STRIDE
GATE_ROWS = 16
V_ROWS = HEAD_DIM + 16

_Q_WIDTH = N_HEADS * HEAD_DIM
_OFF_U = 0
_OFF_Q = POOL_WIDTH
_OFF_KC = _OFF_Q + _Q_WIDTH
_OFF_VC = _OFF_KC + KV_WIDTH
_OFF_KS = _OFF_VC + KV_WIDTH
_OFF_VS = _OFF_KS + KV_WIDTH
_OFF_KW = _OFF_VS + KV_WIDTH
_OFF_VW = _OFF_KW + KV_WIDTH
_OFF_G = _OFF_VW + KV_WIDTH
_IN_PAD = _OFF_G + N_KV * LANES


def _dot(a, b):
    return jnp.dot(a, b, preferred_element_type=F32)


def _dot_nt(a, b):
    return lax.dot_general(a, b, (((1,), (1,)), ((), ())), preferred_element_type=F32)


def _const_spec(shape):
    nd = len(shape)
    return pl.BlockSpec(shape, lambda *_: (0,) * nd, pipeline_mode=pl.Buffered(1))


def _ada_kernel(c_ref, w_ref, b_ref, o_ref):
    o_ref[...] = jnp.dot(c_ref[...], w_ref[...], preferred_element_type=F32,
                         precision=lax.Precision.HIGHEST) + b_ref[...]


def _ada(c8, w, b):
    n = w.shape[1]
    bn = 2048
    return pl.pallas_call(
        _ada_kernel,
        grid=(n // bn,),
        in_specs=[pl.BlockSpec((8, D_MODEL), lambda j: (0, 0)),
                  pl.BlockSpec((D_MODEL, bn), lambda j: (0, j)),
                  pl.BlockSpec((1, bn), lambda j: (0, j))],
        out_specs=pl.BlockSpec((8, bn), lambda j: (0, j)),
        out_shape=jax.ShapeDtypeStruct((8, n), F32),
        compiler_params=pltpu.CompilerParams(vmem_limit_bytes=VMEM_LIMIT_BYTES),
        name="ada",
    )(c8, w, b)


def _rms_modulate(x, g, sc, sh):
    ms = jnp.mean(x * x, axis=-1, keepdims=True)
    return (x * lax.rsqrt(ms + EPS)) * (g * (1.0 + sc)) + sh


def _head_norm_pair(x, gain2):
    lane = lax.broadcasted_iota(jnp.int32, x.shape, 1)
    lo = lane < HEAD_DIM
    sq = x * x
    s_lo = jnp.sum(jnp.where(lo, sq, 0.0), axis=-1, keepdims=True)
    s_hi = jnp.sum(jnp.where(lo, 0.0, sq), axis=-1, keepdims=True)
    ms = jnp.where(lo, s_lo, s_hi) * (1.0 / HEAD_DIM)
    return x * lax.rsqrt(ms + EPS) * gain2


def _in_proj_kernel(x_ref, g_ref, sc_ref, sh_ref, w_ref, wpool_ref, pscale_ref, qg_ref, ksg_ref,
                    kwg_ref, pool_ref, q_ref, kc_ref, vc_ref, ks_ref, vs_ref, kw_ref, vw_ref,
                    gate_ref, ebuf_ref, *, tm):
    i = pl.program_id(0)
    h = _rms_modulate(x_ref[...], g_ref[...], sc_ref[...], sh_ref[...]).astype(BF16)

    def project(off, width):
        return _dot(h, w_ref[:, off:off + width])

    q_all = project(_OFF_Q, _Q_WIDTH)
    for pair in range(N_HEADS // 2):
        qn = _head_norm_pair(q_all[:, pair * LANES:(pair + 1) * LANES], qg_ref[...])
        q_ref[:, pair * LANES:(pair + 1) * LANES] = (qn * (HEAD_DIM ** -0.5 * LOG2E)).astype(BF16)

    kv = project(_OFF_KC, _IN_PAD - _OFF_KC)
    kv_cols = lambda off, width: kv[:, off - _OFF_KC:off - _OFF_KC + width]
    for gg in range(N_KV):
        kc_ref[gg] = kv_cols(_OFF_KC + gg * HEAD_DIM, HEAD_DIM)
        vc_ref[gg] = kv_cols(_OFF_VC + gg * HEAD_DIM, HEAD_DIM)
    ks_ref[...] = _head_norm_pair(kv_cols(_OFF_KS, KV_WIDTH), ksg_ref[...]).astype(BF16)
    kw_ref[...] = _head_norm_pair(kv_cols(_OFF_KW, KV_WIDTH), kwg_ref[...]).astype(BF16)
    def value_tiles(ref, off, chunk):
        ones_row = jnp.where(lax.broadcasted_iota(jnp.int32, (V_ROWS - HEAD_DIM, chunk), 0) == 0,
                             1.0, 0.0)
        for k in range(tm // chunk):
            vt = kv_cols(off, KV_WIDTH)[k * chunk:(k + 1) * chunk].T
            for gg in range(N_KV):
                ref[gg, k] = jnp.concatenate(
                    [vt[gg * HEAD_DIM:(gg + 1) * HEAD_DIM], ones_row], axis=0).astype(BF16)

    value_tiles(vs_ref, _OFF_VS, SEL_CHUNK)
    value_tiles(vw_ref, _OFF_VW, TQ)
    gates_t = jax.nn.sigmoid(kv_cols(_OFF_G, N_KV * LANES)).T
    for gg in range(N_KV):
        gate_ref[gg] = gates_t[gg * LANES:gg * LANES + GATE_ROWS]

    @pl.when(i == 0)
    def _():
        ebuf_ref[0:POOL_HALO, :] = jnp.zeros((POOL_HALO, POOL_WIDTH), F32)

    u = project(_OFF_U, POOL_WIDTH)
    ebuf_ref[POOL_HALO:POOL_HALO + tm, :] = u
    t1 = i * tm + lax.broadcasted_iota(jnp.int32, (tm, POOL_GROUP), 0) + 1
    for gi, w in enumerate(POOL_WINDOWS):
        c0 = gi * POOL_GROUP
        win = u[:, c0:c0 + POOL_GROUP]
        for k in range(1, w):
            win = win + ebuf_ref[POOL_HALO - k:POOL_HALO - k + tm, c0:c0 + POOL_GROUP]
        cnt = jnp.minimum(t1, w).astype(F32)
        pooled = win / cnt - u[:, c0:c0 + POOL_GROUP]
        y = _dot(pooled.astype(BF16), wpool_ref[gi])
        pool_ref[:, c0:c0 + POOL_GROUP] = (y * pscale_ref[:, c0:c0 + POOL_GROUP]).astype(BF16)
    ebuf_ref[0:POOL_HALO, :] = ebuf_ref[tm:tm + POOL_HALO, :]


def _in_proj(x, g1, sc1, sh1, w_in_p, w_pool, pool_scale, qg, ksg, kwg, tm):
    s = x.shape[0]
    row = lambda w: pl.BlockSpec((tm, w), lambda i: (i, 0))
    vec = lambda w: pl.BlockSpec((1, w), lambda i: (0, 0))
    per_head = pl.BlockSpec((N_KV, tm, HEAD_DIM), lambda i: (0, i, 0))
    chunks_t = lambda ch: pl.BlockSpec((N_KV, tm // ch, V_ROWS, ch), lambda i: (0, i, 0, 0))
    out_shape = [
        jax.ShapeDtypeStruct((s, POOL_WIDTH), BF16),
        jax.ShapeDtypeStruct((s, _Q_WIDTH), BF16),
        jax.ShapeDtypeStruct((N_KV, s, HEAD_DIM), F32),
        jax.ShapeDtypeStruct((N_KV, s, HEAD_DIM), F32),
        jax.ShapeDtypeStruct((s, KV_WIDTH), BF16),
        jax.ShapeDtypeStruct((N_KV, s // SEL_CHUNK, V_ROWS, SEL_CHUNK), BF16),
        jax.ShapeDtypeStruct((s, KV_WIDTH), BF16),
        jax.ShapeDtypeStruct((N_KV, s // TQ, V_ROWS, TQ), BF16),
        jax.ShapeDtypeStruct((N_KV, GATE_ROWS, s), F32),
    ]
    return pl.pallas_call(
        functools.partial(_in_proj_kernel, tm=tm),
        grid=(s // tm,),
        in_specs=[row(D_MODEL), vec(D_MODEL), vec(D_MODEL), vec(D_MODEL),
                  _const_spec(w_in_p.shape), _const_spec(w_pool.shape), vec(POOL_WIDTH),
                  vec(LANES), vec(LANES), vec(LANES)],
        out_specs=[row(POOL_WIDTH), row(_Q_WIDTH), per_head, per_head, row(KV_WIDTH),
                   chunks_t(SEL_CHUNK), row(KV_WIDTH), chunks_t(TQ),
                   pl.BlockSpec((N_KV, GATE_ROWS, tm), lambda i: (0, 0, i))],
        out_shape=out_shape,
        scratch_shapes=[pltpu.VMEM((tm + POOL_HALO, POOL_WIDTH), F32)],
        compiler_params=pltpu.CompilerParams(dimension_semantics=("arbitrary",),
                                             vmem_limit_bytes=VMEM_LIMIT_BYTES),
        name="in_proj",
    )(x, g1, sc1, sh1, w_in_p, w_pool, pool_scale, qg, ksg, kwg)


def _compress_kernel(c_ref, pos_ref, w1_ref, b1_ref, w2_ref, b2_ref, gain_ref, o_ref, *,
                     normalize):
    half = CMP_STRIDE * HEAD_DIM
    n_rows = c_ref.shape[1]
    c = c_ref[0]
    first = _dot((c + pos_ref[:, 0:half]).astype(BF16), w1_ref[0:half, :])
    second = _dot((c + pos_ref[:, half:2 * half]).astype(BF16), w1_ref[half:2 * half, :])
    hid = jax.nn.gelu(first + pltpu.roll(second, n_rows - 1, axis=0) + b1_ref[...])
    y = _dot(hid.astype(BF16), w2_ref[...]) + b2_ref[...]
    if normalize:
        ms = jnp.mean(y * y, axis=-1, keepdims=True)
        y = y * lax.rsqrt(ms + EPS) * gain_ref[...]
    o_ref[0] = y.astype(BF16)


def _compress(chunks, pos, w1, b1, w2, b2, gain, normalize):
    _, n_chunks, width = chunks.shape
    vec = lambda w: pl.BlockSpec((1, w), lambda g: (0, 0))
    return pl.pallas_call(
        functools.partial(_compress_kernel, normalize=normalize),
        grid=(N_KV,),
        in_specs=[pl.BlockSpec((1, n_chunks, width), lambda g: (g, 0, 0)),
                  vec(2 * width), _const_spec(w1.shape), vec(CMP_HIDDEN), _const_spec(w2.shape),
                  vec(HEAD_DIM), vec(HEAD_DIM)],
        out_specs=pl.BlockSpec((1, n_chunks, HEAD_DIM), lambda g: (g, 0, 0)),
        out_shape=jax.ShapeDtypeStruct((N_KV, n_chunks, HEAD_DIM), BF16),
        compiler_params=pltpu.CompilerParams(vmem_limit_bytes=VMEM_LIMIT_BYTES),
        name="compress",
    )(chunks, pos, w1, b1, w2, b2, gain)


def _attn_kernel(q_ref, gate_ref, rowt_ref, kc_ref, vct_ref, ks_ref, vst_ref, kw_ref, vwt_ref,
                 o_ref, psum_ref, madd_ref, bias_ref, m_ref, acc_ref, ocmp_ref, owin_ref,
                 flag_ref, list_ref, *, nb):
    g = pl.program_id(0)
    i = pl.program_id(1)
    q0 = i * TQ
    nc = CMP_PER_SEL * nb
    n_chunks = nb // BLOCKS_PER_CHUNK
    gslope = jnp.where(g == 0, LOG2E, LOG2E * 2.0 ** -GQA_GROUP).astype(F32)
    slopes = [gslope * (2.0 ** -(r + 1)) for r in range(GQA_GROUP)]
    heads = [slice(r * TQ, (r + 1) * TQ) for r in range(GQA_GROUP)]

    lane = lax.broadcasted_iota(jnp.int32, (TQ, LANES), 1)
    own_half = (lane >= HEAD_DIM) == (g == 1)
    q_rows = []
    for r in range(GQA_GROUP):
        pair = q_ref[:, (r // 2) * LANES:(r // 2 + 1) * LANES]
        swapped = jnp.concatenate([pair[:, HEAD_DIM:], pair[:, :HEAD_DIM]], axis=1)
        q_rows.append(jnp.where(own_half, jnp.where(g == r % 2, pair, swapped), 0.0))
    q = jnp.concatenate(q_rows, axis=0)
    t_lane = q0 + lax.broadcasted_iota(jnp.int32, (1, TQ), 1)

    w0 = pl.multiple_of(jnp.maximum(q0 - WINDOW, 0), TQ)
    s_w = _dot_nt(kw_ref[pl.ds(w0, WIN_SPAN), :], q)
    row_w = rowt_ref[0:WIN_SPAN, :]
    lane_w = lax.broadcasted_iota(jnp.int32, (WIN_SPAN, TQ), 1).astype(F32)
    dq = (q0 - w0).astype(F32)
    d_w = lane_w - row_w + dq
    ok_w = jnp.abs(d_w - (WINDOW - 1) * 0.5) < WINDOW * 0.5
    rel_w = row_w - dq
    wb = w0 // TQ
    for r in range(GQA_GROUP):
        s = jnp.where(ok_w, s_w[:, heads[r]] + slopes[r] * rel_w, NEG_INF)
        m = jnp.max(s, axis=0, keepdims=True)
        p = jnp.exp2((s - m).astype(BF16))
        o_win = _dot(vwt_ref[0, wb], p[0:TQ])
        for b in range(1, WIN_SPAN // TQ):
            o_win = o_win + _dot(vwt_ref[0, wb + b], p[b * TQ:(b + 1) * TQ])
        owin_ref[:, heads[r]] = o_win

    def compress_and_select(n_rows):
        n_blk = n_rows // CMP_PER_SEL
        rel_c = CMP_STRIDE * rowt_ref[0:n_rows, :] + (CMP_LEN - 1) - q0.astype(F32)
        ok_c = rel_c <= lax.broadcasted_iota(jnp.int32, (1, TQ), 1).astype(F32)
        s_c = _dot_nt(kc_ref[0:n_rows, :], q)
        p_sum = jnp.zeros((n_rows, TQ), F32)
        p_cols = []
        for r in range(GQA_GROUP):
            s = jnp.where(ok_c, s_c[:, heads[r]] + slopes[r] * rel_c, NEG_INF)
            m = jnp.max(s, axis=0, keepdims=True)
            e = jnp.exp2(s - m)
            l = jnp.sum(e, axis=0, keepdims=True)
            p = e * jnp.where(m > 0.5 * NEG_INF, 1.0 / l, 0.0)
            p_sum = p_sum + p
            p_cols.append(p.astype(BF16))
        ocmp_ref[...] = _dot(vct_ref[0, :, 0:n_rows], jnp.concatenate(p_cols, axis=1))

        for h in range(TQ // LANES):
            psum_ref[h, 0:8, :] = jnp.zeros((8, LANES), F32)
            psum_ref[h, 8:8 + n_rows, :] = p_sum[:, h * LANES:(h + 1) * LANES]

        def every4(start):
            parts = [psum_ref[h, pl.ds(8 + start, n_blk, stride=CMP_PER_SEL), :]
                     for h in range(TQ // LANES)]
            return parts[0] if len(parts) == 1 else jnp.concatenate(parts, axis=1)

        imp = every4(0) + every4(1) + every4(2) + 0.5 * every4(3) + 0.5 * every4(-1)
        blk = lax.broadcasted_iota(jnp.int32, (n_blk, TQ), 0)
        cur = lax.shift_right_logical(t_lane, SEL_BLOCK.bit_length() - 1)
        causal = blk <= cur
        forced = jnp.where(blk == 0, 1.0, 0.0) + jnp.where(blk == cur, 1.0, 0.0) \
            + jnp.where(blk == cur - 1, 1.0, 0.0)
        val = jnp.where(causal, jnp.where(forced > 0.0, -2.0, imp), -1.0)

        def pick(_, v):
            blk_f = rowt_ref[0:n_blk, :]
            mx = jnp.max(v, axis=0, keepdims=True)
            idx = jnp.min(jnp.where(v == mx, blk_f, float(n_blk)), axis=0, keepdims=True)
            return jnp.where(blk_f == idx, -2.0, v)

        picked = lax.fori_loop(0, min(N_SEL, n_blk) - N_FORCED, pick, val)
        madd_ref[0:n_blk, :] = jnp.where(causal, jnp.where(picked < -1.5, 0.0, NEG_INF), NEG_INF)
        for c in range(n_blk // BLOCKS_PER_CHUNK):
            rows = madd_ref[c * BLOCKS_PER_CHUNK:(c + 1) * BLOCKS_PER_CHUNK, :]
            flag_ref[c] = (jnp.max(rows) > 0.5 * NEG_INF).astype(jnp.int32)

    rows_per_bucket = nc // CMP_BUCKETS
    bucket = ((q0 + TQ) // CMP_STRIDE - 1) // rows_per_bucket
    for b in range(CMP_BUCKETS):
        pl.when(bucket == b)(functools.partial(compress_and_select, (b + 1) * rows_per_bucket))

    key_row = rowt_ref[0:SEL_CHUNK, :]
    for r in range(GQA_GROUP):
        bias_ref[r] = slopes[r] * key_row
    m_ref[...] = jnp.full(m_ref.shape, NEG_INF, F32)
    acc_ref[...] = jnp.zeros(acc_ref.shape, F32)

    def scores(chunks):
        keys = [ks_ref[pl.ds(pl.multiple_of(c * SEL_CHUNK, SEL_CHUNK), SEL_CHUNK), :]
                for c in chunks]
        s = _dot_nt(jnp.concatenate(keys, axis=0) if len(keys) > 1 else keys[0], q)
        return [s[k * SEL_CHUNK:(k + 1) * SEL_CHUNK] for k in range(len(chunks))]

    def softmax_pv(chunks, s, diagonal):
        madds, rel0 = [], []
        for c in chunks:
            k0 = c * SEL_CHUNK
            madd = jnp.concatenate(
                [jnp.broadcast_to(madd_ref[pl.ds(c * BLOCKS_PER_CHUNK + b, 1), :], (SEL_BLOCK, TQ))
                 for b in range(BLOCKS_PER_CHUNK)], axis=0)
            if diagonal:
                pos = k0 + lax.broadcasted_iota(jnp.int32, (SEL_CHUNK, TQ), 0)
                madd = jnp.where(pos <= t_lane, madd, NEG_INF)
            madds.append(madd)
            rel0.append((k0 - q0).astype(F32))
        p_cols = [[] for _ in chunks]
        alphas = []
        for r in range(GQA_GROUP):
            shifts = [slopes[r] * x for x in rel0]
            us = [sk[:, heads[r]] + bias_ref[r] + mk for sk, mk in zip(s, madds)]
            m_old = m_ref[:, heads[r]]
            m_new = m_old
            for u, sh in zip(us, shifts):
                m_new = jnp.maximum(m_new, jnp.max(u, axis=0, keepdims=True) + sh)
            alphas.append(jnp.exp2(m_old - m_new))
            for k, (u, sh) in enumerate(zip(us, shifts)):
                p_cols[k].append(jnp.exp2((u - (m_new - sh)).astype(BF16)))
            m_ref[:, heads[r]] = m_new
        pv = _dot(vst_ref[0, chunks[0]], jnp.concatenate(p_cols[0], axis=1))
        for k in range(1, len(chunks)):
            pv = pv + _dot(vst_ref[0, chunks[k]], jnp.concatenate(p_cols[k], axis=1))
        acc_ref[...] = jnp.concatenate(alphas, axis=1) * acc_ref[...] + pv

    c_diag = q0 // SEL_CHUNK

    def compact(c, n):
        list_ref[n] = c
        return n + flag_ref[c]

    n_active = lax.fori_loop(0, c_diag, compact, jnp.int32(0))

    def active_group(first, size):
        chunks = [list_ref[first + k] for k in range(size)]
        softmax_pv(chunks, scores(chunks), diagonal=False)

    def group_body(p, carry):
        active_group(p * SEL_GROUP, SEL_GROUP)
        return carry

    n_full = n_active // SEL_GROUP
    lax.fori_loop(0, n_full, group_body, 0)
    done = n_full * SEL_GROUP
    size = SEL_GROUP // 2
    while size >= 1:
        pl.when(((n_active - done) & size) != 0)(functools.partial(active_group, done, size))
        done = done + ((n_active - done) & size)
        size //= 2
    diag_chunks = [c_diag + k for k in range(TQ // SEL_CHUNK)]
    softmax_pv(diag_chunks, scores(diag_chunks), diagonal=True)

    head_out = []
    for r in range(GQA_GROUP):
        gc = gate_ref[0, 3 * r + 0:3 * r + 1, :]
        gs = gate_ref[0, 3 * r + 1:3 * r + 2, :]
        gw = gate_ref[0, 3 * r + 2:3 * r + 3, :]
        acc = acc_ref[:, heads[r]]
        win = owin_ref[:, heads[r]]
        out_t = (gc * ocmp_ref[:, heads[r]]
                 + (gs / acc[HEAD_DIM:HEAD_DIM + 1]) * acc
                 + (gw / win[HEAD_DIM:HEAD_DIM + 1]) * win)
        head_out.append(out_t.T[:, 0:HEAD_DIM])
    for pair in range(GQA_GROUP // 2):
        o_ref[:, pair * LANES:(pair + 1) * LANES] = jnp.concatenate(
            head_out[2 * pair:2 * pair + 2], axis=1).astype(BF16)


def _attention(q, gates_t, kcmp, vcmp_t, ksel, vsel_t, kwin, vwin_t):
    s = q.shape[0]
    nb = s // SEL_BLOCK
    nc = CMP_PER_SEL * nb
    gw = GQA_GROUP * HEAD_DIM
    n_rows = max(nc, WIN_SPAN)
    row_tile = jnp.asarray(np.broadcast_to(np.arange(n_rows)[:, None], (n_rows, TQ))
                           .astype(np.float32))
    per_group = lambda a: pl.BlockSpec((1,) + a.shape[1:], lambda g, i: (g,) + (0,) * (a.ndim - 1),
                                       pipeline_mode=pl.Buffered(1))
    return pl.pallas_call(
        functools.partial(_attn_kernel, nb=nb),
        grid=(N_KV, s // TQ),
        in_specs=[pl.BlockSpec((TQ, gw), lambda g, i: (i, g)),
                  pl.BlockSpec((1, GATE_ROWS, TQ), lambda g, i: (g, 0, i)),
                  _const_spec(row_tile.shape),
                  _const_spec(kcmp.shape), per_group(vcmp_t),
                  _const_spec(ksel.shape), per_group(vsel_t),
                  _const_spec(kwin.shape), per_group(vwin_t)],
        out_specs=pl.BlockSpec((TQ, gw), lambda g, i: (i, g)),
        out_shape=jax.ShapeDtypeStruct((s, N_HEADS * HEAD_DIM), BF16),
        scratch_shapes=[pltpu.VMEM((TQ // LANES, 8 + nc, LANES), F32),
                        pltpu.VMEM((nb, TQ), F32),
                        pltpu.VMEM((GQA_GROUP, SEL_CHUNK, TQ), F32),
                        pltpu.VMEM((1, GQA_GROUP * TQ), F32),
                        pltpu.VMEM((V_ROWS, GQA_GROUP * TQ), F32),
                        pltpu.VMEM((V_ROWS, GQA_GROUP * TQ), F32),
                        pltpu.VMEM((V_ROWS, GQA_GROUP * TQ), F32),
                        pltpu.SMEM((nb // BLOCKS_PER_CHUNK,), jnp.int32),
                        pltpu.SMEM((nb // BLOCKS_PER_CHUNK,), jnp.int32)],
        compiler_params=pltpu.CompilerParams(dimension_semantics=("arbitrary", "arbitrary"),
                                             vmem_limit_bytes=VMEM_LIMIT_BYTES),
        name="attn",
    )(q, gates_t, row_tile, kcmp, vcmp_t, ksel, vsel_t, kwin, vwin_t)


def _out_mlp_kernel(pool_ref, attn_ref, wout_ref, x_ref, ga1_ref, g_ref, sc_ref, sh_ref, w1_ref,
                    w2_ref, ga2_ref, o_ref, *, ff_chunk):
    mix = (_dot(pool_ref[...], wout_ref[0:POOL_WIDTH, :])
           + _dot(attn_ref[...], wout_ref[POOL_WIDTH:, :]))
    x1 = x_ref[...] + ga1_ref[...] * mix
    h = _rms_modulate(x1, g_ref[...], sc_ref[...], sh_ref[...]).astype(BF16)
    acc = jnp.zeros(x1.shape, F32)
    for c in range(D_FF // ff_chunk):
        a = _dot(h, w1_ref[:, c * ff_chunk:(c + 1) * ff_chunk])
        a = jnp.square(jnp.maximum(a, 0.0)).astype(BF16)
        acc = acc + _dot(a, w2_ref[c * ff_chunk:(c + 1) * ff_chunk, :])
    o_ref[...] = x1 + ga2_ref[...] * acc


def _out_mlp(pool_out, attn_out, w_out, x, ga1, g2, sc2, sh2, w1, w2, ga2, tm):
    s = x.shape[0]
    row = lambda w: pl.BlockSpec((tm, w), lambda i: (i, 0))
    vec = lambda w: pl.BlockSpec((1, w), lambda i: (0, 0))
    return pl.pallas_call(
        functools.partial(_out_mlp_kernel, ff_chunk=1024),
        grid=(s // tm,),
        in_specs=[row(POOL_WIDTH), row(N_HEADS * HEAD_DIM), _const_spec(w_out.shape),
                  row(D_MODEL), vec(D_MODEL), vec(D_MODEL), vec(D_MODEL), vec(D_MODEL),
                  _const_spec(w1.shape), _const_spec(w2.shape), vec(D_MODEL)],
        out_specs=row(D_MODEL),
        out_shape=jax.ShapeDtypeStruct((s, D_MODEL), F32),
        compiler_params=pltpu.CompilerParams(vmem_limit_bytes=VMEM_LIMIT_BYTES),
        name="out_mlp",
    )(pool_out, attn_out, w_out, x, ga1, g2, sc2, sh2, w1, w2, ga2)


def _pad_in_proj_weight(w_in):
    src_g = _OFF_G
    per_group = GQA_GROUP * N_BRANCH
    cols = [w_in[:, :src_g]]
    for gg in range(N_KV):
        cols += [w_in[:, src_g + gg * per_group:src_g + (gg + 1) * per_group],
                 jnp.zeros((D_MODEL, LANES - per_group), w_in.dtype)]
    return jnp.concatenate(cols, axis=1).astype(BF16)


def _group_lanes(a):
    return a.transpose(1, 0, 2).reshape(a.shape[1], KV_WIDTH)


def kernel(x, c, w_ada, b_ada, norm1_g, norm2_g, w_in, w_pool, pool_scale, q_gain, kc_gain,
           ks_gain, kw_gain, cmp_pos_k, cmp_w1_k, cmp_b1_k, cmp_w2_k, cmp_b2_k, cmp_pos_v,
           cmp_w1_v, cmp_b1_v, cmp_w2_v, cmp_b2_v, w_out, w_ff1, w_ff2):
    batch, s, _ = x.shape
    assert batch == 1 and w_ada.shape[0] == 1
    assert s % SEL_CHUNK == 0 and s >= WIN_SPAN and (s // SEL_BLOCK) & (s // SEL_BLOCK - 1) == 0
    tm = min(s, 512)
    x2 = x[0]

    mod = _ada(jnp.broadcast_to(c, (8, D_MODEL)), w_ada[0], b_ada)[0:1]
    sh1, sc1, ga1, sh2, sc2, ga2 = [mod[:, k * D_MODEL:(k + 1) * D_MODEL] for k in range(6)]

    pair = lambda gain: jnp.tile(gain, (1, 2))
    (pool_out, q, kc, vc, ksel, vsel_t, kwin, vwin_t, gates_t) = _in_proj(
        x2, norm1_g, sc1, sh1, _pad_in_proj_weight(w_in[0]), w_pool[0].astype(BF16), pool_scale,
        pair(q_gain), pair(ks_gain), pair(kw_gain), min(s, 1024))

    chunks = lambda a: a.reshape(N_KV, s // CMP_STRIDE, CMP_STRIDE * HEAD_DIM)
    kcmp = _compress(chunks(kc), cmp_pos_k.reshape(1, -1), cmp_w1_k[0].astype(BF16), cmp_b1_k,
                     cmp_w2_k[0].astype(BF16), cmp_b2_k, kc_gain, True)
    vcmp = _compress(chunks(vc), cmp_pos_v.reshape(1, -1), cmp_w1_v[0].astype(BF16), cmp_b1_v,
                     cmp_w2_v[0].astype(BF16), cmp_b2_v, kc_gain, False)

    vcmp_t = jnp.pad(vcmp.transpose(0, 2, 1), ((0, 0), (0, V_ROWS - HEAD_DIM), (0, 0)))
    attn_out = _attention(q, gates_t, _group_lanes(kcmp), vcmp_t, ksel, vsel_t, kwin, vwin_t)

    out = _out_mlp(pool_out, attn_out, w_out[0].astype(BF16), x2, ga1, norm2_g, sc2, sh2,
                   w_ff1[0].astype(BF16), w_ff2[0].astype(BF16), ga2, tm)
    return out[None]
```

```python
import functools

import jax
import jax.numpy as jnp
import numpy as np
from jax import lax
from jax.experimental import pallas as pl
from jax.experimental.pallas import tpu as pltpu

F32 = jnp.float32
BF16 = jnp.bfloat16

LANES = 128
VMEM_LIMIT_BYTES = 56 * 1024 * 1024

D_MODEL = 1024
POOL_WIDTH = 512
POOL_WINDOWS = (2, 4, 8, 16)
POOL_GROUP = POOL_WIDTH // len(POOL_WINDOWS)
POOL_HALO = 16
HEAD_DIM = 64
N_HEADS = 8
N_KV = 2
GQA_GROUP = N_HEADS // N_KV
KV_WIDTH = N_KV * HEAD_DIM
N_BRANCH = 3
CMP_LEN = 32
CMP_STRIDE = 16
CMP_HIDDEN = 4 * HEAD_DIM
SEL_BLOCK = 64
N_SEL = 16
WINDOW = 512
D_FF = 4 * D_MODEL
NEG_INF = -1e30
N_FORCED = 3
EPS = 1e-6
LOG2E = 1.4426950408889634

TQ = 256
SEL_CHUNK = 256
BLOCKS_PER_CHUNK = SEL_CHUNK // SEL_BLOCK
SEL_GROUP = 4
CMP_BUCKETS = 8
WIN_SPAN = WINDOW + TQ
CMP_PER_SEL = SEL_BLOCK // CMP_STRIDE
GATE_ROWS = 16
V_ROWS = HEAD_DIM + 16

_Q_WIDTH = N_HEADS * HEAD_DIM
_OFF_U = 0
_OFF_Q = POOL_WIDTH
_OFF_KC = _OFF_Q + _Q_WIDTH
_OFF_VC = _OFF_KC + KV_WIDTH
_OFF_KS = _OFF_VC + KV_WIDTH
_OFF_VS = _OFF_KS + KV_WIDTH
_OFF_KW = _OFF_VS + KV_WIDTH
_OFF_VW = _OFF_KW + KV_WIDTH
_OFF_G = _OFF_VW + KV_WIDTH
_IN_PAD = _OFF_G + N_KV * LANES


def _dot(a, b):
    return jnp.dot(a, b, preferred_element_type=F32)


def _dot_nt(a, b):
    return lax.dot_general(a, b, (((1,), (1,)), ((), ())), preferred_element_type=F32)


def _const_spec(shape):
    nd = len(shape)
    return pl.BlockSpec(shape, lambda *_: (0,) * nd, pipeline_mode=pl.Buffered(1))


def _ada_kernel(c_ref, w_ref, b_ref, o_ref):
    o_ref[...] = jnp.dot(c_ref[...], w_ref[...], preferred_element_type=F32,
                         precision=lax.Precision.HIGHEST) + b_ref[...]


def _ada(c8, w, b):
    n = w.shape[1]
    bn = 1024
    return pl.pallas_call(
        _ada_kernel,
        grid=(n // bn,),
        in_specs=[pl.BlockSpec((8, D_MODEL), lambda j: (0, 0)),
                  pl.BlockSpec((D_MODEL, bn), lambda j: (0, j)),
                  pl.BlockSpec((1, bn), lambda j: (0, j))],
        out_specs=pl.BlockSpec((8, bn), lambda j: (0, j)),
        out_shape=jax.ShapeDtypeStruct((8, n), F32),
        compiler_params=pltpu.CompilerParams(vmem_limit_bytes=VMEM_LIMIT_BYTES),
        name="ada",
    )(c8, w, b)


def _rms_modulate(x, g, sc, sh):
    ms = jnp.mean(x * x, axis=-1, keepdims=True)
    return (x * lax.rsqrt(ms + EPS)) * (g * (1.0 + sc)) + sh


def _head_norm_pair(x, gain2):
    lane = lax.broadcasted_iota(jnp.int32, x.shape, 1)
    lo = lane < HEAD_DIM
    sq = x * x
    s_lo = jnp.sum(jnp.where(lo, sq, 0.0), axis=-1, keepdims=True)
    s_hi = jnp.sum(jnp.where(lo, 0.0, sq), axis=-1, keepdims=True)
    ms = jnp.where(lo, s_lo, s_hi) * (1.0 / HEAD_DIM)
    return x * lax.rsqrt(ms + EPS) * gain2


def _in_proj_kernel(x_ref, g_ref, sc_ref, sh_ref, w_ref, wpool_ref, pscale_ref, qg_ref, ksg_ref,
                    kwg_ref, pool_ref, q_ref, kc_ref, vc_ref, ks_ref, vs_ref, kw_ref, vw_ref,
                    gate_ref, ebuf_ref, *, tm):
    i = pl.program_id(0)
    h = _rms_modulate(x_ref[...], g_ref[...], sc_ref[...], sh_ref[...]).astype(BF16)

    def project(off, width):
        return _dot(h, w_ref[:, off:off + width])

    q_all = project(_OFF_Q, _Q_WIDTH)
    for pair in range(N_HEADS // 2):
        qn = _head_norm_pair(q_all[:, pair * LANES:(pair + 1) * LANES], qg_ref[...])
        q_ref[:, pair * LANES:(pair + 1) * LANES] = (qn * (HEAD_DIM ** -0.5 * LOG2E)).astype(BF16)

    kv = project(_OFF_KC, _IN_PAD - _OFF_KC)
    kv_cols = lambda off, width: kv[:, off - _OFF_KC:off - _OFF_KC + width]
    for gg in range(N_KV):
        kc_ref[gg] = kv_cols(_OFF_KC + gg * HEAD_DIM, HEAD_DIM)
        vc_ref[gg] = kv_cols(_OFF_VC + gg * HEAD_DIM, HEAD_DIM)
    ks_ref[...] = _head_norm_pair(kv_cols(_OFF_KS, KV_WIDTH), ksg_ref[...]).astype(BF16)
    kw_ref[...] = _head_norm_pair(kv_cols(_OFF_KW, KV_WIDTH), kwg_ref[...]).astype(BF16)
    def value_tiles(ref, off, chunk):
        ones_row = jnp.where(lax.broadcasted_iota(jnp.int32, (V_ROWS - HEAD_DIM, chunk), 0) == 0,
                             1.0, 0.0)
        for k in range(tm // chunk):
            vt = kv_cols(off, KV_WIDTH)[k * chunk:(k + 1) * chunk].T
            for gg in range(N_KV):
                ref[gg, k] = jnp.concatenate(
                    [vt[gg * HEAD_DIM:(gg + 1) * HEAD_DIM], ones_row], axis=0).astype(BF16)

    value_tiles(vs_ref, _OFF_VS, SEL_CHUNK)
    value_tiles(vw_ref, _OFF_VW, TQ)
    gates_t = jax.nn.sigmoid(kv_cols(_OFF_G, N_KV * LANES)).T
    for gg in range(N_KV):
        gate_ref[gg] = gates_t[gg * LANES:gg * LANES + GATE_ROWS]

    @pl.when(i == 0)
    def _():
        ebuf_ref[0:POOL_HALO, :] = jnp.zeros((POOL_HALO, POOL_WIDTH), F32)

    u = project(_OFF_U, POOL_WIDTH)
    ebuf_ref[POOL_HALO:POOL_HALO + tm, :] = u
    t1 = i * tm + lax.broadcasted_iota(jnp.int32, (tm, POOL_GROUP), 0) + 1
    for gi, w in enumerate(POOL_WINDOWS):
        c0 = gi * POOL_GROUP
        win = u[:, c0:c0 + POOL_GROUP]
        for k in range(1, w):
            win = win + ebuf_ref[POOL_HALO - k:POOL_HALO - k + tm, c0:c0 + POOL_GROUP]
        cnt = jnp.minimum(t1, w).astype(F32)
        pooled = win / cnt - u[:, c0:c0 + POOL_GROUP]
        y = _dot(pooled.astype(BF16), wpool_ref[gi])
        pool_ref[:, c0:c0 + POOL_GROUP] = (y * pscale_ref[:, c0:c0 + POOL_GROUP]).astype(BF16)
    ebuf_ref[0:POOL_HALO, :] = ebuf_ref[tm:tm + POOL_HALO, :]


def _in_proj(x, g1, sc1, sh1, w_in_p, w_pool, pool_scale, qg, ksg, kwg, tm):
    s = x.shape[0]
    row = lambda w: pl.BlockSpec((tm, w), lambda i: (i, 0))
    vec = lambda w: pl.BlockSpec((1, w), lambda i: (0, 0))
    per_head = pl.BlockSpec((N_KV, tm, HEAD_DIM), lambda i: (0, i, 0))
    chunks_t = lambda ch: pl.BlockSpec((N_KV, tm // ch, V_ROWS, ch), lambda i: (0, i, 0, 0))
    out_shape = [
        jax.ShapeDtypeStruct((s, POOL_WIDTH), BF16),
        jax.ShapeDtypeStruct((s, _Q_WIDTH), BF16),
        jax.ShapeDtypeStruct((N_KV, s, HEAD_DIM), F32),
        jax.ShapeDtypeStruct((N_KV, s, HEAD_DIM), F32),
        jax.ShapeDtypeStruct((s, KV_WIDTH), BF16),
        jax.ShapeDtypeStruct((N_KV, s // SEL_CHUNK, V_ROWS, SEL_CHUNK), BF16),
        jax.ShapeDtypeStruct((s, KV_WIDTH), BF16),
        jax.ShapeDtypeStruct((N_KV, s // TQ, V_ROWS, TQ), BF16),
        jax.ShapeDtypeStruct((N_KV, GATE_ROWS, s), F32),
    ]
    return pl.pallas_call(
        functools.partial(_in_proj_kernel, tm=tm),
        grid=(s // tm,),
        in_specs=[row(D_MODEL), vec(D_MODEL), vec(D_MODEL), vec(D_MODEL),
                  _const_spec(w_in_p.shape), _const_spec(w_pool.shape), vec(POOL_WIDTH),
                  vec(LANES), vec(LANES), vec(LANES)],
        out_specs=[row(POOL_WIDTH), row(_Q_WIDTH), per_head, per_head, row(KV_WIDTH),
                   chunks_t(SEL_CHUNK), row(KV_WIDTH), chunks_t(TQ),
                   pl.BlockSpec((N_KV, GATE_ROWS, tm), lambda i: (0, 0, i))],
        out_shape=out_shape,
        scratch_shapes=[pltpu.VMEM((tm + POOL_HALO, POOL_WIDTH), F32)],
        compiler_params=pltpu.CompilerParams(dimension_semantics=("arbitrary",),
                                             vmem_limit_bytes=VMEM_LIMIT_BYTES),
        name="in_proj",
    )(x, g1, sc1, sh1, w_in_p, w_pool, pool_scale, qg, ksg, kwg)


def _compress_kernel(c_ref, pos_ref, w1_ref, b1_ref, w2_ref, b2_ref, gain_ref, o_ref, *,
                     normalize):
    half = CMP_STRIDE * HEAD_DIM
    n_rows = c_ref.shape[1]
    c = c_ref[0]
    first = _dot((c + pos_ref[:, 0:half]).astype(BF16), w1_ref[0:half, :])
    second = _dot((c + pos_ref[:, half:2 * half]).astype(BF16), w1_ref[half:2 * half, :])
    hid = jax.nn.gelu(first + pltpu.roll(second, n_rows - 1, axis=0) + b1_ref[...])
    y = _dot(hid.astype(BF16), w2_ref[...]) + b2_ref[...]
    if normalize:
        ms = jnp.mean(y * y, axis=-1, keepdims=True)
        y = y * lax.rsqrt(ms + EPS) * gain_ref[...]
    o_ref[0] = y.astype(BF16)


def _compress(chunks, pos, w1, b1, w2, b2, gain, normalize):
    _, n_chunks, width = chunks.shape
    vec = lambda w: pl.BlockSpec((1, w), lambda g: (0, 0))
    return pl.pallas_call(
        functools.partial(_compress_kernel, normalize=normalize),
        grid=(N_KV,),
        in_specs=[pl.BlockSpec((1, n_chunks, width), lambda g: (g, 0, 0)),
                  vec(2 * width), _const_spec(w1.shape), vec(CMP_HIDDEN), _const_spec(w2.shape),
                  vec(HEAD_DIM), vec(HEAD_DIM)],
        out_specs=pl.BlockSpec((1, n_chunks, HEAD_DIM), lambda g: (g, 0, 0)),
        out_shape=jax.ShapeDtypeStruct((N_KV, n_chunks, HEAD_DIM), BF16),
        compiler_params=pltpu.CompilerParams(vmem_limit_bytes=VMEM_LIMIT_BYTES),
        name="compress",
    )(chunks, pos, w1, b1, w2, b2, gain)


def _attn_kernel(q_ref, gate_ref, rowt_ref, kc_ref, vct_ref, ks_ref, vst_ref, kw_ref, vwt_ref,
                 o_ref, psum_ref, madd_ref, bias_ref, wbias_ref, m_ref, acc_ref, ocmp_ref, owin_ref,
                 flag_ref, list_ref, *, nb):
    g = pl.program_id(0)
    i = pl.program_id(1)
    q0 = i * TQ
    nc = CMP_PER_SEL * nb
    n_chunks = nb // BLOCKS_PER_CHUNK
    gslope = jnp.where(g == 0, LOG2E, LOG2E * 2.0 ** -GQA_GROUP).astype(F32)
    slopes = [gslope * (2.0 ** -(r + 1)) for r in range(GQA_GROUP)]
    heads = [slice(r * TQ, (r + 1) * TQ) for r in range(GQA_GROUP)]

    lane = lax.broadcasted_iota(jnp.int32, (TQ, LANES), 1)
    own_half = (lane >= HEAD_DIM) == (g == 1)
    q_rows = []
    for r in range(GQA_GROUP):
        pair = q_ref[:, (r // 2) * LANES:(r // 2 + 1) * LANES]
        swapped = jnp.concatenate([pair[:, HEAD_DIM:], pair[:, :HEAD_DIM]], axis=1)
        q_rows.append(jnp.where(own_half, jnp.where(g == r % 2, pair, swapped), 0.0))
    q = jnp.concatenate(q_rows, axis=0)
    t_lane = q0 + lax.broadcasted_iota(jnp.int32, (1, TQ), 1)

    w0 = pl.multiple_of(jnp.maximum(q0 - WINDOW, 0), TQ)
    s_w = _dot_nt(kw_ref[pl.ds(w0, WIN_SPAN), :], q)

    @pl.when(i <= WINDOW // TQ)
    def _():
        row_w = rowt_ref[0:WIN_SPAN, :]
        lane_w = lax.broadcasted_iota(jnp.int32, (WIN_SPAN, TQ), 1).astype(F32)
        dq = (q0 - w0).astype(F32)
        d_w = lane_w - row_w + dq
        ok_w = jnp.abs(d_w - (WINDOW - 1) * 0.5) < WINDOW * 0.5
        rel_w = row_w - dq
        for r in range(GQA_GROUP):
            wbias_ref[r] = jnp.where(ok_w, slopes[r] * rel_w, NEG_INF)

    wb = w0 // TQ
    for r in range(GQA_GROUP):
        s = s_w[:, heads[r]] + wbias_ref[r]
        m = jnp.max(s, axis=0, keepdims=True)
        p = jnp.exp2((s - m).astype(BF16))
        o_win = _dot(vwt_ref[0, wb], p[0:TQ])
        for b in range(1, WIN_SPAN // TQ):
            o_win = o_win + _dot(vwt_ref[0, wb + b], p[b * TQ:(b + 1) * TQ])
        owin_ref[:, heads[r]] = o_win

    def compress_and_select(n_rows):
        n_blk = n_rows // CMP_PER_SEL
        rel_c = CMP_STRIDE * rowt_ref[0:n_rows, :] + (CMP_LEN - 1) - q0.astype(F32)
        ok_c = rel_c <= lax.broadcasted_iota(jnp.int32, (1, TQ), 1).astype(F32)
        s_c = _dot_nt(kc_ref[0:n_rows, :], q)
        p_sum = jnp.zeros((n_rows, TQ), F32)
        p_cols = []
        for r in range(GQA_GROUP):
            s = jnp.where(ok_c, s_c[:, heads[r]] + slopes[r] * rel_c, NEG_INF)
            m = jnp.max(s, axis=0, keepdims=True)
            e = jnp.exp2(s - m)
            l = jnp.sum(e, axis=0, keepdims=True)
            p = e * jnp.where(m > 0.5 * NEG_INF, 1.0 / l, 0.0)
            p_sum = p_sum + p
            p_cols.append(p.astype(BF16))
        ocmp_ref[...] = _dot(vct_ref[0, :, 0:n_rows], jnp.concatenate(p_cols, axis=1))

        for h in range(TQ // LANES):
            psum_ref[h, 0:8, :] = jnp.zeros((8, LANES), F32)
            psum_ref[h, 8:8 + n_rows, :] = p_sum[:, h * LANES:(h + 1) * LANES]

        def every4(start):
            parts = [psum_ref[h, pl.ds(8 + start, n_blk, stride=CMP_PER_SEL), :]
                     for h in range(TQ // LANES)]
            return parts[0] if len(parts) == 1 else jnp.concatenate(parts, axis=1)

        imp = every4(0) + every4(1) + every4(2) + 0.5 * every4(3) + 0.5 * every4(-1)
        blk = lax.broadcasted_iota(jnp.int32, (n_blk, TQ), 0)
        cur = lax.shift_right_logical(t_lane, SEL_BLOCK.bit_length() - 1)
        causal = blk <= cur
        forced = jnp.where(blk == 0, 1.0, 0.0) + jnp.where(blk == cur, 1.0, 0.0) \
            + jnp.where(blk == cur - 1, 1.0, 0.0)
        val = jnp.where(causal, jnp.where(forced > 0.0, -2.0, imp), -1.0)

        def pick(_, v):
            blk_f = rowt_ref[0:n_blk, :]
            mx = jnp.max(v, axis=0, keepdims=True)
            idx = jnp.min(jnp.where(v == mx, blk_f, float(n_blk)), axis=0, keepdims=True)
            return jnp.where(blk_f == idx, -2.0, v)

        picked = lax.fori_loop(0, min(N_SEL, n_blk) - N_FORCED, pick, val)
        madd_ref[0:n_blk, :] = jnp.where(causal, jnp.where(picked < -1.5, 0.0, NEG_INF), NEG_INF)
        for c in range(n_blk // BLOCKS_PER_CHUNK):
            rows = madd_ref[c * BLOCKS_PER_CHUNK:(c + 1) * BLOCKS_PER_CHUNK, :]
            flag_ref[c] = (jnp.max(rows) > 0.5 * NEG_INF).astype(jnp.int32)

    rows_per_bucket = nc // CMP_BUCKETS
    bucket = ((q0 + TQ) // CMP_STRIDE - 1) // rows_per_bucket
    for b in range(CMP_BUCKETS):
        pl.when(bucket == b)(functools.partial(compress_and_select, (b + 1) * rows_per_bucket))

    key_row = rowt_ref[0:SEL_CHUNK, :]
    for r in range(GQA_GROUP):
        bias_ref[r] = slopes[r] * key_row
    m_ref[...] = jnp.full(m_ref.shape, NEG_INF, F32)
    acc_ref[...] = jnp.zeros(acc_ref.shape, F32)

    def scores(chunks):
        keys = [ks_ref[pl.ds(pl.multiple_of(c * SEL_CHUNK, SEL_CHUNK), SEL_CHUNK), :]
                for c in chunks]
        s = _dot_nt(jnp.concatenate(keys, axis=0) if len(keys) > 1 else keys[0], q)
        return [s[k * SEL_CHUNK:(k + 1) * SEL_CHUNK] for k in range(len(chunks))]

    def softmax_pv(chunks, s, diagonal):
        madds, rel0 = [], []
        for c in chunks:
            k0 = c * SEL_CHUNK
            madd = jnp.concatenate(
                [jnp.broadcast_to(madd_ref[pl.ds(c * BLOCKS_PER_CHUNK + b, 1), :], (SEL_BLOCK, TQ))
                 for b in range(BLOCKS_PER_CHUNK)], axis=0)
            if diagonal:
                pos = k0 + lax.broadcasted_iota(jnp.int32, (SEL_CHUNK, TQ), 0)
                madd = jnp.where(pos <= t_lane, madd, NEG_INF)
            madds.append(madd)
            rel0.append((k0 - q0).astype(F32))
        p_cols = [[] for _ in chunks]
        alphas = []
        for r in range(GQA_GROUP):
            shifts = [slopes[r] * x for x in rel0]
            us = [sk[:, heads[r]] + bias_ref[r] + mk for sk, mk in zip(s, madds)]
            m_old = m_ref[:, heads[r]]
            m_new = m_old
            for u, sh in zip(us, shifts):
                m_new = jnp.maximum(m_new, jnp.max(u, axis=0, keepdims=True) + sh)
            alphas.append(jnp.exp2(m_old - m_new))
            for k, (u, sh) in enumerate(zip(us, shifts)):
                p_cols[k].append(jnp.exp2((u - (m_new - sh)).astype(BF16)))
            m_ref[:, heads[r]] = m_new
        pv = _dot(vst_ref[0, chunks[0]], jnp.concatenate(p_cols[0], axis=1))
        for k in range(1, len(chunks)):
            pv = pv + _dot(vst_ref[0, chunks[k]], jnp.concatenate(p_cols[k], axis=1))
        acc_ref[...] = jnp.concatenate(alphas, axis=1) * acc_ref[...] + pv

    c_diag = q0 // SEL_CHUNK

    def compact(c, n):
        list_ref[n] = c
        return n + flag_ref[c]

    n_active = lax.fori_loop(0, c_diag, compact, jnp.int32(0))

    def active_group(first, size):
        chunks = [list_ref[first + k] for k in range(size)]
        softmax_pv(chunks, scores(chunks), diagonal=False)

    def group_body(p, carry):
        active_group(p * SEL_GROUP, SEL_GROUP)
        return carry

    n_full = n_active // SEL_GROUP
    lax.fori_loop(0, n_full, group_body, 0)
    done = n_full * SEL_GROUP
    size = SEL_GROUP // 2
    while size >= 1:
        pl.when(((n_active - done) & size) != 0)(functools.partial(active_group, done, size))
        done = done + ((n_active - done) & size)
        size //= 2
    diag_chunks = [c_diag + k for k in range(TQ // SEL_CHUNK)]
    softmax_pv(diag_chunks, scores(diag_chunks), diagonal=True)

    head_out = []
    for r in range(GQA_GROUP):
        gc = gate_ref[0, 3 * r + 0:3 * r + 1, :]
        gs = gate_ref[0, 3 * r + 1:3 * r + 2, :]
        gw = gate_ref[0, 3 * r + 2:3 * r + 3, :]
        acc = acc_ref[:, heads[r]]
        win = owin_ref[:, heads[r]]
        out_t = (gc * ocmp_ref[:, heads[r]]
                 + (gs / acc[HEAD_DIM:HEAD_DIM + 1]) * acc
                 + (gw / win[HEAD_DIM:HEAD_DIM + 1]) * win)
        head_out.append(out_t.T[:, 0:HEAD_DIM])
    for pair in range(GQA_GROUP // 2):
        o_ref[:, pair * LANES:(pair + 1) * LANES] = jnp.concatenate(
            head_out[2 * pair:2 * pair + 2], axis=1).astype(BF16)


def _attention(q, gates_t, kcmp, vcmp_t, ksel, vsel_t, kwin, vwin_t):
    s = q.shape[0]
    nb = s // SEL_BLOCK
    nc = CMP_PER_SEL * nb
    gw = GQA_GROUP * HEAD_DIM
    n_rows = max(nc, WIN_SPAN)
    row_tile = jnp.asarray(np.broadcast_to(np.arange(n_rows)[:, None], (n_rows, TQ))
                           .astype(np.float32))
    per_group = lambda a: pl.BlockSpec((1,) + a.shape[1:], lambda g, i: (g,) + (0,) * (a.ndim - 1),
                                       pipeline_mode=pl.Buffered(1))
    return pl.pallas_call(
        functools.partial(_attn_kernel, nb=nb),
        grid=(N_KV, s // TQ),
        in_specs=[pl.BlockSpec((TQ, gw), lambda g, i: (i, g)),
                  pl.BlockSpec((1, GATE_ROWS, TQ), lambda g, i: (g, 0, i)),
                  _const_spec(row_tile.shape),
                  _const_spec(kcmp.shape), per_group(vcmp_t),
                  _const_spec(ksel.shape), per_group(vsel_t),
                  _const_spec(kwin.shape), per_group(vwin_t)],
        out_specs=pl.BlockSpec((TQ, gw), lambda g, i: (i, g)),
        out_shape=jax.ShapeDtypeStruct((s, N_HEADS * HEAD_DIM), BF16),
        scratch_shapes=[pltpu.VMEM((TQ // LANES, 8 + nc, LANES), F32),
                        pltpu.VMEM((nb, TQ), F32),
                        pltpu.VMEM((GQA_GROUP, SEL_CHUNK, TQ), F32),
                        pltpu.VMEM((GQA_GROUP, WIN_SPAN, TQ), F32),
                        pltpu.VMEM((1, GQA_GROUP * TQ), F32),
                        pltpu.VMEM((V_ROWS, GQA_GROUP * TQ), F32),
                        pltpu.VMEM((V_ROWS, GQA_GROUP * TQ), F32),
                        pltpu.VMEM((V_ROWS, GQA_GROUP * TQ), F32),
                        pltpu.SMEM((nb // BLOCKS_PER_CHUNK,), jnp.int32),
                        pltpu.SMEM((nb // BLOCKS_PER_CHUNK,), jnp.int32)],
        compiler_params=pltpu.CompilerParams(dimension_semantics=("arbitrary", "arbitrary"),
                                             vmem_limit_bytes=VMEM_LIMIT_BYTES),
        name="attn",
    )(q, gates_t, row_tile, kcmp, vcmp_t, ksel, vsel_t, kwin, vwin_t)


def _out_mlp_kernel(pool_ref, attn_ref, wout_ref, x_ref, ga1_ref, g_ref, sc_ref, sh_ref, w1_ref,
                    w2_ref, ga2_ref, o_ref, *, ff_chunk):
    mix = (_dot(pool_ref[...], wout_ref[0:POOL_WIDTH, :])
           + _dot(attn_ref[...], wout_ref[POOL_WIDTH:, :]))
    x1 = x_ref[...] + ga1_ref[...] * mix
    h = _rms_modulate(x1, g_ref[...], sc_ref[...], sh_ref[...]).astype(BF16)
    acc = jnp.zeros(x1.shape, F32)
    for c in range(D_FF // ff_chunk):
        a = _dot(h, w1_ref[:, c * ff_chunk:(c + 1) * ff_chunk])
        a = jnp.square(jnp.maximum(a, 0.0)).astype(BF16)
        acc = acc + _dot(a, w2_ref[c * ff_chunk:(c + 1) * ff_chunk, :])
    o_ref[...] = x1 + ga2_ref[...] * acc


def _out_mlp(pool_out, attn_out, w_out, x, ga1, g2, sc2, sh2, w1, w2, ga2, tm):
    s = x.shape[0]
    row = lambda w: pl.BlockSpec((tm, w), lambda i: (i, 0))
    vec = lambda w: pl.BlockSpec((1, w), lambda i: (0, 0))
    return pl.pallas_call(
        functools.partial(_out_mlp_kernel, ff_chunk=1024),
        grid=(s // tm,),
        in_specs=[row(POOL_WIDTH), row(N_HEADS * HEAD_DIM), _const_spec(w_out.shape),
                  row(D_MODEL), vec(D_MODEL), vec(D_MODEL), vec(D_MODEL), vec(D_MODEL),
                  _const_spec(w1.shape), _const_spec(w2.shape), vec(D_MODEL)],
        out_specs=row(D_MODEL),
        out_shape=jax.ShapeDtypeStruct((s, D_MODEL), F32),
        compiler_params=pltpu.CompilerParams(vmem_limit_bytes=VMEM_LIMIT_BYTES),
        name="out_mlp",
    )(pool_out, attn_out, w_out, x, ga1, g2, sc2, sh2, w1, w2, ga2)


def _pad_in_proj_weight(w_in):
    src_g = _OFF_G
    per_group = GQA_GROUP * N_BRANCH
    cols = [w_in[:, :src_g]]
    for gg in range(N_KV):
        cols += [w_in[:, src_g + gg * per_group:src_g + (gg + 1) * per_group],
                 jnp.zeros((D_MODEL, LANES - per_group), w_in.dtype)]
    return jnp.concatenate(cols, axis=1).astype(BF16)


def _group_lanes(a):
    return a.transpose(1, 0, 2).reshape(a.shape[1], KV_WIDTH)


def kernel(x, c, w_ada, b_ada, norm1_g, norm2_g, w_in, w_pool, pool_scale, q_gain, kc_gain,
           ks_gain, kw_gain, cmp_pos_k, cmp_w1_k, cmp_b1_k, cmp_w2_k, cmp_b2_k, cmp_pos_v,
           cmp_w1_v, cmp_b1_v, cmp_w2_v, cmp_b2_v, w_out, w_ff1, w_ff2):
    batch, s, _ = x.shape
    assert batch == 1 and w_ada.shape[0] == 1
    assert s % SEL_CHUNK == 0 and s >= WIN_SPAN and (s // SEL_BLOCK) & (s // SEL_BLOCK - 1) == 0
    tm = min(s, 512)
    x2 = x[0]

    mod = _ada(jnp.broadcast_to(c, (8, D_MODEL)), w_ada[0], b_ada)[0:1]
    sh1, sc1, ga1, sh2, sc2, ga2 = [mod[:, k * D_MODEL:(k + 1) * D_MODEL] for k in range(6)]

    pair = lambda gain: jnp.tile(gain, (1, 2))
    (pool_out, q, kc, vc, ksel, vsel_t, kwin, vwin_t, gates_t) = _in_proj(
        x2, norm1_g, sc1, sh1, _pad_in_proj_weight(w_in[0]), w_pool[0].astype(BF16), pool_scale,
        pair(q_gain), pair(ks_gain), pair(kw_gain), min(s, 1024))

    chunks = lambda a: a.reshape(N_KV, s // CMP_STRIDE, CMP_STRIDE * HEAD_DIM)
    kcmp = _compress(chunks(kc), cmp_pos_k.reshape(1, -1), cmp_w1_k[0].astype(BF16), cmp_b1_k,
                     cmp_w2_k[0].astype(BF16), cmp_b2_k, kc_gain, True)
    vcmp = _compress(chunks(vc), cmp_pos_v.reshape(1, -1), cmp_w1_v[0].astype(BF16), cmp_b1_v,
                     cmp_w2_v[0].astype(BF16), cmp_b2_v, kc_gain, False)

    vcmp_t = jnp.pad(vcmp.transpose(0, 2, 1), ((0, 0), (0, V_ROWS - HEAD_DIM), (0, 0)))
    attn_out = _attention(q, gates_t, _group_lanes(kcmp), vcmp_t, ksel, vsel_t, kwin, vwin_t)

    out = _out_mlp(pool_out, attn_out, w_out[0].astype(BF16), x2, ga1, norm2_g, sc2, sh2,
                   w_ff1[0].astype(BF16), w_ff2[0].astype(BF16), ga2, tm)
    return out[None]
```

```python
import functools

import jax
import jax.numpy as jnp
import numpy as np
from jax import lax
from jax.experimental import pallas as pl
from jax.experimental.pallas import tpu as pltpu

F32 = jnp.float32
BF16 = jnp.bfloat16

LANES = 128
SUBLANES = 8
VMEM_LIMIT_BYTES = 56 * 1024 * 1024

D_MODEL = 1024
POOL_WIDTH = 512
POOL_WINDOWS = (2, 4, 8, 16)
POOL_GROUP = POOL_WIDTH // len(POOL_WINDOWS)
POOL_HALO = 16
HEAD_DIM = 64
N_HEADS = 8
N_KV = 2
GQA_GROUP = N_HEADS // N_KV
KV_WIDTH = N_KV * HEAD_DIM
N_BRANCH = 3
CMP_LEN = 32
CMP_STRIDE = 16
CMP_HIDDEN = 4 * HEAD_DIM
SEL_BLOCK = 64
N_SEL = 16
WINDOW = 512
D_FF = 4 * D_MODEL
NEG_INF = -1e30
N_FORCED = 3
EPS = 1e-6
LOG2E = 1.4426950408889634

TQ = 256
SEL_CHUNK = 256
BLOCKS_PER_CHUNK = SEL_CHUNK // SEL_BLOCK
SEL_GROUP = 8
CMP_BUCKETS = 8
WIN_SPAN = WINDOW + TQ
CMP_PER_SEL = SEL_BLOCK // CMP_STRIDE
GATE_ROWS = 16
V_ROWS = HEAD_DIM + 16

_Q_WIDTH = N_HEADS * HEAD_DIM
_OFF_U = 0
_OFF_Q = POOL_WIDTH
_OFF_KC = _OFF_Q + _Q_WIDTH
_OFF_VC = _OFF_KC + KV_WIDTH
_OFF_KS = _OFF_VC + KV_WIDTH
_OFF_VS = _OFF_KS + KV_WIDTH
_OFF_KW = _OFF_VS + KV_WIDTH
_OFF_VW = _OFF_KW + KV_WIDTH
_OFF_G = _OFF_VW + KV_WIDTH
_IN_PAD = _OFF_G + N_KV * LANES


def _dot(a, b):
    return jnp.dot(a, b, preferred_element_type=F32)


def _dot_nt(a, b):
    return lax.dot_general(a, b, (((1,), (1,)), ((), ())), preferred_element_type=F32)


def _const_spec(shape):
    nd = len(shape)
    return pl.BlockSpec(shape, lambda *_: (0,) * nd, pipeline_mode=pl.Buffered(1))


def _ada_kernel(c_ref, w_ref, b_ref, o_ref):
    o_ref[...] = jnp.dot(c_ref[...], w_ref[...], preferred_element_type=F32,
                         precision=lax.Precision.HIGHEST) + b_ref[...]


def _ada(c8, w, b):
    n = w.shape[1]
    bn = 1024
    return pl.pallas_call(
        _ada_kernel,
        grid=(n // bn,),
        in_specs=[pl.BlockSpec((SUBLANES, D_MODEL), lambda j: (0, 0)),
                  pl.BlockSpec((D_MODEL, bn), lambda j: (0, j)),
                  pl.BlockSpec((1, bn), lambda j: (0, j))],
        out_specs=pl.BlockSpec((SUBLANES, bn), lambda j: (0, j)),
        out_shape=jax.ShapeDtypeStruct((SUBLANES, n), F32),
        compiler_params=pltpu.CompilerParams(vmem_limit_bytes=VMEM_LIMIT_BYTES),
        name="ada",
    )(c8, w, b)


def _rms_modulate(x, g, sc, sh):
    ms = jnp.mean(x * x, axis=-1, keepdims=True)
    return (x * lax.rsqrt(ms + EPS)) * (g * (1.0 + sc)) + sh


def _head_norm_pair(x, gain2):
    lane = lax.broadcasted_iota(jnp.int32, x.shape, 1)
    lo = lane < HEAD_DIM
    sq = x * x
    s_lo = jnp.sum(jnp.where(lo, sq, 0.0), axis=-1, keepdims=True)
    s_hi = jnp.sum(jnp.where(lo, 0.0, sq), axis=-1, keepdims=True)
    ms = jnp.where(lo, s_lo, s_hi) * (1.0 / HEAD_DIM)
    return x * lax.rsqrt(ms + EPS) * gain2


def _in_proj_kernel(x_ref, g_ref, sc_ref, sh_ref, w_ref, wpool_ref, pscale_ref, qg_ref, ksg_ref,
                    kwg_ref, pool_ref, q_ref, kc_ref, vc_ref, ks_ref, vs_ref, kw_ref, vw_ref,
                    gate_ref, ebuf_ref, *, tm):
    i = pl.program_id(0)
    h = _rms_modulate(x_ref[...], g_ref[...], sc_ref[...], sh_ref[...]).astype(BF16)

    def project(off, width):
        return _dot(h, w_ref[:, off:off + width])

    q_all = project(_OFF_Q, _Q_WIDTH)
    for pair in range(N_HEADS // 2):
        qn = _head_norm_pair(q_all[:, pair * LANES:(pair + 1) * LANES], qg_ref[...])
        q_ref[:, pair * LANES:(pair + 1) * LANES] = (qn * (HEAD_DIM ** -0.5 * LOG2E)).astype(BF16)

    kv = project(_OFF_KC, _IN_PAD - _OFF_KC)
    kv_cols = lambda off, width: kv[:, off - _OFF_KC:off - _OFF_KC + width]
    for gg in range(N_KV):
        kc_ref[gg] = kv_cols(_OFF_KC + gg * HEAD_DIM, HEAD_DIM)
        vc_ref[gg] = kv_cols(_OFF_VC + gg * HEAD_DIM, HEAD_DIM)
    ks_ref[...] = _head_norm_pair(kv_cols(_OFF_KS, KV_WIDTH), ksg_ref[...]).astype(BF16)
    kw_ref[...] = _head_norm_pair(kv_cols(_OFF_KW, KV_WIDTH), kwg_ref[...]).astype(BF16)
    def value_tiles(ref, off, chunk):
        ones_row = jnp.where(lax.broadcasted_iota(jnp.int32, (V_ROWS - HEAD_DIM, chunk), 0) == 0,
                             1.0, 0.0)
        for k in range(tm // chunk):
            vt = kv_cols(off, KV_WIDTH)[k * chunk:(k + 1) * chunk].T
            for gg in range(N_KV):
                ref[gg, k] = jnp.concatenate(
                    [vt[gg * HEAD_DIM:(gg + 1) * HEAD_DIM], ones_row], axis=0).astype(BF16)

    value_tiles(vs_ref, _OFF_VS, SEL_CHUNK)
    value_tiles(vw_ref, _OFF_VW, TQ)
    gates_t = jax.nn.sigmoid(kv_cols(_OFF_G, N_KV * LANES)).T
    for gg in range(N_KV):
        gate_ref[gg] = gates_t[gg * LANES:gg * LANES + GATE_ROWS]

    @pl.when(i == 0)
    def _():
        ebuf_ref[0:POOL_HALO, :] = jnp.zeros((POOL_HALO, POOL_WIDTH), F32)

    u = project(_OFF_U, POOL_WIDTH)
    ebuf_ref[POOL_HALO:POOL_HALO + tm, :] = u
    t1 = i * tm + lax.broadcasted_iota(jnp.int32, (tm, POOL_GROUP), 0) + 1
    for gi, w in enumerate(POOL_WINDOWS):
        c0 = gi * POOL_GROUP
        win = u[:, c0:c0 + POOL_GROUP]
        for k in range(1, w):
            win = win + ebuf_ref[POOL_HALO - k:POOL_HALO - k + tm, c0:c0 + POOL_GROUP]
        cnt = jnp.minimum(t1, w).astype(F32)
        pooled = win / cnt - u[:, c0:c0 + POOL_GROUP]
        y = _dot(pooled.astype(BF16), wpool_ref[gi])
        pool_ref[:, c0:c0 + POOL_GROUP] = (y * pscale_ref[:, c0:c0 + POOL_GROUP]).astype(BF16)
    ebuf_ref[0:POOL_HALO, :] = ebuf_ref[tm:tm + POOL_HALO, :]


def _in_proj(x, g1, sc1, sh1, w_in_p, w_pool, pool_scale, qg, ksg, kwg, tm):
    s = x.shape[0]
    row = lambda w: pl.BlockSpec((tm, w), lambda i: (i, 0))
    vec = lambda w: pl.BlockSpec((1, w), lambda i: (0, 0))
    per_head = pl.BlockSpec((N_KV, tm, HEAD_DIM), lambda i: (0, i, 0))
    chunks_t = lambda ch: pl.BlockSpec((N_KV, tm // ch, V_ROWS, ch), lambda i: (0, i, 0, 0))
    out_shape = [
        jax.ShapeDtypeStruct((s, POOL_WIDTH), BF16),
        jax.ShapeDtypeStruct((s, _Q_WIDTH), BF16),
        jax.ShapeDtypeStruct((N_KV, s, HEAD_DIM), F32),
        jax.ShapeDtypeStruct((N_KV, s, HEAD_DIM), F32),
        jax.ShapeDtypeStruct((s, KV_WIDTH), BF16),
        jax.ShapeDtypeStruct((N_KV, s // SEL_CHUNK, V_ROWS, SEL_CHUNK), BF16),
        jax.ShapeDtypeStruct((s, KV_WIDTH), BF16),
        jax.ShapeDtypeStruct((N_KV, s // TQ, V_ROWS, TQ), BF16),
        jax.ShapeDtypeStruct((N_KV, GATE_ROWS, s), F32),
    ]
    return pl.pallas_call(
        functools.partial(_in_proj_kernel, tm=tm),
        grid=(s // tm,),
        in_specs=[row(D_MODEL), vec(D_MODEL), vec(D_MODEL), vec(D_MODEL),
                  _const_spec(w_in_p.shape), _const_spec(w_pool.shape), vec(POOL_WIDTH),
                  vec(LANES), vec(LANES), vec(LANES)],
        out_specs=[row(POOL_WIDTH), row(_Q_WIDTH), per_head, per_head, row(KV_WIDTH),
                   chunks_t(SEL_CHUNK), row(KV_WIDTH), chunks_t(TQ),
                   pl.BlockSpec((N_KV, GATE_ROWS, tm), lambda i: (0, 0, i))],
        out_shape=out_shape,
        scratch_shapes=[pltpu.VMEM((tm + POOL_HALO, POOL_WIDTH), F32)],
        compiler_params=pltpu.CompilerParams(dimension_semantics=("arbitrary",),
                                             vmem_limit_bytes=VMEM_LIMIT_BYTES),
        name="in_proj",
    )(x, g1, sc1, sh1, w_in_p, w_pool, pool_scale, qg, ksg, kwg)


def _compress_kernel(c_ref, pos_ref, w1_ref, b1_ref, w2_ref, b2_ref, gain_ref, o_ref, *,
                     normalize):
    half = CMP_STRIDE * HEAD_DIM
    n_rows = c_ref.shape[1]
    c = c_ref[0]
    first = _dot((c + pos_ref[:, 0:half]).astype(BF16), w1_ref[0:half, :])
    second = _dot((c + pos_ref[:, half:2 * half]).astype(BF16), w1_ref[half:2 * half, :])
    hid = jax.nn.gelu(first + pltpu.roll(second, n_rows - 1, axis=0) + b1_ref[...])
    y = _dot(hid.astype(BF16), w2_ref[...]) + b2_ref[...]
    if normalize:
        ms = jnp.mean(y * y, axis=-1, keepdims=True)
        y = y * lax.rsqrt(ms + EPS) * gain_ref[...]
    o_ref[0] = y.astype(BF16)


def _compress(chunks, pos, w1, b1, w2, b2, gain, normalize):
    _, n_chunks, width = chunks.shape
    vec = lambda w: pl.BlockSpec((1, w), lambda g: (0, 0))
    return pl.pallas_call(
        functools.partial(_compress_kernel, normalize=normalize),
        grid=(N_KV,),
        in_specs=[pl.BlockSpec((1, n_chunks, width), lambda g: (g, 0, 0)),
                  vec(2 * width), _const_spec(w1.shape), vec(CMP_HIDDEN), _const_spec(w2.shape),
                  vec(HEAD_DIM), vec(HEAD_DIM)],
        out_specs=pl.BlockSpec((1, n_chunks, HEAD_DIM), lambda g: (g, 0, 0)),
        out_shape=jax.ShapeDtypeStruct((N_KV, n_chunks, HEAD_DIM), BF16),
        compiler_params=pltpu.CompilerParams(vmem_limit_bytes=VMEM_LIMIT_BYTES),
        name="compress",
    )(chunks, pos, w1, b1, w2, b2, gain)


def _attn_kernel(q_ref, gate_ref, rowt_ref, kc_ref, vct_ref, ks_ref, vst_ref, kw_ref, vwt_ref,
                 o_ref, psum_ref, madd_ref, bias_ref, wbias_ref, m_ref, acc_ref, ocmp_ref, owin_ref,
                 flag_ref, list_ref, *, nb):
    g = pl.program_id(0)
    i = pl.program_id(1)
    q0 = i * TQ
    nc = CMP_PER_SEL * nb
    n_chunks = nb // BLOCKS_PER_CHUNK
    gslope = jnp.where(g == 0, LOG2E, LOG2E * 2.0 ** -GQA_GROUP).astype(F32)
    slopes = [gslope * (2.0 ** -(r + 1)) for r in range(GQA_GROUP)]
    heads = [slice(r * TQ, (r + 1) * TQ) for r in range(GQA_GROUP)]

    lane = lax.broadcasted_iota(jnp.int32, (TQ, LANES), 1)
    own_half = (lane >= HEAD_DIM) == (g == 1)
    q_rows = []
    for r in range(GQA_GROUP):
        pair = q_ref[:, (r // 2) * LANES:(r // 2 + 1) * LANES]
        swapped = jnp.concatenate([pair[:, HEAD_DIM:], pair[:, :HEAD_DIM]], axis=1)
        q_rows.append(jnp.where(own_half, jnp.where(g == r % 2, pair, swapped), 0.0))
    q = jnp.concatenate(q_rows, axis=0)
    t_lane = q0 + lax.broadcasted_iota(jnp.int32, (1, TQ), 1)

    w0 = pl.multiple_of(jnp.maximum(q0 - WINDOW, 0), TQ)
    s_w = _dot_nt(kw_ref[pl.ds(w0, WIN_SPAN), :], q)

    @pl.when(i <= WINDOW // TQ)
    def _():
        row_w = rowt_ref[0:WIN_SPAN, :]
        lane_w = lax.broadcasted_iota(jnp.int32, (WIN_SPAN, TQ), 1).astype(F32)
        dq = (q0 - w0).astype(F32)
        d_w = lane_w - row_w + dq
        ok_w = jnp.abs(d_w - (WINDOW - 1) * 0.5) < WINDOW * 0.5
        rel_w = row_w - dq
        for r in range(GQA_GROUP):
            wbias_ref[r] = jnp.where(ok_w, slopes[r] * rel_w, NEG_INF)

    wb = w0 // TQ
    for r in range(GQA_GROUP):
        s = s_w[:, heads[r]] + wbias_ref[r]
        m = jnp.max(s, axis=0, keepdims=True)
        p = jnp.exp2((s - m).astype(BF16))
        o_win = _dot(vwt_ref[0, wb], p[0:TQ])
        for b in range(1, WIN_SPAN // TQ):
            o_win = o_win + _dot(vwt_ref[0, wb + b], p[b * TQ:(b + 1) * TQ])
        owin_ref[:, heads[r]] = o_win

    def compress_and_select(n_rows):
        n_blk = n_rows // CMP_PER_SEL
        rel_c = CMP_STRIDE * rowt_ref[0:n_rows, :] + (CMP_LEN - 1) - q0.astype(F32)
        ok_c = rel_c <= lax.broadcasted_iota(jnp.int32, (1, TQ), 1).astype(F32)
        s_c = _dot_nt(kc_ref[0:n_rows, :], q)
        p_sum = jnp.zeros((n_rows, TQ), F32)
        p_cols = []
        for r in range(GQA_GROUP):
            s = jnp.where(ok_c, s_c[:, heads[r]] + slopes[r] * rel_c, NEG_INF)
            m = jnp.max(s, axis=0, keepdims=True)
            e = jnp.exp2(s - m)
            l = jnp.sum(e, axis=0, keepdims=True)
            p = e * jnp.where(m > 0.5 * NEG_INF, 1.0 / l, 0.0)
            p_sum = p_sum + p
            p_cols.append(p.astype(BF16))
        ocmp_ref[...] = _dot(vct_ref[0, :, 0:n_rows], jnp.concatenate(p_cols, axis=1))

        for h in range(TQ // LANES):
            psum_ref[h, 0:SUBLANES, :] = jnp.zeros((SUBLANES, LANES), F32)
            psum_ref[h, SUBLANES:SUBLANES + n_rows, :] = p_sum[:, h * LANES:(h + 1) * LANES]

        def every4(start):
            parts = [psum_ref[h, pl.ds(SUBLANES + start, n_blk, stride=CMP_PER_SEL), :]
                     for h in range(TQ // LANES)]
            return parts[0] if len(parts) == 1 else jnp.concatenate(parts, axis=1)

        imp = every4(0) + every4(1) + every4(2) + 0.5 * every4(3) + 0.5 * every4(-1)
        blk = lax.broadcasted_iota(jnp.int32, (n_blk, TQ), 0)
        cur = lax.shift_right_logical(t_lane, SEL_BLOCK.bit_length() - 1)
        causal = blk <= cur
        forced = jnp.where(blk == 0, 1.0, 0.0) + jnp.where(blk == cur, 1.0, 0.0) \
            + jnp.where(blk == cur - 1, 1.0, 0.0)
        val = jnp.where(causal, jnp.where(forced > 0.0, -2.0, imp), -1.0)

        def pick(_, v):
            blk_f = rowt_ref[0:n_blk, :]
            mx = jnp.max(v, axis=0, keepdims=True)
            idx = jnp.min(jnp.where(v == mx, blk_f, float(n_blk)), axis=0, keepdims=True)
            return jnp.where(blk_f == idx, -2.0, v)

        picked = lax.fori_loop(0, min(N_SEL, n_blk) - N_FORCED, pick, val)
        madd_ref[0:n_blk, :] = jnp.where(causal, jnp.where(picked < -1.5, 0.0, NEG_INF), NEG_INF)
        for c in range(n_blk // BLOCKS_PER_CHUNK):
            rows = madd_ref[c * BLOCKS_PER_CHUNK:(c + 1) * BLOCKS_PER_CHUNK, :]
            flag_ref[c] = (jnp.max(rows) > 0.5 * NEG_INF).astype(jnp.int32)

    rows_per_bucket = nc // CMP_BUCKETS
    bucket = ((q0 + TQ) // CMP_STRIDE - 1) // rows_per_bucket
    for b in range(CMP_BUCKETS):
        pl.when(bucket == b)(functools.partial(compress_and_select, (b + 1) * rows_per_bucket))

    @pl.when(i == 0)
    def _():
        key_row = rowt_ref[0:SEL_CHUNK, :]
        for r in range(GQA_GROUP):
            bias_ref[r] = slopes[r] * key_row

    m_ref[...] = jnp.full(m_ref.shape, NEG_INF, F32)
    acc_ref[...] = jnp.zeros(acc_ref.shape, F32)

    def scores(chunks):
        keys = [ks_ref[pl.ds(pl.multiple_of(c * SEL_CHUNK, SEL_CHUNK), SEL_CHUNK), :]
                for c in chunks]
        s = _dot_nt(jnp.concatenate(keys, axis=0) if len(keys) > 1 else keys[0], q)
        return [s[k * SEL_CHUNK:(k + 1) * SEL_CHUNK] for k in range(len(chunks))]

    def softmax_pv(chunks, s, diagonal):
        madds, rel0 = [], []
        for c in chunks:
            k0 = c * SEL_CHUNK
            madd = jnp.concatenate(
                [jnp.broadcast_to(madd_ref[pl.ds(c * BLOCKS_PER_CHUNK + b, 1), :], (SEL_BLOCK, TQ))
                 for b in range(BLOCKS_PER_CHUNK)], axis=0)
            if diagonal:
                pos = k0 + lax.broadcasted_iota(jnp.int32, (SEL_CHUNK, TQ), 0)
                madd = jnp.where(pos <= t_lane, madd, NEG_INF)
            madds.append(madd)
            rel0.append((k0 - q0).astype(F32))
        p_cols = [[] for _ in chunks]
        alphas = []
        for r in range(GQA_GROUP):
            shifts = [slopes[r] * x for x in rel0]
            us = [sk[:, heads[r]] + bias_ref[r] + mk for sk, mk in zip(s, madds)]
            m_old = m_ref[:, heads[r]]
            m_new = m_old
            for u, sh in zip(us, shifts):
                m_new = jnp.maximum(m_new, jnp.max(u, axis=0, keepdims=True) + sh)
            alphas.append(jnp.exp2(m_old - m_new))
            for k, (u, sh) in enumerate(zip(us, shifts)):
                p_cols[k].append(jnp.exp2((u - (m_new - sh)).astype(BF16)))
            m_ref[:, heads[r]] = m_new
        pv = _dot(vst_ref[0, chunks[0]], jnp.concatenate(p_cols[0], axis=1))
        for k in range(1, len(chunks)):
            pv = pv + _dot(vst_ref[0, chunks[k]], jnp.concatenate(p_cols[k], axis=1))
        acc_ref[...] = jnp.concatenate(alphas, axis=1) * acc_ref[...] + pv

    c_diag = q0 // SEL_CHUNK

    def compact(c, n):
        list_ref[n] = c
        return n + flag_ref[c]

    n_active = lax.fori_loop(0, c_diag, compact, jnp.int32(0))

    def active_group(first, size):
        chunks = [list_ref[first + k] for k in range(size)]
        softmax_pv(chunks, scores(chunks), diagonal=False)

    def group_body(p, carry):
        active_group(p * SEL_GROUP, SEL_GROUP)
        return carry

    n_full = n_active // SEL_GROUP
    lax.fori_loop(0, n_full, group_body, 0)
    done = n_full * SEL_GROUP
    size = SEL_GROUP // 2
    while size >= 1:
        pl.when(((n_active - done) & size) != 0)(functools.partial(active_group, done, size))
        done = done + ((n_active - done) & size)
        size //= 2
    diag_chunks = [c_diag + k for k in range(TQ // SEL_CHUNK)]
    softmax_pv(diag_chunks, scores(diag_chunks), diagonal=True)

    head_out = []
    for r in range(GQA_GROUP):
        gc = gate_ref[0, 3 * r + 0:3 * r + 1, :]
        gs = gate_ref[0, 3 * r + 1:3 * r + 2, :]
        gw = gate_ref[0, 3 * r + 2:3 * r + 3, :]
        acc = acc_ref[:, heads[r]]
        win = owin_ref[:, heads[r]]
        out_t = (gc * ocmp_ref[:, heads[r]]
                 + (gs / acc[HEAD_DIM:HEAD_DIM + 1]) * acc
                 + (gw / win[HEAD_DIM:HEAD_DIM + 1]) * win)
        head_out.append(out_t.T[:, 0:HEAD_DIM])
    for pair in range(GQA_GROUP // 2):
        o_ref[:, pair * LANES:(pair + 1) * LANES] = jnp.concatenate(
            head_out[2 * pair:2 * pair + 2], axis=1).astype(BF16)


def _attention(q, gates_t, kcmp, vcmp_t, ksel, vsel_t, kwin, vwin_t):
    s = q.shape[0]
    nb = s // SEL_BLOCK
    nc = CMP_PER_SEL * nb
    gw = GQA_GROUP * HEAD_DIM
    n_rows = max(nc, WIN_SPAN)
    row_tile = jnp.asarray(np.broadcast_to(np.arange(n_rows)[:, None], (n_rows, TQ))
                           .astype(np.float32))
    per_group = lambda a: pl.BlockSpec((1,) + a.shape[1:], lambda g, i: (g,) + (0,) * (a.ndim - 1),
                                       pipeline_mode=pl.Buffered(1))
    return pl.pallas_call(
        functools.partial(_attn_kernel, nb=nb),
        grid=(N_KV, s // TQ),
        in_specs=[pl.BlockSpec((TQ, gw), lambda g, i: (i, g)),
                  pl.BlockSpec((1, GATE_ROWS, TQ), lambda g, i: (g, 0, i)),
                  _const_spec(row_tile.shape),
                  _const_spec(kcmp.shape), per_group(vcmp_t),
                  _const_spec(ksel.shape), per_group(vsel_t),
                  _const_spec(kwin.shape), per_group(vwin_t)],
        out_specs=pl.BlockSpec((TQ, gw), lambda g, i: (i, g)),
        out_shape=jax.ShapeDtypeStruct((s, N_HEADS * HEAD_DIM), BF16),
        scratch_shapes=[pltpu.VMEM((TQ // LANES, SUBLANES + nc, LANES), F32),
                        pltpu.VMEM((nb, TQ), F32),
                        pltpu.VMEM((GQA_GROUP, SEL_CHUNK, TQ), F32),
                        pltpu.VMEM((GQA_GROUP, WIN_SPAN, TQ), F32),
                        pltpu.VMEM((1, GQA_GROUP * TQ), F32),
                        pltpu.VMEM((V_ROWS, GQA_GROUP * TQ), F32),
                        pltpu.VMEM((V_ROWS, GQA_GROUP * TQ), F32),
                        pltpu.VMEM((V_ROWS, GQA_GROUP * TQ), F32),
                        pltpu.SMEM((nb // BLOCKS_PER_CHUNK,), jnp.int32),
                        pltpu.SMEM((nb // BLOCKS_PER_CHUNK,), jnp.int32)],
        compiler_params=pltpu.CompilerParams(dimension_semantics=("arbitrary", "arbitrary"),
                                             vmem_limit_bytes=VMEM_LIMIT_BYTES),
        name="attn",
    )(q, gates_t, row_tile, kcmp, vcmp_t, ksel, vsel_t, kwin, vwin_t)


def _out_mlp_kernel(pool_ref, attn_ref, wout_ref, x_ref, ga1_ref, g_ref, sc_ref, sh_ref, w1_ref,
                    w2_ref, ga2_ref, o_ref, *, ff_chunk):
    mix = (_dot(pool_ref[...], wout_ref[0:POOL_WIDTH, :])
           + _dot(attn_ref[...], wout_ref[POOL_WIDTH:, :]))
    x1 = x_ref[...] + ga1_ref[...] * mix
    h = _rms_modulate(x1, g_ref[...], sc_ref[...], sh_ref[...]).astype(BF16)
    acc = jnp.zeros(x1.shape, F32)
    for c in range(D_FF // ff_chunk):
        a = _dot(h, w1_ref[:, c * ff_chunk:(c + 1) * ff_chunk])
        a = jnp.square(jnp.maximum(a, 0.0)).astype(BF16)
        acc = acc + _dot(a, w2_ref[c * ff_chunk:(c + 1) * ff_chunk, :])
    o_ref[...] = x1 + ga2_ref[...] * acc


def _out_mlp(pool_out, attn_out, w_out, x, ga1, g2, sc2, sh2, w1, w2, ga2, tm):
    s = x.shape[0]
    row = lambda w: pl.BlockSpec((tm, w), lambda i: (i, 0))
    vec = lambda w: pl.BlockSpec((1, w), lambda i: (0, 0))
    return pl.pallas_call(
        functools.partial(_out_mlp_kernel, ff_chunk=1024),
        grid=(s // tm,),
        in_specs=[row(POOL_WIDTH), row(N_HEADS * HEAD_DIM), _const_spec(w_out.shape),
                  row(D_MODEL), vec(D_MODEL), vec(D_MODEL), vec(D_MODEL), vec(D_MODEL),
                  _const_spec(w1.shape), _const_spec(w2.shape), vec(D_MODEL)],
        out_specs=row(D_MODEL),
        out_shape=jax.ShapeDtypeStruct((s, D_MODEL), F32),
        compiler_params=pltpu.CompilerParams(vmem_limit_bytes=VMEM_LIMIT_BYTES),
        name="out_mlp",
    )(pool_out, attn_out, w_out, x, ga1, g2, sc2, sh2, w1, w2, ga2)


def _pad_in_proj_weight(w_in):
    src_g = _OFF_G
    per_group = GQA_GROUP * N_BRANCH
    cols = [w_in[:, :src_g]]
    for gg in range(N_KV):
        cols += [w_in[:, src_g + gg * per_group:src_g + (gg + 1) * per_group],
                 jnp.zeros((D_MODEL, LANES - per_group), w_in.dtype)]
    return jnp.concatenate(cols, axis=1).astype(BF16)


def _group_lanes(a):
    return a.transpose(1, 0, 2).reshape(a.shape[1], KV_WIDTH)


def kernel(x, c, w_ada, b_ada, norm1_g, norm2_g, w_in, w_pool, pool_scale, q_gain, kc_gain,
           ks_gain, kw_gain, cmp_pos_k, cmp_w1_k, cmp_b1_k, cmp_w2_k, cmp_b2_k, cmp_pos_v,
           cmp_w1_v, cmp_b1_v, cmp_w2_v, cmp_b2_v, w_out, w_ff1, w_ff2):
    batch, s, _ = x.shape
    assert batch == 1 and w_ada.shape[0] == 1
    assert s % SEL_CHUNK == 0 and s >= WIN_SPAN and (s // SEL_BLOCK) & (s // SEL_BLOCK - 1) == 0
    tm = min(s, 512)
    x2 = x[0]

    mod = _ada(jnp.broadcast_to(c, (SUBLANES, D_MODEL)), w_ada[0], b_ada)[0:1]
    sh1, sc1, ga1, sh2, sc2, ga2 = [mod[:, k * D_MODEL:(k + 1) * D_MODEL] for k in range(6)]

    pair = lambda gain: jnp.tile(gain, (1, 2))
    (pool_out, q, kc, vc, ksel, vsel_t, kwin, vwin_t, gates_t) = _in_proj(
        x2, norm1_g, sc1, sh1, _pad_in_proj_weight(w_in[0]), w_pool[0].astype(BF16), pool_scale,
        pair(q_gain), pair(ks_gain), pair(kw_gain), min(s, 1024))

    chunks = lambda a: a.reshape(N_KV, s // CMP_STRIDE, CMP_STRIDE * HEAD_DIM)
    kcmp = _compress(chunks(kc), cmp_pos_k.reshape(1, -1), cmp_w1_k[0].astype(BF16), cmp_b1_k,
                     cmp_w2_k[0].astype(BF16), cmp_b2_k, kc_gain, True)
    vcmp = _compress(chunks(vc), cmp_pos_v.reshape(1, -1), cmp_w1_v[0].astype(BF16), cmp_b1_v,
                     cmp_w2_v[0].astype(BF16), cmp_b2_v, kc_gain, False)

    vcmp_t = jnp.pad(vcmp.transpose(0, 2, 1), ((0, 0), (0, V_ROWS - HEAD_DIM), (0, 0)))
    attn_out = _attention(q, gates_t, _group_lanes(kcmp), vcmp_t, ksel, vsel_t, kwin, vwin_t)

    out = _out_mlp(pool_out, attn_out, w_out[0].astype(BF16), x2, ga1, norm2_g, sc2, sh2,
                   w_ff1[0].astype(BF16), w_ff2[0].astype(BF16), ga2, tm)
    return out[None]
```

```python
import functools

import jax
import jax.numpy as jnp
import numpy as np
from jax import lax
from jax.experimental import pallas as pl
from jax.experimental.pallas import tpu as pltpu

F32 = jnp.float32
BF16 = jnp.bfloat16

LANES = 128
SUBLANES = 8
VMEM_LIMIT_BYTES = 56 * 1024 * 1024

D_MODEL = 1024
POOL_WIDTH = 512
POOL_WINDOWS = (2, 4, 8, 16)
POOL_GROUP = POOL_WIDTH // len(POOL_WINDOWS)
POOL_HALO = 16
HEAD_DIM = 64
N_HEADS = 8
N_KV = 2
GQA_GROUP = N_HEADS // N_KV
KV_WIDTH = N_KV * HEAD_DIM
N_BRANCH = 3
CMP_LEN = 32
CMP_STRIDE = 16
CMP_HIDDEN = 4 * HEAD_DIM
SEL_BLOCK = 64
N_SEL = 16
WINDOW = 512
D_FF = 4 * D_MODEL
NEG_INF = -1e30
N_FORCED = 3
EPS = 1e-6
LOG2E = 1.4426950408889634

TQ = 256
SEL_CHUNK = 256
BLOCKS_PER_CHUNK = SEL_CHUNK // SEL_BLOCK
SEL_GROUP = 8
CMP_BUCKETS = 8
WIN_SPAN = WINDOW + TQ
CMP_PER_SEL = SEL_BLOCK // CMP_STRIDE
GATE_ROWS = 16
V_ROWS = HEAD_DIM + 16

_Q_WIDTH = N_HEADS * HEAD_DIM
_OFF_U = 0
_OFF_Q = POOL_WIDTH
_OFF_KC = _OFF_Q + _Q_WIDTH
_OFF_VC = _OFF_KC + KV_WIDTH
_OFF_KS = _OFF_VC + KV_WIDTH
_OFF_VS = _OFF_KS + KV_WIDTH
_OFF_KW = _OFF_VS + KV_WIDTH
_OFF_VW = _OFF_KW + KV_WIDTH
_OFF_G = _OFF_VW + KV_WIDTH
_IN_PAD = _OFF_G + N_KV * LANES


def _dot(a, b):
    return jnp.dot(a, b, preferred_element_type=F32)


def _dot_nt(a, b):
    return lax.dot_general(a, b, (((1,), (1,)), ((), ())), preferred_element_type=F32)


def _const_spec(shape):
    nd = len(shape)
    return pl.BlockSpec(shape, lambda *_: (0,) * nd, pipeline_mode=pl.Buffered(1))


def _ada_kernel(c_ref, w_ref, b_ref, o_ref):
    o_ref[...] = jnp.dot(c_ref[...], w_ref[...], preferred_element_type=F32,
                         precision=lax.Precision.HIGHEST) + b_ref[...]


def _ada(c8, w, b):
    n = w.shape[1]
    bn = 1024
    return pl.pallas_call(
        _ada_kernel,
        grid=(n // bn,),
        in_specs=[pl.BlockSpec((SUBLANES, D_MODEL), lambda j: (0, 0)),
                  pl.BlockSpec((D_MODEL, bn), lambda j: (0, j)),
                  pl.BlockSpec((1, bn), lambda j: (0, j))],
        out_specs=pl.BlockSpec((SUBLANES, bn), lambda j: (0, j)),
        out_shape=jax.ShapeDtypeStruct((SUBLANES, n), F32),
        compiler_params=pltpu.CompilerParams(vmem_limit_bytes=VMEM_LIMIT_BYTES),
        name="ada",
    )(c8, w, b)


def _rms_modulate(x, g, sc, sh):
    ms = jnp.mean(x * x, axis=-1, keepdims=True)
    return (x * lax.rsqrt(ms + EPS)) * (g * (1.0 + sc)) + sh


def _head_norm_pair(x, gain2):
    lane = lax.broadcasted_iota(jnp.int32, x.shape, 1)
    lo = lane < HEAD_DIM
    sq = x * x
    s_lo = jnp.sum(jnp.where(lo, sq, 0.0), axis=-1, keepdims=True)
    s_hi = jnp.sum(jnp.where(lo, 0.0, sq), axis=-1, keepdims=True)
    ms = jnp.where(lo, s_lo, s_hi) * (1.0 / HEAD_DIM)
    return x * lax.rsqrt(ms + EPS) * gain2


def _in_proj_kernel(x_ref, g_ref, sc_ref, sh_ref, w_ref, wpool_ref, pscale_ref, qg_ref, ksg_ref,
                    kwg_ref, pool_ref, q_ref, kc_ref, vc_ref, ks_ref, vs_ref, kw_ref, vw_ref,
                    gate_ref, ebuf_ref, *, tm):
    i = pl.program_id(0)
    h = _rms_modulate(x_ref[...], g_ref[...], sc_ref[...], sh_ref[...]).astype(BF16)

    def project(off, width):
        return _dot(h, w_ref[:, off:off + width])

    q_all = project(_OFF_Q, _Q_WIDTH)
    for pair in range(N_HEADS // 2):
        qn = _head_norm_pair(q_all[:, pair * LANES:(pair + 1) * LANES], qg_ref[...])
        q_ref[:, pair * LANES:(pair + 1) * LANES] = (qn * (HEAD_DIM ** -0.5 * LOG2E)).astype(BF16)

    kv = project(_OFF_KC, _IN_PAD - _OFF_KC)
    kv_cols = lambda off, width: kv[:, off - _OFF_KC:off - _OFF_KC + width]
    for gg in range(N_KV):
        kc_ref[gg] = kv_cols(_OFF_KC + gg * HEAD_DIM, HEAD_DIM)
        vc_ref[gg] = kv_cols(_OFF_VC + gg * HEAD_DIM, HEAD_DIM)
    ks_ref[...] = _head_norm_pair(kv_cols(_OFF_KS, KV_WIDTH), ksg_ref[...]).astype(BF16)
    kw_ref[...] = _head_norm_pair(kv_cols(_OFF_KW, KV_WIDTH), kwg_ref[...]).astype(BF16)
    def value_tiles(ref, off, chunk):
        ones_row = jnp.where(lax.broadcasted_iota(jnp.int32, (V_ROWS - HEAD_DIM, chunk), 0) == 0,
                             1.0, 0.0)
        for k in range(tm // chunk):
            vt = kv_cols(off, KV_WIDTH)[k * chunk:(k + 1) * chunk].T
            for gg in range(N_KV):
                ref[gg, k] = jnp.concatenate(
                    [vt[gg * HEAD_DIM:(gg + 1) * HEAD_DIM], ones_row], axis=0).astype(BF16)

    value_tiles(vs_ref, _OFF_VS, SEL_CHUNK)
    value_tiles(vw_ref, _OFF_VW, TQ)
    gates_t = jax.nn.sigmoid(kv_cols(_OFF_G, N_KV * LANES)).T
    for gg in range(N_KV):
        gate_ref[gg] = gates_t[gg * LANES:gg * LANES + GATE_ROWS]

    @pl.when(i == 0)
    def _():
        ebuf_ref[0:POOL_HALO, :] = jnp.zeros((POOL_HALO, POOL_WIDTH), F32)

    u = project(_OFF_U, POOL_WIDTH)
    ebuf_ref[POOL_HALO:POOL_HALO + tm, :] = u
    t1 = i * tm + lax.broadcasted_iota(jnp.int32, (tm, POOL_GROUP), 0) + 1
    for gi, w in enumerate(POOL_WINDOWS):
        c0 = gi * POOL_GROUP
        win = u[:, c0:c0 + POOL_GROUP]
        for k in range(1, w):
            win = win + ebuf_ref[POOL_HALO - k:POOL_HALO - k + tm, c0:c0 + POOL_GROUP]
        cnt = jnp.minimum(t1, w).astype(F32)
        pooled = win / cnt - u[:, c0:c0 + POOL_GROUP]
        y = _dot(pooled.astype(BF16), wpool_ref[gi])
        pool_ref[:, c0:c0 + POOL_GROUP] = (y * pscale_ref[:, c0:c0 + POOL_GROUP]).astype(BF16)
    ebuf_ref[0:POOL_HALO, :] = ebuf_ref[tm:tm + POOL_HALO, :]


def _in_proj(x, g1, sc1, sh1, w_in_p, w_pool, pool_scale, qg, ksg, kwg, tm):
    s = x.shape[0]
    row = lambda w: pl.BlockSpec((tm, w), lambda i: (i, 0))
    vec = lambda w: pl.BlockSpec((1, w), lambda i: (0, 0))
    per_head = pl.BlockSpec((N_KV, tm, HEAD_DIM), lambda i: (0, i, 0))
    chunks_t = lambda ch: pl.BlockSpec((N_KV, tm // ch, V_ROWS, ch), lambda i: (0, i, 0, 0))
    out_shape = [
        jax.ShapeDtypeStruct((s, POOL_WIDTH), BF16),
        jax.ShapeDtypeStruct((s, _Q_WIDTH), BF16),
        jax.ShapeDtypeStruct((N_KV, s, HEAD_DIM), F32),
        jax.ShapeDtypeStruct((N_KV, s, HEAD_DIM), F32),
        jax.ShapeDtypeStruct((s, KV_WIDTH), BF16),
        jax.ShapeDtypeStruct((N_KV, s // SEL_CHUNK, V_ROWS, SEL_CHUNK), BF16),
        jax.ShapeDtypeStruct((s, KV_WIDTH), BF16),
        jax.ShapeDtypeStruct((N_KV, s // TQ, V_ROWS, TQ), BF16),
        jax.ShapeDtypeStruct((N_KV, GATE_ROWS, s), F32),
    ]
    return pl.pallas_call(
        functools.partial(_in_proj_kernel, tm=tm),
        grid=(s // tm,),
        in_specs=[row(D_MODEL), vec(D_MODEL), vec(D_MODEL), vec(D_MODEL),
                  _const_spec(w_in_p.shape), _const_spec(w_pool.shape), vec(POOL_WIDTH),
                  vec(LANES), vec(LANES), vec(LANES)],
        out_specs=[row(POOL_WIDTH), row(_Q_WIDTH), per_head, per_head, row(KV_WIDTH),
                   chunks_t(SEL_CHUNK), row(KV_WIDTH), chunks_t(TQ),
                   pl.BlockSpec((N_KV, GATE_ROWS, tm), lambda i: (0, 0, i))],
        out_shape=out_shape,
        scratch_shapes=[pltpu.VMEM((tm + POOL_HALO, POOL_WIDTH), F32)],
        compiler_params=pltpu.CompilerParams(dimension_semantics=("arbitrary",),
                                             vmem_limit_bytes=VMEM_LIMIT_BYTES),
        name="in_proj",
    )(x, g1, sc1, sh1, w_in_p, w_pool, pool_scale, qg, ksg, kwg)


def _compress_kernel(c_ref, pos_ref, w1_ref, b1_ref, w2_ref, b2_ref, gain_ref, o_ref, *,
                     normalize):
    half = CMP_STRIDE * HEAD_DIM
    n_rows = c_ref.shape[1]
    c = c_ref[0]
    first = _dot((c + pos_ref[:, 0:half]).astype(BF16), w1_ref[0:half, :])
    second = _dot((c + pos_ref[:, half:2 * half]).astype(BF16), w1_ref[half:2 * half, :])
    hid = jax.nn.gelu(first + pltpu.roll(second, n_rows - 1, axis=0) + b1_ref[...])
    y = _dot(hid.astype(BF16), w2_ref[...]) + b2_ref[...]
    if normalize:
        ms = jnp.mean(y * y, axis=-1, keepdims=True)
        y = y * lax.rsqrt(ms + EPS) * gain_ref[...]
    o_ref[0] = y.astype(BF16)


def _compress(chunks, pos, w1, b1, w2, b2, gain, normalize):
    _, n_chunks, width = chunks.shape
    vec = lambda w: pl.BlockSpec((1, w), lambda g: (0, 0))
    return pl.pallas_call(
        functools.partial(_compress_kernel, normalize=normalize),
        grid=(N_KV,),
        in_specs=[pl.BlockSpec((1, n_chunks, width), lambda g: (g, 0, 0)),
                  vec(2 * width), _const_spec(w1.shape), vec(CMP_HIDDEN), _const_spec(w2.shape),
                  vec(HEAD_DIM), vec(HEAD_DIM)],
        out_specs=pl.BlockSpec((1, n_chunks, HEAD_DIM), lambda g: (g, 0, 0)),
        out_shape=jax.ShapeDtypeStruct((N_KV, n_chunks, HEAD_DIM), BF16),
        compiler_params=pltpu.CompilerParams(vmem_limit_bytes=VMEM_LIMIT_BYTES),
        name="compress",
    )(chunks, pos, w1, b1, w2, b2, gain)


def _attn_kernel(q_ref, gate_ref, rowt_ref, kc_ref, vct_ref, ks_ref, vst_ref, kw_ref, vwt_ref,
                 o_ref, psum_ref, madd_ref, bias_ref, wbias_ref, ctab_ref, m_ref, acc_ref, ocmp_ref,
                 owin_ref, flag_ref, list_ref, *, nb):
    g = pl.program_id(0)
    i = pl.program_id(1)
    q0 = i * TQ
    nc = CMP_PER_SEL * nb
    n_chunks = nb // BLOCKS_PER_CHUNK
    gslope = jnp.where(g == 0, LOG2E, LOG2E * 2.0 ** -GQA_GROUP).astype(F32)
    slopes = [gslope * (2.0 ** -(r + 1)) for r in range(GQA_GROUP)]
    heads = [slice(r * TQ, (r + 1) * TQ) for r in range(GQA_GROUP)]

    lane = lax.broadcasted_iota(jnp.int32, (TQ, LANES), 1)
    own_half = (lane >= HEAD_DIM) == (g == 1)
    q_rows = []
    for r in range(GQA_GROUP):
        pair = q_ref[:, (r // 2) * LANES:(r // 2 + 1) * LANES]
        swapped = jnp.concatenate([pair[:, HEAD_DIM:], pair[:, :HEAD_DIM]], axis=1)
        q_rows.append(jnp.where(own_half, jnp.where(g == r % 2, pair, swapped), 0.0))
    q = jnp.concatenate(q_rows, axis=0)
    t_lane = q0 + lax.broadcasted_iota(jnp.int32, (1, TQ), 1)

    w0 = pl.multiple_of(jnp.maximum(q0 - WINDOW, 0), TQ)
    s_w = _dot_nt(kw_ref[pl.ds(w0, WIN_SPAN), :], q)

    @pl.when(i <= WINDOW // TQ)
    def _():
        row_w = rowt_ref[0:WIN_SPAN, :]
        lane_w = lax.broadcasted_iota(jnp.int32, (WIN_SPAN, TQ), 1).astype(F32)
        dq = (q0 - w0).astype(F32)
        d_w = lane_w - row_w + dq
        ok_w = jnp.abs(d_w - (WINDOW - 1) * 0.5) < WINDOW * 0.5
        rel_w = row_w - dq
        for r in range(GQA_GROUP):
            wbias_ref[r] = jnp.where(ok_w, slopes[r] * rel_w, NEG_INF)

    wb = w0 // TQ
    for r in range(GQA_GROUP):
        s = s_w[:, heads[r]] + wbias_ref[r]
        m = jnp.max(s, axis=0, keepdims=True)
        p = jnp.exp2((s - m).astype(BF16))
        o_win = _dot(vwt_ref[0, wb], p[0:TQ])
        for b in range(1, WIN_SPAN // TQ):
            o_win = o_win + _dot(vwt_ref[0, wb + b], p[b * TQ:(b + 1) * TQ])
        owin_ref[:, heads[r]] = o_win

    rows_per_bucket = nc // CMP_BUCKETS
    tab_rows = ctab_ref.shape[1]

    @pl.when(i == 0)
    def _():
        rel = CMP_STRIDE * (rowt_ref[0:tab_rows, :] - nc) + (CMP_LEN - 1)
        ok = rel <= lax.broadcasted_iota(jnp.int32, (1, TQ), 1).astype(F32)
        for r in range(GQA_GROUP):
            ctab_ref[r] = jnp.where(ok, slopes[r] * rel, NEG_INF)

    tab0 = pl.multiple_of(nc - q0 // CMP_STRIDE, CMP_STRIDE)

    def compress_and_select(n_rows):
        n_blk = n_rows // CMP_PER_SEL
        s_c = _dot_nt(kc_ref[0:n_rows, :], q)
        p_sum = jnp.zeros((n_rows, TQ), F32)
        p_cols = []
        for r in range(GQA_GROUP):
            s = s_c[:, heads[r]] + ctab_ref[r, pl.ds(tab0, n_rows), :]
            m = jnp.max(s, axis=0, keepdims=True)
            e = jnp.exp2(s - m)
            l = jnp.sum(e, axis=0, keepdims=True)
            p = e * jnp.where(m > 0.5 * NEG_INF, 1.0 / l, 0.0)
            p_sum = p_sum + p
            p_cols.append(p.astype(BF16))
        ocmp_ref[...] = _dot(vct_ref[0, :, 0:n_rows], jnp.concatenate(p_cols, axis=1))

        for h in range(TQ // LANES):
            psum_ref[h, 0:SUBLANES, :] = jnp.zeros((SUBLANES, LANES), F32)
            psum_ref[h, SUBLANES:SUBLANES + n_rows, :] = p_sum[:, h * LANES:(h + 1) * LANES]

        def every4(start):
            parts = [psum_ref[h, pl.ds(SUBLANES + start, n_blk, stride=CMP_PER_SEL), :]
                     for h in range(TQ // LANES)]
            return parts[0] if len(parts) == 1 else jnp.concatenate(parts, axis=1)

        imp = every4(0) + every4(1) + every4(2) + 0.5 * every4(3) + 0.5 * every4(-1)
        blk = lax.broadcasted_iota(jnp.int32, (n_blk, TQ), 0)
        cur = lax.shift_right_logical(t_lane, SEL_BLOCK.bit_length() - 1)
        causal = blk <= cur
        forced = jnp.where(blk == 0, 1.0, 0.0) + jnp.where(blk == cur, 1.0, 0.0) \
            + jnp.where(blk == cur - 1, 1.0, 0.0)
        val = jnp.where(causal, jnp.where(forced > 0.0, -2.0, imp), -1.0)

        def pick(_, v):
            blk_f = rowt_ref[0:n_blk, :]
            mx = jnp.max(v, axis=0, keepdims=True)
            idx = jnp.min(jnp.where(v == mx, blk_f, float(n_blk)), axis=0, keepdims=True)
            return jnp.where(blk_f == idx, -2.0, v)

        picked = lax.fori_loop(0, min(N_SEL, n_blk) - N_FORCED, pick, val)
        madd_ref[0:n_blk, :] = jnp.where(causal, jnp.where(picked < -1.5, 0.0, NEG_INF), NEG_INF)
        for c in range(n_blk // BLOCKS_PER_CHUNK):
            rows = madd_ref[c * BLOCKS_PER_CHUNK:(c + 1) * BLOCKS_PER_CHUNK, :]
            flag_ref[c] = (jnp.max(rows) > 0.5 * NEG_INF).astype(jnp.int32)

    bucket = ((q0 + TQ) // CMP_STRIDE - 1) // rows_per_bucket
    for b in range(CMP_BUCKETS):
        pl.when(bucket == b)(functools.partial(compress_and_select, (b + 1) * rows_per_bucket))

    @pl.when(i == 0)
    def _():
        key_row = rowt_ref[0:SEL_CHUNK, :]
        for r in range(GQA_GROUP):
            bias_ref[r] = slopes[r] * key_row

    m_ref[...] = jnp.full(m_ref.shape, NEG_INF, F32)
    acc_ref[...] = jnp.zeros(acc_ref.shape, F32)

    def scores(chunks):
        keys = [ks_ref[pl.ds(pl.multiple_of(c * SEL_CHUNK, SEL_CHUNK), SEL_CHUNK), :]
                for c in chunks]
        s = _dot_nt(jnp.concatenate(keys, axis=0) if len(keys) > 1 else keys[0], q)
        return [s[k * SEL_CHUNK:(k + 1) * SEL_CHUNK] for k in range(len(chunks))]

    def softmax_pv(chunks, s, diagonal):
        madds, rel0 = [], []
        for c in chunks:
            k0 = c * SEL_CHUNK
            madd = jnp.concatenate(
                [jnp.broadcast_to(madd_ref[pl.ds(c * BLOCKS_PER_CHUNK + b, 1), :], (SEL_BLOCK, TQ))
                 for b in range(BLOCKS_PER_CHUNK)], axis=0)
            if diagonal:
                pos = k0 + lax.broadcasted_iota(jnp.int32, (SEL_CHUNK, TQ), 0)
                madd = jnp.where(pos <= t_lane, madd, NEG_INF)
            madds.append(madd)
            rel0.append((k0 - q0).astype(F32))
        p_cols = [[] for _ in chunks]
        alphas = []
        for r in range(GQA_GROUP):
            shifts = [slopes[r] * x for x in rel0]
            us = [sk[:, heads[r]] + bias_ref[r] + mk for sk, mk in zip(s, madds)]
            m_old = m_ref[:, heads[r]]
            m_new = m_old
            for u, sh in zip(us, shifts):
                m_new = jnp.maximum(m_new, jnp.max(u, axis=0, keepdims=True) + sh)
            alphas.append(jnp.exp2(m_old - m_new))
            for k, (u, sh) in enumerate(zip(us, shifts)):
                p_cols[k].append(jnp.exp2((u - (m_new - sh)).astype(BF16)))
            m_ref[:, heads[r]] = m_new
        pv = _dot(vst_ref[0, chunks[0]], jnp.concatenate(p_cols[0], axis=1))
        for k in range(1, len(chunks)):
            pv = pv + _dot(vst_ref[0, chunks[k]], jnp.concatenate(p_cols[k], axis=1))
        acc_ref[...] = jnp.concatenate(alphas, axis=1) * acc_ref[...] + pv

    c_diag = q0 // SEL_CHUNK

    def compact(c, n):
        list_ref[n] = c
        return n + flag_ref[c]

    n_active = lax.fori_loop(0, c_diag, compact, jnp.int32(0))

    def active_group(first, size):
        chunks = [list_ref[first + k] for k in range(size)]
        softmax_pv(chunks, scores(chunks), diagonal=False)

    def group_body(p, carry):
        active_group(p * SEL_GROUP, SEL_GROUP)
        return carry

    n_full = n_active // SEL_GROUP
    lax.fori_loop(0, n_full, group_body, 0)
    done = n_full * SEL_GROUP
    size = SEL_GROUP // 2
    while size >= 1:
        pl.when(((n_active - done) & size) != 0)(functools.partial(active_group, done, size))
        done = done + ((n_active - done) & size)
        size //= 2
    diag_chunks = [c_diag + k for k in range(TQ // SEL_CHUNK)]
    softmax_pv(diag_chunks, scores(diag_chunks), diagonal=True)

    head_out = []
    for r in range(GQA_GROUP):
        gc = gate_ref[0, 3 * r + 0:3 * r + 1, :]
        gs = gate_ref[0, 3 * r + 1:3 * r + 2, :]
        gw = gate_ref[0, 3 * r + 2:3 * r + 3, :]
        acc = acc_ref[:, heads[r]]
        win = owin_ref[:, heads[r]]
        out_t = (gc * ocmp_ref[:, heads[r]]
                 + (gs / acc[HEAD_DIM:HEAD_DIM + 1]) * acc
                 + (gw / win[HEAD_DIM:HEAD_DIM + 1]) * win)
        head_out.append(out_t.T[:, 0:HEAD_DIM])
    for pair in range(GQA_GROUP // 2):
        o_ref[:, pair * LANES:(pair + 1) * LANES] = jnp.concatenate(
            head_out[2 * pair:2 * pair + 2], axis=1).astype(BF16)


def _attention(q, gates_t, kcmp, vcmp_t, ksel, vsel_t, kwin, vwin_t):
    s = q.shape[0]
    nb = s // SEL_BLOCK
    nc = CMP_PER_SEL * nb
    gw = GQA_GROUP * HEAD_DIM
    tab_rows = nc + TQ // CMP_STRIDE + nc // CMP_BUCKETS
    n_rows = max(tab_rows, WIN_SPAN)
    row_tile = jnp.asarray(np.broadcast_to(np.arange(n_rows)[:, None], (n_rows, TQ))
                           .astype(np.float32))
    per_group = lambda a: pl.BlockSpec((1,) + a.shape[1:], lambda g, i: (g,) + (0,) * (a.ndim - 1),
                                       pipeline_mode=pl.Buffered(1))
    return pl.pallas_call(
        functools.partial(_attn_kernel, nb=nb),
        grid=(N_KV, s // TQ),
        in_specs=[pl.BlockSpec((TQ, gw), lambda g, i: (i, g)),
                  pl.BlockSpec((1, GATE_ROWS, TQ), lambda g, i: (g, 0, i)),
                  _const_spec(row_tile.shape),
                  _const_spec(kcmp.shape), per_group(vcmp_t),
                  _const_spec(ksel.shape), per_group(vsel_t),
                  _const_spec(kwin.shape), per_group(vwin_t)],
        out_specs=pl.BlockSpec((TQ, gw), lambda g, i: (i, g)),
        out_shape=jax.ShapeDtypeStruct((s, N_HEADS * HEAD_DIM), BF16),
        scratch_shapes=[pltpu.VMEM((TQ // LANES, SUBLANES + nc, LANES), F32),
                        pltpu.VMEM((nb, TQ), F32),
                        pltpu.VMEM((GQA_GROUP, SEL_CHUNK, TQ), F32),
                        pltpu.VMEM((GQA_GROUP, WIN_SPAN, TQ), F32),
                        pltpu.VMEM((GQA_GROUP, tab_rows, TQ), F32),
                        pltpu.VMEM((1, GQA_GROUP * TQ), F32),
                        pltpu.VMEM((V_ROWS, GQA_GROUP * TQ), F32),
                        pltpu.VMEM((V_ROWS, GQA_GROUP * TQ), F32),
                        pltpu.VMEM((V_ROWS, GQA_GROUP * TQ), F32),
                        pltpu.SMEM((nb // BLOCKS_PER_CHUNK,), jnp.int32),
                        pltpu.SMEM((nb // BLOCKS_PER_CHUNK,), jnp.int32)],
        compiler_params=pltpu.CompilerParams(dimension_semantics=("arbitrary", "arbitrary"),
                                             vmem_limit_bytes=VMEM_LIMIT_BYTES),
        name="attn",
    )(q, gates_t, row_tile, kcmp, vcmp_t, ksel, vsel_t, kwin, vwin_t)


def _out_mlp_kernel(pool_ref, attn_ref, wout_ref, x_ref, ga1_ref, g_ref, sc_ref, sh_ref, w1_ref,
                    w2_ref, ga2_ref, o_ref, *, ff_chunk):
    mix = (_dot(pool_ref[...], wout_ref[0:POOL_WIDTH, :])
           + _dot(attn_ref[...], wout_ref[POOL_WIDTH:, :]))
    x1 = x_ref[...] + ga1_ref[...] * mix
    h = _rms_modulate(x1, g_ref[...], sc_ref[...], sh_ref[...]).astype(BF16)
    acc = jnp.zeros(x1.shape, F32)
    for c in range(D_FF // ff_chunk):
        a = _dot(h, w1_ref[:, c * ff_chunk:(c + 1) * ff_chunk])
        a = jnp.square(jnp.maximum(a, 0.0)).astype(BF16)
        acc = acc + _dot(a, w2_ref[c * ff_chunk:(c + 1) * ff_chunk, :])
    o_ref[...] = x1 + ga2_ref[...] * acc


def _out_mlp(pool_out, attn_out, w_out, x, ga1, g2, sc2, sh2, w1, w2, ga2, tm):
    s = x.shape[0]
    row = lambda w: pl.BlockSpec((tm, w), lambda i: (i, 0))
    vec = lambda w: pl.BlockSpec((1, w), lambda i: (0, 0))
    return pl.pallas_call(
        functools.partial(_out_mlp_kernel, ff_chunk=1024),
        grid=(s // tm,),
        in_specs=[row(POOL_WIDTH), row(N_HEADS * HEAD_DIM), _const_spec(w_out.shape),
                  row(D_MODEL), vec(D_MODEL), vec(D_MODEL), vec(D_MODEL), vec(D_MODEL),
                  _const_spec(w1.shape), _const_spec(w2.shape), vec(D_MODEL)],
        out_specs=row(D_MODEL),
        out_shape=jax.ShapeDtypeStruct((s, D_MODEL), F32),
        compiler_params=pltpu.CompilerParams(vmem_limit_bytes=VMEM_LIMIT_BYTES),
        name="out_mlp",
    )(pool_out, attn_out, w_out, x, ga1, g2, sc2, sh2, w1, w2, ga2)


def _pad_in_proj_weight(w_in):
    src_g = _OFF_G
    per_group = GQA_GROUP * N_BRANCH
    cols = [w_in[:, :src_g]]
    for gg in range(N_KV):
        cols += [w_in[:, src_g + gg * per_group:src_g + (gg + 1) * per_group],
                 jnp.zeros((D_MODEL, LANES - per_group), w_in.dtype)]
    return jnp.concatenate(cols, axis=1).astype(BF16)


def _group_lanes(a):
    return a.transpose(1, 0, 2).reshape(a.shape[1], KV_WIDTH)


def kernel(x, c, w_ada, b_ada, norm1_g, norm2_g, w_in, w_pool, pool_scale, q_gain, kc_gain,
           ks_gain, kw_gain, cmp_pos_k, cmp_w1_k, cmp_b1_k, cmp_w2_k, cmp_b2_k, cmp_pos_v,
           cmp_w1_v, cmp_b1_v, cmp_w2_v, cmp_b2_v, w_out, w_ff1, w_ff2):
    batch, s, _ = x.shape
    assert batch == 1 and w_ada.shape[0] == 1
    assert s % SEL_CHUNK == 0 and s >= WIN_SPAN and (s // SEL_BLOCK) & (s // SEL_BLOCK - 1) == 0
    tm = min(s, 512)
    x2 = x[0]

    mod = _ada(jnp.broadcast_to(c, (SUBLANES, D_MODEL)), w_ada[0], b_ada)[0:1]
    sh1, sc1, ga1, sh2, sc2, ga2 = [mod[:, k * D_MODEL:(k + 1) * D_MODEL] for k in range(6)]

    pair = lambda gain: jnp.tile(gain, (1, 2))
    (pool_out, q, kc, vc, ksel, vsel_t, kwin, vwin_t, gates_t) = _in_proj(
        x2, norm1_g, sc1, sh1, _pad_in_proj_weight(w_in[0]), w_pool[0].astype(BF16), pool_scale,
        pair(q_gain), pair(ks_gain), pair(kw_gain), min(s, 1024))

    chunks = lambda a: a.reshape(N_KV, s // CMP_STRIDE, CMP_STRIDE * HEAD_DIM)
    kcmp = _compress(chunks(kc), cmp_pos_k.reshape(1, -1), cmp_w1_k[0].astype(BF16), cmp_b1_k,
                     cmp_w2_k[0].astype(BF16), cmp_b2_k, kc_gain, True)
    vcmp = _compress(chunks(vc), cmp_pos_v.reshape(1, -1), cmp_w1_v[0].astype(BF16), cmp_b1_v,
                     cmp_w2_v[0].astype(BF16), cmp_b2_v, kc_gain, False)

    vcmp_t = jnp.pad(vcmp.transpose(0, 2, 1), ((0, 0), (0, V_ROWS - HEAD_DIM), (0, 0)))
    attn_out = _attention(q, gates_t, _group_lanes(kcmp), vcmp_t, ksel, vsel_t, kwin, vwin_t)

    out = _out_mlp(pool_out, attn_out, w_out[0].astype(BF16), x2, ga1, norm2_g, sc2, sh2,
                   w_ff1[0].astype(BF16), w_ff2[0].astype(BF16), ga2, tm)
    return out[None]
```

```python
import functools

import jax
import jax.numpy as jnp
import numpy as np
from jax import lax
from jax.experimental import pallas as pl
from jax.experimental.pallas import tpu as pltpu

F32 = jnp.float32
BF16 = jnp.bfloat16

LANES = 128
SUBLANES = 8
VMEM_LIMIT_BYTES = 56 * 1024 * 1024

D_MODEL = 1024
POOL_WIDTH = 512
POOL_WINDOWS = (2, 4, 8, 16)
POOL_GROUP = POOL_WIDTH // len(POOL_WINDOWS)
POOL_HALO = 16
HEAD_DIM = 64
N_HEADS = 8
N_KV = 2
GQA_GROUP = N_HEADS // N_KV
KV_WIDTH = N_KV * HEAD_DIM
N_BRANCH = 3
CMP_LEN = 32
CMP_STRIDE = 16
CMP_HIDDEN = 4 * HEAD_DIM
SEL_BLOCK = 64
N_SEL = 16
WINDOW = 512
D_FF = 4 * D_MODEL
NEG_INF = -1e30
N_FORCED = 3
EPS = 1e-6
LOG2E = 1.4426950408889634

TQ = 256
SEL_CHUNK = 256
BLOCKS_PER_CHUNK = SEL_CHUNK // SEL_BLOCK
SEL_GROUP = 8
CMP_BUCKETS = 8
WIN_SPAN = WINDOW + TQ
CMP_PER_SEL = SEL_BLOCK // CMP_STRIDE
GATE_ROWS = 16
V_ROWS = HEAD_DIM + 16

_Q_WIDTH = N_HEADS * HEAD_DIM
_OFF_U = 0
_OFF_Q = POOL_WIDTH
_OFF_KC = _OFF_Q + _Q_WIDTH
_OFF_VC = _OFF_KC + KV_WIDTH
_OFF_KS = _OFF_VC + KV_WIDTH
_OFF_VS = _OFF_KS + KV_WIDTH
_OFF_KW = _OFF_VS + KV_WIDTH
_OFF_VW = _OFF_KW + KV_WIDTH
_OFF_G = _OFF_VW + KV_WIDTH
_IN_PAD = _OFF_G + N_KV * LANES


def _dot(a, b):
    return jnp.dot(a, b, preferred_element_type=F32)


def _dot_nt(a, b):
    return lax.dot_general(a, b, (((1,), (1,)), ((), ())), preferred_element_type=F32)


def _const_spec(shape):
    nd = len(shape)
    return pl.BlockSpec(shape, lambda *_: (0,) * nd, pipeline_mode=pl.Buffered(1))


def _ada_kernel(c_ref, w_ref, b_ref, o_ref):
    o_ref[...] = jnp.dot(c_ref[...], w_ref[...], preferred_element_type=F32,
                         precision=lax.Precision.HIGHEST) + b_ref[...]


def _ada(c8, w, b):
    n = w.shape[1]
    bn = 1024
    return pl.pallas_call(
        _ada_kernel,
        grid=(n // bn,),
        in_specs=[pl.BlockSpec((SUBLANES, D_MODEL), lambda j: (0, 0)),
                  pl.BlockSpec((D_MODEL, bn), lambda j: (0, j)),
                  pl.BlockSpec((1, bn), lambda j: (0, j))],
        out_specs=pl.BlockSpec((SUBLANES, bn), lambda j: (0, j)),
        out_shape=jax.ShapeDtypeStruct((SUBLANES, n), F32),
        compiler_params=pltpu.CompilerParams(vmem_limit_bytes=VMEM_LIMIT_BYTES),
        name="ada",
    )(c8, w, b)


def _rms_modulate(x, g, sc, sh):
    ms = jnp.mean(x * x, axis=-1, keepdims=True)
    return (x * lax.rsqrt(ms + EPS)) * (g * (1.0 + sc)) + sh


def _head_norm_pair(x, gain2):
    lane = lax.broadcasted_iota(jnp.int32, x.shape, 1)
    lo = lane < HEAD_DIM
    sq = x * x
    s_lo = jnp.sum(jnp.where(lo, sq, 0.0), axis=-1, keepdims=True)
    s_hi = jnp.sum(jnp.where(lo, 0.0, sq), axis=-1, keepdims=True)
    ms = jnp.where(lo, s_lo, s_hi) * (1.0 / HEAD_DIM)
    return x * lax.rsqrt(ms + EPS) * gain2


def _in_proj_kernel(x_ref, g_ref, sc_ref, sh_ref, w_ref, wpool_ref, pscale_ref, qg_ref, ksg_ref,
                    kwg_ref, pool_ref, q_ref, kc_ref, vc_ref, ks_ref, vs_ref, kw_ref, vw_ref,
                    gate_ref, ebuf_ref, *, tm):
    i = pl.program_id(0)
    h = _rms_modulate(x_ref[...], g_ref[...], sc_ref[...], sh_ref[...]).astype(BF16)

    def project(off, width):
        return _dot(h, w_ref[:, off:off + width])

    q_all = project(_OFF_Q, _Q_WIDTH)
    for pair in range(N_HEADS // 2):
        qn = _head_norm_pair(q_all[:, pair * LANES:(pair + 1) * LANES], qg_ref[...])
        q_ref[:, pair * LANES:(pair + 1) * LANES] = (qn * (HEAD_DIM ** -0.5 * LOG2E)).astype(BF16)

    kv = project(_OFF_KC, _IN_PAD - _OFF_KC)
    kv_cols = lambda off, width: kv[:, off - _OFF_KC:off - _OFF_KC + width]
    for gg in range(N_KV):
        kc_ref[gg] = kv_cols(_OFF_KC + gg * HEAD_DIM, HEAD_DIM)
        vc_ref[gg] = kv_cols(_OFF_VC + gg * HEAD_DIM, HEAD_DIM)
    ks_ref[...] = _head_norm_pair(kv_cols(_OFF_KS, KV_WIDTH), ksg_ref[...]).astype(BF16)
    kw_ref[...] = _head_norm_pair(kv_cols(_OFF_KW, KV_WIDTH), kwg_ref[...]).astype(BF16)
    def value_tiles(ref, off, chunk):
        ones_row = jnp.where(lax.broadcasted_iota(jnp.int32, (V_ROWS - HEAD_DIM, chunk), 0) == 0,
                             1.0, 0.0)
        for k in range(tm // chunk):
            vt = kv_cols(off, KV_WIDTH)[k * chunk:(k + 1) * chunk].T
            for gg in range(N_KV):
                ref[gg, k] = jnp.concatenate(
                    [vt[gg * HEAD_DIM:(gg + 1) * HEAD_DIM], ones_row], axis=0).astype(BF16)

    value_tiles(vs_ref, _OFF_VS, SEL_CHUNK)
    value_tiles(vw_ref, _OFF_VW, TQ)
    gates_t = jax.nn.sigmoid(kv_cols(_OFF_G, N_KV * LANES)).T
    for gg in range(N_KV):
        gate_ref[gg] = gates_t[gg * LANES:gg * LANES + GATE_ROWS]

    @pl.when(i == 0)
    def _():
        ebuf_ref[0:POOL_HALO, :] = jnp.zeros((POOL_HALO, POOL_WIDTH), F32)

    u = project(_OFF_U, POOL_WIDTH)
    ebuf_ref[POOL_HALO:POOL_HALO + tm, :] = u
    t1 = i * tm + lax.broadcasted_iota(jnp.int32, (tm, POOL_GROUP), 0) + 1
    for gi, w in enumerate(POOL_WINDOWS):
        c0 = gi * POOL_GROUP
        win = u[:, c0:c0 + POOL_GROUP]
        for k in range(1, w):
            win = win + ebuf_ref[POOL_HALO - k:POOL_HALO - k + tm, c0:c0 + POOL_GROUP]
        cnt = jnp.minimum(t1, w).astype(F32)
        pooled = win / cnt - u[:, c0:c0 + POOL_GROUP]
        y = _dot(pooled.astype(BF16), wpool_ref[gi])
        pool_ref[:, c0:c0 + POOL_GROUP] = (y * pscale_ref[:, c0:c0 + POOL_GROUP]).astype(BF16)
    ebuf_ref[0:POOL_HALO, :] = ebuf_ref[tm:tm + POOL_HALO, :]


def _in_proj(x, g1, sc1, sh1, w_in_p, w_pool, pool_scale, qg, ksg, kwg, tm):
    s = x.shape[0]
    row = lambda w: pl.BlockSpec((tm, w), lambda i: (i, 0))
    vec = lambda w: pl.BlockSpec((1, w), lambda i: (0, 0))
    per_head = pl.BlockSpec((N_KV, tm, HEAD_DIM), lambda i: (0, i, 0))
    chunks_t = lambda ch: pl.BlockSpec((N_KV, tm // ch, V_ROWS, ch), lambda i: (0, i, 0, 0))
    out_shape = [
        jax.ShapeDtypeStruct((s, POOL_WIDTH), BF16),
        jax.ShapeDtypeStruct((s, _Q_WIDTH), BF16),
        jax.ShapeDtypeStruct((N_KV, s, HEAD_DIM), F32),
        jax.ShapeDtypeStruct((N_KV, s, HEAD_DIM), F32),
        jax.ShapeDtypeStruct((s, KV_WIDTH), BF16),
        jax.ShapeDtypeStruct((N_KV, s // SEL_CHUNK, V_ROWS, SEL_CHUNK), BF16),
        jax.ShapeDtypeStruct((s, KV_WIDTH), BF16),
        jax.ShapeDtypeStruct((N_KV, s // TQ, V_ROWS, TQ), BF16),
        jax.ShapeDtypeStruct((N_KV, GATE_ROWS, s), F32),
    ]
    return pl.pallas_call(
        functools.partial(_in_proj_kernel, tm=tm),
        grid=(s // tm,),
        in_specs=[row(D_MODEL), vec(D_MODEL), vec(D_MODEL), vec(D_MODEL),
                  _const_spec(w_in_p.shape), _const_spec(w_pool.shape), vec(POOL_WIDTH),
                  vec(LANES), vec(LANES), vec(LANES)],
        out_specs=[row(POOL_WIDTH), row(_Q_WIDTH), per_head, per_head, row(KV_WIDTH),
                   chunks_t(SEL_CHUNK), row(KV_WIDTH), chunks_t(TQ),
                   pl.BlockSpec((N_KV, GATE_ROWS, tm), lambda i: (0, 0, i))],
        out_shape=out_shape,
        scratch_shapes=[pltpu.VMEM((tm + POOL_HALO, POOL_WIDTH), F32)],
        compiler_params=pltpu.CompilerParams(dimension_semantics=("arbitrary",),
                                             vmem_limit_bytes=VMEM_LIMIT_BYTES),
        name="in_proj",
    )(x, g1, sc1, sh1, w_in_p, w_pool, pool_scale, qg, ksg, kwg)


def _compress_kernel(c_ref, pos_ref, w1_ref, b1_ref, w2_ref, b2_ref, gain_ref, o_ref, *,
                     normalize):
    half = CMP_STRIDE * HEAD_DIM
    n_rows = c_ref.shape[1]
    c = c_ref[0]
    first = _dot((c + pos_ref[:, 0:half]).astype(BF16), w1_ref[0:half, :])
    second = _dot((c + pos_ref[:, half:2 * half]).astype(BF16), w1_ref[half:2 * half, :])
    hid = jax.nn.gelu(first + pltpu.roll(second, n_rows - 1, axis=0) + b1_ref[...])
    y = _dot(hid.astype(BF16), w2_ref[...]) + b2_ref[...]
    if normalize:
        ms = jnp.mean(y * y, axis=-1, keepdims=True)
        y = y * lax.rsqrt(ms + EPS) * gain_ref[...]
    o_ref[0] = y.astype(BF16)


def _compress(chunks, pos, w1, b1, w2, b2, gain, normalize):
    _, n_chunks, width = chunks.shape
    vec = lambda w: pl.BlockSpec((1, w), lambda g: (0, 0))
    return pl.pallas_call(
        functools.partial(_compress_kernel, normalize=normalize),
        grid=(N_KV,),
        in_specs=[pl.BlockSpec((1, n_chunks, width), lambda g: (g, 0, 0)),
                  vec(2 * width), _const_spec(w1.shape), vec(CMP_HIDDEN), _const_spec(w2.shape),
                  vec(HEAD_DIM), vec(HEAD_DIM)],
        out_specs=pl.BlockSpec((1, n_chunks, HEAD_DIM), lambda g: (g, 0, 0)),
        out_shape=jax.ShapeDtypeStruct((N_KV, n_chunks, HEAD_DIM), BF16),
        compiler_params=pltpu.CompilerParams(vmem_limit_bytes=VMEM_LIMIT_BYTES),
        name="compress",
    )(chunks, pos, w1, b1, w2, b2, gain)


def _attn_kernel(q_ref, gate_ref, rowt_ref, kc_ref, vct_ref, ks_ref, vst_ref, kw_ref, vwt_ref,
                 o_ref, psum_ref, madd_ref, bias_ref, wbias_ref, ctab_ref, m_ref, acc_ref, ocmp_ref,
                 owin_ref, flag_ref, list_ref, *, nb):
    g = pl.program_id(0)
    i = pl.program_id(1)
    q0 = i * TQ
    nc = CMP_PER_SEL * nb
    n_chunks = nb // BLOCKS_PER_CHUNK
    gslope = jnp.where(g == 0, LOG2E, LOG2E * 2.0 ** -GQA_GROUP).astype(F32)
    slopes = [gslope * (2.0 ** -(r + 1)) for r in range(GQA_GROUP)]
    heads = [slice(r * TQ, (r + 1) * TQ) for r in range(GQA_GROUP)]

    lane = lax.broadcasted_iota(jnp.int32, (TQ, LANES), 1)
    own_half = (lane >= HEAD_DIM) == (g == 1)
    q_rows = []
    for r in range(GQA_GROUP):
        pair = q_ref[:, (r // 2) * LANES:(r // 2 + 1) * LANES]
        swapped = jnp.concatenate([pair[:, HEAD_DIM:], pair[:, :HEAD_DIM]], axis=1)
        q_rows.append(jnp.where(own_half, jnp.where(g == r % 2, pair, swapped), 0.0))
    q = jnp.concatenate(q_rows, axis=0)
    t_lane = q0 + lax.broadcasted_iota(jnp.int32, (1, TQ), 1)

    w0 = pl.multiple_of(jnp.maximum(q0 - WINDOW, 0), TQ)
    s_w = _dot_nt(kw_ref[pl.ds(w0, WIN_SPAN), :], q)

    @pl.when(i <= WINDOW // TQ)
    def _():
        row_w = rowt_ref[0:WIN_SPAN, :]
        lane_w = lax.broadcasted_iota(jnp.int32, (WIN_SPAN, TQ), 1).astype(F32)
        dq = (q0 - w0).astype(F32)
        d_w = lane_w - row_w + dq
        ok_w = jnp.abs(d_w - (WINDOW - 1) * 0.5) < WINDOW * 0.5
        rel_w = row_w - dq
        for r in range(GQA_GROUP):
            wbias_ref[r] = jnp.where(ok_w, slopes[r] * rel_w, NEG_INF)

    wb = w0 // TQ
    for r in range(GQA_GROUP):
        s = s_w[:, heads[r]] + wbias_ref[r]
        m = jnp.max(s, axis=0, keepdims=True)
        p = jnp.exp2((s - m).astype(BF16))
        o_win = _dot(vwt_ref[0, wb], p[0:TQ])
        for b in range(1, WIN_SPAN // TQ):
            o_win = o_win + _dot(vwt_ref[0, wb + b], p[b * TQ:(b + 1) * TQ])
        owin_ref[:, heads[r]] = o_win

    rows_per_bucket = nc // CMP_BUCKETS
    tab_rows = ctab_ref.shape[1]

    @pl.when(i == 0)
    def _():
        rel = CMP_STRIDE * (rowt_ref[0:tab_rows, :] - nc) + (CMP_LEN - 1)
        ok = rel <= lax.broadcasted_iota(jnp.int32, (1, TQ), 1).astype(F32)
        for r in range(GQA_GROUP):
            ctab_ref[r] = jnp.where(ok, slopes[r] * rel, NEG_INF)

    tab0 = pl.multiple_of(nc - q0 // CMP_STRIDE, CMP_STRIDE)

    def compress_and_select(n_rows):
        n_blk = n_rows // CMP_PER_SEL
        s_c = _dot_nt(kc_ref[0:n_rows, :], q)
        p_sum = jnp.zeros((n_rows, TQ), F32)
        p_cols = []
        for r in range(GQA_GROUP):
            s = s_c[:, heads[r]] + ctab_ref[r, pl.ds(tab0, n_rows), :]
            m = jnp.max(s, axis=0, keepdims=True)
            e = jnp.exp2(s - m)
            l = jnp.sum(e, axis=0, keepdims=True)
            p = e * jnp.where(m > 0.5 * NEG_INF, 1.0 / l, 0.0)
            p_sum = p_sum + p
            p_cols.append(p.astype(BF16))
        ocmp_ref[...] = _dot(vct_ref[0, :, 0:n_rows], jnp.concatenate(p_cols, axis=1))

        for h in range(TQ // LANES):
            psum_ref[h, 0:SUBLANES, :] = jnp.zeros((SUBLANES, LANES), F32)
            psum_ref[h, SUBLANES:SUBLANES + n_rows, :] = p_sum[:, h * LANES:(h + 1) * LANES]

        def every4(start):
            parts = [psum_ref[h, pl.ds(SUBLANES + start, n_blk, stride=CMP_PER_SEL), :]
                     for h in range(TQ // LANES)]
            return parts[0] if len(parts) == 1 else jnp.concatenate(parts, axis=1)

        imp = every4(0) + every4(1) + every4(2) + 0.5 * every4(3) + 0.5 * every4(-1)
        blk = lax.broadcasted_iota(jnp.int32, (n_blk, TQ), 0)
        cur = lax.shift_right_logical(t_lane, SEL_BLOCK.bit_length() - 1)
        causal = blk <= cur
        forced = jnp.where(blk == 0, 1.0, 0.0) + jnp.where(blk == cur, 1.0, 0.0) \
            + jnp.where(blk == cur - 1, 1.0, 0.0)
        val = jnp.where(causal, jnp.where(forced > 0.0, -2.0, imp), -1.0)

        def pick(_, v):
            blk_f = rowt_ref[0:n_blk, :]
            mx = jnp.max(v, axis=0, keepdims=True)
            idx = jnp.min(jnp.where(v == mx, blk_f, float(n_blk)), axis=0, keepdims=True)
            return jnp.where(blk_f == idx, -2.0, v)

        picked = lax.fori_loop(0, min(N_SEL, n_blk) - N_FORCED, pick, val)
        madd_ref[0:n_blk, :] = jnp.where(causal, jnp.where(picked < -1.5, 0.0, NEG_INF), NEG_INF)
        for c in range(n_blk // BLOCKS_PER_CHUNK):
            rows = madd_ref[c * BLOCKS_PER_CHUNK:(c + 1) * BLOCKS_PER_CHUNK, :]
            flag_ref[c] = (jnp.max(rows) > 0.5 * NEG_INF).astype(jnp.int32)

    bucket = ((q0 + TQ) // CMP_STRIDE - 1) // rows_per_bucket
    for b in range(CMP_BUCKETS):
        pl.when(bucket == b)(functools.partial(compress_and_select, (b + 1) * rows_per_bucket))

    @pl.when(i == 0)
    def _():
        key_row = rowt_ref[0:SEL_CHUNK, :]
        for r in range(GQA_GROUP):
            bias_ref[r] = slopes[r] * key_row

    m_ref[...] = jnp.full(m_ref.shape, NEG_INF, F32)
    acc_ref[...] = jnp.zeros(acc_ref.shape, F32)

    def scores(chunks):
        keys = [ks_ref[pl.ds(pl.multiple_of(c * SEL_CHUNK, SEL_CHUNK), SEL_CHUNK), :]
                for c in chunks]
        s = _dot_nt(jnp.concatenate(keys, axis=0) if len(keys) > 1 else keys[0], q)
        return [s[k * SEL_CHUNK:(k + 1) * SEL_CHUNK] for k in range(len(chunks))]

    def softmax_pv(chunks, s, last_is_own):
        madds, rel0 = [], []
        for k, c in enumerate(chunks):
            k0 = c * SEL_CHUNK
            madd = jnp.concatenate(
                [jnp.broadcast_to(madd_ref[pl.ds(c * BLOCKS_PER_CHUNK + b, 1), :], (SEL_BLOCK, TQ))
                 for b in range(BLOCKS_PER_CHUNK)], axis=0)
            if last_is_own and k == len(chunks) - 1:
                pos = k0 + lax.broadcasted_iota(jnp.int32, (SEL_CHUNK, TQ), 0)
                madd = jnp.where(pos <= t_lane, madd, NEG_INF)
            madds.append(madd)
            rel0.append((k0 - q0).astype(F32))
        p_cols = [[] for _ in chunks]
        alphas = []
        for r in range(GQA_GROUP):
            shifts = [slopes[r] * x for x in rel0]
            us = [sk[:, heads[r]] + bias_ref[r] + mk for sk, mk in zip(s, madds)]
            m_old = m_ref[:, heads[r]]
            m_new = m_old
            for u, sh in zip(us, shifts):
                m_new = jnp.maximum(m_new, jnp.max(u, axis=0, keepdims=True) + sh)
            alphas.append(jnp.exp2(m_old - m_new))
            for k, (u, sh) in enumerate(zip(us, shifts)):
                p_cols[k].append(jnp.exp2((u - (m_new - sh)).astype(BF16)))
            m_ref[:, heads[r]] = m_new
        pv = _dot(vst_ref[0, chunks[0]], jnp.concatenate(p_cols[0], axis=1))
        for k in range(1, len(chunks)):
            pv = pv + _dot(vst_ref[0, chunks[k]], jnp.concatenate(p_cols[k], axis=1))
        acc_ref[...] = jnp.concatenate(alphas, axis=1) * acc_ref[...] + pv

    c_diag = q0 // SEL_CHUNK

    def compact(c, n):
        list_ref[n] = c
        return n + flag_ref[c]

    n_earlier = lax.fori_loop(0, c_diag, compact, jnp.int32(0))
    list_ref[n_earlier] = c_diag

    def active_group(first, size, last_is_own):
        chunks = [list_ref[first + k] for k in range(size)]
        softmax_pv(chunks, scores(chunks), last_is_own)

    def group_body(p, carry):
        active_group(p * SEL_GROUP, SEL_GROUP, False)
        return carry

    n_full = n_earlier // SEL_GROUP
    lax.fori_loop(0, n_full, group_body, 0)
    done = n_full * SEL_GROUP
    for size in range(1, SEL_GROUP + 1):
        pl.when(n_earlier + 1 - done == size)(functools.partial(active_group, done, size, True))

    head_out = []
    for r in range(GQA_GROUP):
        gc = gate_ref[0, 3 * r + 0:3 * r + 1, :]
        gs = gate_ref[0, 3 * r + 1:3 * r + 2, :]
        gw = gate_ref[0, 3 * r + 2:3 * r + 3, :]
        acc = acc_ref[:, heads[r]]
        win = owin_ref[:, heads[r]]
        out_t = (gc * ocmp_ref[:, heads[r]]
                 + (gs / acc[HEAD_DIM:HEAD_DIM + 1]) * acc
                 + (gw / win[HEAD_DIM:HEAD_DIM + 1]) * win)
        head_out.append(out_t.T[:, 0:HEAD_DIM])
    for pair in range(GQA_GROUP // 2):
        o_ref[:, pair * LANES:(pair + 1) * LANES] = jnp.concatenate(
            head_out[2 * pair:2 * pair + 2], axis=1).astype(BF16)


def _attention(q, gates_t, kcmp, vcmp_t, ksel, vsel_t, kwin, vwin_t):
    assert TQ == SEL_CHUNK
    s = q.shape[0]
    nb = s // SEL_BLOCK
    nc = CMP_PER_SEL * nb
    gw = GQA_GROUP * HEAD_DIM
    tab_rows = nc + TQ // CMP_STRIDE + nc // CMP_BUCKETS
    n_rows = max(tab_rows, WIN_SPAN)
    row_tile = jnp.asarray(np.broadcast_to(np.arange(n_rows)[:, None], (n_rows, TQ))
                           .astype(np.float32))
    per_group = lambda a: pl.BlockSpec((1,) + a.shape[1:], lambda g, i: (g,) + (0,) * (a.ndim - 1),
                                       pipeline_mode=pl.Buffered(1))
    return pl.pallas_call(
        functools.partial(_attn_kernel, nb=nb),
        grid=(N_KV, s // TQ),
        in_specs=[pl.BlockSpec((TQ, gw), lambda g, i: (i, g)),
                  pl.BlockSpec((1, GATE_ROWS, TQ), lambda g, i: (g, 0, i)),
                  _const_spec(row_tile.shape),
                  _const_spec(kcmp.shape), per_group(vcmp_t),
                  _const_spec(ksel.shape), per_group(vsel_t),
                  _const_spec(kwin.shape), per_group(vwin_t)],
        out_specs=pl.BlockSpec((TQ, gw), lambda g, i: (i, g)),
        out_shape=jax.ShapeDtypeStruct((s, N_HEADS * HEAD_DIM), BF16),
        scratch_shapes=[pltpu.VMEM((TQ // LANES, SUBLANES + nc, LANES), F32),
                        pltpu.VMEM((nb, TQ), F32),
                        pltpu.VMEM((GQA_GROUP, SEL_CHUNK, TQ), F32),
                        pltpu.VMEM((GQA_GROUP, WIN_SPAN, TQ), F32),
                        pltpu.VMEM((GQA_GROUP, tab_rows, TQ), F32),
                        pltpu.VMEM((1, GQA_GROUP * TQ), F32),
                        pltpu.VMEM((V_ROWS, GQA_GROUP * TQ), F32),
                        pltpu.VMEM((V_ROWS, GQA_GROUP * TQ), F32),
                        pltpu.VMEM((V_ROWS, GQA_GROUP * TQ), F32),
                        pltpu.SMEM((nb // BLOCKS_PER_CHUNK,), jnp.int32),
                        pltpu.SMEM((nb // BLOCKS_PER_CHUNK,), jnp.int32)],
        compiler_params=pltpu.CompilerParams(dimension_semantics=("arbitrary", "arbitrary"),
                                             vmem_limit_bytes=VMEM_LIMIT_BYTES),
        name="attn",
    )(q, gates_t, row_tile, kcmp, vcmp_t, ksel, vsel_t, kwin, vwin_t)


def _out_mlp_kernel(pool_ref, attn_ref, wout_ref, x_ref, ga1_ref, g_ref, sc_ref, sh_ref, w1_ref,
                    w2_ref, ga2_ref, o_ref, *, ff_chunk):
    mix = (_dot(pool_ref[...], wout_ref[0:POOL_WIDTH, :])
           + _dot(attn_ref[...], wout_ref[POOL_WIDTH:, :]))
    x1 = x_ref[...] + ga1_ref[...] * mix
    h = _rms_modulate(x1, g_ref[...], sc_ref[...], sh_ref[...]).astype(BF16)
    acc = jnp.zeros(x1.shape, F32)
    for c in range(D_FF // ff_chunk):
        a = _dot(h, w1_ref[:, c * ff_chunk:(c + 1) * ff_chunk])
        a = jnp.square(jnp.maximum(a, 0.0)).astype(BF16)
        acc = acc + _dot(a, w2_ref[c * ff_chunk:(c + 1) * ff_chunk, :])
    o_ref[...] = x1 + ga2_ref[...] * acc


def _out_mlp(pool_out, attn_out, w_out, x, ga1, g2, sc2, sh2, w1, w2, ga2, tm):
    s = x.shape[0]
    row = lambda w: pl.BlockSpec((tm, w), lambda i: (i, 0))
    vec = lambda w: pl.BlockSpec((1, w), lambda i: (0, 0))
    return pl.pallas_call(
        functools.partial(_out_mlp_kernel, ff_chunk=1024),
        grid=(s // tm,),
        in_specs=[row(POOL_WIDTH), row(N_HEADS * HEAD_DIM), _const_spec(w_out.shape),
                  row(D_MODEL), vec(D_MODEL), vec(D_MODEL), vec(D_MODEL), vec(D_MODEL),
                  _const_spec(w1.shape), _const_spec(w2.shape), vec(D_MODEL)],
        out_specs=row(D_MODEL),
        out_shape=jax.ShapeDtypeStruct((s, D_MODEL), F32),
        compiler_params=pltpu.CompilerParams(vmem_limit_bytes=VMEM_LIMIT_BYTES),
        name="out_mlp",
    )(pool_out, attn_out, w_out, x, ga1, g2, sc2, sh2, w1, w2, ga2)


def _pad_in_proj_weight(w_in):
    src_g = _OFF_G
    per_group = GQA_GROUP * N_BRANCH
    cols = [w_in[:, :src_g]]
    for gg in range(N_KV):
        cols += [w_in[:, src_g + gg * per_group:src_g + (gg + 1) * per_group],
                 jnp.zeros((D_MODEL, LANES - per_group), w_in.dtype)]
    return jnp.concatenate(cols, axis=1).astype(BF16)


def _group_lanes(a):
    return a.transpose(1, 0, 2).reshape(a.shape[1], KV_WIDTH)


def kernel(x, c, w_ada, b_ada, norm1_g, norm2_g, w_in, w_pool, pool_scale, q_gain, kc_gain,
           ks_gain, kw_gain, cmp_pos_k, cmp_w1_k, cmp_b1_k, cmp_w2_k, cmp_b2_k, cmp_pos_v,
           cmp_w1_v, cmp_b1_v, cmp_w2_v, cmp_b2_v, w_out, w_ff1, w_ff2):
    batch, s, _ = x.shape
    assert batch == 1 and w_ada.shape[0] == 1
    assert s % SEL_CHUNK == 0 and s >= WIN_SPAN and (s // SEL_BLOCK) & (s // SEL_BLOCK - 1) == 0
    tm = min(s, 512)
    x2 = x[0]

    mod = _ada(jnp.broadcast_to(c, (SUBLANES, D_MODEL)), w_ada[0], b_ada)[0:1]
    sh1, sc1, ga1, sh2, sc2, ga2 = [mod[:, k * D_MODEL:(k + 1) * D_MODEL] for k in range(6)]

    pair = lambda gain: jnp.tile(gain, (1, 2))
    (pool_out, q, kc, vc, ksel, vsel_t, kwin, vwin_t, gates_t) = _in_proj(
        x2, norm1_g, sc1, sh1, _pad_in_proj_weight(w_in[0]), w_pool[0].astype(BF16), pool_scale,
        pair(q_gain), pair(ks_gain), pair(kw_gain), min(s, 1024))

    chunks = lambda a: a.reshape(N_KV, s // CMP_STRIDE, CMP_STRIDE * HEAD_DIM)
    kcmp = _compress(chunks(kc), cmp_pos_k.reshape(1, -1), cmp_w1_k[0].astype(BF16), cmp_b1_k,
                     cmp_w2_k[0].astype(BF16), cmp_b2_k, kc_gain, True)
    vcmp = _compress(chunks(vc), cmp_pos_v.reshape(1, -1), cmp_w1_v[0].astype(BF16), cmp_b1_v,
                     cmp_w2_v[0].astype(BF16), cmp_b2_v, kc_gain, False)

    vcmp_t = jnp.pad(vcmp.transpose(0, 2, 1), ((0, 0), (0, V_ROWS - HEAD_DIM), (0, 0)))
    attn_out = _attention(q, gates_t, _group_lanes(kcmp), vcmp_t, ksel, vsel_t, kwin, vwin_t)

    out = _out_mlp(pool_out, attn_out, w_out[0].astype(BF16), x2, ga1, norm2_g, sc2, sh2,
                   w_ff1[0].astype(BF16), w_ff2[0].astype(BF16), ga2, tm)
    return out[None]
```

```python
import functools

import jax
import jax.numpy as jnp
import numpy as np
from jax import lax
from jax.experimental import pallas as pl
from jax.experimental.pallas import tpu as pltpu

F32 = jnp.float32
BF16 = jnp.bfloat16

LANES = 128
SUBLANES = 8
VMEM_LIMIT_BYTES = 56 * 1024 * 1024

D_MODEL = 1024
POOL_WIDTH = 512
POOL_WINDOWS = (2, 4, 8, 16)
POOL_GROUP = POOL_WIDTH // len(POOL_WINDOWS)
POOL_HALO = 16
HEAD_DIM = 64
N_HEADS = 8
N_KV = 2
GQA_GROUP = N_HEADS // N_KV
KV_WIDTH = N_KV * HEAD_DIM
N_BRANCH = 3
CMP_LEN = 32
CMP_STRIDE = 16
CMP_HIDDEN = 4 * HEAD_DIM
SEL_BLOCK = 64
N_SEL = 16
WINDOW = 512
D_FF = 4 * D_MODEL
NEG_INF = -1e30
N_FORCED = 3
EPS = 1e-6
LOG2E = 1.4426950408889634

TQ = 256
SEL_CHUNK = 256
BLOCKS_PER_CHUNK = SEL_CHUNK // SEL_BLOCK
SEL_GROUP = 8
CMP_BUCKETS = 8
WIN_SPAN = WINDOW + TQ
CMP_PER_SEL = SEL_BLOCK // CMP_STRIDE
GATE_ROWS = 16
V_ROWS = HEAD_DIM + 16

_Q_WIDTH = N_HEADS * HEAD_DIM
_OFF_U = 0
_OFF_Q = POOL_WIDTH
_OFF_KC = _OFF_Q + _Q_WIDTH
_OFF_VC = _OFF_KC + KV_WIDTH
_OFF_KS = _OFF_VC + KV_WIDTH
_OFF_VS = _OFF_KS + KV_WIDTH
_OFF_KW = _OFF_VS + KV_WIDTH
_OFF_VW = _OFF_KW + KV_WIDTH
_OFF_G = _OFF_VW + KV_WIDTH
_IN_PAD = _OFF_G + N_KV * LANES


def _dot(a, b):
    return jnp.dot(a, b, preferred_element_type=F32)


def _dot_nt(a, b):
    return lax.dot_general(a, b, (((1,), (1,)), ((), ())), preferred_element_type=F32)


def _const_spec(shape):
    nd = len(shape)
    return pl.BlockSpec(shape, lambda *_: (0,) * nd, pipeline_mode=pl.Buffered(1))


def _ada_kernel(c_ref, w_ref, b_ref, o_ref):
    o_ref[...] = jnp.dot(c_ref[...], w_ref[...], preferred_element_type=F32,
                         precision=lax.Precision.HIGHEST) + b_ref[...]


def _ada(c8, w, b):
    n = w.shape[1]
    bn = 1024
    return pl.pallas_call(
        _ada_kernel,
        grid=(n // bn,),
        in_specs=[pl.BlockSpec((SUBLANES, D_MODEL), lambda j: (0, 0)),
                  pl.BlockSpec((D_MODEL, bn), lambda j: (0, j)),
                  pl.BlockSpec((1, bn), lambda j: (0, j))],
        out_specs=pl.BlockSpec((SUBLANES, bn), lambda j: (0, j)),
        out_shape=jax.ShapeDtypeStruct((SUBLANES, n), F32),
        compiler_params=pltpu.CompilerParams(vmem_limit_bytes=VMEM_LIMIT_BYTES),
        name="ada",
    )(c8, w, b)


def _rms_modulate(x, g, sc, sh):
    ms = jnp.mean(x * x, axis=-1, keepdims=True)
    return (x * lax.rsqrt(ms + EPS)) * (g * (1.0 + sc)) + sh


def _head_norm_pair(x, gain2):
    lane = lax.broadcasted_iota(jnp.int32, x.shape, 1)
    lo = lane < HEAD_DIM
    sq = x * x
    s_lo = jnp.sum(jnp.where(lo, sq, 0.0), axis=-1, keepdims=True)
    s_hi = jnp.sum(jnp.where(lo, 0.0, sq), axis=-1, keepdims=True)
    ms = jnp.where(lo, s_lo, s_hi) * (1.0 / HEAD_DIM)
    return x * lax.rsqrt(ms + EPS) * gain2


def _in_proj_kernel(x_ref, g_ref, sc_ref, sh_ref, w_ref, wpool_ref, pscale_ref, qg_ref, ksg_ref,
                    kwg_ref, pool_ref, q_ref, kc_ref, vc_ref, ks_ref, vs_ref, kw_ref, vw_ref,
                    gate_ref, ebuf_ref, *, tm):
    i = pl.program_id(0)
    h = _rms_modulate(x_ref[...], g_ref[...], sc_ref[...], sh_ref[...]).astype(BF16)

    def project(off, width):
        return _dot(h, w_ref[:, off:off + width])

    q_all = project(_OFF_Q, _Q_WIDTH)
    for pair in range(N_HEADS // 2):
        qn = _head_norm_pair(q_all[:, pair * LANES:(pair + 1) * LANES], qg_ref[...])
        q_ref[:, pair * LANES:(pair + 1) * LANES] = (qn * (HEAD_DIM ** -0.5 * LOG2E)).astype(BF16)

    kv = project(_OFF_KC, _IN_PAD - _OFF_KC)
    kv_cols = lambda off, width: kv[:, off - _OFF_KC:off - _OFF_KC + width]
    for gg in range(N_KV):
        kc_ref[gg] = kv_cols(_OFF_KC + gg * HEAD_DIM, HEAD_DIM)
        vc_ref[gg] = kv_cols(_OFF_VC + gg * HEAD_DIM, HEAD_DIM)
    ks_ref[...] = _head_norm_pair(kv_cols(_OFF_KS, KV_WIDTH), ksg_ref[...]).astype(BF16)
    kw_ref[...] = _head_norm_pair(kv_cols(_OFF_KW, KV_WIDTH), kwg_ref[...]).astype(BF16)
    def value_tiles(ref, off, chunk):
        ones_row = jnp.where(lax.broadcasted_iota(jnp.int32, (V_ROWS - HEAD_DIM, chunk), 0) == 0,
                             1.0, 0.0)
        for k in range(tm // chunk):
            vt = kv_cols(off, KV_WIDTH)[k * chunk:(k + 1) * chunk].T
            for gg in range(N_KV):
                ref[gg, k] = jnp.concatenate(
                    [vt[gg * HEAD_DIM:(gg + 1) * HEAD_DIM], ones_row], axis=0).astype(BF16)

    value_tiles(vs_ref, _OFF_VS, SEL_CHUNK)
    value_tiles(vw_ref, _OFF_VW, TQ)
    gates_t = jax.nn.sigmoid(kv_cols(_OFF_G, N_KV * LANES)).T
    for gg in range(N_KV):
        gate_ref[gg] = gates_t[gg * LANES:gg * LANES + GATE_ROWS]

    @pl.when(i == 0)
    def _():
        ebuf_ref[0:POOL_HALO, :] = jnp.zeros((POOL_HALO, POOL_WIDTH), F32)

    u = project(_OFF_U, POOL_WIDTH)
    ebuf_ref[POOL_HALO:POOL_HALO + tm, :] = u
    t1 = i * tm + lax.broadcasted_iota(jnp.int32, (tm, POOL_GROUP), 0) + 1
    for gi, w in enumerate(POOL_WINDOWS):
        c0 = gi * POOL_GROUP
        win = u[:, c0:c0 + POOL_GROUP]
        for k in range(1, w):
            win = win + ebuf_ref[POOL_HALO - k:POOL_HALO - k + tm, c0:c0 + POOL_GROUP]
        cnt = jnp.minimum(t1, w).astype(F32)
        pooled = win / cnt - u[:, c0:c0 + POOL_GROUP]
        y = _dot(pooled.astype(BF16), wpool_ref[gi])
        pool_ref[:, c0:c0 + POOL_GROUP] = (y * pscale_ref[:, c0:c0 + POOL_GROUP]).astype(BF16)
    ebuf_ref[0:POOL_HALO, :] = ebuf_ref[tm:tm + POOL_HALO, :]


def _in_proj(x, g1, sc1, sh1, w_in_p, w_pool, pool_scale, qg, ksg, kwg, tm):
    s = x.shape[0]
    row = lambda w: pl.BlockSpec((tm, w), lambda i: (i, 0))
    vec = lambda w: pl.BlockSpec((1, w), lambda i: (0, 0))
    per_head = pl.BlockSpec((N_KV, tm, HEAD_DIM), lambda i: (0, i, 0))
    chunks_t = lambda ch: pl.BlockSpec((N_KV, tm // ch, V_ROWS, ch), lambda i: (0, i, 0, 0))
    out_shape = [
        jax.ShapeDtypeStruct((s, POOL_WIDTH), BF16),
        jax.ShapeDtypeStruct((s, _Q_WIDTH), BF16),
        jax.ShapeDtypeStruct((N_KV, s, HEAD_DIM), F32),
        jax.ShapeDtypeStruct((N_KV, s, HEAD_DIM), F32),
        jax.ShapeDtypeStruct((s, KV_WIDTH), BF16),
        jax.ShapeDtypeStruct((N_KV, s // SEL_CHUNK, V_ROWS, SEL_CHUNK), BF16),
        jax.ShapeDtypeStruct((s, KV_WIDTH), BF16),
        jax.ShapeDtypeStruct((N_KV, s // TQ, V_ROWS, TQ), BF16),
        jax.ShapeDtypeStruct((N_KV, GATE_ROWS, s), F32),
    ]
    return pl.pallas_call(
        functools.partial(_in_proj_kernel, tm=tm),
        grid=(s // tm,),
        in_specs=[row(D_MODEL), vec(D_MODEL), vec(D_MODEL), vec(D_MODEL),
                  _const_spec(w_in_p.shape), _const_spec(w_pool.shape), vec(POOL_WIDTH),
                  vec(LANES), vec(LANES), vec(LANES)],
        out_specs=[row(POOL_WIDTH), row(_Q_WIDTH), per_head, per_head, row(KV_WIDTH),
                   chunks_t(SEL_CHUNK), row(KV_WIDTH), chunks_t(TQ),
                   pl.BlockSpec((N_KV, GATE_ROWS, tm), lambda i: (0, 0, i))],
        out_shape=out_shape,
        scratch_shapes=[pltpu.VMEM((tm + POOL_HALO, POOL_WIDTH), F32)],
        compiler_params=pltpu.CompilerParams(dimension_semantics=("arbitrary",),
                                             vmem_limit_bytes=VMEM_LIMIT_BYTES),
        name="in_proj",
    )(x, g1, sc1, sh1, w_in_p, w_pool, pool_scale, qg, ksg, kwg)


def _compress_kernel(c_ref, pos_ref, w1_ref, b1_ref, w2_ref, b2_ref, gain_ref, o_ref, *,
                     normalize):
    half = CMP_STRIDE * HEAD_DIM
    n_rows = c_ref.shape[1]
    c = c_ref[0]
    first = _dot((c + pos_ref[:, 0:half]).astype(BF16), w1_ref[0:half, :])
    second = _dot((c + pos_ref[:, half:2 * half]).astype(BF16), w1_ref[half:2 * half, :])
    hid = jax.nn.gelu(first + pltpu.roll(second, n_rows - 1, axis=0) + b1_ref[...])
    y = _dot(hid.astype(BF16), w2_ref[...]) + b2_ref[...]
    if normalize:
        ms = jnp.mean(y * y, axis=-1, keepdims=True)
        y = y * lax.rsqrt(ms + EPS) * gain_ref[...]
    o_ref[0] = y.astype(BF16)


def _compress(chunks, pos, w1, b1, w2, b2, gain, normalize):
    _, n_chunks, width = chunks.shape
    vec = lambda w: pl.BlockSpec((1, w), lambda g: (0, 0))
    return pl.pallas_call(
        functools.partial(_compress_kernel, normalize=normalize),
        grid=(N_KV,),
        in_specs=[pl.BlockSpec((1, n_chunks, width), lambda g: (g, 0, 0)),
                  vec(2 * width), _const_spec(w1.shape), vec(CMP_HIDDEN), _const_spec(w2.shape),
                  vec(HEAD_DIM), vec(HEAD_DIM)],
        out_specs=pl.BlockSpec((1, n_chunks, HEAD_DIM), lambda g: (g, 0, 0)),
        out_shape=jax.ShapeDtypeStruct((N_KV, n_chunks, HEAD_DIM), BF16),
        compiler_params=pltpu.CompilerParams(vmem_limit_bytes=VMEM_LIMIT_BYTES),
        name="compress",
    )(chunks, pos, w1, b1, w2, b2, gain)


def _attn_kernel(q_ref, gate_ref, rowt_ref, kc_ref, vct_ref, ks_ref, vst_ref, kw_ref, vwt_ref,
                 o_ref, psum_ref, madd_ref, bias_ref, wbias_ref, ctab_ref, m_ref, acc_ref, ocmp_ref,
                 owin_ref, flag_ref, list_ref, *, nb):
    g = pl.program_id(0)
    i = pl.program_id(1)
    q0 = i * TQ
    nc = CMP_PER_SEL * nb
    n_chunks = nb // BLOCKS_PER_CHUNK
    gslope = jnp.where(g == 0, LOG2E, LOG2E * 2.0 ** -GQA_GROUP).astype(F32)
    slopes = [gslope * (2.0 ** -(r + 1)) for r in range(GQA_GROUP)]
    heads = [slice(r * TQ, (r + 1) * TQ) for r in range(GQA_GROUP)]

    lane = lax.broadcasted_iota(jnp.int32, (TQ, LANES), 1)
    own_half = (lane >= HEAD_DIM) == (g == 1)
    q_rows = []
    for r in range(GQA_GROUP):
        pair = q_ref[:, (r // 2) * LANES:(r // 2 + 1) * LANES]
        swapped = jnp.concatenate([pair[:, HEAD_DIM:], pair[:, :HEAD_DIM]], axis=1)
        q_rows.append(jnp.where(own_half, jnp.where(g == r % 2, pair, swapped), 0.0))
    q = jnp.concatenate(q_rows, axis=0)
    t_lane = q0 + lax.broadcasted_iota(jnp.int32, (1, TQ), 1)

    w0 = pl.multiple_of(jnp.maximum(q0 - WINDOW, 0), TQ)
    s_w = _dot_nt(kw_ref[pl.ds(w0, WIN_SPAN), :], q)

    @pl.when(i <= WINDOW // TQ)
    def _():
        row_w = rowt_ref[0:WIN_SPAN, :]
        lane_w = lax.broadcasted_iota(jnp.int32, (WIN_SPAN, TQ), 1).astype(F32)
        dq = (q0 - w0).astype(F32)
        d_w = lane_w - row_w + dq
        ok_w = jnp.abs(d_w - (WINDOW - 1) * 0.5) < WINDOW * 0.5
        rel_w = row_w - dq
        for r in range(GQA_GROUP):
            wbias_ref[r] = jnp.where(ok_w, slopes[r] * rel_w, NEG_INF)

    wb = w0 // TQ
    for r in range(GQA_GROUP):
        s = s_w[:, heads[r]] + wbias_ref[r]
        m = jnp.max(s, axis=0, keepdims=True)
        p = jnp.exp2((s - m).astype(BF16))
        o_win = _dot(vwt_ref[0, wb], p[0:TQ])
        for b in range(1, WIN_SPAN // TQ):
            o_win = o_win + _dot(vwt_ref[0, wb + b], p[b * TQ:(b + 1) * TQ])
        owin_ref[:, heads[r]] = o_win

    rows_per_bucket = nc // CMP_BUCKETS
    tab_rows = ctab_ref.shape[1]

    @pl.when(i == 0)
    def _():
        rel = CMP_STRIDE * (rowt_ref[0:tab_rows, :] - nc) + (CMP_LEN - 1)
        ok = rel <= lax.broadcasted_iota(jnp.int32, (1, TQ), 1).astype(F32)
        for r in range(GQA_GROUP):
            ctab_ref[r] = jnp.where(ok, slopes[r] * rel, NEG_INF)

    tab0 = pl.multiple_of(nc - q0 // CMP_STRIDE, CMP_STRIDE)

    def compress_and_select(n_rows):
        n_blk = n_rows // CMP_PER_SEL
        s_c = _dot_nt(kc_ref[0:n_rows, :], q)
        p_sum = jnp.zeros((n_rows, TQ), F32)
        p_cols = []
        for r in range(GQA_GROUP):
            s = s_c[:, heads[r]] + ctab_ref[r, pl.ds(tab0, n_rows), :]
            m = jnp.max(s, axis=0, keepdims=True)
            e = jnp.exp2(s - m)
            l = jnp.sum(e, axis=0, keepdims=True)
            p = e * jnp.where(m > 0.5 * NEG_INF, 1.0 / l, 0.0)
            p_sum = p_sum + p
            p_cols.append(p.astype(BF16))
        ocmp_ref[...] = _dot(vct_ref[0, :, 0:n_rows], jnp.concatenate(p_cols, axis=1))

        for h in range(TQ // LANES):
            psum_ref[h, 0:SUBLANES, :] = jnp.zeros((SUBLANES, LANES), F32)
            psum_ref[h, SUBLANES:SUBLANES + n_rows, :] = p_sum[:, h * LANES:(h + 1) * LANES]

        def every4(start):
            parts = [psum_ref[h, pl.ds(SUBLANES + start, n_blk, stride=CMP_PER_SEL), :]
                     for h in range(TQ // LANES)]
            return parts[0] if len(parts) == 1 else jnp.concatenate(parts, axis=1)

        imp = every4(0) + every4(1) + every4(2) + 0.5 * every4(3) + 0.5 * every4(-1)
        blk = lax.broadcasted_iota(jnp.int32, (n_blk, TQ), 0)
        cur = lax.shift_right_logical(t_lane, SEL_BLOCK.bit_length() - 1)
        causal = blk <= cur
        forced = jnp.where(blk == 0, 1.0, 0.0) + jnp.where(blk == cur, 1.0, 0.0) \
            + jnp.where(blk == cur - 1, 1.0, 0.0)
        val = jnp.where(causal, jnp.where(forced > 0.0, -2.0, imp), -1.0)

        def pick(_, v):
            blk_f = rowt_ref[0:n_blk, :]
            mx = jnp.max(v, axis=0, keepdims=True)
            idx = jnp.min(jnp.where(v == mx, blk_f, float(n_blk)), axis=0, keepdims=True)
            return jnp.where(blk_f == idx, -2.0, v)

        picked = lax.fori_loop(0, min(N_SEL, n_blk) - N_FORCED, pick, val)
        madd_ref[0:n_blk, :] = jnp.where(causal, jnp.where(picked < -1.5, 0.0, NEG_INF), NEG_INF)
        madd_ref[0:1, :] = jnp.where(i >= 1, NEG_INF, madd_ref[0:1, :])
        for c in range(n_blk // BLOCKS_PER_CHUNK):
            rows = madd_ref[c * BLOCKS_PER_CHUNK:(c + 1) * BLOCKS_PER_CHUNK, :]
            flag_ref[c] = (jnp.max(rows) > 0.5 * NEG_INF).astype(jnp.int32)

    bucket = ((q0 + TQ) // CMP_STRIDE - 1) // rows_per_bucket
    for b in range(CMP_BUCKETS):
        pl.when(bucket == b)(functools.partial(compress_and_select, (b + 1) * rows_per_bucket))

    @pl.when(i == 0)
    def _():
        key_row = rowt_ref[0:SEL_CHUNK, :]
        for r in range(GQA_GROUP):
            bias_ref[r] = slopes[r] * key_row

    m_ref[...] = jnp.full(m_ref.shape, NEG_INF, F32)
    acc_ref[...] = jnp.zeros(acc_ref.shape, F32)

    def attend(chunks, final):
        sizes = ([SEL_BLOCK] if final else []) + [SEL_CHUNK] * len(chunks)
        keys = ([ks_ref[0:SEL_BLOCK, :]] if final else []) + [
            ks_ref[pl.ds(pl.multiple_of(c * SEL_CHUNK, SEL_CHUNK), SEL_CHUNK), :] for c in chunks]
        s_all = _dot_nt(jnp.concatenate(keys, axis=0) if len(keys) > 1 else keys[0], q)
        offsets = np.cumsum([0] + sizes)
        s = [s_all[offsets[k]:offsets[k + 1]] for k in range(len(sizes))]

        madds, values, rel0 = [], [], []
        if final:
            madds.append(jnp.where(i >= 1, 0.0, NEG_INF))
            values.append(vst_ref[0, 0][:, 0:SEL_BLOCK])
            rel0.append((-q0).astype(F32))
        for k, c in enumerate(chunks):
            k0 = c * SEL_CHUNK
            madd = jnp.concatenate(
                [jnp.broadcast_to(madd_ref[pl.ds(c * BLOCKS_PER_CHUNK + b, 1), :], (SEL_BLOCK, TQ))
                 for b in range(BLOCKS_PER_CHUNK)], axis=0)
            if final and k == len(chunks) - 1:
                pos = k0 + lax.broadcasted_iota(jnp.int32, (SEL_CHUNK, TQ), 0)
                madd = jnp.where(pos <= t_lane, madd, NEG_INF)
            madds.append(madd)
            values.append(vst_ref[0, c])
            rel0.append((k0 - q0).astype(F32))

        p_cols = [[] for _ in sizes]
        alphas = []
        for r in range(GQA_GROUP):
            shifts = [slopes[r] * x for x in rel0]
            us = [sk[:, heads[r]] + bias_ref[r, 0:n, :] + mk for sk, mk, n in zip(s, madds, sizes)]
            m_old = m_ref[:, heads[r]]
            m_new = m_old
            for u, sh in zip(us, shifts):
                m_new = jnp.maximum(m_new, jnp.max(u, axis=0, keepdims=True) + sh)
            alphas.append(jnp.exp2(m_old - m_new))
            for k, (u, sh) in enumerate(zip(us, shifts)):
                p_cols[k].append(jnp.exp2((u - (m_new - sh)).astype(BF16)))
            m_ref[:, heads[r]] = m_new
        pv = _dot(values[0], jnp.concatenate(p_cols[0], axis=1))
        for k in range(1, len(sizes)):
            pv = pv + _dot(values[k], jnp.concatenate(p_cols[k], axis=1))
        acc_ref[...] = jnp.concatenate(alphas, axis=1) * acc_ref[...] + pv

    c_diag = q0 // SEL_CHUNK

    def compact(c, n):
        list_ref[n] = c
        return n + flag_ref[c]

    n_earlier = lax.fori_loop(0, c_diag, compact, jnp.int32(0))
    list_ref[n_earlier] = c_diag

    def active_group(first, size, final):
        attend([list_ref[first + k] for k in range(size)], final)

    def group_body(p, carry):
        active_group(p * SEL_GROUP, SEL_GROUP, False)
        return carry

    n_full = n_earlier // SEL_GROUP
    lax.fori_loop(0, n_full, group_body, 0)
    done = n_full * SEL_GROUP
    for size in range(1, SEL_GROUP + 1):
        pl.when(n_earlier + 1 - done == size)(functools.partial(active_group, done, size, True))

    head_out = []
    for r in range(GQA_GROUP):
        gc = gate_ref[0, 3 * r + 0:3 * r + 1, :]
        gs = gate_ref[0, 3 * r + 1:3 * r + 2, :]
        gw = gate_ref[0, 3 * r + 2:3 * r + 3, :]
        acc = acc_ref[:, heads[r]]
        win = owin_ref[:, heads[r]]
        out_t = (gc * ocmp_ref[:, heads[r]]
                 + (gs / acc[HEAD_DIM:HEAD_DIM + 1]) * acc
                 + (gw / win[HEAD_DIM:HEAD_DIM + 1]) * win)
        head_out.append(out_t.T[:, 0:HEAD_DIM])
    for pair in range(GQA_GROUP // 2):
        o_ref[:, pair * LANES:(pair + 1) * LANES] = jnp.concatenate(
            head_out[2 * pair:2 * pair + 2], axis=1).astype(BF16)


def _attention(q, gates_t, kcmp, vcmp_t, ksel, vsel_t, kwin, vwin_t):
    assert TQ == SEL_CHUNK
    s = q.shape[0]
    nb = s // SEL_BLOCK
    nc = CMP_PER_SEL * nb
    gw = GQA_GROUP * HEAD_DIM
    tab_rows = nc + TQ // CMP_STRIDE + nc // CMP_BUCKETS
    n_rows = max(tab_rows, WIN_SPAN)
    row_tile = jnp.asarray(np.broadcast_to(np.arange(n_rows)[:, None], (n_rows, TQ))
                           .astype(np.float32))
    per_group = lambda a: pl.BlockSpec((1,) + a.shape[1:], lambda g, i: (g,) + (0,) * (a.ndim - 1),
                                       pipeline_mode=pl.Buffered(1))
    return pl.pallas_call(
        functools.partial(_attn_kernel, nb=nb),
        grid=(N_KV, s // TQ),
        in_specs=[pl.BlockSpec((TQ, gw), lambda g, i: (i, g)),
                  pl.BlockSpec((1, GATE_ROWS, TQ), lambda g, i: (g, 0, i)),
                  _const_spec(row_tile.shape),
                  _const_spec(kcmp.shape), per_group(vcmp_t),
                  _const_spec(ksel.shape), per_group(vsel_t),
                  _const_spec(kwin.shape), per_group(vwin_t)],
        out_specs=pl.BlockSpec((TQ, gw), lambda g, i: (i, g)),
        out_shape=jax.ShapeDtypeStruct((s, N_HEADS * HEAD_DIM), BF16),
        scratch_shapes=[pltpu.VMEM((TQ // LANES, SUBLANES + nc, LANES), F32),
                        pltpu.VMEM((nb, TQ), F32),
                        pltpu.VMEM((GQA_GROUP, SEL_CHUNK, TQ), F32),
                        pltpu.VMEM((GQA_GROUP, WIN_SPAN, TQ), F32),
                        pltpu.VMEM((GQA_GROUP, tab_rows, TQ), F32),
                        pltpu.VMEM((1, GQA_GROUP * TQ), F32),
                        pltpu.VMEM((V_ROWS, GQA_GROUP * TQ), F32),
                        pltpu.VMEM((V_ROWS, GQA_GROUP * TQ), F32),
                        pltpu.VMEM((V_ROWS, GQA_GROUP * TQ), F32),
                        pltpu.SMEM((nb // BLOCKS_PER_CHUNK,), jnp.int32),
                        pltpu.SMEM((nb // BLOCKS_PER_CHUNK,), jnp.int32)],
        compiler_params=pltpu.CompilerParams(dimension_semantics=("arbitrary", "arbitrary"),
                                             vmem_limit_bytes=VMEM_LIMIT_BYTES),
        name="attn",
    )(q, gates_t, row_tile, kcmp, vcmp_t, ksel, vsel_t, kwin, vwin_t)


def _out_mlp_kernel(pool_ref, attn_ref, wout_ref, x_ref, ga1_ref, g_ref, sc_ref, sh_ref, w1_ref,
                    w2_ref, ga2_ref, o_ref, *, ff_chunk):
    mix = (_dot(pool_ref[...], wout_ref[0:POOL_WIDTH, :])
           + _dot(attn_ref[...], wout_ref[POOL_WIDTH:, :]))
    x1 = x_ref[...] + ga1_ref[...] * mix
    h = _rms_modulate(x1, g_ref[...], sc_ref[...], sh_ref[...]).astype(BF16)
    acc = jnp.zeros(x1.shape, F32)
    for c in range(D_FF // ff_chunk):
        a = _dot(h, w1_ref[:, c * ff_chunk:(c + 1) * ff_chunk])
        a = jnp.square(jnp.maximum(a, 0.0)).astype(BF16)
        acc = acc + _dot(a, w2_ref[c * ff_chunk:(c + 1) * ff_chunk, :])
    o_ref[...] = x1 + ga2_ref[...] * acc


def _out_mlp(pool_out, attn_out, w_out, x, ga1, g2, sc2, sh2, w1, w2, ga2, tm):
    s = x.shape[0]
    row = lambda w: pl.BlockSpec((tm, w), lambda i: (i, 0))
    vec = lambda w: pl.BlockSpec((1, w), lambda i: (0, 0))
    return pl.pallas_call(
        functools.partial(_out_mlp_kernel, ff_chunk=1024),
        grid=(s // tm,),
        in_specs=[row(POOL_WIDTH), row(N_HEADS * HEAD_DIM), _const_spec(w_out.shape),
                  row(D_MODEL), vec(D_MODEL), vec(D_MODEL), vec(D_MODEL), vec(D_MODEL),
                  _const_spec(w1.shape), _const_spec(w2.shape), vec(D_MODEL)],
        out_specs=row(D_MODEL),
        out_shape=jax.ShapeDtypeStruct((s, D_MODEL), F32),
        compiler_params=pltpu.CompilerParams(vmem_limit_bytes=VMEM_LIMIT_BYTES),
        name="out_mlp",
    )(pool_out, attn_out, w_out, x, ga1, g2, sc2, sh2, w1, w2, ga2)


def _pad_in_proj_weight(w_in):
    src_g = _OFF_G
    per_group = GQA_GROUP * N_BRANCH
    cols = [w_in[:, :src_g]]
    for gg in range(N_KV):
        cols += [w_in[:, src_g + gg * per_group:src_g + (gg + 1) * per_group],
                 jnp.zeros((D_MODEL, LANES - per_group), w_in.dtype)]
    return jnp.concatenate(cols, axis=1).astype(BF16)


def _group_lanes(a):
    return a.transpose(1, 0, 2).reshape(a.shape[1], KV_WIDTH)


def kernel(x, c, w_ada, b_ada, norm1_g, norm2_g, w_in, w_pool, pool_scale, q_gain, kc_gain,
           ks_gain, kw_gain, cmp_pos_k, cmp_w1_k, cmp_b1_k, cmp_w2_k, cmp_b2_k, cmp_pos_v,
           cmp_w1_v, cmp_b1_v, cmp_w2_v, cmp_b2_v, w_out, w_ff1, w_ff2):
    batch, s, _ = x.shape
    assert batch == 1 and w_ada.shape[0] == 1
    assert s % SEL_CHUNK == 0 and s >= WIN_SPAN and (s // SEL_BLOCK) & (s // SEL_BLOCK - 1) == 0
    tm = min(s, 512)
    x2 = x[0]

    mod = _ada(jnp.broadcast_to(c, (SUBLANES, D_MODEL)), w_ada[0], b_ada)[0:1]
    sh1, sc1, ga1, sh2, sc2, ga2 = [mod[:, k * D_MODEL:(k + 1) * D_MODEL] for k in range(6)]

    pair = lambda gain: jnp.tile(gain, (1, 2))
    (pool_out, q, kc, vc, ksel, vsel_t, kwin, vwin_t, gates_t) = _in_proj(
        x2, norm1_g, sc1, sh1, _pad_in_proj_weight(w_in[0]), w_pool[0].astype(BF16), pool_scale,
        pair(q_gain), pair(ks_gain), pair(kw_gain), min(s, 1024))

    chunks = lambda a: a.reshape(N_KV, s // CMP_STRIDE, CMP_STRIDE * HEAD_DIM)
    kcmp = _compress(chunks(kc), cmp_pos_k.reshape(1, -1), cmp_w1_k[0].astype(BF16), cmp_b1_k,
                     cmp_w2_k[0].astype(BF16), cmp_b2_k, kc_gain, True)
    vcmp = _compress(chunks(vc), cmp_pos_v.reshape(1, -1), cmp_w1_v[0].astype(BF16), cmp_b1_v,
                     cmp_w2_v[0].astype(BF16), cmp_b2_v, kc_gain, False)

    vcmp_t = jnp.pad(vcmp.transpose(0, 2, 1), ((0, 0), (0, V_ROWS - HEAD_DIM), (0, 0)))
    attn_out = _attention(q, gates_t, _group_lanes(kcmp), vcmp_t, ksel, vsel_t, kwin, vwin_t)

    out = _out_mlp(pool_out, attn_out, w_out[0].astype(BF16), x2, ga1, norm2_g, sc2, sh2,
                   w_ff1[0].astype(BF16), w_ff2[0].astype(BF16), ga2, tm)
    return out[None]
```

```python
import functools

import jax
import jax.numpy as jnp
import numpy as np
from jax import lax
from jax.experimental import pallas as pl
from jax.experimental.pallas import tpu as pltpu

F32 = jnp.float32
BF16 = jnp.bfloat16

LANES = 128
SUBLANES = 8
VMEM_LIMIT_BYTES = 56 * 1024 * 1024

D_MODEL = 1024
POOL_WIDTH = 512
POOL_WINDOWS = (2, 4, 8, 16)
POOL_GROUP = POOL_WIDTH // len(POOL_WINDOWS)
POOL_HALO = 16
HEAD_DIM = 64
N_HEADS = 8
N_KV = 2
GQA_GROUP = N_HEADS // N_KV
KV_WIDTH = N_KV * HEAD_DIM
N_BRANCH = 3
CMP_LEN = 32
CMP_STRIDE = 16
CMP_HIDDEN = 4 * HEAD_DIM
SEL_BLOCK = 64
N_SEL = 16
WINDOW = 512
D_FF = 4 * D_MODEL
NEG_INF = -1e30
N_FORCED = 3
EPS = 1e-6
LOG2E = 1.4426950408889634

TQ = 256
SEL_CHUNK = 256
BLOCKS_PER_CHUNK = SEL_CHUNK // SEL_BLOCK
SEL_GROUP = 8
CMP_BUCKETS = 16
WIN_SPAN = WINDOW + TQ
CMP_PER_SEL = SEL_BLOCK // CMP_STRIDE
GATE_ROWS = 16
V_ROWS = HEAD_DIM + 16

_Q_WIDTH = N_HEADS * HEAD_DIM
_OFF_U = 0
_OFF_Q = POOL_WIDTH
_OFF_KC = _OFF_Q + _Q_WIDTH
_OFF_VC = _OFF_KC + KV_WIDTH
_OFF_KS = _OFF_VC + KV_WIDTH
_OFF_VS = _OFF_KS + KV_WIDTH
_OFF_KW = _OFF_VS + KV_WIDTH
_OFF_VW = _OFF_KW + KV_WIDTH
_OFF_G = _OFF_VW + KV_WIDTH
_IN_PAD = _OFF_G + N_KV * LANES


def _dot(a, b):
    return jnp.dot(a, b, preferred_element_type=F32)


def _dot_nt(a, b):
    return lax.dot_general(a, b, (((1,), (1,)), ((), ())), preferred_element_type=F32)


def _const_spec(shape):
    nd = len(shape)
    return pl.BlockSpec(shape, lambda *_: (0,) * nd, pipeline_mode=pl.Buffered(1))


def _ada_kernel(c_ref, w_ref, b_ref, o_ref):
    o_ref[...] = jnp.dot(c_ref[...], w_ref[...], preferred_element_type=F32,
                         precision=lax.Precision.HIGHEST) + b_ref[...]


def _ada(c8, w, b):
    n = w.shape[1]
    bn = 1024
    return pl.pallas_call(
        _ada_kernel,
        grid=(n // bn,),
        in_specs=[pl.BlockSpec((SUBLANES, D_MODEL), lambda j: (0, 0)),
                  pl.BlockSpec((D_MODEL, bn), lambda j: (0, j)),
                  pl.BlockSpec((1, bn), lambda j: (0, j))],
        out_specs=pl.BlockSpec((SUBLANES, bn), lambda j: (0, j)),
        out_shape=jax.ShapeDtypeStruct((SUBLANES, n), F32),
        compiler_params=pltpu.CompilerParams(vmem_limit_bytes=VMEM_LIMIT_BYTES),
        name="ada",
    )(c8, w, b)


def _rms_modulate(x, g, sc, sh):
    ms = jnp.mean(x * x, axis=-1, keepdims=True)
    return (x * lax.rsqrt(ms + EPS)) * (g * (1.0 + sc)) + sh


def _head_norm_pair(x, gain2):
    lane = lax.broadcasted_iota(jnp.int32, x.shape, 1)
    lo = lane < HEAD_DIM
    sq = x * x
    s_lo = jnp.sum(jnp.where(lo, sq, 0.0), axis=-1, keepdims=True)
    s_hi = jnp.sum(jnp.where(lo, 0.0, sq), axis=-1, keepdims=True)
    ms = jnp.where(lo, s_lo, s_hi) * (1.0 / HEAD_DIM)
    return x * lax.rsqrt(ms + EPS) * gain2


def _in_proj_kernel(x_ref, g_ref, sc_ref, sh_ref, w_ref, wpool_ref, pscale_ref, qg_ref, ksg_ref,
                    kwg_ref, pool_ref, q_ref, kc_ref, vc_ref, ks_ref, vs_ref, kw_ref, vw_ref,
                    gate_ref, ebuf_ref, *, tm):
    i = pl.program_id(0)
    h = _rms_modulate(x_ref[...], g_ref[...], sc_ref[...], sh_ref[...]).astype(BF16)

    def project(off, width):
        return _dot(h, w_ref[:, off:off + width])

    q_all = project(_OFF_Q, _Q_WIDTH)
    for pair in range(N_HEADS // 2):
        qn = _head_norm_pair(q_all[:, pair * LANES:(pair + 1) * LANES], qg_ref[...])
        q_ref[:, pair * LANES:(pair + 1) * LANES] = (qn * (HEAD_DIM ** -0.5 * LOG2E)).astype(BF16)

    kv = project(_OFF_KC, _IN_PAD - _OFF_KC)
    kv_cols = lambda off, width: kv[:, off - _OFF_KC:off - _OFF_KC + width]
    for gg in range(N_KV):
        kc_ref[gg] = kv_cols(_OFF_KC + gg * HEAD_DIM, HEAD_DIM)
        vc_ref[gg] = kv_cols(_OFF_VC + gg * HEAD_DIM, HEAD_DIM)
    ks_ref[...] = _head_norm_pair(kv_cols(_OFF_KS, KV_WIDTH), ksg_ref[...]).astype(BF16)
    kw_ref[...] = _head_norm_pair(kv_cols(_OFF_KW, KV_WIDTH), kwg_ref[...]).astype(BF16)
    def value_tiles(ref, off, chunk):
        ones_row = jnp.where(lax.broadcasted_iota(jnp.int32, (V_ROWS - HEAD_DIM, chunk), 0) == 0,
                             1.0, 0.0)
        for k in range(tm // chunk):
            vt = kv_cols(off, KV_WIDTH)[k * chunk:(k + 1) * chunk].T
            for gg in range(N_KV):
                ref[gg, k] = jnp.concatenate(
                    [vt[gg * HEAD_DIM:(gg + 1) * HEAD_DIM], ones_row], axis=0).astype(BF16)

    value_tiles(vs_ref, _OFF_VS, SEL_CHUNK)
    value_tiles(vw_ref, _OFF_VW, TQ)
    gates_t = jax.nn.sigmoid(kv_cols(_OFF_G, N_KV * LANES)).T
    for gg in range(N_KV):
        gate_ref[gg] = gates_t[gg * LANES:gg * LANES + GATE_ROWS]

    @pl.when(i == 0)
    def _():
        ebuf_ref[0:POOL_HALO, :] = jnp.zeros((POOL_HALO, POOL_WIDTH), F32)

    u = project(_OFF_U, POOL_WIDTH)
    ebuf_ref[POOL_HALO:POOL_HALO + tm, :] = u
    t1 = i * tm + lax.broadcasted_iota(jnp.int32, (tm, POOL_GROUP), 0) + 1
    for gi, w in enumerate(POOL_WINDOWS):
        c0 = gi * POOL_GROUP
        win = u[:, c0:c0 + POOL_GROUP]
        for k in range(1, w):
            win = win + ebuf_ref[POOL_HALO - k:POOL_HALO - k + tm, c0:c0 + POOL_GROUP]
        cnt = jnp.minimum(t1, w).astype(F32)
        pooled = win / cnt - u[:, c0:c0 + POOL_GROUP]
        y = _dot(pooled.astype(BF16), wpool_ref[gi])
        pool_ref[:, c0:c0 + POOL_GROUP] = (y * pscale_ref[:, c0:c0 + POOL_GROUP]).astype(BF16)
    ebuf_ref[0:POOL_HALO, :] = ebuf_ref[tm:tm + POOL_HALO, :]


def _in_proj(x, g1, sc1, sh1, w_in_p, w_pool, pool_scale, qg, ksg, kwg, tm):
    s = x.shape[0]
    row = lambda w: pl.BlockSpec((tm, w), lambda i: (i, 0))
    vec = lambda w: pl.BlockSpec((1, w), lambda i: (0, 0))
    per_head = pl.BlockSpec((N_KV, tm, HEAD_DIM), lambda i: (0, i, 0))
    chunks_t = lambda ch: pl.BlockSpec((N_KV, tm // ch, V_ROWS, ch), lambda i: (0, i, 0, 0))
    out_shape = [
        jax.ShapeDtypeStruct((s, POOL_WIDTH), BF16),
        jax.ShapeDtypeStruct((s, _Q_WIDTH), BF16),
        jax.ShapeDtypeStruct((N_KV, s, HEAD_DIM), F32),
        jax.ShapeDtypeStruct((N_KV, s, HEAD_DIM), F32),
        jax.ShapeDtypeStruct((s, KV_WIDTH), BF16),
        jax.ShapeDtypeStruct((N_KV, s // SEL_CHUNK, V_ROWS, SEL_CHUNK), BF16),
        jax.ShapeDtypeStruct((s, KV_WIDTH), BF16),
        jax.ShapeDtypeStruct((N_KV, s // TQ, V_ROWS, TQ), BF16),
        jax.ShapeDtypeStruct((N_KV, GATE_ROWS, s), F32),
    ]
    return pl.pallas_call(
        functools.partial(_in_proj_kernel, tm=tm),
        grid=(s // tm,),
        in_specs=[row(D_MODEL), vec(D_MODEL), vec(D_MODEL), vec(D_MODEL),
                  _const_spec(w_in_p.shape), _const_spec(w_pool.shape), vec(POOL_WIDTH),
                  vec(LANES), vec(LANES), vec(LANES)],
        out_specs=[row(POOL_WIDTH), row(_Q_WIDTH), per_head, per_head, row(KV_WIDTH),
                   chunks_t(SEL_CHUNK), row(KV_WIDTH), chunks_t(TQ),
                   pl.BlockSpec((N_KV, GATE_ROWS, tm), lambda i: (0, 0, i))],
        out_shape=out_shape,
        scratch_shapes=[pltpu.VMEM((tm + POOL_HALO, POOL_WIDTH), F32)],
        compiler_params=pltpu.CompilerParams(dimension_semantics=("arbitrary",),
                                             vmem_limit_bytes=VMEM_LIMIT_BYTES),
        name="in_proj",
    )(x, g1, sc1, sh1, w_in_p, w_pool, pool_scale, qg, ksg, kwg)


def _compress_kernel(c_ref, pos_ref, w1_ref, b1_ref, w2_ref, b2_ref, gain_ref, o_ref, *,
                     normalize):
    half = CMP_STRIDE * HEAD_DIM
    n_rows = c_ref.shape[1]
    c = c_ref[0]
    first = _dot((c + pos_ref[:, 0:half]).astype(BF16), w1_ref[0:half, :])
    second = _dot((c + pos_ref[:, half:2 * half]).astype(BF16), w1_ref[half:2 * half, :])
    hid = jax.nn.gelu(first + pltpu.roll(second, n_rows - 1, axis=0) + b1_ref[...])
    y = _dot(hid.astype(BF16), w2_ref[...]) + b2_ref[...]
    if normalize:
        ms = jnp.mean(y * y, axis=-1, keepdims=True)
        y = y * lax.rsqrt(ms + EPS) * gain_ref[...]
    o_ref[0] = y.astype(BF16)


def _compress(chunks, pos, w1, b1, w2, b2, gain, normalize):
    _, n_chunks, width = chunks.shape
    vec = lambda w: pl.BlockSpec((1, w), lambda g: (0, 0))
    return pl.pallas_call(
        functools.partial(_compress_kernel, normalize=normalize),
        grid=(N_KV,),
        in_specs=[pl.BlockSpec((1, n_chunks, width), lambda g: (g, 0, 0)),
                  vec(2 * width), _const_spec(w1.shape), vec(CMP_HIDDEN), _const_spec(w2.shape),
                  vec(HEAD_DIM), vec(HEAD_DIM)],
        out_specs=pl.BlockSpec((1, n_chunks, HEAD_DIM), lambda g: (g, 0, 0)),
        out_shape=jax.ShapeDtypeStruct((N_KV, n_chunks, HEAD_DIM), BF16),
        compiler_params=pltpu.CompilerParams(vmem_limit_bytes=VMEM_LIMIT_BYTES),
        name="compress",
    )(chunks, pos, w1, b1, w2, b2, gain)


def _attn_kernel(q_ref, gate_ref, rowt_ref, kc_ref, vct_ref, ks_ref, vst_ref, kw_ref, vwt_ref,
                 o_ref, psum_ref, madd_ref, bias_ref, wbias_ref, ctab_ref, m_ref, acc_ref, ocmp_ref,
                 owin_ref, flag_ref, list_ref, *, nb):
    g = pl.program_id(0)
    i = pl.program_id(1)
    q0 = i * TQ
    nc = CMP_PER_SEL * nb
    n_chunks = nb // BLOCKS_PER_CHUNK
    gslope = jnp.where(g == 0, LOG2E, LOG2E * 2.0 ** -GQA_GROUP).astype(F32)
    slopes = [gslope * (2.0 ** -(r + 1)) for r in range(GQA_GROUP)]
    heads = [slice(r * TQ, (r + 1) * TQ) for r in range(GQA_GROUP)]

    lane = lax.broadcasted_iota(jnp.int32, (TQ, LANES), 1)
    own_half = (lane >= HEAD_DIM) == (g == 1)
    q_rows = []
    for r in range(GQA_GROUP):
        pair = q_ref[:, (r // 2) * LANES:(r // 2 + 1) * LANES]
        swapped = jnp.concatenate([pair[:, HEAD_DIM:], pair[:, :HEAD_DIM]], axis=1)
        q_rows.append(jnp.where(own_half, jnp.where(g == r % 2, pair, swapped), 0.0))
    q = jnp.concatenate(q_rows, axis=0)
    t_lane = q0 + lax.broadcasted_iota(jnp.int32, (1, TQ), 1)

    w0 = pl.multiple_of(jnp.maximum(q0 - WINDOW, 0), TQ)
    s_w = _dot_nt(kw_ref[pl.ds(w0, WIN_SPAN), :], q)

    @pl.when(i <= WINDOW // TQ)
    def _():
        row_w = rowt_ref[0:WIN_SPAN, :]
        lane_w = lax.broadcasted_iota(jnp.int32, (WIN_SPAN, TQ), 1).astype(F32)
        dq = (q0 - w0).astype(F32)
        d_w = lane_w - row_w + dq
        ok_w = jnp.abs(d_w - (WINDOW - 1) * 0.5) < WINDOW * 0.5
        rel_w = row_w - dq
        for r in range(GQA_GROUP):
            wbias_ref[r] = jnp.where(ok_w, slopes[r] * rel_w, NEG_INF)

    wb = w0 // TQ
    for r in range(GQA_GROUP):
        s = s_w[:, heads[r]] + wbias_ref[r]
        m = jnp.max(s, axis=0, keepdims=True)
        p = jnp.exp2((s - m).astype(BF16))
        o_win = _dot(vwt_ref[0, wb], p[0:TQ])
        for b in range(1, WIN_SPAN // TQ):
            o_win = o_win + _dot(vwt_ref[0, wb + b], p[b * TQ:(b + 1) * TQ])
        owin_ref[:, heads[r]] = o_win

    rows_per_bucket = nc // CMP_BUCKETS
    tab_rows = ctab_ref.shape[1]

    @pl.when(i == 0)
    def _():
        rel = CMP_STRIDE * (rowt_ref[0:tab_rows, :] - nc) + (CMP_LEN - 1)
        ok = rel <= lax.broadcasted_iota(jnp.int32, (1, TQ), 1).astype(F32)
        for r in range(GQA_GROUP):
            ctab_ref[r] = jnp.where(ok, slopes[r] * rel, NEG_INF)

    tab0 = pl.multiple_of(nc - q0 // CMP_STRIDE, CMP_STRIDE)

    def compress_and_select(n_rows):
        n_blk = n_rows // CMP_PER_SEL
        s_c = _dot_nt(kc_ref[0:n_rows, :], q)
        p_sum = jnp.zeros((n_rows, TQ), F32)
        p_cols = []
        for r in range(GQA_GROUP):
            s = s_c[:, heads[r]] + ctab_ref[r, pl.ds(tab0, n_rows), :]
            m = jnp.max(s, axis=0, keepdims=True)
            e = jnp.exp2(s - m)
            l = jnp.sum(e, axis=0, keepdims=True)
            p = e * jnp.where(m > 0.5 * NEG_INF, 1.0 / l, 0.0)
            p_sum = p_sum + p
            p_cols.append(p.astype(BF16))
        ocmp_ref[...] = _dot(vct_ref[0, :, 0:n_rows], jnp.concatenate(p_cols, axis=1))

        for h in range(TQ // LANES):
            psum_ref[h, 0:SUBLANES, :] = jnp.zeros((SUBLANES, LANES), F32)
            psum_ref[h, SUBLANES:SUBLANES + n_rows, :] = p_sum[:, h * LANES:(h + 1) * LANES]

        def every4(start):
            parts = [psum_ref[h, pl.ds(SUBLANES + start, n_blk, stride=CMP_PER_SEL), :]
                     for h in range(TQ // LANES)]
            return parts[0] if len(parts) == 1 else jnp.concatenate(parts, axis=1)

        imp = every4(0) + every4(1) + every4(2) + 0.5 * every4(3) + 0.5 * every4(-1)
        blk = lax.broadcasted_iota(jnp.int32, (n_blk, TQ), 0)
        cur = lax.shift_right_logical(t_lane, SEL_BLOCK.bit_length() - 1)
        causal = blk <= cur
        forced = jnp.where(blk == 0, 1.0, 0.0) + jnp.where(blk == cur, 1.0, 0.0) \
            + jnp.where(blk == cur - 1, 1.0, 0.0)
        val = jnp.where(causal, jnp.where(forced > 0.0, -2.0, imp), -1.0)

        def pick(_, v):
            blk_f = rowt_ref[0:n_blk, :]
            mx = jnp.max(v, axis=0, keepdims=True)
            idx = jnp.min(jnp.where(v == mx, blk_f, float(n_blk)), axis=0, keepdims=True)
            return jnp.where(blk_f == idx, -2.0, v)

        picked = lax.fori_loop(0, min(N_SEL, n_blk) - N_FORCED, pick, val)
        madd_ref[0:n_blk, :] = jnp.where(causal, jnp.where(picked < -1.5, 0.0, NEG_INF), NEG_INF)
        madd_ref[0:1, :] = jnp.where(i >= 1, NEG_INF, madd_ref[0:1, :])
        for c in range(n_blk // BLOCKS_PER_CHUNK):
            rows = madd_ref[c * BLOCKS_PER_CHUNK:(c + 1) * BLOCKS_PER_CHUNK, :]
            flag_ref[c] = (jnp.max(rows) > 0.5 * NEG_INF).astype(jnp.int32)

    bucket = ((q0 + TQ) // CMP_STRIDE - 1) // rows_per_bucket
    for b in range(CMP_BUCKETS):
        pl.when(bucket == b)(functools.partial(compress_and_select, (b + 1) * rows_per_bucket))

    @pl.when(i == 0)
    def _():
        key_row = rowt_ref[0:SEL_CHUNK, :]
        for r in range(GQA_GROUP):
            bias_ref[r] = slopes[r] * key_row

    m_ref[...] = jnp.full(m_ref.shape, NEG_INF, F32)
    acc_ref[...] = jnp.zeros(acc_ref.shape, F32)

    def attend(chunks, final):
        sizes = ([SEL_BLOCK] if final else []) + [SEL_CHUNK] * len(chunks)
        keys = ([ks_ref[0:SEL_BLOCK, :]] if final else []) + [
            ks_ref[pl.ds(pl.multiple_of(c * SEL_CHUNK, SEL_CHUNK), SEL_CHUNK), :] for c in chunks]
        s_all = _dot_nt(jnp.concatenate(keys, axis=0) if len(keys) > 1 else keys[0], q)
        offsets = np.cumsum([0] + sizes)
        s = [s_all[offsets[k]:offsets[k + 1]] for k in range(len(sizes))]

        madds, values, rel0 = [], [], []
        if final:
            madds.append(jnp.where(i >= 1, 0.0, NEG_INF))
            values.append(vst_ref[0, 0][:, 0:SEL_BLOCK])
            rel0.append((-q0).astype(F32))
        for k, c in enumerate(chunks):
            k0 = c * SEL_CHUNK
            madd = jnp.concatenate(
                [jnp.broadcast_to(madd_ref[pl.ds(c * BLOCKS_PER_CHUNK + b, 1), :], (SEL_BLOCK, TQ))
                 for b in range(BLOCKS_PER_CHUNK)], axis=0)
            if final and k == len(chunks) - 1:
                pos = k0 + lax.broadcasted_iota(jnp.int32, (SEL_CHUNK, TQ), 0)
                madd = jnp.where(pos <= t_lane, madd, NEG_INF)
            madds.append(madd)
            values.append(vst_ref[0, c])
            rel0.append((k0 - q0).astype(F32))

        p_cols = [[] for _ in sizes]
        alphas = []
        for r in range(GQA_GROUP):
            shifts = [slopes[r] * x for x in rel0]
            us = [sk[:, heads[r]] + bias_ref[r, 0:n, :] + mk for sk, mk, n in zip(s, madds, sizes)]
            m_old = m_ref[:, heads[r]]
            m_new = m_old
            for u, sh in zip(us, shifts):
                m_new = jnp.maximum(m_new, jnp.max(u, axis=0, keepdims=True) + sh)
            alphas.append(jnp.exp2(m_old - m_new))
            for k, (u, sh) in enumerate(zip(us, shifts)):
                p_cols[k].append(jnp.exp2((u - (m_new - sh)).astype(BF16)))
            m_ref[:, heads[r]] = m_new
        pv = _dot(values[0], jnp.concatenate(p_cols[0], axis=1))
        for k in range(1, len(sizes)):
            pv = pv + _dot(values[k], jnp.concatenate(p_cols[k], axis=1))
        acc_ref[...] = jnp.concatenate(alphas, axis=1) * acc_ref[...] + pv

    c_diag = q0 // SEL_CHUNK

    def compact(c, n):
        list_ref[n] = c
        return n + flag_ref[c]

    n_earlier = lax.fori_loop(0, c_diag, compact, jnp.int32(0))
    list_ref[n_earlier] = c_diag

    def active_group(first, size, final):
        attend([list_ref[first + k] for k in range(size)], final)

    def group_body(p, carry):
        active_group(p * SEL_GROUP, SEL_GROUP, False)
        return carry

    n_full = n_earlier // SEL_GROUP
    lax.fori_loop(0, n_full, group_body, 0)
    done = n_full * SEL_GROUP
    for size in range(1, SEL_GROUP + 1):
        pl.when(n_earlier + 1 - done == size)(functools.partial(active_group, done, size, True))

    head_out = []
    for r in range(GQA_GROUP):
        gc = gate_ref[0, 3 * r + 0:3 * r + 1, :]
        gs = gate_ref[0, 3 * r + 1:3 * r + 2, :]
        gw = gate_ref[0, 3 * r + 2:3 * r + 3, :]
        acc = acc_ref[:, heads[r]]
        win = owin_ref[:, heads[r]]
        out_t = (gc * ocmp_ref[:, heads[r]]
                 + (gs / acc[HEAD_DIM:HEAD_DIM + 1]) * acc
                 + (gw / win[HEAD_DIM:HEAD_DIM + 1]) * win)
        head_out.append(out_t.T[:, 0:HEAD_DIM])
    for pair in range(GQA_GROUP // 2):
        o_ref[:, pair * LANES:(pair + 1) * LANES] = jnp.concatenate(
            head_out[2 * pair:2 * pair + 2], axis=1).astype(BF16)


def _attention(q, gates_t, kcmp, vcmp_t, ksel, vsel_t, kwin, vwin_t):
    assert TQ == SEL_CHUNK
    s = q.shape[0]
    nb = s // SEL_BLOCK
    nc = CMP_PER_SEL * nb
    gw = GQA_GROUP * HEAD_DIM
    tab_rows = nc + TQ // CMP_STRIDE + nc // CMP_BUCKETS
    n_rows = max(tab_rows, WIN_SPAN)
    row_tile = jnp.asarray(np.broadcast_to(np.arange(n_rows)[:, None], (n_rows, TQ))
                           .astype(np.float32))
    per_group = lambda a: pl.BlockSpec((1,) + a.shape[1:], lambda g, i: (g,) + (0,) * (a.ndim - 1),
                                       pipeline_mode=pl.Buffered(1))
    return pl.pallas_call(
        functools.partial(_attn_kernel, nb=nb),
        grid=(N_KV, s // TQ),
        in_specs=[pl.BlockSpec((TQ, gw), lambda g, i: (i, g)),
                  pl.BlockSpec((1, GATE_ROWS, TQ), lambda g, i: (g, 0, i)),
                  _const_spec(row_tile.shape),
                  _const_spec(kcmp.shape), per_group(vcmp_t),
                  _const_spec(ksel.shape), per_group(vsel_t),
                  _const_spec(kwin.shape), per_group(vwin_t)],
        out_specs=pl.BlockSpec((TQ, gw), lambda g, i: (i, g)),
        out_shape=jax.ShapeDtypeStruct((s, N_HEADS * HEAD_DIM), BF16),
        scratch_shapes=[pltpu.VMEM((TQ // LANES, SUBLANES + nc, LANES), F32),
                        pltpu.VMEM((nb, TQ), F32),
                        pltpu.VMEM((GQA_GROUP, SEL_CHUNK, TQ), F32),
                        pltpu.VMEM((GQA_GROUP, WIN_SPAN, TQ), F32),
                        pltpu.VMEM((GQA_GROUP, tab_rows, TQ), F32),
                        pltpu.VMEM((1, GQA_GROUP * TQ), F32),
                        pltpu.VMEM((V_ROWS, GQA_GROUP * TQ), F32),
                        pltpu.VMEM((V_ROWS, GQA_GROUP * TQ), F32),
                        pltpu.VMEM((V_ROWS, GQA_GROUP * TQ), F32),
                        pltpu.SMEM((nb // BLOCKS_PER_CHUNK,), jnp.int32),
                        pltpu.SMEM((nb // BLOCKS_PER_CHUNK,), jnp.int32)],
        compiler_params=pltpu.CompilerParams(dimension_semantics=("arbitrary", "arbitrary"),
                                             vmem_limit_bytes=VMEM_LIMIT_BYTES),
        name="attn",
    )(q, gates_t, row_tile, kcmp, vcmp_t, ksel, vsel_t, kwin, vwin_t)


def _out_mlp_kernel(pool_ref, attn_ref, wout_ref, x_ref, ga1_ref, g_ref, sc_ref, sh_ref, w1_ref,
                    w2_ref, ga2_ref, o_ref, *, ff_chunk):
    mix = (_dot(pool_ref[...], wout_ref[0:POOL_WIDTH, :])
           + _dot(attn_ref[...], wout_ref[POOL_WIDTH:, :]))
    x1 = x_ref[...] + ga1_ref[...] * mix
    h = _rms_modulate(x1, g_ref[...], sc_ref[...], sh_ref[...]).astype(BF16)
    acc = jnp.zeros(x1.shape, F32)
    for c in range(D_FF // ff_chunk):
        a = _dot(h, w1_ref[:, c * ff_chunk:(c + 1) * ff_chunk])
        a = jnp.square(jnp.maximum(a, 0.0)).astype(BF16)
        acc = acc + _dot(a, w2_ref[c * ff_chunk:(c + 1) * ff_chunk, :])
    o_ref[...] = x1 + ga2_ref[...] * acc


def _out_mlp(pool_out, attn_out, w_out, x, ga1, g2, sc2, sh2, w1, w2, ga2, tm):
    s = x.shape[0]
    row = lambda w: pl.BlockSpec((tm, w), lambda i: (i, 0))
    vec = lambda w: pl.BlockSpec((1, w), lambda i: (0, 0))
    return pl.pallas_call(
        functools.partial(_out_mlp_kernel, ff_chunk=1024),
        grid=(s // tm,),
        in_specs=[row(POOL_WIDTH), row(N_HEADS * HEAD_DIM), _const_spec(w_out.shape),
                  row(D_MODEL), vec(D_MODEL), vec(D_MODEL), vec(D_MODEL), vec(D_MODEL),
                  _const_spec(w1.shape), _const_spec(w2.shape), vec(D_MODEL)],
        out_specs=row(D_MODEL),
        out_shape=jax.ShapeDtypeStruct((s, D_MODEL), F32),
        compiler_params=pltpu.CompilerParams(vmem_limit_bytes=VMEM_LIMIT_BYTES),
        name="out_mlp",
    )(pool_out, attn_out, w_out, x, ga1, g2, sc2, sh2, w1, w2, ga2)


def _pad_in_proj_weight(w_in):
    src_g = _OFF_G
    per_group = GQA_GROUP * N_BRANCH
    cols = [w_in[:, :src_g]]
    for gg in range(N_KV):
        cols += [w_in[:, src_g + gg * per_group:src_g + (gg + 1) * per_group],
                 jnp.zeros((D_MODEL, LANES - per_group), w_in.dtype)]
    return jnp.concatenate(cols, axis=1).astype(BF16)


def _group_lanes(a):
    return a.transpose(1, 0, 2).reshape(a.shape[1], KV_WIDTH)


def kernel(x, c, w_ada, b_ada, norm1_g, norm2_g, w_in, w_pool, pool_scale, q_gain, kc_gain,
           ks_gain, kw_gain, cmp_pos_k, cmp_w1_k, cmp_b1_k, cmp_w2_k, cmp_b2_k, cmp_pos_v,
           cmp_w1_v, cmp_b1_v, cmp_w2_v, cmp_b2_v, w_out, w_ff1, w_ff2):
    batch, s, _ = x.shape
    assert batch == 1 and w_ada.shape[0] == 1
    assert s % SEL_CHUNK == 0 and s >= WIN_SPAN and (s // SEL_BLOCK) & (s // SEL_BLOCK - 1) == 0
    tm = min(s, 512)
    x2 = x[0]

    mod = _ada(jnp.broadcast_to(c, (SUBLANES, D_MODEL)), w_ada[0], b_ada)[0:1]
    sh1, sc1, ga1, sh2, sc2, ga2 = [mod[:, k * D_MODEL:(k + 1) * D_MODEL] for k in range(6)]

    pair = lambda gain: jnp.tile(gain, (1, 2))
    (pool_out, q, kc, vc, ksel, vsel_t, kwin, vwin_t, gates_t) = _in_proj(
        x2, norm1_g, sc1, sh1, _pad_in_proj_weight(w_in[0]), w_pool[0].astype(BF16), pool_scale,
        pair(q_gain), pair(ks_gain), pair(kw_gain), min(s, 1024))

    chunks = lambda a: a.reshape(N_KV, s // CMP_STRIDE, CMP_STRIDE * HEAD_DIM)
    kcmp = _compress(chunks(kc), cmp_pos_k.reshape(1, -1), cmp_w1_k[0].astype(BF16), cmp_b1_k,
                     cmp_w2_k[0].astype(BF16), cmp_b2_k, kc_gain, True)
    vcmp = _compress(chunks(vc), cmp_pos_v.reshape(1, -1), cmp_w1_v[0].astype(BF16), cmp_b1_v,
                     cmp_w2_v[0].astype(BF16), cmp_b2_v, kc_gain, False)

    vcmp_t = jnp.pad(vcmp.transpose(0, 2, 1), ((0, 0), (0, V_ROWS - HEAD_DIM), (0, 0)))
    attn_out = _attention(q, gates_t, _group_lanes(kcmp), vcmp_t, ksel, vsel_t, kwin, vwin_t)

    out = _out_mlp(pool_out, attn_out, w_out[0].astype(BF16), x2, ga1, norm2_g, sc2, sh2,
                   w_ff1[0].astype(BF16), w_ff2[0].astype(BF16), ga2, tm)
    return out[None]
```

```python
import functools

import jax
import jax.numpy as jnp
import numpy as np
from jax import lax
from jax.experimental import pallas as pl
from jax.experimental.pallas import tpu as pltpu

F32 = jnp.float32
BF16 = jnp.bfloat16

LANES = 128
SUBLANES = 8
VMEM_LIMIT_BYTES = 56 * 1024 * 1024

D_MODEL = 1024
POOL_WIDTH = 512
POOL_WINDOWS = (2, 4, 8, 16)
POOL_GROUP = POOL_WIDTH // len(POOL_WINDOWS)
POOL_HALO = 16
HEAD_DIM = 64
N_HEADS = 8
N_KV = 2
GQA_GROUP = N_HEADS // N_KV
KV_WIDTH = N_KV * HEAD_DIM
N_BRANCH = 3
CMP_LEN = 32
CMP_STRIDE = 16
CMP_HIDDEN = 4 * HEAD_DIM
SEL_BLOCK = 64
N_SEL = 16
WINDOW = 512
D_FF = 4 * D_MODEL
NEG_INF = -1e30
N_FORCED = 3
EPS = 1e-6
LOG2E = 1.4426950408889634

TQ = 256
SEL_CHUNK = 256
BLOCKS_PER_CHUNK = SEL_CHUNK // SEL_BLOCK
SEL_GROUP = 6
CMP_BUCKETS = 8
WIN_SPAN = WINDOW + TQ
CMP_PER_SEL = SEL_BLOCK // CMP_STRIDE
GATE_ROWS = 16
V_ROWS = HEAD_DIM + 16

_Q_WIDTH = N_HEADS * HEAD_DIM
_OFF_U = 0
_OFF_Q = POOL_WIDTH
_OFF_KC = _OFF_Q + _Q_WIDTH
_OFF_VC = _OFF_KC + KV_WIDTH
_OFF_KS = _OFF_VC + KV_WIDTH
_OFF_VS = _OFF_KS + KV_WIDTH
_OFF_KW = _OFF_VS + KV_WIDTH
_OFF_VW = _OFF_KW + KV_WIDTH
_OFF_G = _OFF_VW + KV_WIDTH
_IN_PAD = _OFF_G + N_KV * LANES


def _dot(a, b):
    return jnp.dot(a, b, preferred_element_type=F32)


def _dot_nt(a, b):
    return lax.dot_general(a, b, (((1,), (1,)), ((), ())), preferred_element_type=F32)


def _const_spec(shape):
    nd = len(shape)
    return pl.BlockSpec(shape, lambda *_: (0,) * nd, pipeline_mode=pl.Buffered(1))


def _ada_kernel(c_ref, w_ref, b_ref, o_ref):
    o_ref[...] = jnp.dot(c_ref[...], w_ref[...], preferred_element_type=F32,
                         precision=lax.Precision.HIGHEST) + b_ref[...]


def _ada(c8, w, b):
    n = w.shape[1]
    bn = 1024
    return pl.pallas_call(
        _ada_kernel,
        grid=(n // bn,),
        in_specs=[pl.BlockSpec((SUBLANES, D_MODEL), lambda j: (0, 0)),
                  pl.BlockSpec((D_MODEL, bn), lambda j: (0, j)),
                  pl.BlockSpec((1, bn), lambda j: (0, j))],
        out_specs=pl.BlockSpec((SUBLANES, bn), lambda j: (0, j)),
        out_shape=jax.ShapeDtypeStruct((SUBLANES, n), F32),
        compiler_params=pltpu.CompilerParams(vmem_limit_bytes=VMEM_LIMIT_BYTES),
        name="ada",
    )(c8, w, b)


def _rms_modulate(x, g, sc, sh):
    ms = jnp.mean(x * x, axis=-1, keepdims=True)
    return (x * lax.rsqrt(ms + EPS)) * (g * (1.0 + sc)) + sh


def _head_norm_pair(x, gain2):
    lane = lax.broadcasted_iota(jnp.int32, x.shape, 1)
    lo = lane < HEAD_DIM
    sq = x * x
    s_lo = jnp.sum(jnp.where(lo, sq, 0.0), axis=-1, keepdims=True)
    s_hi = jnp.sum(jnp.where(lo, 0.0, sq), axis=-1, keepdims=True)
    ms = jnp.where(lo, s_lo, s_hi) * (1.0 / HEAD_DIM)
    return x * lax.rsqrt(ms + EPS) * gain2


def _in_proj_kernel(x_ref, g_ref, sc_ref, sh_ref, w_ref, wpool_ref, pscale_ref, qg_ref, ksg_ref,
                    kwg_ref, pool_ref, q_ref, kc_ref, vc_ref, ks_ref, vs_ref, kw_ref, vw_ref,
                    gate_ref, ebuf_ref, *, tm):
    i = pl.program_id(0)
    h = _rms_modulate(x_ref[...], g_ref[...], sc_ref[...], sh_ref[...]).astype(BF16)

    def project(off, width):
        return _dot(h, w_ref[:, off:off + width])

    q_all = project(_OFF_Q, _Q_WIDTH)
    for pair in range(N_HEADS // 2):
        qn = _head_norm_pair(q_all[:, pair * LANES:(pair + 1) * LANES], qg_ref[...])
        q_ref[:, pair * LANES:(pair + 1) * LANES] = (qn * (HEAD_DIM ** -0.5 * LOG2E)).astype(BF16)

    kv = project(_OFF_KC, _IN_PAD - _OFF_KC)
    kv_cols = lambda off, width: kv[:, off - _OFF_KC:off - _OFF_KC + width]
    for gg in range(N_KV):
        kc_ref[gg] = kv_cols(_OFF_KC + gg * HEAD_DIM, HEAD_DIM)
        vc_ref[gg] = kv_cols(_OFF_VC + gg * HEAD_DIM, HEAD_DIM)
    ks_ref[...] = _head_norm_pair(kv_cols(_OFF_KS, KV_WIDTH), ksg_ref[...]).astype(BF16)
    kw_ref[...] = _head_norm_pair(kv_cols(_OFF_KW, KV_WIDTH), kwg_ref[...]).astype(BF16)
    def value_tiles(ref, off, chunk):
        ones_row = jnp.where(lax.broadcasted_iota(jnp.int32, (V_ROWS - HEAD_DIM, chunk), 0) == 0,
                             1.0, 0.0)
        for k in range(tm // chunk):
            vt = kv_cols(off, KV_WIDTH)[k * chunk:(k + 1) * chunk].T
            for gg in range(N_KV):
                ref[gg, k] = jnp.concatenate(
                    [vt[gg * HEAD_DIM:(gg + 1) * HEAD_DIM], ones_row], axis=0).astype(BF16)

    value_tiles(vs_ref, _OFF_VS, SEL_CHUNK)
    value_tiles(vw_ref, _OFF_VW, TQ)
    gates_t = jax.nn.sigmoid(kv_cols(_OFF_G, N_KV * LANES)).T
    for gg in range(N_KV):
        gate_ref[gg] = gates_t[gg * LANES:gg * LANES + GATE_ROWS]

    @pl.when(i == 0)
    def _():
        ebuf_ref[0:POOL_HALO, :] = jnp.zeros((POOL_HALO, POOL_WIDTH), F32)

    u = project(_OFF_U, POOL_WIDTH)
    ebuf_ref[POOL_HALO:POOL_HALO + tm, :] = u
    t1 = i * tm + lax.broadcasted_iota(jnp.int32, (tm, POOL_GROUP), 0) + 1
    for gi, w in enumerate(POOL_WINDOWS):
        c0 = gi * POOL_GROUP
        win = u[:, c0:c0 + POOL_GROUP]
        for k in range(1, w):
            win = win + ebuf_ref[POOL_HALO - k:POOL_HALO - k + tm, c0:c0 + POOL_GROUP]
        cnt = jnp.minimum(t1, w).astype(F32)
        pooled = win / cnt - u[:, c0:c0 + POOL_GROUP]
        y = _dot(pooled.astype(BF16), wpool_ref[gi])
        pool_ref[:, c0:c0 + POOL_GROUP] = (y * pscale_ref[:, c0:c0 + POOL_GROUP]).astype(BF16)
    ebuf_ref[0:POOL_HALO, :] = ebuf_ref[tm:tm + POOL_HALO, :]


def _in_proj(x, g1, sc1, sh1, w_in_p, w_pool, pool_scale, qg, ksg, kwg, tm):
    s = x.shape[0]
    row = lambda w: pl.BlockSpec((tm, w), lambda i: (i, 0))
    vec = lambda w: pl.BlockSpec((1, w), lambda i: (0, 0))
    per_head = pl.BlockSpec((N_KV, tm, HEAD_DIM), lambda i: (0, i, 0))
    chunks_t = lambda ch: pl.BlockSpec((N_KV, tm // ch, V_ROWS, ch), lambda i: (0, i, 0, 0))
    out_shape = [
        jax.ShapeDtypeStruct((s, POOL_WIDTH), BF16),
        jax.ShapeDtypeStruct((s, _Q_WIDTH), BF16),
        jax.ShapeDtypeStruct((N_KV, s, HEAD_DIM), F32),
        jax.ShapeDtypeStruct((N_KV, s, HEAD_DIM), F32),
        jax.ShapeDtypeStruct((s, KV_WIDTH), BF16),
        jax.ShapeDtypeStruct((N_KV, s // SEL_CHUNK, V_ROWS, SEL_CHUNK), BF16),
        jax.ShapeDtypeStruct((s, KV_WIDTH), BF16),
        jax.ShapeDtypeStruct((N_KV, s // TQ, V_ROWS, TQ), BF16),
        jax.ShapeDtypeStruct((N_KV, GATE_ROWS, s), F32),
    ]
    return pl.pallas_call(
        functools.partial(_in_proj_kernel, tm=tm),
        grid=(s // tm,),
        in_specs=[row(D_MODEL), vec(D_MODEL), vec(D_MODEL), vec(D_MODEL),
                  _const_spec(w_in_p.shape), _const_spec(w_pool.shape), vec(POOL_WIDTH),
                  vec(LANES), vec(LANES), vec(LANES)],
        out_specs=[row(POOL_WIDTH), row(_Q_WIDTH), per_head, per_head, row(KV_WIDTH),
                   chunks_t(SEL_CHUNK), row(KV_WIDTH), chunks_t(TQ),
                   pl.BlockSpec((N_KV, GATE_ROWS, tm), lambda i: (0, 0, i))],
        out_shape=out_shape,
        scratch_shapes=[pltpu.VMEM((tm + POOL_HALO, POOL_WIDTH), F32)],
        compiler_params=pltpu.CompilerParams(dimension_semantics=("arbitrary",),
                                             vmem_limit_bytes=VMEM_LIMIT_BYTES),
        name="in_proj",
    )(x, g1, sc1, sh1, w_in_p, w_pool, pool_scale, qg, ksg, kwg)


def _compress_kernel(c_ref, pos_ref, w1_ref, b1_ref, w2_ref, b2_ref, gain_ref, o_ref, *,
                     normalize):
    half = CMP_STRIDE * HEAD_DIM
    n_rows = c_ref.shape[1]
    c = c_ref[0]
    first = _dot((c + pos_ref[:, 0:half]).astype(BF16), w1_ref[0:half, :])
    second = _dot((c + pos_ref[:, half:2 * half]).astype(BF16), w1_ref[half:2 * half, :])
    hid = jax.nn.gelu(first + pltpu.roll(second, n_rows - 1, axis=0) + b1_ref[...])
    y = _dot(hid.astype(BF16), w2_ref[...]) + b2_ref[...]
    if normalize:
        ms = jnp.mean(y * y, axis=-1, keepdims=True)
        y = y * lax.rsqrt(ms + EPS) * gain_ref[...]
    o_ref[0] = y.astype(BF16)


def _compress(chunks, pos, w1, b1, w2, b2, gain, normalize):
    _, n_chunks, width = chunks.shape
    vec = lambda w: pl.BlockSpec((1, w), lambda g: (0, 0))
    return pl.pallas_call(
        functools.partial(_compress_kernel, normalize=normalize),
        grid=(N_KV,),
        in_specs=[pl.BlockSpec((1, n_chunks, width), lambda g: (g, 0, 0)),
                  vec(2 * width), _const_spec(w1.shape), vec(CMP_HIDDEN), _const_spec(w2.shape),
                  vec(HEAD_DIM), vec(HEAD_DIM)],
        out_specs=pl.BlockSpec((1, n_chunks, HEAD_DIM), lambda g: (g, 0, 0)),
        out_shape=jax.ShapeDtypeStruct((N_KV, n_chunks, HEAD_DIM), BF16),
        compiler_params=pltpu.CompilerParams(vmem_limit_bytes=VMEM_LIMIT_BYTES),
        name="compress",
    )(chunks, pos, w1, b1, w2, b2, gain)


def _attn_kernel(q_ref, gate_ref, rowt_ref, kc_ref, vct_ref, ks_ref, vst_ref, kw_ref, vwt_ref,
                 o_ref, psum_ref, madd_ref, bias_ref, wbias_ref, ctab_ref, m_ref, acc_ref, ocmp_ref,
                 owin_ref, flag_ref, list_ref, *, nb):
    g = pl.program_id(0)
    i = pl.program_id(1)
    q0 = i * TQ
    nc = CMP_PER_SEL * nb
    n_chunks = nb // BLOCKS_PER_CHUNK
    gslope = jnp.where(g == 0, LOG2E, LOG2E * 2.0 ** -GQA_GROUP).astype(F32)
    slopes = [gslope * (2.0 ** -(r + 1)) for r in range(GQA_GROUP)]
    heads = [slice(r * TQ, (r + 1) * TQ) for r in range(GQA_GROUP)]

    lane = lax.broadcasted_iota(jnp.int32, (TQ, LANES), 1)
    own_half = (lane >= HEAD_DIM) == (g == 1)
    q_rows = []
    for r in range(GQA_GROUP):
        pair = q_ref[:, (r // 2) * LANES:(r // 2 + 1) * LANES]
        swapped = jnp.concatenate([pair[:, HEAD_DIM:], pair[:, :HEAD_DIM]], axis=1)
        q_rows.append(jnp.where(own_half, jnp.where(g == r % 2, pair, swapped), 0.0))
    q = jnp.concatenate(q_rows, axis=0)
    t_lane = q0 + lax.broadcasted_iota(jnp.int32, (1, TQ), 1)

    w0 = pl.multiple_of(jnp.maximum(q0 - WINDOW, 0), TQ)
    s_w = _dot_nt(kw_ref[pl.ds(w0, WIN_SPAN), :], q)

    @pl.when(i <= WINDOW // TQ)
    def _():
        row_w = rowt_ref[0:WIN_SPAN, :]
        lane_w = lax.broadcasted_iota(jnp.int32, (WIN_SPAN, TQ), 1).astype(F32)
        dq = (q0 - w0).astype(F32)
        d_w = lane_w - row_w + dq
        ok_w = jnp.abs(d_w - (WINDOW - 1) * 0.5) < WINDOW * 0.5
        rel_w = row_w - dq
        for r in range(GQA_GROUP):
            wbias_ref[r] = jnp.where(ok_w, slopes[r] * rel_w, NEG_INF)

    wb = w0 // TQ
    for r in range(GQA_GROUP):
        s = s_w[:, heads[r]] + wbias_ref[r]
        m = jnp.max(s, axis=0, keepdims=True)
        p = jnp.exp2((s - m).astype(BF16))
        o_win = _dot(vwt_ref[0, wb], p[0:TQ])
        for b in range(1, WIN_SPAN // TQ):
            o_win = o_win + _dot(vwt_ref[0, wb + b], p[b * TQ:(b + 1) * TQ])
        owin_ref[:, heads[r]] = o_win

    rows_per_bucket = nc // CMP_BUCKETS
    tab_rows = ctab_ref.shape[1]

    @pl.when(i == 0)
    def _():
        rel = CMP_STRIDE * (rowt_ref[0:tab_rows, :] - nc) + (CMP_LEN - 1)
        ok = rel <= lax.broadcasted_iota(jnp.int32, (1, TQ), 1).astype(F32)
        for r in range(GQA_GROUP):
            ctab_ref[r] = jnp.where(ok, slopes[r] * rel, NEG_INF)

    tab0 = pl.multiple_of(nc - q0 // CMP_STRIDE, CMP_STRIDE)

    def compress_and_select(n_rows):
        n_blk = n_rows // CMP_PER_SEL
        s_c = _dot_nt(kc_ref[0:n_rows, :], q)
        p_sum = jnp.zeros((n_rows, TQ), F32)
        p_cols = []
        for r in range(GQA_GROUP):
            s = s_c[:, heads[r]] + ctab_ref[r, pl.ds(tab0, n_rows), :]
            m = jnp.max(s, axis=0, keepdims=True)
            e = jnp.exp2(s - m)
            l = jnp.sum(e, axis=0, keepdims=True)
            p = e * jnp.where(m > 0.5 * NEG_INF, 1.0 / l, 0.0)
            p_sum = p_sum + p
            p_cols.append(p.astype(BF16))
        ocmp_ref[...] = _dot(vct_ref[0, :, 0:n_rows], jnp.concatenate(p_cols, axis=1))

        for h in range(TQ // LANES):
            psum_ref[h, 0:SUBLANES, :] = jnp.zeros((SUBLANES, LANES), F32)
            psum_ref[h, SUBLANES:SUBLANES + n_rows, :] = p_sum[:, h * LANES:(h + 1) * LANES]

        def every4(start):
            parts = [psum_ref[h, pl.ds(SUBLANES + start, n_blk, stride=CMP_PER_SEL), :]
                     for h in range(TQ // LANES)]
            return parts[0] if len(parts) == 1 else jnp.concatenate(parts, axis=1)

        imp = every4(0) + every4(1) + every4(2) + 0.5 * every4(3) + 0.5 * every4(-1)
        blk = lax.broadcasted_iota(jnp.int32, (n_blk, TQ), 0)
        cur = lax.shift_right_logical(t_lane, SEL_BLOCK.bit_length() - 1)
        causal = blk <= cur
        forced = jnp.where(blk == 0, 1.0, 0.0) + jnp.where(blk == cur, 1.0, 0.0) \
            + jnp.where(blk == cur - 1, 1.0, 0.0)
        val = jnp.where(causal, jnp.where(forced > 0.0, -2.0, imp), -1.0)

        def pick(_, v):
            blk_f = rowt_ref[0:n_blk, :]
            mx = jnp.max(v, axis=0, keepdims=True)
            idx = jnp.min(jnp.where(v == mx, blk_f, float(n_blk)), axis=0, keepdims=True)
            return jnp.where(blk_f == idx, -2.0, v)

        picked = lax.fori_loop(0, min(N_SEL, n_blk) - N_FORCED, pick, val)
        madd_ref[0:n_blk, :] = jnp.where(causal, jnp.where(picked < -1.5, 0.0, NEG_INF), NEG_INF)
        madd_ref[0:1, :] = jnp.where(i >= 1, NEG_INF, madd_ref[0:1, :])
        for c in range(n_blk // BLOCKS_PER_CHUNK):
            rows = madd_ref[c * BLOCKS_PER_CHUNK:(c + 1) * BLOCKS_PER_CHUNK, :]
            flag_ref[c] = (jnp.max(rows) > 0.5 * NEG_INF).astype(jnp.int32)

    bucket = ((q0 + TQ) // CMP_STRIDE - 1) // rows_per_bucket
    for b in range(CMP_BUCKETS):
        pl.when(bucket == b)(functools.partial(compress_and_select, (b + 1) * rows_per_bucket))

    @pl.when(i == 0)
    def _():
        key_row = rowt_ref[0:SEL_CHUNK, :]
        for r in range(GQA_GROUP):
            bias_ref[r] = slopes[r] * key_row

    m_ref[...] = jnp.full(m_ref.shape, NEG_INF, F32)
    acc_ref[...] = jnp.zeros(acc_ref.shape, F32)

    def attend(chunks, final):
        sizes = ([SEL_BLOCK] if final else []) + [SEL_CHUNK] * len(chunks)
        keys = ([ks_ref[0:SEL_BLOCK, :]] if final else []) + [
            ks_ref[pl.ds(pl.multiple_of(c * SEL_CHUNK, SEL_CHUNK), SEL_CHUNK), :] for c in chunks]
        s_all = _dot_nt(jnp.concatenate(keys, axis=0) if len(keys) > 1 else keys[0], q)
        offsets = np.cumsum([0] + sizes)
        s = [s_all[offsets[k]:offsets[k + 1]] for k in range(len(sizes))]

        madds, values, rel0 = [], [], []
        if final:
            madds.append(jnp.where(i >= 1, 0.0, NEG_INF))
            values.append(vst_ref[0, 0][:, 0:SEL_BLOCK])
            rel0.append((-q0).astype(F32))
        for k, c in enumerate(chunks):
            k0 = c * SEL_CHUNK
            madd = jnp.concatenate(
                [jnp.broadcast_to(madd_ref[pl.ds(c * BLOCKS_PER_CHUNK + b, 1), :], (SEL_BLOCK, TQ))
                 for b in range(BLOCKS_PER_CHUNK)], axis=0)
            if final and k == len(chunks) - 1:
                pos = k0 + lax.broadcasted_iota(jnp.int32, (SEL_CHUNK, TQ), 0)
                madd = jnp.where(pos <= t_lane, madd, NEG_INF)
            madds.append(madd)
            values.append(vst_ref[0, c])
            rel0.append((k0 - q0).astype(F32))

        p_cols = [[] for _ in sizes]
        alphas = []
        for r in range(GQA_GROUP):
            shifts = [slopes[r] * x for x in rel0]
            us = [sk[:, heads[r]] + bias_ref[r, 0:n, :] + mk for sk, mk, n in zip(s, madds, sizes)]
            m_old = m_ref[:, heads[r]]
            m_new = m_old
            for u, sh in zip(us, shifts):
                m_new = jnp.maximum(m_new, jnp.max(u, axis=0, keepdims=True) + sh)
            alphas.append(jnp.exp2(m_old - m_new))
            for k, (u, sh) in enumerate(zip(us, shifts)):
                p_cols[k].append(jnp.exp2((u - (m_new - sh)).astype(BF16)))
            m_ref[:, heads[r]] = m_new
        pv = _dot(values[0], jnp.concatenate(p_cols[0], axis=1))
        for k in range(1, len(sizes)):
            pv = pv + _dot(values[k], jnp.concatenate(p_cols[k], axis=1))
        acc_ref[...] = jnp.concatenate(alphas, axis=1) * acc_ref[...] + pv

    c_diag = q0 // SEL_CHUNK

    def compact(c, n):
        list_ref[n] = c
        return n + flag_ref[c]

    n_earlier = lax.fori_loop(0, c_diag, compact, jnp.int32(0))
    list_ref[n_earlier] = c_diag

    def active_group(first, size, final):
        attend([list_ref[first + k] for k in range(size)], final)

    def group_body(p, carry):
        active_group(p * SEL_GROUP, SEL_GROUP, False)
        return carry

    n_full = n_earlier // SEL_GROUP
    lax.fori_loop(0, n_full, group_body, 0)
    done = n_full * SEL_GROUP
    for size in range(1, SEL_GROUP + 1):
        pl.when(n_earlier + 1 - done == size)(functools.partial(active_group, done, size, True))

    head_out = []
    for r in range(GQA_GROUP):
        gc = gate_ref[0, 3 * r + 0:3 * r + 1, :]
        gs = gate_ref[0, 3 * r + 1:3 * r + 2, :]
        gw = gate_ref[0, 3 * r + 2:3 * r + 3, :]
        acc = acc_ref[:, heads[r]]
        win = owin_ref[:, heads[r]]
        out_t = (gc * ocmp_ref[:, heads[r]]
                 + (gs / acc[HEAD_DIM:HEAD_DIM + 1]) * acc
                 + (gw / win[HEAD_DIM:HEAD_DIM + 1]) * win)
        head_out.append(out_t.T[:, 0:HEAD_DIM])
    for pair in range(GQA_GROUP // 2):
        o_ref[:, pair * LANES:(pair + 1) * LANES] = jnp.concatenate(
            head_out[2 * pair:2 * pair + 2], axis=1).astype(BF16)


def _attention(q, gates_t, kcmp, vcmp_t, ksel, vsel_t, kwin, vwin_t):
    assert TQ == SEL_CHUNK
    s = q.shape[0]
    nb = s // SEL_BLOCK
    nc = CMP_PER_SEL * nb
    gw = GQA_GROUP * HEAD_DIM
    tab_rows = nc + TQ // CMP_STRIDE + nc // CMP_BUCKETS
    n_rows = max(tab_rows, WIN_SPAN)
    row_tile = jnp.asarray(np.broadcast_to(np.arange(n_rows)[:, None], (n_rows, TQ))
                           .astype(np.float32))
    per_group = lambda a: pl.BlockSpec((1,) + a.shape[1:], lambda g, i: (g,) + (0,) * (a.ndim - 1),
                                       pipeline_mode=pl.Buffered(1))
    return pl.pallas_call(
        functools.partial(_attn_kernel, nb=nb),
        grid=(N_KV, s // TQ),
        in_specs=[pl.BlockSpec((TQ, gw), lambda g, i: (i, g)),
                  pl.BlockSpec((1, GATE_ROWS, TQ), lambda g, i: (g, 0, i)),
                  _const_spec(row_tile.shape),
                  _const_spec(kcmp.shape), per_group(vcmp_t),
                  _const_spec(ksel.shape), per_group(vsel_t),
                  _const_spec(kwin.shape), per_group(vwin_t)],
        out_specs=pl.BlockSpec((TQ, gw), lambda g, i: (i, g)),
        out_shape=jax.ShapeDtypeStruct((s, N_HEADS * HEAD_DIM), BF16),
        scratch_shapes=[pltpu.VMEM((TQ // LANES, SUBLANES + nc, LANES), F32),
                        pltpu.VMEM((nb, TQ), F32),
                        pltpu.VMEM((GQA_GROUP, SEL_CHUNK, TQ), F32),
                        pltpu.VMEM((GQA_GROUP, WIN_SPAN, TQ), F32),
                        pltpu.VMEM((GQA_GROUP, tab_rows, TQ), F32),
                        pltpu.VMEM((1, GQA_GROUP * TQ), F32),
                        pltpu.VMEM((V_ROWS, GQA_GROUP * TQ), F32),
                        pltpu.VMEM((V_ROWS, GQA_GROUP * TQ), F32),
                        pltpu.VMEM((V_ROWS, GQA_GROUP * TQ), F32),
                        pltpu.SMEM((nb // BLOCKS_PER_CHUNK,), jnp.int32),
                        pltpu.SMEM((nb // BLOCKS_PER_CHUNK,), jnp.int32)],
        compiler_params=pltpu.CompilerParams(dimension_semantics=("arbitrary", "arbitrary"),
                                             vmem_limit_bytes=VMEM_LIMIT_BYTES),
        name="attn",
    )(q, gates_t, row_tile, kcmp, vcmp_t, ksel, vsel_t, kwin, vwin_t)


def _out_mlp_kernel(pool_ref, attn_ref, wout_ref, x_ref, ga1_ref, g_ref, sc_ref, sh_ref, w1_ref,
                    w2_ref, ga2_ref, o_ref, *, ff_chunk):
    mix = (_dot(pool_ref[...], wout_ref[0:POOL_WIDTH, :])
           + _dot(attn_ref[...], wout_ref[POOL_WIDTH:, :]))
    x1 = x_ref[...] + ga1_ref[...] * mix
    h = _rms_modulate(x1, g_ref[...], sc_ref[...], sh_ref[...]).astype(BF16)
    acc = jnp.zeros(x1.shape, F32)
    for c in range(D_FF // ff_chunk):
        a = _dot(h, w1_ref[:, c * ff_chunk:(c + 1) * ff_chunk])
        a = jnp.square(jnp.maximum(a, 0.0)).astype(BF16)
        acc = acc + _dot(a, w2_ref[c * ff_chunk:(c + 1) * ff_chunk, :])
    o_ref[...] = x1 + ga2_ref[...] * acc


def _out_mlp(pool_out, attn_out, w_out, x, ga1, g2, sc2, sh2, w1, w2, ga2, tm):
    s = x.shape[0]
    row = lambda w: pl.BlockSpec((tm, w), lambda i: (i, 0))
    vec = lambda w: pl.BlockSpec((1, w), lambda i: (0, 0))
    return pl.pallas_call(
        functools.partial(_out_mlp_kernel, ff_chunk=1024),
        grid=(s // tm,),
        in_specs=[row(POOL_WIDTH), row(N_HEADS * HEAD_DIM), _const_spec(w_out.shape),
                  row(D_MODEL), vec(D_MODEL), vec(D_MODEL), vec(D_MODEL), vec(D_MODEL),
                  _const_spec(w1.shape), _const_spec(w2.shape), vec(D_MODEL)],
        out_specs=row(D_MODEL),
        out_shape=jax.ShapeDtypeStruct((s, D_MODEL), F32),
        compiler_params=pltpu.CompilerParams(vmem_limit_bytes=VMEM_LIMIT_BYTES),
        name="out_mlp",
    )(pool_out, attn_out, w_out, x, ga1, g2, sc2, sh2, w1, w2, ga2)


def _pad_in_proj_weight(w_in):
    src_g = _OFF_G
    per_group = GQA_GROUP * N_BRANCH
    cols = [w_in[:, :src_g]]
    for gg in range(N_KV):
        cols += [w_in[:, src_g + gg * per_group:src_g + (gg + 1) * per_group],
                 jnp.zeros((D_MODEL, LANES - per_group), w_in.dtype)]
    return jnp.concatenate(cols, axis=1).astype(BF16)


def _group_lanes(a):
    return a.transpose(1, 0, 2).reshape(a.shape[1], KV_WIDTH)


def kernel(x, c, w_ada, b_ada, norm1_g, norm2_g, w_in, w_pool, pool_scale, q_gain, kc_gain,
           ks_gain, kw_gain, cmp_pos_k, cmp_w1_k, cmp_b1_k, cmp_w2_k, cmp_b2_k, cmp_pos_v,
           cmp_w1_v, cmp_b1_v, cmp_w2_v, cmp_b2_v, w_out, w_ff1, w_ff2):
    batch, s, _ = x.shape
    assert batch == 1 and w_ada.shape[0] == 1
    assert s % SEL_CHUNK == 0 and s >= WIN_SPAN and (s // SEL_BLOCK) & (s // SEL_BLOCK - 1) == 0
    tm = min(s, 512)
    x2 = x[0]

    mod = _ada(jnp.broadcast_to(c, (SUBLANES, D_MODEL)), w_ada[0], b_ada)[0:1]
    sh1, sc1, ga1, sh2, sc2, ga2 = [mod[:, k * D_MODEL:(k + 1) * D_MODEL] for k in range(6)]

    pair = lambda gain: jnp.tile(gain, (1, 2))
    (pool_out, q, kc, vc, ksel, vsel_t, kwin, vwin_t, gates_t) = _in_proj(
        x2, norm1_g, sc1, sh1, _pad_in_proj_weight(w_in[0]), w_pool[0].astype(BF16), pool_scale,
        pair(q_gain), pair(ks_gain), pair(kw_gain), min(s, 1024))

    chunks = lambda a: a.reshape(N_KV, s // CMP_STRIDE, CMP_STRIDE * HEAD_DIM)
    kcmp = _compress(chunks(kc), cmp_pos_k.reshape(1, -1), cmp_w1_k[0].astype(BF16), cmp_b1_k,
                     cmp_w2_k[0].astype(BF16), cmp_b2_k, kc_gain, True)
    vcmp = _compress(chunks(vc), cmp_pos_v.reshape(1, -1), cmp_w1_v[0].astype(BF16), cmp_b1_v,
                     cmp_w2_v[0].astype(BF16), cmp_b2_v, kc_gain, False)

    vcmp_t = jnp.pad(vcmp.transpose(0, 2, 1), ((0, 0), (0, V_ROWS - HEAD_DIM), (0, 0)))
    attn_out = _attention(q, gates_t, _group_lanes(kcmp), vcmp_t, ksel, vsel_t, kwin, vwin_t)

    out = _out_mlp(pool_out, attn_out, w_out[0].astype(BF16), x2, ga1, norm2_g, sc2, sh2,
                   w_ff1[0].astype(BF16), w_ff2[0].astype(BF16), ga2, tm)
    return out[None]
```

```python
import functools

import jax
import jax.numpy as jnp
import numpy as np
from jax import lax
from jax.experimental import pallas as pl
from jax.experimental.pallas import tpu as pltpu

F32 = jnp.float32
BF16 = jnp.bfloat16

LANES = 128
SUBLANES = 8
VMEM_LIMIT_BYTES = 56 * 1024 * 1024

D_MODEL = 1024
POOL_WIDTH = 512
POOL_WINDOWS = (2, 4, 8, 16)
POOL_GROUP = POOL_WIDTH // len(POOL_WINDOWS)
POOL_HALO = 16
HEAD_DIM = 64
N_HEADS = 8
N_KV = 2
GQA_GROUP = N_HEADS // N_KV
KV_WIDTH = N_KV * HEAD_DIM
N_BRANCH = 3
CMP_LEN = 32
CMP_STRIDE = 16
CMP_HIDDEN = 4 * HEAD_DIM
SEL_BLOCK = 64
N_SEL = 16
WINDOW = 512
D_FF = 4 * D_MODEL
NEG_INF = -1e30
N_FORCED = 3
EPS = 1e-6
LOG2E = 1.4426950408889634

TQ = 256
SEL_CHUNK = 256
BLOCKS_PER_CHUNK = SEL_CHUNK // SEL_BLOCK
SEL_GROUP = 6
CMP_BUCKETS = 8
WIN_SPAN = WINDOW + TQ
CMP_PER_SEL = SEL_BLOCK // CMP_STRIDE
GATE_ROWS = 16
V_ROWS = HEAD_DIM + 16

_Q_WIDTH = N_HEADS * HEAD_DIM
_OFF_U = 0
_OFF_Q = POOL_WIDTH
_OFF_KC = _OFF_Q + _Q_WIDTH
_OFF_VC = _OFF_KC + KV_WIDTH
_OFF_KS = _OFF_VC + KV_WIDTH
_OFF_VS = _OFF_KS + KV_WIDTH
_OFF_KW = _OFF_VS + KV_WIDTH
_OFF_VW = _OFF_KW + KV_WIDTH
_OFF_G = _OFF_VW + KV_WIDTH
_IN_PAD = _OFF_G + N_KV * LANES


def _dot(a, b):
    return jnp.dot(a, b, preferred_element_type=F32)


def _dot_nt(a, b):
    return lax.dot_general(a, b, (((1,), (1,)), ((), ())), preferred_element_type=F32)


def _const_spec(shape):
    nd = len(shape)
    return pl.BlockSpec(shape, lambda *_: (0,) * nd, pipeline_mode=pl.Buffered(1))


def _ada_kernel(c_ref, w_ref, b_ref, o_ref):
    o_ref[...] = jnp.dot(c_ref[...], w_ref[...], preferred_element_type=F32,
                         precision=lax.Precision.HIGHEST) + b_ref[...]


def _ada(c8, w, b):
    n = w.shape[1]
    bn = 1024
    return pl.pallas_call(
        _ada_kernel,
        grid=(n // bn,),
        in_specs=[pl.BlockSpec((SUBLANES, D_MODEL), lambda j: (0, 0)),
                  pl.BlockSpec((D_MODEL, bn), lambda j: (0, j)),
                  pl.BlockSpec((1, bn), lambda j: (0, j))],
        out_specs=pl.BlockSpec((SUBLANES, bn), lambda j: (0, j)),
        out_shape=jax.ShapeDtypeStruct((SUBLANES, n), F32),
        compiler_params=pltpu.CompilerParams(vmem_limit_bytes=VMEM_LIMIT_BYTES),
        name="ada",
    )(c8, w, b)


def _rms_modulate(x, g, sc, sh):
    ms = jnp.mean(x * x, axis=-1, keepdims=True)
    return (x * lax.rsqrt(ms + EPS)) * (g * (1.0 + sc)) + sh


def _head_norm_pair(x, gain2):
    lane = lax.broadcasted_iota(jnp.int32, x.shape, 1)
    lo = lane < HEAD_DIM
    sq = x * x
    s_lo = jnp.sum(jnp.where(lo, sq, 0.0), axis=-1, keepdims=True)
    s_hi = jnp.sum(jnp.where(lo, 0.0, sq), axis=-1, keepdims=True)
    ms = jnp.where(lo, s_lo, s_hi) * (1.0 / HEAD_DIM)
    return x * lax.rsqrt(ms + EPS) * gain2


def _in_proj_kernel(x_ref, g_ref, sc_ref, sh_ref, w_ref, wpool_ref, pscale_ref, qg_ref, ksg_ref,
                    kwg_ref, pool_ref, q_ref, kc_ref, vc_ref, ks_ref, vs_ref, kw_ref, vw_ref,
                    gate_ref, ebuf_ref, *, tm):
    i = pl.program_id(0)
    h = _rms_modulate(x_ref[...], g_ref[...], sc_ref[...], sh_ref[...]).astype(BF16)

    def project(off, width):
        return _dot(h, w_ref[:, off:off + width])

    q_all = project(_OFF_Q, _Q_WIDTH)
    for pair in range(N_HEADS // 2):
        qn = _head_norm_pair(q_all[:, pair * LANES:(pair + 1) * LANES], qg_ref[...])
        q_ref[:, pair * LANES:(pair + 1) * LANES] = (qn * (HEAD_DIM ** -0.5 * LOG2E)).astype(BF16)

    kv = project(_OFF_KC, _IN_PAD - _OFF_KC)
    kv_cols = lambda off, width: kv[:, off - _OFF_KC:off - _OFF_KC + width]
    for gg in range(N_KV):
        kc_ref[gg] = kv_cols(_OFF_KC + gg * HEAD_DIM, HEAD_DIM)
        vc_ref[gg] = kv_cols(_OFF_VC + gg * HEAD_DIM, HEAD_DIM)
    ks_ref[...] = _head_norm_pair(kv_cols(_OFF_KS, KV_WIDTH), ksg_ref[...]).astype(BF16)
    kw_ref[...] = _head_norm_pair(kv_cols(_OFF_KW, KV_WIDTH), kwg_ref[...]).astype(BF16)
    def value_tiles(ref, off, chunk):
        ones_row = jnp.where(lax.broadcasted_iota(jnp.int32, (V_ROWS - HEAD_DIM, chunk), 0) == 0,
                             1.0, 0.0)
        for k in range(tm // chunk):
            vt = kv_cols(off, KV_WIDTH)[k * chunk:(k + 1) * chunk].T
            for gg in range(N_KV):
                ref[gg, k] = jnp.concatenate(
                    [vt[gg * HEAD_DIM:(gg + 1) * HEAD_DIM], ones_row], axis=0).astype(BF16)

    value_tiles(vs_ref, _OFF_VS, SEL_CHUNK)
    value_tiles(vw_ref, _OFF_VW, TQ)
    gates_t = jax.nn.sigmoid(kv_cols(_OFF_G, N_KV * LANES)).T
    for gg in range(N_KV):
        gate_ref[gg] = gates_t[gg * LANES:gg * LANES + GATE_ROWS]

    @pl.when(i == 0)
    def _():
        ebuf_ref[0:POOL_HALO, :] = jnp.zeros((POOL_HALO, POOL_WIDTH), F32)

    u = project(_OFF_U, POOL_WIDTH)
    ebuf_ref[POOL_HALO:POOL_HALO + tm, :] = u
    t1 = i * tm + lax.broadcasted_iota(jnp.int32, (tm, POOL_GROUP), 0) + 1
    for gi, w in enumerate(POOL_WINDOWS):
        c0 = gi * POOL_GROUP
        win = u[:, c0:c0 + POOL_GROUP]
        for k in range(1, w):
            win = win + ebuf_ref[POOL_HALO - k:POOL_HALO - k + tm, c0:c0 + POOL_GROUP]
        cnt = jnp.minimum(t1, w).astype(F32)
        pooled = win / cnt - u[:, c0:c0 + POOL_GROUP]
        y = _dot(pooled.astype(BF16), wpool_ref[gi])
        pool_ref[:, c0:c0 + POOL_GROUP] = (y * pscale_ref[:, c0:c0 + POOL_GROUP]).astype(BF16)
    ebuf_ref[0:POOL_HALO, :] = ebuf_ref[tm:tm + POOL_HALO, :]


def _in_proj(x, g1, sc1, sh1, w_in_p, w_pool, pool_scale, qg, ksg, kwg, tm):
    s = x.shape[0]
    row = lambda w: pl.BlockSpec((tm, w), lambda i: (i, 0))
    vec = lambda w: pl.BlockSpec((1, w), lambda i: (0, 0))
    per_head = pl.BlockSpec((N_KV, tm, HEAD_DIM), lambda i: (0, i, 0))
    chunks_t = lambda ch: pl.BlockSpec((N_KV, tm // ch, V_ROWS, ch), lambda i: (0, i, 0, 0))
    out_shape = [
        jax.ShapeDtypeStruct((s, POOL_WIDTH), BF16),
        jax.ShapeDtypeStruct((s, _Q_WIDTH), BF16),
        jax.ShapeDtypeStruct((N_KV, s, HEAD_DIM), F32),
        jax.ShapeDtypeStruct((N_KV, s, HEAD_DIM), F32),
        jax.ShapeDtypeStruct((s, KV_WIDTH), BF16),
        jax.ShapeDtypeStruct((N_KV, s // SEL_CHUNK, V_ROWS, SEL_CHUNK), BF16),
        jax.ShapeDtypeStruct((s, KV_WIDTH), BF16),
        jax.ShapeDtypeStruct((N_KV, s // TQ, V_ROWS, TQ), BF16),
        jax.ShapeDtypeStruct((N_KV, GATE_ROWS, s), F32),
    ]
    return pl.pallas_call(
        functools.partial(_in_proj_kernel, tm=tm),
        grid=(s // tm,),
        in_specs=[row(D_MODEL), vec(D_MODEL), vec(D_MODEL), vec(D_MODEL),
                  _const_spec(w_in_p.shape), _const_spec(w_pool.shape), vec(POOL_WIDTH),
                  vec(LANES), vec(LANES), vec(LANES)],
        out_specs=[row(POOL_WIDTH), row(_Q_WIDTH), per_head, per_head, row(KV_WIDTH),
                   chunks_t(SEL_CHUNK), row(KV_WIDTH), chunks_t(TQ),
                   pl.BlockSpec((N_KV, GATE_ROWS, tm), lambda i: (0, 0, i))],
        out_shape=out_shape,
        scratch_shapes=[pltpu.VMEM((tm + POOL_HALO, POOL_WIDTH), F32)],
        compiler_params=pltpu.CompilerParams(dimension_semantics=("arbitrary",),
                                             vmem_limit_bytes=VMEM_LIMIT_BYTES),
        name="in_proj",
    )(x, g1, sc1, sh1, w_in_p, w_pool, pool_scale, qg, ksg, kwg)


def _compress_kernel(c_ref, pos_ref, w1_ref, b1_ref, w2_ref, b2_ref, gain_ref, o_ref, *,
                     normalize):
    half = CMP_STRIDE * HEAD_DIM
    n_rows = c_ref.shape[1]
    c = c_ref[0]
    first = _dot((c + pos_ref[:, 0:half]).astype(BF16), w1_ref[0:half, :])
    second = _dot((c + pos_ref[:, half:2 * half]).astype(BF16), w1_ref[half:2 * half, :])
    hid = jax.nn.gelu(first + pltpu.roll(second, n_rows - 1, axis=0) + b1_ref[...])
    y = _dot(hid.astype(BF16), w2_ref[...]) + b2_ref[...]
    if normalize:
        ms = jnp.mean(y * y, axis=-1, keepdims=True)
        y = y * lax.rsqrt(ms + EPS) * gain_ref[...]
    o_ref[0] = y.astype(BF16)


def _compress(chunks, pos, w1, b1, w2, b2, gain, normalize):
    _, n_chunks, width = chunks.shape
    vec = lambda w: pl.BlockSpec((1, w), lambda g: (0, 0))
    return pl.pallas_call(
        functools.partial(_compress_kernel, normalize=normalize),
        grid=(N_KV,),
        in_specs=[pl.BlockSpec((1, n_chunks, width), lambda g: (g, 0, 0)),
                  vec(2 * width), _const_spec(w1.shape), vec(CMP_HIDDEN), _const_spec(w2.shape),
                  vec(HEAD_DIM), vec(HEAD_DIM)],
        out_specs=pl.BlockSpec((1, n_chunks, HEAD_DIM), lambda g: (g, 0, 0)),
        out_shape=jax.ShapeDtypeStruct((N_KV, n_chunks, HEAD_DIM), BF16),
        compiler_params=pltpu.CompilerParams(vmem_limit_bytes=VMEM_LIMIT_BYTES),
        name="compress",
    )(chunks, pos, w1, b1, w2, b2, gain)


def _attn_kernel(q_ref, gate_ref, rowt_ref, kc_ref, vct_ref, ks_ref, vst_ref, kw_ref, vwt_ref,
                 o_ref, psum_ref, madd_ref, bias_ref, wbias_ref, ctab_ref, m_ref, acc_ref, ocmp_ref,
                 owin_ref, count_ref, list_ref, *, nb):
    g = pl.program_id(0)
    i = pl.program_id(1)
    q0 = i * TQ
    nc = CMP_PER_SEL * nb
    n_chunks = nb // BLOCKS_PER_CHUNK
    gslope = jnp.where(g == 0, LOG2E, LOG2E * 2.0 ** -GQA_GROUP).astype(F32)
    slopes = [gslope * (2.0 ** -(r + 1)) for r in range(GQA_GROUP)]
    heads = [slice(r * TQ, (r + 1) * TQ) for r in range(GQA_GROUP)]

    lane = lax.broadcasted_iota(jnp.int32, (TQ, LANES), 1)
    own_half = (lane >= HEAD_DIM) == (g == 1)
    q_rows = []
    for r in range(GQA_GROUP):
        pair = q_ref[:, (r // 2) * LANES:(r // 2 + 1) * LANES]
        swapped = jnp.concatenate([pair[:, HEAD_DIM:], pair[:, :HEAD_DIM]], axis=1)
        q_rows.append(jnp.where(own_half, jnp.where(g == r % 2, pair, swapped), 0.0))
    q = jnp.concatenate(q_rows, axis=0)
    t_lane = q0 + lax.broadcasted_iota(jnp.int32, (1, TQ), 1)

    w0 = pl.multiple_of(jnp.maximum(q0 - WINDOW, 0), TQ)
    s_w = _dot_nt(kw_ref[pl.ds(w0, WIN_SPAN), :], q)

    @pl.when(i <= WINDOW // TQ)
    def _():
        row_w = rowt_ref[0:WIN_SPAN, :]
        lane_w = lax.broadcasted_iota(jnp.int32, (WIN_SPAN, TQ), 1).astype(F32)
        dq = (q0 - w0).astype(F32)
        d_w = lane_w - row_w + dq
        ok_w = jnp.abs(d_w - (WINDOW - 1) * 0.5) < WINDOW * 0.5
        rel_w = row_w - dq
        for r in range(GQA_GROUP):
            wbias_ref[r] = jnp.where(ok_w, slopes[r] * rel_w, NEG_INF)

    wb = w0 // TQ
    for r in range(GQA_GROUP):
        s = s_w[:, heads[r]] + wbias_ref[r]
        m = jnp.max(s, axis=0, keepdims=True)
        p = jnp.exp2((s - m).astype(BF16))
        o_win = _dot(vwt_ref[0, wb], p[0:TQ])
        for b in range(1, WIN_SPAN // TQ):
            o_win = o_win + _dot(vwt_ref[0, wb + b], p[b * TQ:(b + 1) * TQ])
        owin_ref[:, heads[r]] = o_win

    rows_per_bucket = nc // CMP_BUCKETS
    tab_rows = ctab_ref.shape[1]

    @pl.when(i == 0)
    def _():
        rel = CMP_STRIDE * (rowt_ref[0:tab_rows, :] - nc) + (CMP_LEN - 1)
        ok = rel <= lax.broadcasted_iota(jnp.int32, (1, TQ), 1).astype(F32)
        for r in range(GQA_GROUP):
            ctab_ref[r] = jnp.where(ok, slopes[r] * rel, NEG_INF)

    tab0 = pl.multiple_of(nc - q0 // CMP_STRIDE, CMP_STRIDE)

    def compress_and_select(n_rows):
        n_blk = n_rows // CMP_PER_SEL
        s_c = _dot_nt(kc_ref[0:n_rows, :], q)
        p_sum = jnp.zeros((n_rows, TQ), F32)
        p_cols = []
        for r in range(GQA_GROUP):
            s = s_c[:, heads[r]] + ctab_ref[r, pl.ds(tab0, n_rows), :]
            m = jnp.max(s, axis=0, keepdims=True)
            e = jnp.exp2(s - m)
            l = jnp.sum(e, axis=0, keepdims=True)
            p = e * jnp.where(m > 0.5 * NEG_INF, 1.0 / l, 0.0)
            p_sum = p_sum + p
            p_cols.append(p.astype(BF16))
        ocmp_ref[...] = _dot(vct_ref[0, :, 0:n_rows], jnp.concatenate(p_cols, axis=1))

        for h in range(TQ // LANES):
            psum_ref[h, 0:SUBLANES, :] = jnp.zeros((SUBLANES, LANES), F32)
            psum_ref[h, SUBLANES:SUBLANES + n_rows, :] = p_sum[:, h * LANES:(h + 1) * LANES]

        def every4(start):
            parts = [psum_ref[h, pl.ds(SUBLANES + start, n_blk, stride=CMP_PER_SEL), :]
                     for h in range(TQ // LANES)]
            return parts[0] if len(parts) == 1 else jnp.concatenate(parts, axis=1)

        imp = every4(0) + every4(1) + every4(2) + 0.5 * every4(3) + 0.5 * every4(-1)
        blk = lax.broadcasted_iota(jnp.int32, (n_blk, TQ), 0)
        cur = lax.shift_right_logical(t_lane, SEL_BLOCK.bit_length() - 1)
        causal = blk <= cur
        forced = jnp.where(blk == 0, 1.0, 0.0) + jnp.where(blk == cur, 1.0, 0.0) \
            + jnp.where(blk == cur - 1, 1.0, 0.0)
        val = jnp.where(causal, jnp.where(forced > 0.0, -2.0, imp), -1.0)

        def pick(_, v):
            blk_f = rowt_ref[0:n_blk, :]
            mx = jnp.max(v, axis=0, keepdims=True)
            idx = jnp.min(jnp.where(v == mx, blk_f, float(n_blk)), axis=0, keepdims=True)
            return jnp.where(blk_f == idx, -2.0, v)

        picked = lax.fori_loop(0, min(N_SEL, n_blk) - N_FORCED, pick, val)
        madd_ref[0:n_blk, :] = jnp.where(causal, jnp.where(picked < -1.5, 0.0, NEG_INF), NEG_INF)
        madd_ref[0:1, :] = jnp.where(i >= 1, NEG_INF, madd_ref[0:1, :])
        n = jnp.int32(0)
        for c in range(n_blk // BLOCKS_PER_CHUNK):
            rows = madd_ref[c * BLOCKS_PER_CHUNK:(c + 1) * BLOCKS_PER_CHUNK, :]
            active = (jnp.max(rows) > 0.5 * NEG_INF) & (c < c_diag)
            list_ref[n] = c
            n = n + active.astype(jnp.int32)
        count_ref[0] = n

    c_diag = q0 // SEL_CHUNK
    bucket = ((q0 + TQ) // CMP_STRIDE - 1) // rows_per_bucket
    for b in range(CMP_BUCKETS):
        pl.when(bucket == b)(functools.partial(compress_and_select, (b + 1) * rows_per_bucket))

    @pl.when(i == 0)
    def _():
        key_row = rowt_ref[0:SEL_CHUNK, :]
        for r in range(GQA_GROUP):
            bias_ref[r] = slopes[r] * key_row

    m_ref[...] = jnp.full(m_ref.shape, NEG_INF, F32)
    acc_ref[...] = jnp.zeros(acc_ref.shape, F32)

    def attend(chunks, final):
        sizes = ([SEL_BLOCK] if final else []) + [SEL_CHUNK] * len(chunks)
        keys = ([ks_ref[0:SEL_BLOCK, :]] if final else []) + [
            ks_ref[pl.ds(pl.multiple_of(c * SEL_CHUNK, SEL_CHUNK), SEL_CHUNK), :] for c in chunks]
        s_all = _dot_nt(jnp.concatenate(keys, axis=0) if len(keys) > 1 else keys[0], q)
        offsets = np.cumsum([0] + sizes)
        s = [s_all[offsets[k]:offsets[k + 1]] for k in range(len(sizes))]

        madds, values, rel0 = [], [], []
        if final:
            madds.append(jnp.where(i >= 1, 0.0, NEG_INF))
            values.append(vst_ref[0, 0][:, 0:SEL_BLOCK])
            rel0.append((-q0).astype(F32))
        for k, c in enumerate(chunks):
            k0 = c * SEL_CHUNK
            madd = jnp.concatenate(
                [jnp.broadcast_to(madd_ref[pl.ds(c * BLOCKS_PER_CHUNK + b, 1), :], (SEL_BLOCK, TQ))
                 for b in range(BLOCKS_PER_CHUNK)], axis=0)
            if final and k == len(chunks) - 1:
                pos = k0 + lax.broadcasted_iota(jnp.int32, (SEL_CHUNK, TQ), 0)
                madd = jnp.where(pos <= t_lane, madd, NEG_INF)
            madds.append(madd)
            values.append(vst_ref[0, c])
            rel0.append((k0 - q0).astype(F32))

        p_cols = [[] for _ in sizes]
        alphas = []
        for r in range(GQA_GROUP):
            shifts = [slopes[r] * x for x in rel0]
            us = [sk[:, heads[r]] + bias_ref[r, 0:n, :] + mk for sk, mk, n in zip(s, madds, sizes)]
            m_old = m_ref[:, heads[r]]
            m_new = m_old
            for u, sh in zip(us, shifts):
                m_new = jnp.maximum(m_new, jnp.max(u, axis=0, keepdims=True) + sh)
            alphas.append(jnp.exp2(m_old - m_new))
            for k, (u, sh) in enumerate(zip(us, shifts)):
                p_cols[k].append(jnp.exp2((u - (m_new - sh)).astype(BF16)))
            m_ref[:, heads[r]] = m_new
        pv = _dot(values[0], jnp.concatenate(p_cols[0], axis=1))
        for k in range(1, len(sizes)):
            pv = pv + _dot(values[k], jnp.concatenate(p_cols[k], axis=1))
        acc_ref[...] = jnp.concatenate(alphas, axis=1) * acc_ref[...] + pv

    n_earlier = count_ref[0]
    list_ref[n_earlier] = c_diag

    def active_group(first, size, final):
        attend([list_ref[first + k] for k in range(size)], final)

    def group_body(p, carry):
        active_group(p * SEL_GROUP, SEL_GROUP, False)
        return carry

    n_full = n_earlier // SEL_GROUP
    lax.fori_loop(0, n_full, group_body, 0)
    done = n_full * SEL_GROUP
    for size in range(1, SEL_GROUP + 1):
        pl.when(n_earlier + 1 - done == size)(functools.partial(active_group, done, size, True))

    head_out = []
    for r in range(GQA_GROUP):
        gc = gate_ref[0, 3 * r + 0:3 * r + 1, :]
        gs = gate_ref[0, 3 * r + 1:3 * r + 2, :]
        gw = gate_ref[0, 3 * r + 2:3 * r + 3, :]
        acc = acc_ref[:, heads[r]]
        win = owin_ref[:, heads[r]]
        out_t = (gc * ocmp_ref[:, heads[r]]
                 + (gs / acc[HEAD_DIM:HEAD_DIM + 1]) * acc
                 + (gw / win[HEAD_DIM:HEAD_DIM + 1]) * win)
        head_out.append(out_t.T[:, 0:HEAD_DIM])
    for pair in range(GQA_GROUP // 2):
        o_ref[:, pair * LANES:(pair + 1) * LANES] = jnp.concatenate(
            head_out[2 * pair:2 * pair + 2], axis=1).astype(BF16)


def _attention(q, gates_t, kcmp, vcmp_t, ksel, vsel_t, kwin, vwin_t):
    assert TQ == SEL_CHUNK
    s = q.shape[0]
    nb = s // SEL_BLOCK
    nc = CMP_PER_SEL * nb
    gw = GQA_GROUP * HEAD_DIM
    tab_rows = nc + TQ // CMP_STRIDE + nc // CMP_BUCKETS
    n_rows = max(tab_rows, WIN_SPAN)
    row_tile = jnp.asarray(np.broadcast_to(np.arange(n_rows)[:, None], (n_rows, TQ))
                           .astype(np.float32))
    per_group = lambda a: pl.BlockSpec((1,) + a.shape[1:], lambda g, i: (g,) + (0,) * (a.ndim - 1),
                                       pipeline_mode=pl.Buffered(1))
    return pl.pallas_call(
        functools.partial(_attn_kernel, nb=nb),
        grid=(N_KV, s // TQ),
        in_specs=[pl.BlockSpec((TQ, gw), lambda g, i: (i, g)),
                  pl.BlockSpec((1, GATE_ROWS, TQ), lambda g, i: (g, 0, i)),
                  _const_spec(row_tile.shape),
                  _const_spec(kcmp.shape), per_group(vcmp_t),
                  _const_spec(ksel.shape), per_group(vsel_t),
                  _const_spec(kwin.shape), per_group(vwin_t)],
        out_specs=pl.BlockSpec((TQ, gw), lambda g, i: (i, g)),
        out_shape=jax.ShapeDtypeStruct((s, N_HEADS * HEAD_DIM), BF16),
        scratch_shapes=[pltpu.VMEM((TQ // LANES, SUBLANES + nc, LANES), F32),
                        pltpu.VMEM((nb, TQ), F32),
                        pltpu.VMEM((GQA_GROUP, SEL_CHUNK, TQ), F32),
                        pltpu.VMEM((GQA_GROUP, WIN_SPAN, TQ), F32),
                        pltpu.VMEM((GQA_GROUP, tab_rows, TQ), F32),
                        pltpu.VMEM((1, GQA_GROUP * TQ), F32),
                        pltpu.VMEM((V_ROWS, GQA_GROUP * TQ), F32),
                        pltpu.VMEM((V_ROWS, GQA_GROUP * TQ), F32),
                        pltpu.VMEM((V_ROWS, GQA_GROUP * TQ), F32),
                        pltpu.SMEM((1,), jnp.int32),
                        pltpu.SMEM((nb // BLOCKS_PER_CHUNK,), jnp.int32)],
        compiler_params=pltpu.CompilerParams(dimension_semantics=("arbitrary", "arbitrary"),
                                             vmem_limit_bytes=VMEM_LIMIT_BYTES),
        name="attn",
    )(q, gates_t, row_tile, kcmp, vcmp_t, ksel, vsel_t, kwin, vwin_t)


def _out_mlp_kernel(pool_ref, attn_ref, wout_ref, x_ref, ga1_ref, g_ref, sc_ref, sh_ref, w1_ref,
                    w2_ref, ga2_ref, o_ref, *, ff_chunk):
    mix = (_dot(pool_ref[...], wout_ref[0:POOL_WIDTH, :])
           + _dot(attn_ref[...], wout_ref[POOL_WIDTH:, :]))
    x1 = x_ref[...] + ga1_ref[...] * mix
    h = _rms_modulate(x1, g_ref[...], sc_ref[...], sh_ref[...]).astype(BF16)
    acc = jnp.zeros(x1.shape, F32)
    for c in range(D_FF // ff_chunk):
        a = _dot(h, w1_ref[:, c * ff_chunk:(c + 1) * ff_chunk])
        a = jnp.square(jnp.maximum(a, 0.0)).astype(BF16)
        acc = acc + _dot(a, w2_ref[c * ff_chunk:(c + 1) * ff_chunk, :])
    o_ref[...] = x1 + ga2_ref[...] * acc


def _out_mlp(pool_out, attn_out, w_out, x, ga1, g2, sc2, sh2, w1, w2, ga2, tm):
    s = x.shape[0]
    row = lambda w: pl.BlockSpec((tm, w), lambda i: (i, 0))
    vec = lambda w: pl.BlockSpec((1, w), lambda i: (0, 0))
    return pl.pallas_call(
        functools.partial(_out_mlp_kernel, ff_chunk=1024),
        grid=(s // tm,),
        in_specs=[row(POOL_WIDTH), row(N_HEADS * HEAD_DIM), _const_spec(w_out.shape),
                  row(D_MODEL), vec(D_MODEL), vec(D_MODEL), vec(D_MODEL), vec(D_MODEL),
                  _const_spec(w1.shape), _const_spec(w2.shape), vec(D_MODEL)],
        out_specs=row(D_MODEL),
        out_shape=jax.ShapeDtypeStruct((s, D_MODEL), F32),
        compiler_params=pltpu.CompilerParams(vmem_limit_bytes=VMEM_LIMIT_BYTES),
        name="out_mlp",
    )(pool_out, attn_out, w_out, x, ga1, g2, sc2, sh2, w1, w2, ga2)


def _pad_in_proj_weight(w_in):
    src_g = _OFF_G
    per_group = GQA_GROUP * N_BRANCH
    cols = [w_in[:, :src_g]]
    for gg in range(N_KV):
        cols += [w_in[:, src_g + gg * per_group:src_g + (gg + 1) * per_group],
                 jnp.zeros((D_MODEL, LANES - per_group), w_in.dtype)]
    return jnp.concatenate(cols, axis=1).astype(BF16)


def _group_lanes(a):
    return a.transpose(1, 0, 2).reshape(a.shape[1], KV_WIDTH)


def kernel(x, c, w_ada, b_ada, norm1_g, norm2_g, w_in, w_pool, pool_scale, q_gain, kc_gain,
           ks_gain, kw_gain, cmp_pos_k, cmp_w1_k, cmp_b1_k, cmp_w2_k, cmp_b2_k, cmp_pos_v,
           cmp_w1_v, cmp_b1_v, cmp_w2_v, cmp_b2_v, w_out, w_ff1, w_ff2):
    batch, s, _ = x.shape
    assert batch == 1 and w_ada.shape[0] == 1
    assert s % SEL_CHUNK == 0 and s >= WIN_SPAN and (s // SEL_BLOCK) & (s // SEL_BLOCK - 1) == 0
    tm = min(s, 512)
    x2 = x[0]

    mod = _ada(jnp.broadcast_to(c, (SUBLANES, D_MODEL)), w_ada[0], b_ada)[0:1]
    sh1, sc1, ga1, sh2, sc2, ga2 = [mod[:, k * D_MODEL:(k + 1) * D_MODEL] for k in range(6)]

    pair = lambda gain: jnp.tile(gain, (1, 2))
    (pool_out, q, kc, vc, ksel, vsel_t, kwin, vwin_t, gates_t) = _in_proj(
        x2, norm1_g, sc1, sh1, _pad_in_proj_weight(w_in[0]), w_pool[0].astype(BF16), pool_scale,
        pair(q_gain), pair(ks_gain), pair(kw_gain), min(s, 1024))

    chunks = lambda a: a.reshape(N_KV, s // CMP_STRIDE, CMP_STRIDE * HEAD_DIM)
    kcmp = _compress(chunks(kc), cmp_pos_k.reshape(1, -1), cmp_w1_k[0].astype(BF16), cmp_b1_k,
                     cmp_w2_k[0].astype(BF16), cmp_b2_k, kc_gain, True)
    vcmp = _compress(chunks(vc), cmp_pos_v.reshape(1, -1), cmp_w1_v[0].astype(BF16), cmp_b1_v,
                     cmp_w2_v[0].astype(BF16), cmp_b2_v, kc_gain, False)

    vcmp_t = jnp.pad(vcmp.transpose(0, 2, 1), ((0, 0), (0, V_ROWS - HEAD_DIM), (0, 0)))
    attn_out = _attention(q, gates_t, _group_lanes(kcmp), vcmp_t, ksel, vsel_t, kwin, vwin_t)

    out = _out_mlp(pool_out, attn_out, w_out[0].astype(BF16), x2, ga1, norm2_g, sc2, sh2,
                   w_ff1[0].astype(BF16), w_ff2[0].astype(BF16), ga2, tm)
    return out[None]
```

```python
import functools

import jax
import jax.numpy as jnp
import numpy as np
from jax import lax
from jax.experimental import pallas as pl
from jax.experimental.pallas import tpu as pltpu

F32 = jnp.float32
BF16 = jnp.bfloat16

LANES = 128
SUBLANES = 8
VMEM_LIMIT_BYTES = 56 * 1024 * 1024

D_MODEL = 1024
POOL_WIDTH = 512
POOL_WINDOWS = (2, 4, 8, 16)
POOL_GROUP = POOL_WIDTH // len(POOL_WINDOWS)
POOL_HALO = 16
HEAD_DIM = 64
N_HEADS = 8
N_KV = 2
GQA_GROUP = N_HEADS // N_KV
KV_WIDTH = N_KV * HEAD_DIM
N_BRANCH = 3
CMP_LEN = 32
CMP_STRIDE = 16
CMP_HIDDEN = 4 * HEAD_DIM
SEL_BLOCK = 64
N_SEL = 16
WINDOW = 512
D_FF = 4 * D_MODEL
NEG_INF = -1e30
N_FORCED = 3
EPS = 1e-6
LOG2E = 1.4426950408889634

TQ = 256
SEL_CHUNK = 256
BLOCKS_PER_CHUNK = SEL_CHUNK // SEL_BLOCK
SEL_GROUP = 6
CMP_BUCKETS = 8
WIN_SPAN = WINDOW + TQ
CMP_PER_SEL = SEL_BLOCK // CMP_STRIDE
GATE_ROWS = 16
V_ROWS = HEAD_DIM + 16

_Q_WIDTH = N_HEADS * HEAD_DIM
_OFF_U = 0
_OFF_Q = POOL_WIDTH
_OFF_KC = _OFF_Q + _Q_WIDTH
_OFF_VC = _OFF_KC + KV_WIDTH
_OFF_KS = _OFF_VC + KV_WIDTH
_OFF_VS = _OFF_KS + KV_WIDTH
_OFF_KW = _OFF_VS + KV_WIDTH
_OFF_VW = _OFF_KW + KV_WIDTH
_OFF_G = _OFF_VW + KV_WIDTH
_IN_PAD = _OFF_G + N_KV * LANES


def _dot(a, b):
    return jnp.dot(a, b, preferred_element_type=F32)


def _dot_nt(a, b):
    return lax.dot_general(a, b, (((1,), (1,)), ((), ())), preferred_element_type=F32)


def _const_spec(shape):
    nd = len(shape)
    return pl.BlockSpec(shape, lambda *_: (0,) * nd, pipeline_mode=pl.Buffered(1))


def _ada_kernel(c_ref, w_ref, b_ref, o_ref):
    o_ref[...] = jnp.dot(c_ref[...], w_ref[...], preferred_element_type=F32,
                         precision=lax.Precision.HIGHEST) + b_ref[...]


def _ada(c8, w, b):
    n = w.shape[1]
    bn = 1024
    return pl.pallas_call(
        _ada_kernel,
        grid=(n // bn,),
        in_specs=[pl.BlockSpec((SUBLANES, D_MODEL), lambda j: (0, 0)),
                  pl.BlockSpec((D_MODEL, bn), lambda j: (0, j)),
                  pl.BlockSpec((1, bn), lambda j: (0, j))],
        out_specs=pl.BlockSpec((SUBLANES, bn), lambda j: (0, j)),
        out_shape=jax.ShapeDtypeStruct((SUBLANES, n), F32),
        compiler_params=pltpu.CompilerParams(vmem_limit_bytes=VMEM_LIMIT_BYTES),
        name="ada",
    )(c8, w, b)


def _rms_modulate(x, g, sc, sh):
    ms = jnp.mean(x * x, axis=-1, keepdims=True)
    return (x * lax.rsqrt(ms + EPS)) * (g * (1.0 + sc)) + sh


def _head_norm_pair(x, gain2):
    lane = lax.broadcasted_iota(jnp.int32, x.shape, 1)
    lo = lane < HEAD_DIM
    sq = x * x
    s_lo = jnp.sum(jnp.where(lo, sq, 0.0), axis=-1, keepdims=True)
    s_hi = jnp.sum(jnp.where(lo, 0.0, sq), axis=-1, keepdims=True)
    ms = jnp.where(lo, s_lo, s_hi) * (1.0 / HEAD_DIM)
    return x * lax.rsqrt(ms + EPS) * gain2


def _in_proj_kernel(x_ref, g_ref, sc_ref, sh_ref, w_ref, wpool_ref, pscale_ref, qg_ref, ksg_ref,
                    kwg_ref, pool_ref, q_ref, kc_ref, vc_ref, ks_ref, vs_ref, kw_ref, vw_ref,
                    gate_ref, ebuf_ref, *, tm):
    i = pl.program_id(0)
    h = _rms_modulate(x_ref[...], g_ref[...], sc_ref[...], sh_ref[...]).astype(BF16)

    def project(off, width):
        return _dot(h, w_ref[:, off:off + width])

    q_all = project(_OFF_Q, _Q_WIDTH)
    for pair in range(N_HEADS // 2):
        qn = _head_norm_pair(q_all[:, pair * LANES:(pair + 1) * LANES], qg_ref[...])
        q_ref[:, pair * LANES:(pair + 1) * LANES] = (qn * (HEAD_DIM ** -0.5 * LOG2E)).astype(BF16)

    kv = project(_OFF_KC, _IN_PAD - _OFF_KC)
    kv_cols = lambda off, width: kv[:, off - _OFF_KC:off - _OFF_KC + width]
    for gg in range(N_KV):
        kc_ref[gg] = kv_cols(_OFF_KC + gg * HEAD_DIM, HEAD_DIM)
        vc_ref[gg] = kv_cols(_OFF_VC + gg * HEAD_DIM, HEAD_DIM)
    ks_ref[...] = _head_norm_pair(kv_cols(_OFF_KS, KV_WIDTH), ksg_ref[...]).astype(BF16)
    kw_ref[...] = _head_norm_pair(kv_cols(_OFF_KW, KV_WIDTH), kwg_ref[...]).astype(BF16)
    def value_tiles(ref, off, chunk):
        ones_row = jnp.where(lax.broadcasted_iota(jnp.int32, (V_ROWS - HEAD_DIM, chunk), 0) == 0,
                             1.0, 0.0)
        for k in range(tm // chunk):
            vt = kv_cols(off, KV_WIDTH)[k * chunk:(k + 1) * chunk].T
            for gg in range(N_KV):
                ref[gg, k] = jnp.concatenate(
                    [vt[gg * HEAD_DIM:(gg + 1) * HEAD_DIM], ones_row], axis=0).astype(BF16)

    value_tiles(vs_ref, _OFF_VS, SEL_CHUNK)
    value_tiles(vw_ref, _OFF_VW, TQ)
    gates_t = jax.nn.sigmoid(kv_cols(_OFF_G, N_KV * LANES)).T
    for gg in range(N_KV):
        gate_ref[gg] = gates_t[gg * LANES:gg * LANES + GATE_ROWS]

    @pl.when(i == 0)
    def _():
        ebuf_ref[0:POOL_HALO, :] = jnp.zeros((POOL_HALO, POOL_WIDTH), F32)

    u = project(_OFF_U, POOL_WIDTH)
    ebuf_ref[POOL_HALO:POOL_HALO + tm, :] = u
    t1 = i * tm + lax.broadcasted_iota(jnp.int32, (tm, POOL_GROUP), 0) + 1
    for gi, w in enumerate(POOL_WINDOWS):
        c0 = gi * POOL_GROUP
        win = u[:, c0:c0 + POOL_GROUP]
        for k in range(1, w):
            win = win + ebuf_ref[POOL_HALO - k:POOL_HALO - k + tm, c0:c0 + POOL_GROUP]
        cnt = jnp.minimum(t1, w).astype(F32)
        pooled = win / cnt - u[:, c0:c0 + POOL_GROUP]
        y = _dot(pooled.astype(BF16), wpool_ref[gi])
        pool_ref[:, c0:c0 + POOL_GROUP] = (y * pscale_ref[:, c0:c0 + POOL_GROUP]).astype(BF16)
    ebuf_ref[0:POOL_HALO, :] = ebuf_ref[tm:tm + POOL_HALO, :]


def _in_proj(x, g1, sc1, sh1, w_in_p, w_pool, pool_scale, qg, ksg, kwg, tm):
    s = x.shape[0]
    row = lambda w: pl.BlockSpec((tm, w), lambda i: (i, 0))
    vec = lambda w: pl.BlockSpec((1, w), lambda i: (0, 0))
    per_head = pl.BlockSpec((N_KV, tm, HEAD_DIM), lambda i: (0, i, 0))
    chunks_t = lambda ch: pl.BlockSpec((N_KV, tm // ch, V_ROWS, ch), lambda i: (0, i, 0, 0))
    out_shape = [
        jax.ShapeDtypeStruct((s, POOL_WIDTH), BF16),
        jax.ShapeDtypeStruct((s, _Q_WIDTH), BF16),
        jax.ShapeDtypeStruct((N_KV, s, HEAD_DIM), F32),
        jax.ShapeDtypeStruct((N_KV, s, HEAD_DIM), F32),
        jax.ShapeDtypeStruct((s, KV_WIDTH), BF16),
        jax.ShapeDtypeStruct((N_KV, s // SEL_CHUNK, V_ROWS, SEL_CHUNK), BF16),
        jax.ShapeDtypeStruct((s, KV_WIDTH), BF16),
        jax.ShapeDtypeStruct((N_KV, s // TQ, V_ROWS, TQ), BF16),
        jax.ShapeDtypeStruct((N_KV, GATE_ROWS, s), F32),
    ]
    return pl.pallas_call(
        functools.partial(_in_proj_kernel, tm=tm),
        grid=(s // tm,),
        in_specs=[row(D_MODEL), vec(D_MODEL), vec(D_MODEL), vec(D_MODEL),
                  _const_spec(w_in_p.shape), _const_spec(w_pool.shape), vec(POOL_WIDTH),
                  vec(LANES), vec(LANES), vec(LANES)],
        out_specs=[row(POOL_WIDTH), row(_Q_WIDTH), per_head, per_head, row(KV_WIDTH),
                   chunks_t(SEL_CHUNK), row(KV_WIDTH), chunks_t(TQ),
                   pl.BlockSpec((N_KV, GATE_ROWS, tm), lambda i: (0, 0, i))],
        out_shape=out_shape,
        scratch_shapes=[pltpu.VMEM((tm + POOL_HALO, POOL_WIDTH), F32)],
        compiler_params=pltpu.CompilerParams(dimension_semantics=("arbitrary",),
                                             vmem_limit_bytes=VMEM_LIMIT_BYTES),
        name="in_proj",
    )(x, g1, sc1, sh1, w_in_p, w_pool, pool_scale, qg, ksg, kwg)


def _compress_kernel(c_ref, pos_ref, w1_ref, b1_ref, w2_ref, b2_ref, gain_ref, o_ref, *,
                     normalize):
    half = CMP_STRIDE * HEAD_DIM
    n_rows = c_ref.shape[1]
    c = c_ref[0]
    first = _dot((c + pos_ref[:, 0:half]).astype(BF16), w1_ref[0:half, :])
    second = _dot((c + pos_ref[:, half:2 * half]).astype(BF16), w1_ref[half:2 * half, :])
    hid = jax.nn.gelu(first + pltpu.roll(second, n_rows - 1, axis=0) + b1_ref[...])
    y = _dot(hid.astype(BF16), w2_ref[...]) + b2_ref[...]
    if normalize:
        ms = jnp.mean(y * y, axis=-1, keepdims=True)
        y = y * lax.rsqrt(ms + EPS) * gain_ref[...]
    o_ref[0] = y.astype(BF16)


def _compress(chunks, pos, w1, b1, w2, b2, gain, normalize):
    _, n_chunks, width = chunks.shape
    vec = lambda w: pl.BlockSpec((1, w), lambda g: (0, 0))
    return pl.pallas_call(
        functools.partial(_compress_kernel, normalize=normalize),
        grid=(N_KV,),
        in_specs=[pl.BlockSpec((1, n_chunks, width), lambda g: (g, 0, 0)),
                  vec(2 * width), _const_spec(w1.shape), vec(CMP_HIDDEN), _const_spec(w2.shape),
                  vec(HEAD_DIM), vec(HEAD_DIM)],
        out_specs=pl.BlockSpec((1, n_chunks, HEAD_DIM), lambda g: (g, 0, 0)),
        out_shape=jax.ShapeDtypeStruct((N_KV, n_chunks, HEAD_DIM), BF16),
        compiler_params=pltpu.CompilerParams(vmem_limit_bytes=VMEM_LIMIT_BYTES),
        name="compress",
    )(chunks, pos, w1, b1, w2, b2, gain)


def _attn_kernel(q_ref, gate_ref, rowt_ref, kc_ref, vct_ref, ks_ref, vst_ref, kw_ref, vwt_ref,
                 o_ref, psum_ref, madd_ref, bias_ref, wbias_ref, ctab_ref, m_ref, acc_ref, ocmp_ref,
                 owin_ref, count_ref, list_ref, *, nb):
    g = pl.program_id(0)
    i = pl.program_id(1)
    q0 = i * TQ
    nc = CMP_PER_SEL * nb
    n_chunks = nb // BLOCKS_PER_CHUNK
    gslope = jnp.where(g == 0, LOG2E, LOG2E * 2.0 ** -GQA_GROUP).astype(F32)
    slopes = [gslope * (2.0 ** -(r + 1)) for r in range(GQA_GROUP)]
    heads = [slice(r * TQ, (r + 1) * TQ) for r in range(GQA_GROUP)]

    lane = lax.broadcasted_iota(jnp.int32, (TQ, LANES), 1)
    own_half = (lane >= HEAD_DIM) == (g == 1)
    q_rows = []
    for r in range(GQA_GROUP):
        pair = q_ref[:, (r // 2) * LANES:(r // 2 + 1) * LANES]
        swapped = jnp.concatenate([pair[:, HEAD_DIM:], pair[:, :HEAD_DIM]], axis=1)
        q_rows.append(jnp.where(own_half, jnp.where(g == r % 2, pair, swapped), 0.0))
    q = jnp.concatenate(q_rows, axis=0)
    t_lane = q0 + lax.broadcasted_iota(jnp.int32, (1, TQ), 1)

    w0 = pl.multiple_of(jnp.maximum(q0 - WINDOW, 0), TQ)
    s_w = _dot_nt(kw_ref[pl.ds(w0, WIN_SPAN), :], q)

    @pl.when(i <= WINDOW // TQ)
    def _():
        row_w = rowt_ref[0:WIN_SPAN, :]
        lane_w = lax.broadcasted_iota(jnp.int32, (WIN_SPAN, TQ), 1).astype(F32)
        dq = (q0 - w0).astype(F32)
        d_w = lane_w - row_w + dq
        ok_w = jnp.abs(d_w - (WINDOW - 1) * 0.5) < WINDOW * 0.5
        rel_w = row_w - dq
        for r in range(GQA_GROUP):
            wbias_ref[r] = jnp.where(ok_w, slopes[r] * rel_w, NEG_INF)

    wb = w0 // TQ
    for r in range(GQA_GROUP):
        s = s_w[:, heads[r]] + wbias_ref[r]
        m = jnp.max(s, axis=0, keepdims=True)
        p = jnp.exp2((s - m).astype(BF16))
        o_win = _dot(vwt_ref[0, wb], p[0:TQ])
        for b in range(1, WIN_SPAN // TQ):
            o_win = o_win + _dot(vwt_ref[0, wb + b], p[b * TQ:(b + 1) * TQ])
        owin_ref[:, heads[r]] = o_win

    rows_per_bucket = nc // CMP_BUCKETS
    tab_rows = ctab_ref.shape[1]

    @pl.when(i == 0)
    def _():
        rel = CMP_STRIDE * (rowt_ref[0:tab_rows, :] - nc) + (CMP_LEN - 1)
        ok = rel <= lax.broadcasted_iota(jnp.int32, (1, TQ), 1).astype(F32)
        for r in range(GQA_GROUP):
            ctab_ref[r] = jnp.where(ok, slopes[r] * rel, NEG_INF)

    tab0 = pl.multiple_of(nc - q0 // CMP_STRIDE, CMP_STRIDE)

    def compress_and_select(n_rows):
        n_blk = n_rows // CMP_PER_SEL
        s_c = _dot_nt(kc_ref[0:n_rows, :], q)
        p_sum = jnp.zeros((n_rows, TQ), F32)
        p_cols = []
        for r in range(GQA_GROUP):
            s = s_c[:, heads[r]] + ctab_ref[r, pl.ds(tab0, n_rows), :]
            m = jnp.max(s, axis=0, keepdims=True)
            e = jnp.exp2(s - m)
            l = jnp.sum(e, axis=0, keepdims=True)
            p = e * jnp.where(m > 0.5 * NEG_INF, 1.0 / l, 0.0)
            p_sum = p_sum + p
            p_cols.append(p.astype(BF16))
        ocmp_ref[...] = _dot(vct_ref[0, :, 0:n_rows], jnp.concatenate(p_cols, axis=1))

        for h in range(TQ // LANES):
            psum_ref[h, 0:SUBLANES, :] = jnp.zeros((SUBLANES, LANES), F32)
            psum_ref[h, SUBLANES:SUBLANES + n_rows, :] = p_sum[:, h * LANES:(h + 1) * LANES]

        def every4(start):
            parts = [psum_ref[h, pl.ds(SUBLANES + start, n_blk, stride=CMP_PER_SEL), :]
                     for h in range(TQ // LANES)]
            return parts[0] if len(parts) == 1 else jnp.concatenate(parts, axis=1)

        imp = every4(0) + every4(1) + every4(2) + 0.5 * every4(3) + 0.5 * every4(-1)
        blk = lax.broadcasted_iota(jnp.int32, (n_blk, TQ), 0)
        cur = lax.shift_right_logical(t_lane, SEL_BLOCK.bit_length() - 1)
        causal = blk <= cur
        forced = jnp.where(blk == 0, 1.0, 0.0) + jnp.where(blk == cur, 1.0, 0.0) \
            + jnp.where(blk == cur - 1, 1.0, 0.0)
        val = jnp.where(causal, jnp.where(forced > 0.0, -2.0, imp), -1.0)

        def pick(_, v):
            blk_f = rowt_ref[0:n_blk, :]
            mx = jnp.max(v, axis=0, keepdims=True)
            idx = jnp.min(jnp.where(v == mx, blk_f, float(n_blk)), axis=0, keepdims=True)
            return jnp.where(blk_f == idx, -2.0, v)

        picked = lax.fori_loop(0, min(N_SEL, n_blk) - N_FORCED, pick, val, unroll=True)
        madd_ref[0:n_blk, :] = jnp.where(causal, jnp.where(picked < -1.5, 0.0, NEG_INF), NEG_INF)
        madd_ref[0:1, :] = jnp.where(i >= 1, NEG_INF, madd_ref[0:1, :])
        n = jnp.int32(0)
        for c in range(n_blk // BLOCKS_PER_CHUNK):
            rows = madd_ref[c * BLOCKS_PER_CHUNK:(c + 1) * BLOCKS_PER_CHUNK, :]
            active = (jnp.max(rows) > 0.5 * NEG_INF) & (c < c_diag)
            list_ref[n] = c
            n = n + active.astype(jnp.int32)
        count_ref[0] = n

    c_diag = q0 // SEL_CHUNK
    bucket = ((q0 + TQ) // CMP_STRIDE - 1) // rows_per_bucket
    for b in range(CMP_BUCKETS):
        pl.when(bucket == b)(functools.partial(compress_and_select, (b + 1) * rows_per_bucket))

    @pl.when(i == 0)
    def _():
        key_row = rowt_ref[0:SEL_CHUNK, :]
        for r in range(GQA_GROUP):
            bias_ref[r] = slopes[r] * key_row

    m_ref[...] = jnp.full(m_ref.shape, NEG_INF, F32)
    acc_ref[...] = jnp.zeros(acc_ref.shape, F32)

    def attend(chunks, final):
        sizes = ([SEL_BLOCK] if final else []) + [SEL_CHUNK] * len(chunks)
        keys = ([ks_ref[0:SEL_BLOCK, :]] if final else []) + [
            ks_ref[pl.ds(pl.multiple_of(c * SEL_CHUNK, SEL_CHUNK), SEL_CHUNK), :] for c in chunks]
        s_all = _dot_nt(jnp.concatenate(keys, axis=0) if len(keys) > 1 else keys[0], q)
        offsets = np.cumsum([0] + sizes)
        s = [s_all[offsets[k]:offsets[k + 1]] for k in range(len(sizes))]

        madds, values, rel0 = [], [], []
        if final:
            madds.append(jnp.where(i >= 1, 0.0, NEG_INF))
            values.append(vst_ref[0, 0][:, 0:SEL_BLOCK])
            rel0.append((-q0).astype(F32))
        for k, c in enumerate(chunks):
            k0 = c * SEL_CHUNK
            madd = jnp.concatenate(
                [jnp.broadcast_to(madd_ref[pl.ds(c * BLOCKS_PER_CHUNK + b, 1), :], (SEL_BLOCK, TQ))
                 for b in range(BLOCKS_PER_CHUNK)], axis=0)
            if final and k == len(chunks) - 1:
                pos = k0 + lax.broadcasted_iota(jnp.int32, (SEL_CHUNK, TQ), 0)
                madd = jnp.where(pos <= t_lane, madd, NEG_INF)
            madds.append(madd)
            values.append(vst_ref[0, c])
            rel0.append((k0 - q0).astype(F32))

        p_cols = [[] for _ in sizes]
        alphas = []
        for r in range(GQA_GROUP):
            shifts = [slopes[r] * x for x in rel0]
            us = [sk[:, heads[r]] + bias_ref[r, 0:n, :] + mk for sk, mk, n in zip(s, madds, sizes)]
            m_old = m_ref[:, heads[r]]
            m_new = m_old
            for u, sh in zip(us, shifts):
                m_new = jnp.maximum(m_new, jnp.max(u, axis=0, keepdims=True) + sh)
            alphas.append(jnp.exp2(m_old - m_new))
            for k, (u, sh) in enumerate(zip(us, shifts)):
                p_cols[k].append(jnp.exp2((u - (m_new - sh)).astype(BF16)))
            m_ref[:, heads[r]] = m_new
        pv = _dot(values[0], jnp.concatenate(p_cols[0], axis=1))
        for k in range(1, len(sizes)):
            pv = pv + _dot(values[k], jnp.concatenate(p_cols[k], axis=1))
        acc_ref[...] = jnp.concatenate(alphas, axis=1) * acc_ref[...] + pv

    n_earlier = count_ref[0]
    list_ref[n_earlier] = c_diag

    def active_group(first, size, final):
        attend([list_ref[first + k] for k in range(size)], final)

    def group_body(p, carry):
        active_group(p * SEL_GROUP, SEL_GROUP, False)
        return carry

    n_full = n_earlier // SEL_GROUP
    lax.fori_loop(0, n_full, group_body, 0)
    done = n_full * SEL_GROUP
    for size in range(1, SEL_GROUP + 1):
        pl.when(n_earlier + 1 - done == size)(functools.partial(active_group, done, size, True))

    head_out = []
    for r in range(GQA_GROUP):
        gc = gate_ref[0, 3 * r + 0:3 * r + 1, :]
        gs = gate_ref[0, 3 * r + 1:3 * r + 2, :]
        gw = gate_ref[0, 3 * r + 2:3 * r + 3, :]
        acc = acc_ref[:, heads[r]]
        win = owin_ref[:, heads[r]]
        out_t = (gc * ocmp_ref[:, heads[r]]
                 + (gs / acc[HEAD_DIM:HEAD_DIM + 1]) * acc
                 + (gw / win[HEAD_DIM:HEAD_DIM + 1]) * win)
        head_out.append(out_t.T[:, 0:HEAD_DIM])
    for pair in range(GQA_GROUP // 2):
        o_ref[:, pair * LANES:(pair + 1) * LANES] = jnp.concatenate(
            head_out[2 * pair:2 * pair + 2], axis=1).astype(BF16)


def _attention(q, gates_t, kcmp, vcmp_t, ksel, vsel_t, kwin, vwin_t):
    assert TQ == SEL_CHUNK
    s = q.shape[0]
    nb = s // SEL_BLOCK
    nc = CMP_PER_SEL * nb
    gw = GQA_GROUP * HEAD_DIM
    tab_rows = nc + TQ // CMP_STRIDE + nc // CMP_BUCKETS
    n_rows = max(tab_rows, WIN_SPAN)
    row_tile = jnp.asarray(np.broadcast_to(np.arange(n_rows)[:, None], (n_rows, TQ))
                           .astype(np.float32))
    per_group = lambda a: pl.BlockSpec((1,) + a.shape[1:], lambda g, i: (g,) + (0,) * (a.ndim - 1),
                                       pipeline_mode=pl.Buffered(1))
    return pl.pallas_call(
        functools.partial(_attn_kernel, nb=nb),
        grid=(N_KV, s // TQ),
        in_specs=[pl.BlockSpec((TQ, gw), lambda g, i: (i, g)),
                  pl.BlockSpec((1, GATE_ROWS, TQ), lambda g, i: (g, 0, i)),
                  _const_spec(row_tile.shape),
                  _const_spec(kcmp.shape), per_group(vcmp_t),
                  _const_spec(ksel.shape), per_group(vsel_t),
                  _const_spec(kwin.shape), per_group(vwin_t)],
        out_specs=pl.BlockSpec((TQ, gw), lambda g, i: (i, g)),
        out_shape=jax.ShapeDtypeStruct((s, N_HEADS * HEAD_DIM), BF16),
        scratch_shapes=[pltpu.VMEM((TQ // LANES, SUBLANES + nc, LANES), F32),
                        pltpu.VMEM((nb, TQ), F32),
                        pltpu.VMEM((GQA_GROUP, SEL_CHUNK, TQ), F32),
                        pltpu.VMEM((GQA_GROUP, WIN_SPAN, TQ), F32),
                        pltpu.VMEM((GQA_GROUP, tab_rows, TQ), F32),
                        pltpu.VMEM((1, GQA_GROUP * TQ), F32),
                        pltpu.VMEM((V_ROWS, GQA_GROUP * TQ), F32),
                        pltpu.VMEM((V_ROWS, GQA_GROUP * TQ), F32),
                        pltpu.VMEM((V_ROWS, GQA_GROUP * TQ), F32),
                        pltpu.SMEM((1,), jnp.int32),
                        pltpu.SMEM((nb // BLOCKS_PER_CHUNK,), jnp.int32)],
        compiler_params=pltpu.CompilerParams(dimension_semantics=("arbitrary", "arbitrary"),
                                             vmem_limit_bytes=VMEM_LIMIT_BYTES),
        name="attn",
    )(q, gates_t, row_tile, kcmp, vcmp_t, ksel, vsel_t, kwin, vwin_t)


def _out_mlp_kernel(pool_ref, attn_ref, wout_ref, x_ref, ga1_ref, g_ref, sc_ref, sh_ref, w1_ref,
                    w2_ref, ga2_ref, o_ref, *, ff_chunk):
    mix = (_dot(pool_ref[...], wout_ref[0:POOL_WIDTH, :])
           + _dot(attn_ref[...], wout_ref[POOL_WIDTH:, :]))
    x1 = x_ref[...] + ga1_ref[...] * mix
    h = _rms_modulate(x1, g_ref[...], sc_ref[...], sh_ref[...]).astype(BF16)
    acc = jnp.zeros(x1.shape, F32)
    for c in range(D_FF // ff_chunk):
        a = _dot(h, w1_ref[:, c * ff_chunk:(c + 1) * ff_chunk])
        a = jnp.square(jnp.maximum(a, 0.0)).astype(BF16)
        acc = acc + _dot(a, w2_ref[c * ff_chunk:(c + 1) * ff_chunk, :])
    o_ref[...] = x1 + ga2_ref[...] * acc


def _out_mlp(pool_out, attn_out, w_out, x, ga1, g2, sc2, sh2, w1, w2, ga2, tm):
    s = x.shape[0]
    row = lambda w: pl.BlockSpec((tm, w), lambda i: (i, 0))
    vec = lambda w: pl.BlockSpec((1, w), lambda i: (0, 0))
    return pl.pallas_call(
        functools.partial(_out_mlp_kernel, ff_chunk=1024),
        grid=(s // tm,),
        in_specs=[row(POOL_WIDTH), row(N_HEADS * HEAD_DIM), _const_spec(w_out.shape),
                  row(D_MODEL), vec(D_MODEL), vec(D_MODEL), vec(D_MODEL), vec(D_MODEL),
                  _const_spec(w1.shape), _const_spec(w2.shape), vec(D_MODEL)],
        out_specs=row(D_MODEL),
        out_shape=jax.ShapeDtypeStruct((s, D_MODEL), F32),
        compiler_params=pltpu.CompilerParams(vmem_limit_bytes=VMEM_LIMIT_BYTES),
        name="out_mlp",
    )(pool_out, attn_out, w_out, x, ga1, g2, sc2, sh2, w1, w2, ga2)


def _pad_in_proj_weight(w_in):
    src_g = _OFF_G
    per_group = GQA_GROUP * N_BRANCH
    cols = [w_in[:, :src_g]]
    for gg in range(N_KV):
        cols += [w_in[:, src_g + gg * per_group:src_g + (gg + 1) * per_group],
                 jnp.zeros((D_MODEL, LANES - per_group), w_in.dtype)]
    return jnp.concatenate(cols, axis=1).astype(BF16)


def _group_lanes(a):
    return a.transpose(1, 0, 2).reshape(a.shape[1], KV_WIDTH)


def kernel(x, c, w_ada, b_ada, norm1_g, norm2_g, w_in, w_pool, pool_scale, q_gain, kc_gain,
           ks_gain, kw_gain, cmp_pos_k, cmp_w1_k, cmp_b1_k, cmp_w2_k, cmp_b2_k, cmp_pos_v,
           cmp_w1_v, cmp_b1_v, cmp_w2_v, cmp_b2_v, w_out, w_ff1, w_ff2):
    batch, s, _ = x.shape
    assert batch == 1 and w_ada.shape[0] == 1
    assert s % SEL_CHUNK == 0 and s >= WIN_SPAN and (s // SEL_BLOCK) & (s // SEL_BLOCK - 1) == 0
    tm = min(s, 512)
    x2 = x[0]

    mod = _ada(jnp.broadcast_to(c, (SUBLANES, D_MODEL)), w_ada[0], b_ada)[0:1]
    sh1, sc1, ga1, sh2, sc2, ga2 = [mod[:, k * D_MODEL:(k + 1) * D_MODEL] for k in range(6)]

    pair = lambda gain: jnp.tile(gain, (1, 2))
    (pool_out, q, kc, vc, ksel, vsel_t, kwin, vwin_t, gates_t) = _in_proj(
        x2, norm1_g, sc1, sh1, _pad_in_proj_weight(w_in[0]), w_pool[0].astype(BF16), pool_scale,
        pair(q_gain), pair(ks_gain), pair(kw_gain), min(s, 1024))

    chunks = lambda a: a.reshape(N_KV, s // CMP_STRIDE, CMP_STRIDE * HEAD_DIM)
    kcmp = _compress(chunks(kc), cmp_pos_k.reshape(1, -1), cmp_w1_k[0].astype(BF16), cmp_b1_k,
                     cmp_w2_k[0].astype(BF16), cmp_b2_k, kc_gain, True)
    vcmp = _compress(chunks(vc), cmp_pos_v.reshape(1, -1), cmp_w1_v[0].astype(BF16), cmp_b1_v,
                     cmp_w2_v[0].astype(BF16), cmp_b2_v, kc_gain, False)

    vcmp_t = jnp.pad(vcmp.transpose(0, 2, 1), ((0, 0), (0, V_ROWS - HEAD_DIM), (0, 0)))
    attn_out = _attention(q, gates_t, _group_lanes(kcmp), vcmp_t, ksel, vsel_t, kwin, vwin_t)

    out = _out_mlp(pool_out, attn_out, w_out[0].astype(BF16), x2, ga1, norm2_g, sc2, sh2,
                   w_ff1[0].astype(BF16), w_ff2[0].astype(BF16), ga2, tm)
    return out[None]
```

```python
import functools

import jax
import jax.numpy as jnp
import numpy as np
from jax import lax
from jax.experimental import pallas as pl
from jax.experimental.pallas import tpu as pltpu

F32 = jnp.float32
BF16 = jnp.bfloat16

LANES = 128
SUBLANES = 8
VMEM_LIMIT_BYTES = 56 * 1024 * 1024

D_MODEL = 1024
POOL_WIDTH = 512
POOL_WINDOWS = (2, 4, 8, 16)
POOL_GROUP = POOL_WIDTH // len(POOL_WINDOWS)
POOL_HALO = 16
HEAD_DIM = 64
N_HEADS = 8
N_KV = 2
GQA_GROUP = N_HEADS // N_KV
KV_WIDTH = N_KV * HEAD_DIM
N_BRANCH = 3
CMP_LEN = 32
CMP_STRIDE = 16
CMP_HIDDEN = 4 * HEAD_DIM
SEL_BLOCK = 64
N_SEL = 16
WINDOW = 512
D_FF = 4 * D_MODEL
NEG_INF = -1e30
N_FORCED = 3
EPS = 1e-6
LOG2E = 1.4426950408889634

TQ = 256
SEL_CHUNK = 256
BLOCKS_PER_CHUNK = SEL_CHUNK // SEL_BLOCK
SEL_GROUP = 6
CMP_BUCKETS = 8
WIN_SPAN = WINDOW + TQ
CMP_PER_SEL = SEL_BLOCK // CMP_STRIDE
GATE_ROWS = 16
V_ROWS = HEAD_DIM + 16
ADA_COLS = 1024
IN_PROJ_ROWS = 1024
OUT_MLP_ROWS = 512
FF_CHUNK = 1024

_Q_WIDTH = N_HEADS * HEAD_DIM
_OFF_U = 0
_OFF_Q = POOL_WIDTH
_OFF_KC = _OFF_Q + _Q_WIDTH
_OFF_VC = _OFF_KC + KV_WIDTH
_OFF_KS = _OFF_VC + KV_WIDTH
_OFF_VS = _OFF_KS + KV_WIDTH
_OFF_KW = _OFF_VS + KV_WIDTH
_OFF_VW = _OFF_KW + KV_WIDTH
_OFF_G = _OFF_VW + KV_WIDTH
_IN_PAD = _OFF_G + N_KV * LANES


def _dot(a, b):
    return jnp.dot(a, b, preferred_element_type=F32)


def _dot_nt(a, b):
    return lax.dot_general(a, b, (((1,), (1,)), ((), ())), preferred_element_type=F32)


def _const_spec(shape):
    nd = len(shape)
    return pl.BlockSpec(shape, lambda *_: (0,) * nd, pipeline_mode=pl.Buffered(1))


def _ada_kernel(c_ref, w_ref, b_ref, o_ref):
    o_ref[...] = jnp.dot(c_ref[...], w_ref[...], preferred_element_type=F32,
                         precision=lax.Precision.HIGHEST) + b_ref[...]


def _ada(c8, w, b):
    n = w.shape[1]
    bn = ADA_COLS
    return pl.pallas_call(
        _ada_kernel,
        grid=(n // bn,),
        in_specs=[pl.BlockSpec((SUBLANES, D_MODEL), lambda j: (0, 0)),
                  pl.BlockSpec((D_MODEL, bn), lambda j: (0, j)),
                  pl.BlockSpec((1, bn), lambda j: (0, j))],
        out_specs=pl.BlockSpec((SUBLANES, bn), lambda j: (0, j)),
        out_shape=jax.ShapeDtypeStruct((SUBLANES, n), F32),
        compiler_params=pltpu.CompilerParams(vmem_limit_bytes=VMEM_LIMIT_BYTES),
        name="ada",
    )(c8, w, b)


def _rms_modulate(x, g, sc, sh):
    ms = jnp.mean(x * x, axis=-1, keepdims=True)
    return (x * lax.rsqrt(ms + EPS)) * (g * (1.0 + sc)) + sh


def _head_norm_pair(x, gain2):
    lane = lax.broadcasted_iota(jnp.int32, x.shape, 1)
    lo = lane < HEAD_DIM
    sq = x * x
    s_lo = jnp.sum(jnp.where(lo, sq, 0.0), axis=-1, keepdims=True)
    s_hi = jnp.sum(jnp.where(lo, 0.0, sq), axis=-1, keepdims=True)
    ms = jnp.where(lo, s_lo, s_hi) * (1.0 / HEAD_DIM)
    return x * lax.rsqrt(ms + EPS) * gain2


def _in_proj_kernel(x_ref, g_ref, sc_ref, sh_ref, w_ref, wpool_ref, pscale_ref, qg_ref, ksg_ref,
                    kwg_ref, pool_ref, q_ref, kc_ref, vc_ref, ks_ref, vs_ref, kw_ref, vw_ref,
                    gate_ref, ebuf_ref, *, tm):
    i = pl.program_id(0)
    h = _rms_modulate(x_ref[...], g_ref[...], sc_ref[...], sh_ref[...]).astype(BF16)

    def project(off, width):
        return _dot(h, w_ref[:, off:off + width])

    q_all = project(_OFF_Q, _Q_WIDTH)
    for pair in range(N_HEADS // 2):
        qn = _head_norm_pair(q_all[:, pair * LANES:(pair + 1) * LANES], qg_ref[...])
        q_ref[:, pair * LANES:(pair + 1) * LANES] = (qn * (HEAD_DIM ** -0.5 * LOG2E)).astype(BF16)

    kv = project(_OFF_KC, _IN_PAD - _OFF_KC)
    kv_cols = lambda off, width: kv[:, off - _OFF_KC:off - _OFF_KC + width]
    for gg in range(N_KV):
        kc_ref[gg] = kv_cols(_OFF_KC + gg * HEAD_DIM, HEAD_DIM)
        vc_ref[gg] = kv_cols(_OFF_VC + gg * HEAD_DIM, HEAD_DIM)
    ks_ref[...] = _head_norm_pair(kv_cols(_OFF_KS, KV_WIDTH), ksg_ref[...]).astype(BF16)
    kw_ref[...] = _head_norm_pair(kv_cols(_OFF_KW, KV_WIDTH), kwg_ref[...]).astype(BF16)
    def value_tiles(ref, off, chunk):
        ones_row = jnp.where(lax.broadcasted_iota(jnp.int32, (V_ROWS - HEAD_DIM, chunk), 0) == 0,
                             1.0, 0.0)
        for k in range(tm // chunk):
            vt = kv_cols(off, KV_WIDTH)[k * chunk:(k + 1) * chunk].T
            for gg in range(N_KV):
                ref[gg, k] = jnp.concatenate(
                    [vt[gg * HEAD_DIM:(gg + 1) * HEAD_DIM], ones_row], axis=0).astype(BF16)

    value_tiles(vs_ref, _OFF_VS, SEL_CHUNK)
    value_tiles(vw_ref, _OFF_VW, TQ)
    gates_t = jax.nn.sigmoid(kv_cols(_OFF_G, N_KV * LANES)).T
    for gg in range(N_KV):
        gate_ref[gg] = gates_t[gg * LANES:gg * LANES + GATE_ROWS]

    @pl.when(i == 0)
    def _():
        ebuf_ref[0:POOL_HALO, :] = jnp.zeros((POOL_HALO, POOL_WIDTH), F32)

    u = project(_OFF_U, POOL_WIDTH)
    ebuf_ref[POOL_HALO:POOL_HALO + tm, :] = u
    t1 = i * tm + lax.broadcasted_iota(jnp.int32, (tm, POOL_GROUP), 0) + 1
    for gi, w in enumerate(POOL_WINDOWS):
        c0 = gi * POOL_GROUP
        win = u[:, c0:c0 + POOL_GROUP]
        for k in range(1, w):
            win = win + ebuf_ref[POOL_HALO - k:POOL_HALO - k + tm, c0:c0 + POOL_GROUP]
        cnt = jnp.minimum(t1, w).astype(F32)
        pooled = win / cnt - u[:, c0:c0 + POOL_GROUP]
        y = _dot(pooled.astype(BF16), wpool_ref[gi])
        pool_ref[:, c0:c0 + POOL_GROUP] = (y * pscale_ref[:, c0:c0 + POOL_GROUP]).astype(BF16)
    ebuf_ref[0:POOL_HALO, :] = ebuf_ref[tm:tm + POOL_HALO, :]


def _in_proj(x, g1, sc1, sh1, w_in_p, w_pool, pool_scale, qg, ksg, kwg, tm):
    s = x.shape[0]
    row = lambda w: pl.BlockSpec((tm, w), lambda i: (i, 0))
    vec = lambda w: pl.BlockSpec((1, w), lambda i: (0, 0))
    per_head = pl.BlockSpec((N_KV, tm, HEAD_DIM), lambda i: (0, i, 0))
    chunks_t = lambda ch: pl.BlockSpec((N_KV, tm // ch, V_ROWS, ch), lambda i: (0, i, 0, 0))
    out_shape = [
        jax.ShapeDtypeStruct((s, POOL_WIDTH), BF16),
        jax.ShapeDtypeStruct((s, _Q_WIDTH), BF16),
        jax.ShapeDtypeStruct((N_KV, s, HEAD_DIM), F32),
        jax.ShapeDtypeStruct((N_KV, s, HEAD_DIM), F32),
        jax.ShapeDtypeStruct((s, KV_WIDTH), BF16),
        jax.ShapeDtypeStruct((N_KV, s // SEL_CHUNK, V_ROWS, SEL_CHUNK), BF16),
        jax.ShapeDtypeStruct((s, KV_WIDTH), BF16),
        jax.ShapeDtypeStruct((N_KV, s // TQ, V_ROWS, TQ), BF16),
        jax.ShapeDtypeStruct((N_KV, GATE_ROWS, s), F32),
    ]
    return pl.pallas_call(
        functools.partial(_in_proj_kernel, tm=tm),
        grid=(s // tm,),
        in_specs=[row(D_MODEL), vec(D_MODEL), vec(D_MODEL), vec(D_MODEL),
                  _const_spec(w_in_p.shape), _const_spec(w_pool.shape), vec(POOL_WIDTH),
                  vec(LANES), vec(LANES), vec(LANES)],
        out_specs=[row(POOL_WIDTH), row(_Q_WIDTH), per_head, per_head, row(KV_WIDTH),
                   chunks_t(SEL_CHUNK), row(KV_WIDTH), chunks_t(TQ),
                   pl.BlockSpec((N_KV, GATE_ROWS, tm), lambda i: (0, 0, i))],
        out_shape=out_shape,
        scratch_shapes=[pltpu.VMEM((tm + POOL_HALO, POOL_WIDTH), F32)],
        compiler_params=pltpu.CompilerParams(dimension_semantics=("arbitrary",),
                                             vmem_limit_bytes=VMEM_LIMIT_BYTES),
        name="in_proj",
    )(x, g1, sc1, sh1, w_in_p, w_pool, pool_scale, qg, ksg, kwg)


def _compress_kernel(c_ref, pos_ref, w1_ref, b1_ref, w2_ref, b2_ref, gain_ref, o_ref, *,
                     normalize):
    half = CMP_STRIDE * HEAD_DIM
    n_rows = c_ref.shape[1]
    c = c_ref[0]
    first = _dot((c + pos_ref[:, 0:half]).astype(BF16), w1_ref[0:half, :])
    second = _dot((c + pos_ref[:, half:2 * half]).astype(BF16), w1_ref[half:2 * half, :])
    hid = jax.nn.gelu(first + pltpu.roll(second, n_rows - 1, axis=0) + b1_ref[...])
    y = _dot(hid.astype(BF16), w2_ref[...]) + b2_ref[...]
    if normalize:
        ms = jnp.mean(y * y, axis=-1, keepdims=True)
        y = y * lax.rsqrt(ms + EPS) * gain_ref[...]
    o_ref[0] = y.astype(BF16)


def _compress(chunks, pos, w1, b1, w2, b2, gain, normalize):
    _, n_chunks, width = chunks.shape
    vec = lambda w: pl.BlockSpec((1, w), lambda g: (0, 0))
    return pl.pallas_call(
        functools.partial(_compress_kernel, normalize=normalize),
        grid=(N_KV,),
        in_specs=[pl.BlockSpec((1, n_chunks, width), lambda g: (g, 0, 0)),
                  vec(2 * width), _const_spec(w1.shape), vec(CMP_HIDDEN), _const_spec(w2.shape),
                  vec(HEAD_DIM), vec(HEAD_DIM)],
        out_specs=pl.BlockSpec((1, n_chunks, HEAD_DIM), lambda g: (g, 0, 0)),
        out_shape=jax.ShapeDtypeStruct((N_KV, n_chunks, HEAD_DIM), BF16),
        compiler_params=pltpu.CompilerParams(vmem_limit_bytes=VMEM_LIMIT_BYTES),
        name="compress",
    )(chunks, pos, w1, b1, w2, b2, gain)


def _attn_kernel(q_ref, gate_ref, rowt_ref, kc_ref, vct_ref, ks_ref, vst_ref, kw_ref, vwt_ref,
                 o_ref, psum_ref, madd_ref, bias_ref, wbias_ref, ctab_ref, m_ref, acc_ref, ocmp_ref,
                 owin_ref, count_ref, list_ref, *, nb):
    g = pl.program_id(0)
    i = pl.program_id(1)
    q0 = i * TQ
    nc = CMP_PER_SEL * nb
    n_chunks = nb // BLOCKS_PER_CHUNK
    gslope = jnp.where(g == 0, LOG2E, LOG2E * 2.0 ** -GQA_GROUP).astype(F32)
    slopes = [gslope * (2.0 ** -(r + 1)) for r in range(GQA_GROUP)]
    heads = [slice(r * TQ, (r + 1) * TQ) for r in range(GQA_GROUP)]

    lane = lax.broadcasted_iota(jnp.int32, (TQ, LANES), 1)
    own_half = (lane >= HEAD_DIM) == (g == 1)
    q_rows = []
    for r in range(GQA_GROUP):
        pair = q_ref[:, (r // 2) * LANES:(r // 2 + 1) * LANES]
        swapped = jnp.concatenate([pair[:, HEAD_DIM:], pair[:, :HEAD_DIM]], axis=1)
        q_rows.append(jnp.where(own_half, jnp.where(g == r % 2, pair, swapped), 0.0))
    q = jnp.concatenate(q_rows, axis=0)
    t_lane = q0 + lax.broadcasted_iota(jnp.int32, (1, TQ), 1)

    w0 = pl.multiple_of(jnp.maximum(q0 - WINDOW, 0), TQ)
    s_w = _dot_nt(kw_ref[pl.ds(w0, WIN_SPAN), :], q)

    @pl.when(i <= WINDOW // TQ)
    def _():
        row_w = rowt_ref[0:WIN_SPAN, :]
        lane_w = lax.broadcasted_iota(jnp.int32, (WIN_SPAN, TQ), 1).astype(F32)
        dq = (q0 - w0).astype(F32)
        d_w = lane_w - row_w + dq
        ok_w = jnp.abs(d_w - (WINDOW - 1) * 0.5) < WINDOW * 0.5
        rel_w = row_w - dq
        for r in range(GQA_GROUP):
            wbias_ref[r] = jnp.where(ok_w, slopes[r] * rel_w, NEG_INF)

    wb = w0 // TQ
    for r in range(GQA_GROUP):
        s = s_w[:, heads[r]] + wbias_ref[r]
        m = jnp.max(s, axis=0, keepdims=True)
        p = jnp.exp2((s - m).astype(BF16))
        o_win = _dot(vwt_ref[0, wb], p[0:TQ])
        for b in range(1, WIN_SPAN // TQ):
            o_win = o_win + _dot(vwt_ref[0, wb + b], p[b * TQ:(b + 1) * TQ])
        owin_ref[:, heads[r]] = o_win

    rows_per_bucket = nc // CMP_BUCKETS
    tab_rows = ctab_ref.shape[1]

    @pl.when(i == 0)
    def _():
        rel = CMP_STRIDE * (rowt_ref[0:tab_rows, :] - nc) + (CMP_LEN - 1)
        ok = rel <= lax.broadcasted_iota(jnp.int32, (1, TQ), 1).astype(F32)
        for r in range(GQA_GROUP):
            ctab_ref[r] = jnp.where(ok, slopes[r] * rel, NEG_INF)

    tab0 = pl.multiple_of(nc - q0 // CMP_STRIDE, CMP_STRIDE)

    def compress_and_select(n_rows):
        n_blk = n_rows // CMP_PER_SEL
        s_c = _dot_nt(kc_ref[0:n_rows, :], q)
        p_sum = jnp.zeros((n_rows, TQ), F32)
        p_cols = []
        for r in range(GQA_GROUP):
            s = s_c[:, heads[r]] + ctab_ref[r, pl.ds(tab0, n_rows), :]
            m = jnp.max(s, axis=0, keepdims=True)
            e = jnp.exp2(s - m)
            l = jnp.sum(e, axis=0, keepdims=True)
            p = e * jnp.where(m > 0.5 * NEG_INF, 1.0 / l, 0.0)
            p_sum = p_sum + p
            p_cols.append(p.astype(BF16))
        ocmp_ref[...] = _dot(vct_ref[0, :, 0:n_rows], jnp.concatenate(p_cols, axis=1))

        for h in range(TQ // LANES):
            psum_ref[h, 0:SUBLANES, :] = jnp.zeros((SUBLANES, LANES), F32)
            psum_ref[h, SUBLANES:SUBLANES + n_rows, :] = p_sum[:, h * LANES:(h + 1) * LANES]

        def every4(start):
            parts = [psum_ref[h, pl.ds(SUBLANES + start, n_blk, stride=CMP_PER_SEL), :]
                     for h in range(TQ // LANES)]
            return parts[0] if len(parts) == 1 else jnp.concatenate(parts, axis=1)

        imp = every4(0) + every4(1) + every4(2) + 0.5 * every4(3) + 0.5 * every4(-1)
        blk = lax.broadcasted_iota(jnp.int32, (n_blk, TQ), 0)
        cur = lax.shift_right_logical(t_lane, SEL_BLOCK.bit_length() - 1)
        causal = blk <= cur
        forced = jnp.where(blk == 0, 1.0, 0.0) + jnp.where(blk == cur, 1.0, 0.0) \
            + jnp.where(blk == cur - 1, 1.0, 0.0)
        val = jnp.where(causal, jnp.where(forced > 0.0, -2.0, imp), -1.0)

        def pick(_, v):
            blk_f = rowt_ref[0:n_blk, :]
            mx = jnp.max(v, axis=0, keepdims=True)
            idx = jnp.min(jnp.where(v == mx, blk_f, float(n_blk)), axis=0, keepdims=True)
            return jnp.where(blk_f == idx, -2.0, v)

        picked = lax.fori_loop(0, min(N_SEL, n_blk) - N_FORCED, pick, val, unroll=True)
        madd_ref[0:n_blk, :] = jnp.where(causal, jnp.where(picked < -1.5, 0.0, NEG_INF), NEG_INF)
        madd_ref[0:1, :] = jnp.where(i >= 1, NEG_INF, madd_ref[0:1, :])
        n = jnp.int32(0)
        for c in range(n_blk // BLOCKS_PER_CHUNK):
            rows = madd_ref[c * BLOCKS_PER_CHUNK:(c + 1) * BLOCKS_PER_CHUNK, :]
            active = (jnp.max(rows) > 0.5 * NEG_INF) & (c < c_diag)
            list_ref[n] = c
            n = n + active.astype(jnp.int32)
        count_ref[0] = n

    c_diag = q0 // SEL_CHUNK
    bucket = ((q0 + TQ) // CMP_STRIDE - 1) // rows_per_bucket
    for b in range(CMP_BUCKETS):
        pl.when(bucket == b)(functools.partial(compress_and_select, (b + 1) * rows_per_bucket))


    @pl.when(i == 0)
    def _():
        key_row = rowt_ref[0:SEL_CHUNK, :]
        for r in range(GQA_GROUP):
            bias_ref[r] = slopes[r] * key_row

    m_ref[...] = jnp.full(m_ref.shape, NEG_INF, F32)
    acc_ref[...] = jnp.zeros(acc_ref.shape, F32)

    def attend(chunks, final):
        sizes = ([SEL_BLOCK] if final else []) + [SEL_CHUNK] * len(chunks)
        keys = ([ks_ref[0:SEL_BLOCK, :]] if final else []) + [
            ks_ref[pl.ds(pl.multiple_of(c * SEL_CHUNK, SEL_CHUNK), SEL_CHUNK), :] for c in chunks]
        s_all = _dot_nt(jnp.concatenate(keys, axis=0) if len(keys) > 1 else keys[0], q)
        offsets = np.cumsum([0] + sizes)
        s = [s_all[offsets[k]:offsets[k + 1]] for k in range(len(sizes))]

        madds, values, rel0 = [], [], []
        if final:
            madds.append(jnp.where(i >= 1, 0.0, NEG_INF))
            values.append(vst_ref[0, 0][:, 0:SEL_BLOCK])
            rel0.append((-q0).astype(F32))
        for k, c in enumerate(chunks):
            k0 = c * SEL_CHUNK
            madd = jnp.concatenate(
                [jnp.broadcast_to(madd_ref[pl.ds(c * BLOCKS_PER_CHUNK + b, 1), :], (SEL_BLOCK, TQ))
                 for b in range(BLOCKS_PER_CHUNK)], axis=0)
            if final and k == len(chunks) - 1:
                pos = k0 + lax.broadcasted_iota(jnp.int32, (SEL_CHUNK, TQ), 0)
                madd = jnp.where(pos <= t_lane, madd, NEG_INF)
            madds.append(madd)
            values.append(vst_ref[0, c])
            rel0.append((k0 - q0).astype(F32))

        p_cols = [[] for _ in sizes]
        alphas = []
        for r in range(GQA_GROUP):
            shifts = [slopes[r] * x for x in rel0]
            us = [sk[:, heads[r]] + bias_ref[r, 0:n, :] + mk for sk, mk, n in zip(s, madds, sizes)]
            m_old = m_ref[:, heads[r]]
            m_new = m_old
            for u, sh in zip(us, shifts):
                m_new = jnp.maximum(m_new, jnp.max(u, axis=0, keepdims=True) + sh)
            alphas.append(jnp.exp2(m_old - m_new))
            for k, (u, sh) in enumerate(zip(us, shifts)):
                p_cols[k].append(jnp.exp2((u - (m_new - sh)).astype(BF16)))
            m_ref[:, heads[r]] = m_new
        pv = _dot(values[0], jnp.concatenate(p_cols[0], axis=1))
        for k in range(1, len(sizes)):
            pv = pv + _dot(values[k], jnp.concatenate(p_cols[k], axis=1))
        acc_ref[...] = jnp.concatenate(alphas, axis=1) * acc_ref[...] + pv

    n_earlier = count_ref[0]
    list_ref[n_earlier] = c_diag

    def active_group(first, size, final):
        attend([list_ref[first + k] for k in range(size)], final)

    def group_body(p, carry):
        active_group(p * SEL_GROUP, SEL_GROUP, False)
        return carry

    n_full = n_earlier // SEL_GROUP
    lax.fori_loop(0, n_full, group_body, 0)
    done = n_full * SEL_GROUP
    for size in range(1, SEL_GROUP + 1):
        pl.when(n_earlier + 1 - done == size)(functools.partial(active_group, done, size, True))

    head_out = []
    for r in range(GQA_GROUP):
        gc = gate_ref[0, 3 * r + 0:3 * r + 1, :]
        gs = gate_ref[0, 3 * r + 1:3 * r + 2, :]
        gw = gate_ref[0, 3 * r + 2:3 * r + 3, :]
        acc = acc_ref[:, heads[r]]
        win = owin_ref[:, heads[r]]
        out_t = (gc * ocmp_ref[:, heads[r]]
                 + (gs / acc[HEAD_DIM:HEAD_DIM + 1]) * acc
                 + (gw / win[HEAD_DIM:HEAD_DIM + 1]) * win)
        head_out.append(out_t.T[:, 0:HEAD_DIM])
    for pair in range(GQA_GROUP // 2):
        o_ref[:, pair * LANES:(pair + 1) * LANES] = jnp.concatenate(
            head_out[2 * pair:2 * pair + 2], axis=1).astype(BF16)


def _attention(q, gates_t, kcmp, vcmp_t, ksel, vsel_t, kwin, vwin_t):
    assert TQ == SEL_CHUNK
    s = q.shape[0]
    nb = s // SEL_BLOCK
    nc = CMP_PER_SEL * nb
    gw = GQA_GROUP * HEAD_DIM
    tab_rows = nc + TQ // CMP_STRIDE + nc // CMP_BUCKETS
    n_rows = max(tab_rows, WIN_SPAN)
    row_tile = jnp.asarray(np.broadcast_to(np.arange(n_rows)[:, None], (n_rows, TQ))
                           .astype(np.float32))
    per_group = lambda a: pl.BlockSpec((1,) + a.shape[1:], lambda g, i: (g,) + (0,) * (a.ndim - 1),
                                       pipeline_mode=pl.Buffered(1))
    return pl.pallas_call(
        functools.partial(_attn_kernel, nb=nb),
        grid=(N_KV, s // TQ),
        in_specs=[pl.BlockSpec((TQ, gw), lambda g, i: (i, g)),
                  pl.BlockSpec((1, GATE_ROWS, TQ), lambda g, i: (g, 0, i)),
                  _const_spec(row_tile.shape),
                  _const_spec(kcmp.shape), per_group(vcmp_t),
                  _const_spec(ksel.shape), per_group(vsel_t),
                  _const_spec(kwin.shape), per_group(vwin_t)],
        out_specs=pl.BlockSpec((TQ, gw), lambda g, i: (i, g)),
        out_shape=jax.ShapeDtypeStruct((s, N_HEADS * HEAD_DIM), BF16),
        scratch_shapes=[pltpu.VMEM((TQ // LANES, SUBLANES + nc, LANES), F32),
                        pltpu.VMEM((nb, TQ), F32),
                        pltpu.VMEM((GQA_GROUP, SEL_CHUNK, TQ), F32),
                        pltpu.VMEM((GQA_GROUP, WIN_SPAN, TQ), F32),
                        pltpu.VMEM((GQA_GROUP, tab_rows, TQ), F32),
                        pltpu.VMEM((1, GQA_GROUP * TQ), F32),
                        pltpu.VMEM((V_ROWS, GQA_GROUP * TQ), F32),
                        pltpu.VMEM((V_ROWS, GQA_GROUP * TQ), F32),
                        pltpu.VMEM((V_ROWS, GQA_GROUP * TQ), F32),
                        pltpu.SMEM((1,), jnp.int32),
                        pltpu.SMEM((nb // BLOCKS_PER_CHUNK,), jnp.int32)],
        compiler_params=pltpu.CompilerParams(dimension_semantics=("arbitrary", "arbitrary"),
                                             vmem_limit_bytes=VMEM_LIMIT_BYTES),
        name="attn",
    )(q, gates_t, row_tile, kcmp, vcmp_t, ksel, vsel_t, kwin, vwin_t)


def _out_mlp_kernel(pool_ref, attn_ref, wout_ref, x_ref, ga1_ref, g_ref, sc_ref, sh_ref, w1_ref,
                    w2_ref, ga2_ref, o_ref, *, ff_chunk):
    mix = (_dot(pool_ref[...], wout_ref[0:POOL_WIDTH, :])
           + _dot(attn_ref[...], wout_ref[POOL_WIDTH:, :]))
    x1 = x_ref[...] + ga1_ref[...] * mix
    h = _rms_modulate(x1, g_ref[...], sc_ref[...], sh_ref[...]).astype(BF16)
    acc = jnp.zeros(x1.shape, F32)
    for c in range(D_FF // ff_chunk):
        a = _dot(h, w1_ref[:, c * ff_chunk:(c + 1) * ff_chunk])
        a = jnp.square(jnp.maximum(a, 0.0)).astype(BF16)
        acc = acc + _dot(a, w2_ref[c * ff_chunk:(c + 1) * ff_chunk, :])
    o_ref[...] = x1 + ga2_ref[...] * acc


def _out_mlp(pool_out, attn_out, w_out, x, ga1, g2, sc2, sh2, w1, w2, ga2, tm):
    s = x.shape[0]
    row = lambda w: pl.BlockSpec((tm, w), lambda i: (i, 0))
    vec = lambda w: pl.BlockSpec((1, w), lambda i: (0, 0))
    return pl.pallas_call(
        functools.partial(_out_mlp_kernel, ff_chunk=FF_CHUNK),
        grid=(s // tm,),
        in_specs=[row(POOL_WIDTH), row(N_HEADS * HEAD_DIM), _const_spec(w_out.shape),
                  row(D_MODEL), vec(D_MODEL), vec(D_MODEL), vec(D_MODEL), vec(D_MODEL),
                  _const_spec(w1.shape), _const_spec(w2.shape), vec(D_MODEL)],
        out_specs=row(D_MODEL),
        out_shape=jax.ShapeDtypeStruct((s, D_MODEL), F32),
        compiler_params=pltpu.CompilerParams(vmem_limit_bytes=VMEM_LIMIT_BYTES),
        name="out_mlp",
    )(pool_out, attn_out, w_out, x, ga1, g2, sc2, sh2, w1, w2, ga2)


def _pad_in_proj_weight(w_in):
    src_g = _OFF_G
    per_group = GQA_GROUP * N_BRANCH
    cols = [w_in[:, :src_g]]
    for gg in range(N_KV):
        cols += [w_in[:, src_g + gg * per_group:src_g + (gg + 1) * per_group],
                 jnp.zeros((D_MODEL, LANES - per_group), w_in.dtype)]
    return jnp.concatenate(cols, axis=1).astype(BF16)


def _group_lanes(a):
    return a.transpose(1, 0, 2).reshape(a.shape[1], KV_WIDTH)


def kernel(x, c, w_ada, b_ada, norm1_g, norm2_g, w_in, w_pool, pool_scale, q_gain, kc_gain,
           ks_gain, kw_gain, cmp_pos_k, cmp_w1_k, cmp_b1_k, cmp_w2_k, cmp_b2_k, cmp_pos_v,
           cmp_w1_v, cmp_b1_v, cmp_w2_v, cmp_b2_v, w_out, w_ff1, w_ff2):
    batch, s, _ = x.shape
    assert batch == 1 and w_ada.shape[0] == 1
    assert s % SEL_CHUNK == 0 and s >= WIN_SPAN and (s // SEL_BLOCK) & (s // SEL_BLOCK - 1) == 0
    tm = min(s, OUT_MLP_ROWS)
    x2 = x[0]

    mod = _ada(jnp.broadcast_to(c, (SUBLANES, D_MODEL)), w_ada[0], b_ada)[0:1]
    sh1, sc1, ga1, sh2, sc2, ga2 = [mod[:, k * D_MODEL:(k + 1) * D_MODEL] for k in range(6)]

    pair = lambda gain: jnp.tile(gain, (1, 2))
    (pool_out, q, kc, vc, ksel, vsel_t, kwin, vwin_t, gates_t) = _in_proj(
        x2, norm1_g, sc1, sh1, _pad_in_proj_weight(w_in[0]), w_pool[0].astype(BF16), pool_scale,
        pair(q_gain), pair(ks_gain), pair(kw_gain), min(s, IN_PROJ_ROWS))

    chunks = lambda a: a.reshape(N_KV, s // CMP_STRIDE, CMP_STRIDE * HEAD_DIM)
    kcmp = _compress(chunks(kc), cmp_pos_k.reshape(1, -1), cmp_w1_k[0].astype(BF16), cmp_b1_k,
                     cmp_w2_k[0].astype(BF16), cmp_b2_k, kc_gain, True)
    vcmp = _compress(chunks(vc), cmp_pos_v.reshape(1, -1), cmp_w1_v[0].astype(BF16), cmp_b1_v,
                     cmp_w2_v[0].astype(BF16), cmp_b2_v, kc_gain, False)

    vcmp_t = jnp.pad(vcmp.transpose(0, 2, 1), ((0, 0), (0, V_ROWS - HEAD_DIM), (0, 0)))
    attn_out = _attention(q, gates_t, _group_lanes(kcmp), vcmp_t, ksel, vsel_t, kwin, vwin_t)

    out = _out_mlp(pool_out, attn_out, w_out[0].astype(BF16), x2, ga1, norm2_g, sc2, sh2,
                   w_ff1[0].astype(BF16), w_ff2[0].astype(BF16), ga2, tm)
    return out[None]
```

```python
import functools

import jax
import jax.numpy as jnp
import numpy as np
from jax import lax
from jax.experimental import pallas as pl
from jax.experimental.pallas import tpu as pltpu

F32 = jnp.float32
BF16 = jnp.bfloat16

LANES = 128
SUBLANES = 8
VMEM_LIMIT_BYTES = 56 * 1024 * 1024

D_MODEL = 1024
POOL_WIDTH = 512
POOL_WINDOWS = (2, 4, 8, 16)
POOL_GROUP = POOL_WIDTH // len(POOL_WINDOWS)
POOL_HALO = 16
HEAD_DIM = 64
N_HEADS = 8
N_KV = 2
GQA_GROUP = N_HEADS // N_KV
KV_WIDTH = N_KV * HEAD_DIM
N_BRANCH = 3
CMP_LEN = 32
CMP_STRIDE = 16
CMP_HIDDEN = 4 * HEAD_DIM
SEL_BLOCK = 64
N_SEL = 16
WINDOW = 512
D_FF = 4 * D_MODEL
NEG_INF = -1e30
N_FORCED = 3
EPS = 1e-6
LOG2E = 1.4426950408889634

TQ = 256
SEL_CHUNK = 256
BLOCKS_PER_CHUNK = SEL_CHUNK // SEL_BLOCK
SEL_GROUP = 6
CMP_BUCKETS = 8
WIN_SPAN = WINDOW + TQ
CMP_PER_SEL = SEL_BLOCK // CMP_STRIDE
GATE_ROWS = 16
V_ROWS = HEAD_DIM + 16
ADA_COLS = 1024
IN_PROJ_ROWS = 1024
OUT_MLP_ROWS = 512
FF_CHUNK = 1024

_Q_WIDTH = N_HEADS * HEAD_DIM
_OFF_U = 0
_OFF_Q = POOL_WIDTH
_OFF_KC = _OFF_Q + _Q_WIDTH
_OFF_VC = _OFF_KC + KV_WIDTH
_OFF_KS = _OFF_VC + KV_WIDTH
_OFF_VS = _OFF_KS + KV_WIDTH
_OFF_KW = _OFF_VS + KV_WIDTH
_OFF_VW = _OFF_KW + KV_WIDTH
_OFF_G = _OFF_VW + KV_WIDTH
_IN_PAD = _OFF_G + N_KV * LANES


def _dot(a, b):
    return jnp.dot(a, b, preferred_element_type=F32)


def _dot_nt(a, b):
    return lax.dot_general(a, b, (((1,), (1,)), ((), ())), preferred_element_type=F32)


def _const_spec(shape):
    nd = len(shape)
    return pl.BlockSpec(shape, lambda *_: (0,) * nd, pipeline_mode=pl.Buffered(1))


def _ada_kernel(c_ref, w_ref, b_ref, o_ref):
    o_ref[...] = jnp.dot(c_ref[...], w_ref[...], preferred_element_type=F32,
                         precision=lax.Precision.HIGHEST) + b_ref[...]


def _ada(c8, w, b):
    n = w.shape[1]
    bn = ADA_COLS
    return pl.pallas_call(
        _ada_kernel,
        grid=(n // bn,),
        in_specs=[pl.BlockSpec((SUBLANES, D_MODEL), lambda j: (0, 0)),
                  pl.BlockSpec((D_MODEL, bn), lambda j: (0, j)),
                  pl.BlockSpec((1, bn), lambda j: (0, j))],
        out_specs=pl.BlockSpec((SUBLANES, bn), lambda j: (0, j)),
        out_shape=jax.ShapeDtypeStruct((SUBLANES, n), F32),
        compiler_params=pltpu.CompilerParams(vmem_limit_bytes=VMEM_LIMIT_BYTES),
        name="ada",
    )(c8, w, b)


def _rms_modulate(x, g, sc, sh):
    ms = jnp.mean(x * x, axis=-1, keepdims=True)
    return (x * lax.rsqrt(ms + EPS)) * (g * (1.0 + sc)) + sh


def _head_norm_pair(x, gain2):
    lane = lax.broadcasted_iota(jnp.int32, x.shape, 1)
    lo = lane < HEAD_DIM
    sq = x * x
    s_lo = jnp.sum(jnp.where(lo, sq, 0.0), axis=-1, keepdims=True)
    s_hi = jnp.sum(jnp.where(lo, 0.0, sq), axis=-1, keepdims=True)
    ms = jnp.where(lo, s_lo, s_hi) * (1.0 / HEAD_DIM)
    return x * lax.rsqrt(ms + EPS) * gain2


def _in_proj_kernel(x_ref, g_ref, sc_ref, sh_ref, w_ref, wpool_ref, pscale_ref, qg_ref, ksg_ref,
                    kwg_ref, pool_ref, q_ref, kc_ref, vc_ref, ks_ref, vs_ref, kw_ref, vw_ref,
                    gate_ref, ebuf_ref, *, tm):
    i = pl.program_id(0)
    h = _rms_modulate(x_ref[...], g_ref[...], sc_ref[...], sh_ref[...]).astype(BF16)

    def project(off, width):
        return _dot(h, w_ref[:, off:off + width])

    q_all = project(_OFF_Q, _Q_WIDTH)
    for pair in range(N_HEADS // 2):
        qn = _head_norm_pair(q_all[:, pair * LANES:(pair + 1) * LANES], qg_ref[...])
        q_ref[:, pair * LANES:(pair + 1) * LANES] = (qn * (HEAD_DIM ** -0.5 * LOG2E)).astype(BF16)

    kv = project(_OFF_KC, _IN_PAD - _OFF_KC)
    kv_cols = lambda off, width: kv[:, off - _OFF_KC:off - _OFF_KC + width]
    chunked = (tm // CMP_STRIDE, CMP_STRIDE, KV_WIDTH)
    kc_ref[...] = kv_cols(_OFF_KC, KV_WIDTH).reshape(chunked)
    vc_ref[...] = kv_cols(_OFF_VC, KV_WIDTH).reshape(chunked)
    ks_ref[...] = _head_norm_pair(kv_cols(_OFF_KS, KV_WIDTH), ksg_ref[...]).astype(BF16)
    kw_ref[...] = _head_norm_pair(kv_cols(_OFF_KW, KV_WIDTH), kwg_ref[...]).astype(BF16)
    def value_tiles(ref, off, chunk):
        ones_row = jnp.where(lax.broadcasted_iota(jnp.int32, (V_ROWS - HEAD_DIM, chunk), 0) == 0,
                             1.0, 0.0)
        for k in range(tm // chunk):
            vt = kv_cols(off, KV_WIDTH)[k * chunk:(k + 1) * chunk].T
            for gg in range(N_KV):
                ref[gg, k] = jnp.concatenate(
                    [vt[gg * HEAD_DIM:(gg + 1) * HEAD_DIM], ones_row], axis=0).astype(BF16)

    value_tiles(vs_ref, _OFF_VS, SEL_CHUNK)
    value_tiles(vw_ref, _OFF_VW, TQ)
    gates_t = jax.nn.sigmoid(kv_cols(_OFF_G, N_KV * LANES)).T
    for gg in range(N_KV):
        gate_ref[gg] = gates_t[gg * LANES:gg * LANES + GATE_ROWS]

    @pl.when(i == 0)
    def _():
        ebuf_ref[0:POOL_HALO, :] = jnp.zeros((POOL_HALO, POOL_WIDTH), F32)

    u = project(_OFF_U, POOL_WIDTH)
    ebuf_ref[POOL_HALO:POOL_HALO + tm, :] = u
    t1 = i * tm + lax.broadcasted_iota(jnp.int32, (tm, POOL_GROUP), 0) + 1
    for gi, w in enumerate(POOL_WINDOWS):
        c0 = gi * POOL_GROUP
        win = u[:, c0:c0 + POOL_GROUP]
        for k in range(1, w):
            win = win + ebuf_ref[POOL_HALO - k:POOL_HALO - k + tm, c0:c0 + POOL_GROUP]
        cnt = jnp.minimum(t1, w).astype(F32)
        pooled = win / cnt - u[:, c0:c0 + POOL_GROUP]
        y = _dot(pooled.astype(BF16), wpool_ref[gi])
        pool_ref[:, c0:c0 + POOL_GROUP] = (y * pscale_ref[:, c0:c0 + POOL_GROUP]).astype(BF16)
    ebuf_ref[0:POOL_HALO, :] = ebuf_ref[tm:tm + POOL_HALO, :]


def _in_proj(x, g1, sc1, sh1, w_in_p, w_pool, pool_scale, qg, ksg, kwg, tm):
    s = x.shape[0]
    row = lambda w: pl.BlockSpec((tm, w), lambda i: (i, 0))
    vec = lambda w: pl.BlockSpec((1, w), lambda i: (0, 0))
    per_head = pl.BlockSpec((tm // CMP_STRIDE, CMP_STRIDE, KV_WIDTH), lambda i: (i, 0, 0))
    chunks_t = lambda ch: pl.BlockSpec((N_KV, tm // ch, V_ROWS, ch), lambda i: (0, i, 0, 0))
    out_shape = [
        jax.ShapeDtypeStruct((s, POOL_WIDTH), BF16),
        jax.ShapeDtypeStruct((s, _Q_WIDTH), BF16),
        jax.ShapeDtypeStruct((s // CMP_STRIDE, CMP_STRIDE, KV_WIDTH), F32),
        jax.ShapeDtypeStruct((s // CMP_STRIDE, CMP_STRIDE, KV_WIDTH), F32),
        jax.ShapeDtypeStruct((s, KV_WIDTH), BF16),
        jax.ShapeDtypeStruct((N_KV, s // SEL_CHUNK, V_ROWS, SEL_CHUNK), BF16),
        jax.ShapeDtypeStruct((s, KV_WIDTH), BF16),
        jax.ShapeDtypeStruct((N_KV, s // TQ, V_ROWS, TQ), BF16),
        jax.ShapeDtypeStruct((N_KV, GATE_ROWS, s), F32),
    ]
    return pl.pallas_call(
        functools.partial(_in_proj_kernel, tm=tm),
        grid=(s // tm,),
        in_specs=[row(D_MODEL), vec(D_MODEL), vec(D_MODEL), vec(D_MODEL),
                  _const_spec(w_in_p.shape), _const_spec(w_pool.shape), vec(POOL_WIDTH),
                  vec(LANES), vec(LANES), vec(LANES)],
        out_specs=[row(POOL_WIDTH), row(_Q_WIDTH), per_head, per_head, row(KV_WIDTH),
                   chunks_t(SEL_CHUNK), row(KV_WIDTH), chunks_t(TQ),
                   pl.BlockSpec((N_KV, GATE_ROWS, tm), lambda i: (0, 0, i))],
        out_shape=out_shape,
        scratch_shapes=[pltpu.VMEM((tm + POOL_HALO, POOL_WIDTH), F32)],
        compiler_params=pltpu.CompilerParams(dimension_semantics=("arbitrary",),
                                             vmem_limit_bytes=VMEM_LIMIT_BYTES),
        name="in_proj",
    )(x, g1, sc1, sh1, w_in_p, w_pool, pool_scale, qg, ksg, kwg)


def _compress_kernel(c_ref, pos_ref, w1_ref, b1_ref, w2_ref, b2_ref, gain_ref, o_ref, *,
                     normalize):
    n_rows = c_ref.shape[0]
    first = second = None
    for l in range(CMP_STRIDE):
        tok = c_ref[:, l, :]
        a = _dot((tok + pos_ref[l:l + 1, :]).astype(BF16), w1_ref[l])
        b = _dot((tok + pos_ref[CMP_STRIDE + l:CMP_STRIDE + l + 1, :]).astype(BF16),
                 w1_ref[CMP_STRIDE + l])
        first = a if first is None else first + a
        second = b if second is None else second + b
    hid = jax.nn.gelu(first + pltpu.roll(second, n_rows - 1, axis=0) + b1_ref[...])
    y = _dot(hid.astype(BF16), w2_ref[...]) + b2_ref[...]
    if normalize:
        y = _head_norm_pair(y, gain_ref[...])
    o_ref[...] = y.astype(BF16)


def _compress(c3, pos, w1, b1, w2, b2, gain, normalize):
    n_chunks = c3.shape[0]
    blockdiag = lambda w: jnp.concatenate(
        [jnp.concatenate([w, jnp.zeros_like(w)], axis=-1),
         jnp.concatenate([jnp.zeros_like(w), w], axis=-1)], axis=-2)
    pair = lambda v: jnp.tile(v, (1, N_KV))
    w1_l = blockdiag(w1.reshape(CMP_LEN, HEAD_DIM, CMP_HIDDEN))
    args = (c3, pair(pos.reshape(CMP_LEN, HEAD_DIM)), w1_l, pair(b1), blockdiag(w2), pair(b2),
            pair(gain))
    return pl.pallas_call(
        functools.partial(_compress_kernel, normalize=normalize),
        grid=(1,),
        in_specs=[_const_spec(a.shape) for a in args],
        out_specs=pl.BlockSpec((n_chunks, KV_WIDTH), lambda i: (0, 0)),
        out_shape=jax.ShapeDtypeStruct((n_chunks, KV_WIDTH), BF16),
        compiler_params=pltpu.CompilerParams(vmem_limit_bytes=VMEM_LIMIT_BYTES),
        name="compress",
    )(*args)


def _attn_kernel(q_ref, gate_ref, rowt_ref, kc_ref, vct_ref, ks_ref, vst_ref, kw_ref, vwt_ref,
                 o_ref, psum_ref, madd_ref, bias_ref, wbias_ref, ctab_ref, m_ref, acc_ref, ocmp_ref,
                 owin_ref, count_ref, list_ref, *, nb):
    g = pl.program_id(0)
    i = pl.program_id(1)
    q0 = i * TQ
    nc = CMP_PER_SEL * nb
    n_chunks = nb // BLOCKS_PER_CHUNK
    gslope = jnp.where(g == 0, LOG2E, LOG2E * 2.0 ** -GQA_GROUP).astype(F32)
    slopes = [gslope * (2.0 ** -(r + 1)) for r in range(GQA_GROUP)]
    heads = [slice(r * TQ, (r + 1) * TQ) for r in range(GQA_GROUP)]

    lane = lax.broadcasted_iota(jnp.int32, (TQ, LANES), 1)
    own_half = (lane >= HEAD_DIM) == (g == 1)
    q_rows = []
    for r in range(GQA_GROUP):
        pair = q_ref[:, (r // 2) * LANES:(r // 2 + 1) * LANES]
        swapped = jnp.concatenate([pair[:, HEAD_DIM:], pair[:, :HEAD_DIM]], axis=1)
        q_rows.append(jnp.where(own_half, jnp.where(g == r % 2, pair, swapped), 0.0))
    q = jnp.concatenate(q_rows, axis=0)
    t_lane = q0 + lax.broadcasted_iota(jnp.int32, (1, TQ), 1)

    w0 = pl.multiple_of(jnp.maximum(q0 - WINDOW, 0), TQ)
    s_w = _dot_nt(kw_ref[pl.ds(w0, WIN_SPAN), :], q)

    @pl.when(i <= WINDOW // TQ)
    def _():
        row_w = rowt_ref[0:WIN_SPAN, :]
        lane_w = lax.broadcasted_iota(jnp.int32, (WIN_SPAN, TQ), 1).astype(F32)
        dq = (q0 - w0).astype(F32)
        d_w = lane_w - row_w + dq
        ok_w = jnp.abs(d_w - (WINDOW - 1) * 0.5) < WINDOW * 0.5
        rel_w = row_w - dq
        for r in range(GQA_GROUP):
            wbias_ref[r] = jnp.where(ok_w, slopes[r] * rel_w, NEG_INF)

    wb = w0 // TQ
    for r in range(GQA_GROUP):
        s = s_w[:, heads[r]] + wbias_ref[r]
        m = jnp.max(s, axis=0, keepdims=True)
        p = jnp.exp2((s - m).astype(BF16))
        o_win = _dot(vwt_ref[0, wb], p[0:TQ])
        for b in range(1, WIN_SPAN // TQ):
            o_win = o_win + _dot(vwt_ref[0, wb + b], p[b * TQ:(b + 1) * TQ])
        owin_ref[:, heads[r]] = o_win

    rows_per_bucket = nc // CMP_BUCKETS
    tab_rows = ctab_ref.shape[1]

    @pl.when(i == 0)
    def _():
        rel = CMP_STRIDE * (rowt_ref[0:tab_rows, :] - nc) + (CMP_LEN - 1)
        ok = rel <= lax.broadcasted_iota(jnp.int32, (1, TQ), 1).astype(F32)
        for r in range(GQA_GROUP):
            ctab_ref[r] = jnp.where(ok, slopes[r] * rel, NEG_INF)

    tab0 = pl.multiple_of(nc - q0 // CMP_STRIDE, CMP_STRIDE)

    def compress_and_select(n_rows):
        n_blk = n_rows // CMP_PER_SEL
        s_c = _dot_nt(kc_ref[0:n_rows, :], q)
        p_sum = jnp.zeros((n_rows, TQ), F32)
        p_cols = []
        for r in range(GQA_GROUP):
            s = s_c[:, heads[r]] + ctab_ref[r, pl.ds(tab0, n_rows), :]
            m = jnp.max(s, axis=0, keepdims=True)
            e = jnp.exp2(s - m)
            l = jnp.sum(e, axis=0, keepdims=True)
            p = e * jnp.where(m > 0.5 * NEG_INF, 1.0 / l, 0.0)
            p_sum = p_sum + p
            p_cols.append(p.astype(BF16))
        ocmp_ref[...] = _dot(vct_ref[0, :, 0:n_rows], jnp.concatenate(p_cols, axis=1))

        for h in range(TQ // LANES):
            psum_ref[h, 0:SUBLANES, :] = jnp.zeros((SUBLANES, LANES), F32)
            psum_ref[h, SUBLANES:SUBLANES + n_rows, :] = p_sum[:, h * LANES:(h + 1) * LANES]

        def every4(start):
            parts = [psum_ref[h, pl.ds(SUBLANES + start, n_blk, stride=CMP_PER_SEL), :]
                     for h in range(TQ // LANES)]
            return parts[0] if len(parts) == 1 else jnp.concatenate(parts, axis=1)

        imp = every4(0) + every4(1) + every4(2) + 0.5 * every4(3) + 0.5 * every4(-1)
        blk = lax.broadcasted_iota(jnp.int32, (n_blk, TQ), 0)
        cur = lax.shift_right_logical(t_lane, SEL_BLOCK.bit_length() - 1)
        causal = blk <= cur
        forced = jnp.where(blk == 0, 1.0, 0.0) + jnp.where(blk == cur, 1.0, 0.0) \
            + jnp.where(blk == cur - 1, 1.0, 0.0)
        val = jnp.where(causal, jnp.where(forced > 0.0, -2.0, imp), -1.0)

        def pick(_, v):
            blk_f = rowt_ref[0:n_blk, :]
            mx = jnp.max(v, axis=0, keepdims=True)
            idx = jnp.min(jnp.where(v == mx, blk_f, float(n_blk)), axis=0, keepdims=True)
            return jnp.where(blk_f == idx, -2.0, v)

        picked = lax.fori_loop(0, min(N_SEL, n_blk) - N_FORCED, pick, val, unroll=True)
        madd_ref[0:n_blk, :] = jnp.where(causal, jnp.where(picked < -1.5, 0.0, NEG_INF), NEG_INF)
        madd_ref[0:1, :] = jnp.where(i >= 1, NEG_INF, madd_ref[0:1, :])
        n = jnp.int32(0)
        for c in range(n_blk // BLOCKS_PER_CHUNK):
            rows = madd_ref[c * BLOCKS_PER_CHUNK:(c + 1) * BLOCKS_PER_CHUNK, :]
            active = (jnp.max(rows) > 0.5 * NEG_INF) & (c < c_diag)
            list_ref[n] = c
            n = n + active.astype(jnp.int32)
        count_ref[0] = n

    c_diag = q0 // SEL_CHUNK
    bucket = ((q0 + TQ) // CMP_STRIDE - 1) // rows_per_bucket
    for b in range(CMP_BUCKETS):
        pl.when(bucket == b)(functools.partial(compress_and_select, (b + 1) * rows_per_bucket))


    @pl.when(i == 0)
    def _():
        key_row = rowt_ref[0:SEL_CHUNK, :]
        for r in range(GQA_GROUP):
            bias_ref[r] = slopes[r] * key_row

    m_ref[...] = jnp.full(m_ref.shape, NEG_INF, F32)
    acc_ref[...] = jnp.zeros(acc_ref.shape, F32)

    def attend(chunks, final):
        sizes = ([SEL_BLOCK] if final else []) + [SEL_CHUNK] * len(chunks)
        keys = ([ks_ref[0:SEL_BLOCK, :]] if final else []) + [
            ks_ref[pl.ds(pl.multiple_of(c * SEL_CHUNK, SEL_CHUNK), SEL_CHUNK), :] for c in chunks]
        s_all = _dot_nt(jnp.concatenate(keys, axis=0) if len(keys) > 1 else keys[0], q)
        offsets = np.cumsum([0] + sizes)
        s = [s_all[offsets[k]:offsets[k + 1]] for k in range(len(sizes))]

        madds, values, rel0 = [], [], []
        if final:
            madds.append(jnp.where(i >= 1, 0.0, NEG_INF))
            values.append(vst_ref[0, 0][:, 0:SEL_BLOCK])
            rel0.append((-q0).astype(F32))
        for k, c in enumerate(chunks):
            k0 = c * SEL_CHUNK
            madd = jnp.concatenate(
                [jnp.broadcast_to(madd_ref[pl.ds(c * BLOCKS_PER_CHUNK + b, 1), :], (SEL_BLOCK, TQ))
                 for b in range(BLOCKS_PER_CHUNK)], axis=0)
            if final and k == len(chunks) - 1:
                pos = k0 + lax.broadcasted_iota(jnp.int32, (SEL_CHUNK, TQ), 0)
                madd = jnp.where(pos <= t_lane, madd, NEG_INF)
            madds.append(madd)
            values.append(vst_ref[0, c])
            rel0.append((k0 - q0).astype(F32))

        p_cols = [[] for _ in sizes]
        alphas = []
        for r in range(GQA_GROUP):
            shifts = [slopes[r] * x for x in rel0]
            us = [sk[:, heads[r]] + bias_ref[r, 0:n, :] + mk for sk, mk, n in zip(s, madds, sizes)]
            m_old = m_ref[:, heads[r]]
            m_new = m_old
            for u, sh in zip(us, shifts):
                m_new = jnp.maximum(m_new, jnp.max(u, axis=0, keepdims=True) + sh)
            alphas.append(jnp.exp2(m_old - m_new))
            for k, (u, sh) in enumerate(zip(us, shifts)):
                p_cols[k].append(jnp.exp2((u - (m_new - sh)).astype(BF16)))
            m_ref[:, heads[r]] = m_new
        pv = _dot(values[0], jnp.concatenate(p_cols[0], axis=1))
        for k in range(1, len(sizes)):
            pv = pv + _dot(values[k], jnp.concatenate(p_cols[k], axis=1))
        acc_ref[...] = jnp.concatenate(alphas, axis=1) * acc_ref[...] + pv

    n_earlier = count_ref[0]
    list_ref[n_earlier] = c_diag

    def active_group(first, size, final):
        attend([list_ref[first + k] for k in range(size)], final)

    def group_body(p, carry):
        active_group(p * SEL_GROUP, SEL_GROUP, False)
        return carry

    n_full = n_earlier // SEL_GROUP
    lax.fori_loop(0, n_full, group_body, 0)
    done = n_full * SEL_GROUP
    for size in range(1, SEL_GROUP + 1):
        pl.when(n_earlier + 1 - done == size)(functools.partial(active_group, done, size, True))

    head_out = []
    for r in range(GQA_GROUP):
        gc = gate_ref[0, 3 * r + 0:3 * r + 1, :]
        gs = gate_ref[0, 3 * r + 1:3 * r + 2, :]
        gw = gate_ref[0, 3 * r + 2:3 * r + 3, :]
        acc = acc_ref[:, heads[r]]
        win = owin_ref[:, heads[r]]
        out_t = (gc * ocmp_ref[:, heads[r]]
                 + (gs / acc[HEAD_DIM:HEAD_DIM + 1]) * acc
                 + (gw / win[HEAD_DIM:HEAD_DIM + 1]) * win)
        head_out.append(out_t.T[:, 0:HEAD_DIM])
    for pair in range(GQA_GROUP // 2):
        o_ref[:, pair * LANES:(pair + 1) * LANES] = jnp.concatenate(
            head_out[2 * pair:2 * pair + 2], axis=1).astype(BF16)


def _attention(q, gates_t, kcmp, vcmp_t, ksel, vsel_t, kwin, vwin_t):
    assert TQ == SEL_CHUNK
    s = q.shape[0]
    nb = s // SEL_BLOCK
    nc = CMP_PER_SEL * nb
    gw = GQA_GROUP * HEAD_DIM
    tab_rows = nc + TQ // CMP_STRIDE + nc // CMP_BUCKETS
    n_rows = max(tab_rows, WIN_SPAN)
    row_tile = jnp.asarray(np.broadcast_to(np.arange(n_rows)[:, None], (n_rows, TQ))
                           .astype(np.float32))
    per_group = lambda a: pl.BlockSpec((1,) + a.shape[1:], lambda g, i: (g,) + (0,) * (a.ndim - 1),
                                       pipeline_mode=pl.Buffered(1))
    return pl.pallas_call(
        functools.partial(_attn_kernel, nb=nb),
        grid=(N_KV, s // TQ),
        in_specs=[pl.BlockSpec((TQ, gw), lambda g, i: (i, g)),
                  pl.BlockSpec((1, GATE_ROWS, TQ), lambda g, i: (g, 0, i)),
                  _const_spec(row_tile.shape),
                  _const_spec(kcmp.shape), per_group(vcmp_t),
                  _const_spec(ksel.shape), per_group(vsel_t),
                  _const_spec(kwin.shape), per_group(vwin_t)],
        out_specs=pl.BlockSpec((TQ, gw), lambda g, i: (i, g)),
        out_shape=jax.ShapeDtypeStruct((s, N_HEADS * HEAD_DIM), BF16),
        scratch_shapes=[pltpu.VMEM((TQ // LANES, SUBLANES + nc, LANES), F32),
                        pltpu.VMEM((nb, TQ), F32),
                        pltpu.VMEM((GQA_GROUP, SEL_CHUNK, TQ), F32),
                        pltpu.VMEM((GQA_GROUP, WIN_SPAN, TQ), F32),
                        pltpu.VMEM((GQA_GROUP, tab_rows, TQ), F32),
                        pltpu.VMEM((1, GQA_GROUP * TQ), F32),
                        pltpu.VMEM((V_ROWS, GQA_GROUP * TQ), F32),
                        pltpu.VMEM((V_ROWS, GQA_GROUP * TQ), F32),
                        pltpu.VMEM((V_ROWS, GQA_GROUP * TQ), F32),
                        pltpu.SMEM((1,), jnp.int32),
                        pltpu.SMEM((nb // BLOCKS_PER_CHUNK,), jnp.int32)],
        compiler_params=pltpu.CompilerParams(dimension_semantics=("arbitrary", "arbitrary"),
                                             vmem_limit_bytes=VMEM_LIMIT_BYTES),
        name="attn",
    )(q, gates_t, row_tile, kcmp, vcmp_t, ksel, vsel_t, kwin, vwin_t)


def _out_mlp_kernel(pool_ref, attn_ref, wout_ref, x_ref, ga1_ref, g_ref, sc_ref, sh_ref, w1_ref,
                    w2_ref, ga2_ref, o_ref, *, ff_chunk):
    mix = (_dot(pool_ref[...], wout_ref[0:POOL_WIDTH, :])
           + _dot(attn_ref[...], wout_ref[POOL_WIDTH:, :]))
    x1 = x_ref[...] + ga1_ref[...] * mix
    h = _rms_modulate(x1, g_ref[...], sc_ref[...], sh_ref[...]).astype(BF16)
    acc = jnp.zeros(x1.shape, F32)
    for c in range(D_FF // ff_chunk):
        a = _dot(h, w1_ref[:, c * ff_chunk:(c + 1) * ff_chunk])
        a = jnp.square(jnp.maximum(a, 0.0)).astype(BF16)
        acc = acc + _dot(a, w2_ref[c * ff_chunk:(c + 1) * ff_chunk, :])
    o_ref[...] = x1 + ga2_ref[...] * acc


def _out_mlp(pool_out, attn_out, w_out, x, ga1, g2, sc2, sh2, w1, w2, ga2, tm):
    s = x.shape[0]
    row = lambda w: pl.BlockSpec((tm, w), lambda i: (i, 0))
    vec = lambda w: pl.BlockSpec((1, w), lambda i: (0, 0))
    return pl.pallas_call(
        functools.partial(_out_mlp_kernel, ff_chunk=FF_CHUNK),
        grid=(s // tm,),
        in_specs=[row(POOL_WIDTH), row(N_HEADS * HEAD_DIM), _const_spec(w_out.shape),
                  row(D_MODEL), vec(D_MODEL), vec(D_MODEL), vec(D_MODEL), vec(D_MODEL),
                  _const_spec(w1.shape), _const_spec(w2.shape), vec(D_MODEL)],
        out_specs=row(D_MODEL),
        out_shape=jax.ShapeDtypeStruct((s, D_MODEL), F32),
        compiler_params=pltpu.CompilerParams(vmem_limit_bytes=VMEM_LIMIT_BYTES),
        name="out_mlp",
    )(pool_out, attn_out, w_out, x, ga1, g2, sc2, sh2, w1, w2, ga2)


def _pad_in_proj_weight(w_in):
    src_g = _OFF_G
    per_group = GQA_GROUP * N_BRANCH
    cols = [w_in[:, :src_g]]
    for gg in range(N_KV):
        cols += [w_in[:, src_g + gg * per_group:src_g + (gg + 1) * per_group],
                 jnp.zeros((D_MODEL, LANES - per_group), w_in.dtype)]
    return jnp.concatenate(cols, axis=1).astype(BF16)


def kernel(x, c, w_ada, b_ada, norm1_g, norm2_g, w_in, w_pool, pool_scale, q_gain, kc_gain,
           ks_gain, kw_gain, cmp_pos_k, cmp_w1_k, cmp_b1_k, cmp_w2_k, cmp_b2_k, cmp_pos_v,
           cmp_w1_v, cmp_b1_v, cmp_w2_v, cmp_b2_v, w_out, w_ff1, w_ff2):
    batch, s, _ = x.shape
    assert batch == 1 and w_ada.shape[0] == 1
    assert s % SEL_CHUNK == 0 and s >= WIN_SPAN and (s // SEL_BLOCK) & (s // SEL_BLOCK - 1) == 0
    tm = min(s, OUT_MLP_ROWS)
    x2 = x[0]

    mod = _ada(jnp.broadcast_to(c, (SUBLANES, D_MODEL)), w_ada[0], b_ada)[0:1]
    sh1, sc1, ga1, sh2, sc2, ga2 = [mod[:, k * D_MODEL:(k + 1) * D_MODEL] for k in range(6)]

    pair = lambda gain: jnp.tile(gain, (1, 2))
    (pool_out, q, kc, vc, ksel, vsel_t, kwin, vwin_t, gates_t) = _in_proj(
        x2, norm1_g, sc1, sh1, _pad_in_proj_weight(w_in[0]), w_pool[0].astype(BF16), pool_scale,
        pair(q_gain), pair(ks_gain), pair(kw_gain), min(s, IN_PROJ_ROWS))

    kcmp = _compress(kc, cmp_pos_k[0], cmp_w1_k[0].astype(BF16), cmp_b1_k,
                     cmp_w2_k[0].astype(BF16), cmp_b2_k, kc_gain, True)
    vcmp = _compress(vc, cmp_pos_v[0], cmp_w1_v[0].astype(BF16), cmp_b1_v,
                     cmp_w2_v[0].astype(BF16), cmp_b2_v, kc_gain, False)

    vcmp_t = vcmp.reshape(-1, N_KV, HEAD_DIM).transpose(1, 2, 0)
    vcmp_t = jnp.pad(vcmp_t, ((0, 0), (0, V_ROWS - HEAD_DIM), (0, 0)))
    attn_out = _attention(q, gates_t, kcmp, vcmp_t, ksel, vsel_t, kwin, vwin_t)

    out = _out_mlp(pool_out, attn_out, w_out[0].astype(BF16), x2, ga1, norm2_g, sc2, sh2,
                   w_ff1[0].astype(BF16), w_ff2[0].astype(BF16), ga2, tm)
    return out[None]
```

```python
import functools

import jax
import jax.numpy as jnp
import numpy as np
from jax import lax
from jax.experimental import pallas as pl
from jax.experimental.pallas import tpu as pltpu

F32 = jnp.float32
BF16 = jnp.bfloat16

LANES = 128
SUBLANES = 8
VMEM_LIMIT_BYTES = 56 * 1024 * 1024

D_MODEL = 1024
POOL_WIDTH = 512
POOL_WINDOWS = (2, 4, 8, 16)
POOL_GROUP = POOL_WIDTH // len(POOL_WINDOWS)
POOL_HALO = 16
HEAD_DIM = 64
N_HEADS = 8
N_KV = 2
GQA_GROUP = N_HEADS // N_KV
KV_WIDTH = N_KV * HEAD_DIM
N_BRANCH = 3
CMP_LEN = 32
CMP_STRIDE = 16
CMP_HIDDEN = 4 * HEAD_DIM
SEL_BLOCK = 64
N_SEL = 16
WINDOW = 512
D_FF = 4 * D_MODEL
NEG_INF = -1e30
N_FORCED = 3
EPS = 1e-6
LOG2E = 1.4426950408889634

TQ = 256
SEL_CHUNK = 256
BLOCKS_PER_CHUNK = SEL_CHUNK // SEL_BLOCK
SEL_GROUP = 6
CMP_BUCKETS = 8
WIN_SPAN = WINDOW + TQ
CMP_PER_SEL = SEL_BLOCK // CMP_STRIDE
GATE_ROWS = 16
V_ROWS = HEAD_DIM + 16
ADA_COLS = 1024
IN_PROJ_ROWS = 1024
OUT_MLP_ROWS = 512
FF_CHUNK = 1024

_Q_WIDTH = N_HEADS * HEAD_DIM
_OFF_U = 0
_OFF_Q = POOL_WIDTH
_OFF_KC = _OFF_Q + _Q_WIDTH
_OFF_VC = _OFF_KC + KV_WIDTH
_OFF_KS = _OFF_VC + KV_WIDTH
_OFF_VS = _OFF_KS + KV_WIDTH
_OFF_KW = _OFF_VS + KV_WIDTH
_OFF_VW = _OFF_KW + KV_WIDTH
_OFF_G = _OFF_VW + KV_WIDTH
_IN_PAD = _OFF_G + N_KV * LANES


def _dot(a, b):
    return jnp.dot(a, b, preferred_element_type=F32)


def _dot_nt(a, b):
    return lax.dot_general(a, b, (((1,), (1,)), ((), ())), preferred_element_type=F32)


def _const_spec(shape):
    nd = len(shape)
    return pl.BlockSpec(shape, lambda *_: (0,) * nd, pipeline_mode=pl.Buffered(1))


def _ada_kernel(c_ref, w_ref, b_ref, o_ref):
    o_ref[...] = jnp.dot(c_ref[...], w_ref[...], preferred_element_type=F32,
                         precision=lax.Precision.HIGHEST) + b_ref[...]


def _ada(c8, w, b):
    n = w.shape[1]
    bn = ADA_COLS
    return pl.pallas_call(
        _ada_kernel,
        grid=(n // bn,),
        in_specs=[pl.BlockSpec((SUBLANES, D_MODEL), lambda j: (0, 0)),
                  pl.BlockSpec((D_MODEL, bn), lambda j: (0, j)),
                  pl.BlockSpec((1, bn), lambda j: (0, j))],
        out_specs=pl.BlockSpec((SUBLANES, bn), lambda j: (0, j)),
        out_shape=jax.ShapeDtypeStruct((SUBLANES, n), F32),
        compiler_params=pltpu.CompilerParams(vmem_limit_bytes=VMEM_LIMIT_BYTES),
        name="ada",
    )(c8, w, b)


def _rms_modulate(x, g, sc, sh):
    ms = jnp.mean(x * x, axis=-1, keepdims=True)
    return (x * lax.rsqrt(ms + EPS)) * (g * (1.0 + sc)) + sh


def _head_norm_pair(x, gain2):
    lane = lax.broadcasted_iota(jnp.int32, x.shape, 1)
    lo = lane < HEAD_DIM
    sq = x * x
    s_lo = jnp.sum(jnp.where(lo, sq, 0.0), axis=-1, keepdims=True)
    s_hi = jnp.sum(jnp.where(lo, 0.0, sq), axis=-1, keepdims=True)
    ms = jnp.where(lo, s_lo, s_hi) * (1.0 / HEAD_DIM)
    return x * lax.rsqrt(ms + EPS) * gain2


def _in_proj_kernel(x_ref, g_ref, sc_ref, sh_ref, w_ref, wpool_ref, pscale_ref, qg_ref, ksg_ref,
                    kwg_ref, pool_ref, q_ref, kc_ref, vc_ref, ks_ref, vs_ref, kw_ref, vw_ref,
                    gate_ref, ebuf_ref, *, tm):
    i = pl.program_id(0)
    h = _rms_modulate(x_ref[...], g_ref[...], sc_ref[...], sh_ref[...]).astype(BF16)

    def project(off, width):
        return _dot(h, w_ref[:, off:off + width])

    q_all = project(_OFF_Q, _Q_WIDTH)
    for pair in range(N_HEADS // 2):
        qn = _head_norm_pair(q_all[:, pair * LANES:(pair + 1) * LANES], qg_ref[...])
        q_ref[:, pair * LANES:(pair + 1) * LANES] = (qn * (HEAD_DIM ** -0.5 * LOG2E)).astype(BF16)

    kv = project(_OFF_KC, _IN_PAD - _OFF_KC)
    kv_cols = lambda off, width: kv[:, off - _OFF_KC:off - _OFF_KC + width]
    chunked = (tm // CMP_STRIDE, CMP_STRIDE, KV_WIDTH)
    kc_ref[...] = kv_cols(_OFF_KC, KV_WIDTH).reshape(chunked)
    vc_ref[...] = kv_cols(_OFF_VC, KV_WIDTH).reshape(chunked)
    ks_ref[...] = _head_norm_pair(kv_cols(_OFF_KS, KV_WIDTH), ksg_ref[...]).astype(BF16)
    kw_ref[...] = _head_norm_pair(kv_cols(_OFF_KW, KV_WIDTH), kwg_ref[...]).astype(BF16)
    def value_tiles(ref, off, chunk):
        ones_row = jnp.where(lax.broadcasted_iota(jnp.int32, (V_ROWS - HEAD_DIM, chunk), 0) == 0,
                             1.0, 0.0)
        for k in range(tm // chunk):
            vt = kv_cols(off, KV_WIDTH)[k * chunk:(k + 1) * chunk].T
            for gg in range(N_KV):
                ref[gg, k] = jnp.concatenate(
                    [vt[gg * HEAD_DIM:(gg + 1) * HEAD_DIM], ones_row], axis=0).astype(BF16)

    value_tiles(vs_ref, _OFF_VS, SEL_CHUNK)
    value_tiles(vw_ref, _OFF_VW, TQ)
    gates_t = jax.nn.sigmoid(kv_cols(_OFF_G, N_KV * LANES)).T
    for gg in range(N_KV):
        gate_ref[gg] = gates_t[gg * LANES:gg * LANES + GATE_ROWS]

    @pl.when(i == 0)
    def _():
        ebuf_ref[0:POOL_HALO, :] = jnp.zeros((POOL_HALO, POOL_WIDTH), F32)

    u = project(_OFF_U, POOL_WIDTH)
    ebuf_ref[POOL_HALO:POOL_HALO + tm, :] = u
    t1 = i * tm + lax.broadcasted_iota(jnp.int32, (tm, POOL_GROUP), 0) + 1
    for gi, w in enumerate(POOL_WINDOWS):
        c0 = gi * POOL_GROUP
        win = u[:, c0:c0 + POOL_GROUP]
        for k in range(1, w):
            win = win + ebuf_ref[POOL_HALO - k:POOL_HALO - k + tm, c0:c0 + POOL_GROUP]
        cnt = jnp.minimum(t1, w).astype(F32)
        pooled = win / cnt - u[:, c0:c0 + POOL_GROUP]
        y = _dot(pooled.astype(BF16), wpool_ref[gi])
        pool_ref[:, c0:c0 + POOL_GROUP] = (y * pscale_ref[:, c0:c0 + POOL_GROUP]).astype(BF16)
    ebuf_ref[0:POOL_HALO, :] = ebuf_ref[tm:tm + POOL_HALO, :]


def _in_proj(x, g1, sc1, sh1, w_in_p, w_pool, pool_scale, qg, ksg, kwg, tm):
    s = x.shape[0]
    row = lambda w: pl.BlockSpec((tm, w), lambda i: (i, 0))
    vec = lambda w: pl.BlockSpec((1, w), lambda i: (0, 0))
    per_head = pl.BlockSpec((tm // CMP_STRIDE, CMP_STRIDE, KV_WIDTH), lambda i: (i, 0, 0))
    chunks_t = lambda ch: pl.BlockSpec((N_KV, tm // ch, V_ROWS, ch), lambda i: (0, i, 0, 0))
    out_shape = [
        jax.ShapeDtypeStruct((s, POOL_WIDTH), BF16),
        jax.ShapeDtypeStruct((s, _Q_WIDTH), BF16),
        jax.ShapeDtypeStruct((s // CMP_STRIDE, CMP_STRIDE, KV_WIDTH), F32),
        jax.ShapeDtypeStruct((s // CMP_STRIDE, CMP_STRIDE, KV_WIDTH), F32),
        jax.ShapeDtypeStruct((s, KV_WIDTH), BF16),
        jax.ShapeDtypeStruct((N_KV, s // SEL_CHUNK, V_ROWS, SEL_CHUNK), BF16),
        jax.ShapeDtypeStruct((s, KV_WIDTH), BF16),
        jax.ShapeDtypeStruct((N_KV, s // TQ, V_ROWS, TQ), BF16),
        jax.ShapeDtypeStruct((N_KV, GATE_ROWS, s), F32),
    ]
    return pl.pallas_call(
        functools.partial(_in_proj_kernel, tm=tm),
        grid=(s // tm,),
        in_specs=[row(D_MODEL), vec(D_MODEL), vec(D_MODEL), vec(D_MODEL),
                  _const_spec(w_in_p.shape), _const_spec(w_pool.shape), vec(POOL_WIDTH),
                  vec(LANES), vec(LANES), vec(LANES)],
        out_specs=[row(POOL_WIDTH), row(_Q_WIDTH), per_head, per_head, row(KV_WIDTH),
                   chunks_t(SEL_CHUNK), row(KV_WIDTH), chunks_t(TQ),
                   pl.BlockSpec((N_KV, GATE_ROWS, tm), lambda i: (0, 0, i))],
        out_shape=out_shape,
        scratch_shapes=[pltpu.VMEM((tm + POOL_HALO, POOL_WIDTH), F32)],
        compiler_params=pltpu.CompilerParams(dimension_semantics=("arbitrary",),
                                             vmem_limit_bytes=VMEM_LIMIT_BYTES),
        name="in_proj",
    )(x, g1, sc1, sh1, w_in_p, w_pool, pool_scale, qg, ksg, kwg)


def _compress_kernel(c_ref, pos_ref, w1_ref, b1_ref, w2_ref, b2_ref, gain_ref, o_ref, *,
                     normalize):
    n_rows = c_ref.shape[0]
    first = second = None
    by_offset = jnp.transpose(c_ref[...], (1, 0, 2))
    for l in range(CMP_STRIDE):
        tok = by_offset[l]
        a = _dot((tok + pos_ref[l:l + 1, :]).astype(BF16), w1_ref[l])
        b = _dot((tok + pos_ref[CMP_STRIDE + l:CMP_STRIDE + l + 1, :]).astype(BF16),
                 w1_ref[CMP_STRIDE + l])
        first = a if first is None else first + a
        second = b if second is None else second + b
    hid = jax.nn.gelu(first + pltpu.roll(second, n_rows - 1, axis=0) + b1_ref[...])
    y = _dot(hid.astype(BF16), w2_ref[...]) + b2_ref[...]
    if normalize:
        y = _head_norm_pair(y, gain_ref[...])
    o_ref[...] = y.astype(BF16)


def _compress(c3, pos, w1, b1, w2, b2, gain, normalize):
    n_chunks = c3.shape[0]
    blockdiag = lambda w: jnp.concatenate(
        [jnp.concatenate([w, jnp.zeros_like(w)], axis=-1),
         jnp.concatenate([jnp.zeros_like(w), w], axis=-1)], axis=-2)
    pair = lambda v: jnp.tile(v, (1, N_KV))
    w1_l = blockdiag(w1.reshape(CMP_LEN, HEAD_DIM, CMP_HIDDEN))
    args = (c3, pair(pos.reshape(CMP_LEN, HEAD_DIM)), w1_l, pair(b1), blockdiag(w2), pair(b2),
            pair(gain))
    return pl.pallas_call(
        functools.partial(_compress_kernel, normalize=normalize),
        grid=(1,),
        in_specs=[_const_spec(a.shape) for a in args],
        out_specs=pl.BlockSpec((n_chunks, KV_WIDTH), lambda i: (0, 0)),
        out_shape=jax.ShapeDtypeStruct((n_chunks, KV_WIDTH), BF16),
        compiler_params=pltpu.CompilerParams(vmem_limit_bytes=VMEM_LIMIT_BYTES),
        name="compress",
    )(*args)


def _attn_kernel(q_ref, gate_ref, rowt_ref, kc_ref, vct_ref, ks_ref, vst_ref, kw_ref, vwt_ref,
                 o_ref, psum_ref, madd_ref, bias_ref, wbias_ref, ctab_ref, m_ref, acc_ref, ocmp_ref,
                 owin_ref, count_ref, list_ref, *, nb):
    g = pl.program_id(0)
    i = pl.program_id(1)
    q0 = i * TQ
    nc = CMP_PER_SEL * nb
    n_chunks = nb // BLOCKS_PER_CHUNK
    gslope = jnp.where(g == 0, LOG2E, LOG2E * 2.0 ** -GQA_GROUP).astype(F32)
    slopes = [gslope * (2.0 ** -(r + 1)) for r in range(GQA_GROUP)]
    heads = [slice(r * TQ, (r + 1) * TQ) for r in range(GQA_GROUP)]

    lane = lax.broadcasted_iota(jnp.int32, (TQ, LANES), 1)
    own_half = (lane >= HEAD_DIM) == (g == 1)
    q_rows = []
    for r in range(GQA_GROUP):
        pair = q_ref[:, (r // 2) * LANES:(r // 2 + 1) * LANES]
        swapped = jnp.concatenate([pair[:, HEAD_DIM:], pair[:, :HEAD_DIM]], axis=1)
        q_rows.append(jnp.where(own_half, jnp.where(g == r % 2, pair, swapped), 0.0))
    q = jnp.concatenate(q_rows, axis=0)
    t_lane = q0 + lax.broadcasted_iota(jnp.int32, (1, TQ), 1)

    w0 = pl.multiple_of(jnp.maximum(q0 - WINDOW, 0), TQ)
    s_w = _dot_nt(kw_ref[pl.ds(w0, WIN_SPAN), :], q)

    @pl.when(i <= WINDOW // TQ)
    def _():
        row_w = rowt_ref[0:WIN_SPAN, :]
        lane_w = lax.broadcasted_iota(jnp.int32, (WIN_SPAN, TQ), 1).astype(F32)
        dq = (q0 - w0).astype(F32)
        d_w = lane_w - row_w + dq
        ok_w = jnp.abs(d_w - (WINDOW - 1) * 0.5) < WINDOW * 0.5
        rel_w = row_w - dq
        for r in range(GQA_GROUP):
            wbias_ref[r] = jnp.where(ok_w, slopes[r] * rel_w, NEG_INF)

    wb = w0 // TQ
    for r in range(GQA_GROUP):
        s = s_w[:, heads[r]] + wbias_ref[r]
        m = jnp.max(s, axis=0, keepdims=True)
        p = jnp.exp2((s - m).astype(BF16))
        o_win = _dot(vwt_ref[0, wb], p[0:TQ])
        for b in range(1, WIN_SPAN // TQ):
            o_win = o_win + _dot(vwt_ref[0, wb + b], p[b * TQ:(b + 1) * TQ])
        owin_ref[:, heads[r]] = o_win

    rows_per_bucket = nc // CMP_BUCKETS
    tab_rows = ctab_ref.shape[1]

    @pl.when(i == 0)
    def _():
        rel = CMP_STRIDE * (rowt_ref[0:tab_rows, :] - nc) + (CMP_LEN - 1)
        ok = rel <= lax.broadcasted_iota(jnp.int32, (1, TQ), 1).astype(F32)
        for r in range(GQA_GROUP):
            ctab_ref[r] = jnp.where(ok, slopes[r] * rel, NEG_INF)

    tab0 = pl.multiple_of(nc - q0 // CMP_STRIDE, CMP_STRIDE)

    def compress_and_select(n_rows):
        n_blk = n_rows // CMP_PER_SEL
        s_c = _dot_nt(kc_ref[0:n_rows, :], q)
        p_sum = jnp.zeros((n_rows, TQ), F32)
        p_cols = []
        for r in range(GQA_GROUP):
            s = s_c[:, heads[r]] + ctab_ref[r, pl.ds(tab0, n_rows), :]
            m = jnp.max(s, axis=0, keepdims=True)
            e = jnp.exp2(s - m)
            l = jnp.sum(e, axis=0, keepdims=True)
            p = e * jnp.where(m > 0.5 * NEG_INF, 1.0 / l, 0.0)
            p_sum = p_sum + p
            p_cols.append(p.astype(BF16))
        ocmp_ref[...] = _dot(vct_ref[0, :, 0:n_rows], jnp.concatenate(p_cols, axis=1))

        for h in range(TQ // LANES):
            psum_ref[h, 0:SUBLANES, :] = jnp.zeros((SUBLANES, LANES), F32)
            psum_ref[h, SUBLANES:SUBLANES + n_rows, :] = p_sum[:, h * LANES:(h + 1) * LANES]

        def every4(start):
            parts = [psum_ref[h, pl.ds(SUBLANES + start, n_blk, stride=CMP_PER_SEL), :]
                     for h in range(TQ // LANES)]
            return parts[0] if len(parts) == 1 else jnp.concatenate(parts, axis=1)

        imp = every4(0) + every4(1) + every4(2) + 0.5 * every4(3) + 0.5 * every4(-1)
        blk = lax.broadcasted_iota(jnp.int32, (n_blk, TQ), 0)
        cur = lax.shift_right_logical(t_lane, SEL_BLOCK.bit_length() - 1)
        causal = blk <= cur
        forced = jnp.where(blk == 0, 1.0, 0.0) + jnp.where(blk == cur, 1.0, 0.0) \
            + jnp.where(blk == cur - 1, 1.0, 0.0)
        val = jnp.where(causal, jnp.where(forced > 0.0, -2.0, imp), -1.0)

        def pick(_, v):
            blk_f = rowt_ref[0:n_blk, :]
            mx = jnp.max(v, axis=0, keepdims=True)
            idx = jnp.min(jnp.where(v == mx, blk_f, float(n_blk)), axis=0, keepdims=True)
            return jnp.where(blk_f == idx, -2.0, v)

        picked = lax.fori_loop(0, min(N_SEL, n_blk) - N_FORCED, pick, val, unroll=True)
        madd_ref[0:n_blk, :] = jnp.where(causal, jnp.where(picked < -1.5, 0.0, NEG_INF), NEG_INF)
        madd_ref[0:1, :] = jnp.where(i >= 1, NEG_INF, madd_ref[0:1, :])
        n = jnp.int32(0)
        for c in range(n_blk // BLOCKS_PER_CHUNK):
            rows = madd_ref[c * BLOCKS_PER_CHUNK:(c + 1) * BLOCKS_PER_CHUNK, :]
            active = (jnp.max(rows) > 0.5 * NEG_INF) & (c < c_diag)
            list_ref[n] = c
            n = n + active.astype(jnp.int32)
        count_ref[0] = n

    c_diag = q0 // SEL_CHUNK
    bucket = ((q0 + TQ) // CMP_STRIDE - 1) // rows_per_bucket
    for b in range(CMP_BUCKETS):
        pl.when(bucket == b)(functools.partial(compress_and_select, (b + 1) * rows_per_bucket))


    @pl.when(i == 0)
    def _():
        key_row = rowt_ref[0:SEL_CHUNK, :]
        for r in range(GQA_GROUP):
            bias_ref[r] = slopes[r] * key_row

    m_ref[...] = jnp.full(m_ref.shape, NEG_INF, F32)
    acc_ref[...] = jnp.zeros(acc_ref.shape, F32)

    def attend(chunks, final):
        sizes = ([SEL_BLOCK] if final else []) + [SEL_CHUNK] * len(chunks)
        keys = ([ks_ref[0:SEL_BLOCK, :]] if final else []) + [
            ks_ref[pl.ds(pl.multiple_of(c * SEL_CHUNK, SEL_CHUNK), SEL_CHUNK), :] for c in chunks]
        s_all = _dot_nt(jnp.concatenate(keys, axis=0) if len(keys) > 1 else keys[0], q)
        offsets = np.cumsum([0] + sizes)
        s = [s_all[offsets[k]:offsets[k + 1]] for k in range(len(sizes))]

        madds, values, rel0 = [], [], []
        if final:
            madds.append(jnp.where(i >= 1, 0.0, NEG_INF))
            values.append(vst_ref[0, 0][:, 0:SEL_BLOCK])
            rel0.append((-q0).astype(F32))
        for k, c in enumerate(chunks):
            k0 = c * SEL_CHUNK
            madd = jnp.concatenate(
                [jnp.broadcast_to(madd_ref[pl.ds(c * BLOCKS_PER_CHUNK + b, 1), :], (SEL_BLOCK, TQ))
                 for b in range(BLOCKS_PER_CHUNK)], axis=0)
            if final and k == len(chunks) - 1:
                pos = k0 + lax.broadcasted_iota(jnp.int32, (SEL_CHUNK, TQ), 0)
                madd = jnp.where(pos <= t_lane, madd, NEG_INF)
            madds.append(madd)
            values.append(vst_ref[0, c])
            rel0.append((k0 - q0).astype(F32))

        p_cols = [[] for _ in sizes]
        alphas = []
        for r in range(GQA_GROUP):
            shifts = [slopes[r] * x for x in rel0]
            us = [sk[:, heads[r]] + bias_ref[r, 0:n, :] + mk for sk, mk, n in zip(s, madds, sizes)]
            m_old = m_ref[:, heads[r]]
            m_new = m_old
            for u, sh in zip(us, shifts):
                m_new = jnp.maximum(m_new, jnp.max(u, axis=0, keepdims=True) + sh)
            alphas.append(jnp.exp2(m_old - m_new))
            for k, (u, sh) in enumerate(zip(us, shifts)):
                p_cols[k].append(jnp.exp2((u - (m_new - sh)).astype(BF16)))
            m_ref[:, heads[r]] = m_new
        pv = _dot(values[0], jnp.concatenate(p_cols[0], axis=1))
        for k in range(1, len(sizes)):
            pv = pv + _dot(values[k], jnp.concatenate(p_cols[k], axis=1))
        acc_ref[...] = jnp.concatenate(alphas, axis=1) * acc_ref[...] + pv

    n_earlier = count_ref[0]
    list_ref[n_earlier] = c_diag

    def active_group(first, size, final):
        attend([list_ref[first + k] for k in range(size)], final)

    def group_body(p, carry):
        active_group(p * SEL_GROUP, SEL_GROUP, False)
        return carry

    n_full = n_earlier // SEL_GROUP
    lax.fori_loop(0, n_full, group_body, 0)
    done = n_full * SEL_GROUP
    for size in range(1, SEL_GROUP + 1):
        pl.when(n_earlier + 1 - done == size)(functools.partial(active_group, done, size, True))

    head_out = []
    for r in range(GQA_GROUP):
        gc = gate_ref[0, 3 * r + 0:3 * r + 1, :]
        gs = gate_ref[0, 3 * r + 1:3 * r + 2, :]
        gw = gate_ref[0, 3 * r + 2:3 * r + 3, :]
        acc = acc_ref[:, heads[r]]
        win = owin_ref[:, heads[r]]
        out_t = (gc * ocmp_ref[:, heads[r]]
                 + (gs / acc[HEAD_DIM:HEAD_DIM + 1]) * acc
                 + (gw / win[HEAD_DIM:HEAD_DIM + 1]) * win)
        head_out.append(out_t.T[:, 0:HEAD_DIM])
    for pair in range(GQA_GROUP // 2):
        o_ref[:, pair * LANES:(pair + 1) * LANES] = jnp.concatenate(
            head_out[2 * pair:2 * pair + 2], axis=1).astype(BF16)


def _attention(q, gates_t, kcmp, vcmp_t, ksel, vsel_t, kwin, vwin_t):
    assert TQ == SEL_CHUNK
    s = q.shape[0]
    nb = s // SEL_BLOCK
    nc = CMP_PER_SEL * nb
    gw = GQA_GROUP * HEAD_DIM
    tab_rows = nc + TQ // CMP_STRIDE + nc // CMP_BUCKETS
    n_rows = max(tab_rows, WIN_SPAN)
    row_tile = jnp.asarray(np.broadcast_to(np.arange(n_rows)[:, None], (n_rows, TQ))
                           .astype(np.float32))
    per_group = lambda a: pl.BlockSpec((1,) + a.shape[1:], lambda g, i: (g,) + (0,) * (a.ndim - 1),
                                       pipeline_mode=pl.Buffered(1))
    return pl.pallas_call(
        functools.partial(_attn_kernel, nb=nb),
        grid=(N_KV, s // TQ),
        in_specs=[pl.BlockSpec((TQ, gw), lambda g, i: (i, g)),
                  pl.BlockSpec((1, GATE_ROWS, TQ), lambda g, i: (g, 0, i)),
                  _const_spec(row_tile.shape),
                  _const_spec(kcmp.shape), per_group(vcmp_t),
                  _const_spec(ksel.shape), per_group(vsel_t),
                  _const_spec(kwin.shape), per_group(vwin_t)],
        out_specs=pl.BlockSpec((TQ, gw), lambda g, i: (i, g)),
        out_shape=jax.ShapeDtypeStruct((s, N_HEADS * HEAD_DIM), BF16),
        scratch_shapes=[pltpu.VMEM((TQ // LANES, SUBLANES + nc, LANES), F32),
                        pltpu.VMEM((nb, TQ), F32),
                        pltpu.VMEM((GQA_GROUP, SEL_CHUNK, TQ), F32),
                        pltpu.VMEM((GQA_GROUP, WIN_SPAN, TQ), F32),
                        pltpu.VMEM((GQA_GROUP, tab_rows, TQ), F32),
                        pltpu.VMEM((1, GQA_GROUP * TQ), F32),
                        pltpu.VMEM((V_ROWS, GQA_GROUP * TQ), F32),
                        pltpu.VMEM((V_ROWS, GQA_GROUP * TQ), F32),
                        pltpu.VMEM((V_ROWS, GQA_GROUP * TQ), F32),
                        pltpu.SMEM((1,), jnp.int32),
                        pltpu.SMEM((nb // BLOCKS_PER_CHUNK,), jnp.int32)],
        compiler_params=pltpu.CompilerParams(dimension_semantics=("arbitrary", "arbitrary"),
                                             vmem_limit_bytes=VMEM_LIMIT_BYTES),
        name="attn",
    )(q, gates_t, row_tile, kcmp, vcmp_t, ksel, vsel_t, kwin, vwin_t)


def _out_mlp_kernel(pool_ref, attn_ref, wout_ref, x_ref, ga1_ref, g_ref, sc_ref, sh_ref, w1_ref,
                    w2_ref, ga2_ref, o_ref, *, ff_chunk):
    mix = (_dot(pool_ref[...], wout_ref[0:POOL_WIDTH, :])
           + _dot(attn_ref[...], wout_ref[POOL_WIDTH:, :]))
    x1 = x_ref[...] + ga1_ref[...] * mix
    h = _rms_modulate(x1, g_ref[...], sc_ref[...], sh_ref[...]).astype(BF16)
    acc = jnp.zeros(x1.shape, F32)
    for c in range(D_FF // ff_chunk):
        a = _dot(h, w1_ref[:, c * ff_chunk:(c + 1) * ff_chunk])
        a = jnp.square(jnp.maximum(a, 0.0)).astype(BF16)
        acc = acc + _dot(a, w2_ref[c * ff_chunk:(c + 1) * ff_chunk, :])
    o_ref[...] = x1 + ga2_ref[...] * acc


def _out_mlp(pool_out, attn_out, w_out, x, ga1, g2, sc2, sh2, w1, w2, ga2, tm):
    s = x.shape[0]
    row = lambda w: pl.BlockSpec((tm, w), lambda i: (i, 0))
    vec = lambda w: pl.BlockSpec((1, w), lambda i: (0, 0))
    return pl.pallas_call(
        functools.partial(_out_mlp_kernel, ff_chunk=FF_CHUNK),
        grid=(s // tm,),
        in_specs=[row(POOL_WIDTH), row(N_HEADS * HEAD_DIM), _const_spec(w_out.shape),
                  row(D_MODEL), vec(D_MODEL), vec(D_MODEL), vec(D_MODEL), vec(D_MODEL),
                  _const_spec(w1.shape), _const_spec(w2.shape), vec(D_MODEL)],
        out_specs=row(D_MODEL),
        out_shape=jax.ShapeDtypeStruct((s, D_MODEL), F32),
        compiler_params=pltpu.CompilerParams(vmem_limit_bytes=VMEM_LIMIT_BYTES),
        name="out_mlp",
    )(pool_out, attn_out, w_out, x, ga1, g2, sc2, sh2, w1, w2, ga2)


def _pad_in_proj_weight(w_in):
    src_g = _OFF_G
    per_group = GQA_GROUP * N_BRANCH
    cols = [w_in[:, :src_g]]
    for gg in range(N_KV):
        cols += [w_in[:, src_g + gg * per_group:src_g + (gg + 1) * per_group],
                 jnp.zeros((D_MODEL, LANES - per_group), w_in.dtype)]
    return jnp.concatenate(cols, axis=1).astype(BF16)


def kernel(x, c, w_ada, b_ada, norm1_g, norm2_g, w_in, w_pool, pool_scale, q_gain, kc_gain,
           ks_gain, kw_gain, cmp_pos_k, cmp_w1_k, cmp_b1_k, cmp_w2_k, cmp_b2_k, cmp_pos_v,
           cmp_w1_v, cmp_b1_v, cmp_w2_v, cmp_b2_v, w_out, w_ff1, w_ff2):
    batch, s, _ = x.shape
    assert batch == 1 and w_ada.shape[0] == 1
    assert s % SEL_CHUNK == 0 and s >= WIN_SPAN and (s // SEL_BLOCK) & (s // SEL_BLOCK - 1) == 0
    tm = min(s, OUT_MLP_ROWS)
    x2 = x[0]

    mod = _ada(jnp.broadcast_to(c, (SUBLANES, D_MODEL)), w_ada[0], b_ada)[0:1]
    sh1, sc1, ga1, sh2, sc2, ga2 = [mod[:, k * D_MODEL:(k + 1) * D_MODEL] for k in range(6)]

    pair = lambda gain: jnp.tile(gain, (1, 2))
    (pool_out, q, kc, vc, ksel, vsel_t, kwin, vwin_t, gates_t) = _in_proj(
        x2, norm1_g, sc1, sh1, _pad_in_proj_weight(w_in[0]), w_pool[0].astype(BF16), pool_scale,
        pair(q_gain), pair(ks_gain), pair(kw_gain), min(s, IN_PROJ_ROWS))

    kcmp = _compress(kc, cmp_pos_k[0], cmp_w1_k[0].astype(BF16), cmp_b1_k,
                     cmp_w2_k[0].astype(BF16), cmp_b2_k, kc_gain, True)
    vcmp = _compress(vc, cmp_pos_v[0], cmp_w1_v[0].astype(BF16), cmp_b1_v,
                     cmp_w2_v[0].astype(BF16), cmp_b2_v, kc_gain, False)

    vcmp_t = vcmp.reshape(-1, N_KV, HEAD_DIM).transpose(1, 2, 0)
    vcmp_t = jnp.pad(vcmp_t, ((0, 0), (0, V_ROWS - HEAD_DIM), (0, 0)))
    attn_out = _attention(q, gates_t, kcmp, vcmp_t, ksel, vsel_t, kwin, vwin_t)

    out = _out_mlp(pool_out, attn_out, w_out[0].astype(BF16), x2, ga1, norm2_g, sc2, sh2,
                   w_ff1[0].astype(BF16), w_ff2[0].astype(BF16), ga2, tm)
    return out[None]
```
